```python
import jax, jax.numpy as jnp
from jax import lax
import numpy as np

D_MODEL = 1024
BATCH = 16
SEQ = 2048
DEPTH = 2

GRID_W = 64
CTX_LEN = 256
HEAD_DIM = 64
N_MIXERS = 4
GROUP_WIDTH = D_MODEL // N_MIXERS
GROUP_HEADS = GROUP_WIDTH // HEAD_DIM
WIN_ROWS = 8
WIN_COLS = 16
COL_QBLOCK = 16
COL_KBLOCK = WIN_COLS + COL_QBLOCK
GMLP_CHUNK = 128
MLSTM_CHUNK = 128
CONV_W = 3
FNET_GROUPS = 4
ROPE_THETA = 10000.0
D_FF = 4 * D_MODEL
N_MOD = 6
EPS = 1e-6
NEG_INF = -1e30
N_GATES = 4 * GROUP_HEADS
OFF_A = 0
OFF_B = 3 * GROUP_WIDTH
OFF_C = 5 * GROUP_WIDTH
OFF_D = 9 * GROUP_WIDTH
OFF_G = 10 * GROUP_WIDTH
D_IN = OFF_G + N_GATES

kernel_name = 'hybrid_nat_gmlp_mlstm_fnet_dit_block'


def rms_norm(x, g):
    x32 = x.astype(jnp.float32)
    y = x32 * lax.rsqrt(jnp.mean(x32 * x32, axis=-1, keepdims=True) + EPS)
    return (y * g.astype(jnp.float32)).astype(x.dtype)


def split_heads(x):
    b, t, _ = x.shape
    return x.reshape(b, t, GROUP_HEADS, HEAD_DIM).transpose(0, 2, 1, 3)


def merge_heads(x):
    b, h, t, d = x.shape
    return x.transpose(0, 2, 1, 3).reshape(b, t, h * d)


def sq_relu_mlp(h, w1, w2):
    return jnp.square(jax.nn.relu(h @ w1)) @ w2


def dense_attention(q, k, v):
    s = jnp.einsum('bhqd,bhkd->bhqk', q, k).astype(jnp.float32) * HEAD_DIM ** -0.5
    p = jax.nn.softmax(s, axis=-1).astype(v.dtype)
    return jnp.einsum('bhqk,bhkd->bhqd', p, v)


def neighbourhood_attention(q, k, v, k_ctx, v_ctx, rpb):
    b, h, s, d = q.shape
    rows = s // GRID_W
    wr = min(WIN_ROWS, rows)
    ncb = GRID_W // COL_QBLOCK
    nk = wr * COL_KBLOCK
    r = jnp.arange(rows)
    row_idx = jnp.clip(r - wr // 2, 0, rows - wr)[:, None] + jnp.arange(wr)[None, :]
    c0 = jnp.arange(ncb) * COL_QBLOCK
    col_idx = jnp.clip(c0 - WIN_COLS // 2, 0, GRID_W - COL_KBLOCK)[:, None] + jnp.arange(COL_KBLOCK)[None, :]
    qcol = c0[:, None] + jnp.arange(COL_QBLOCK)[None, :]
    qstart = jnp.clip(qcol - WIN_COLS // 2, 0, GRID_W - WIN_COLS)
    in_win = (col_idx[:, None, :] >= qstart[:, :, None]) & (col_idx[:, None, :] < qstart[:, :, None] + WIN_COLS)
    d_row = row_idx[:, None, None, :, None] - r[:, None, None, None, None]
    d_col = col_idx[None, :, None, None, :] - qcol[None, :, :, None, None]
    bias = rpb[:, d_row + WIN_ROWS - 1, jnp.clip(d_col + WIN_COLS - 1, 0, 2 * WIN_COLS - 2)].astype(jnp.float32)
    bias = jnp.where(in_win[None, None, :, :, None, :], bias, NEG_INF).reshape(h, rows, ncb, COL_QBLOCK, nk)
    qb = q.reshape(b, h, rows, ncb, COL_QBLOCK, d)
    ri = row_idx[:, None, :, None]
    ci = col_idx[None, :, None, :]
    kb = k.reshape(b, h, rows, GRID_W, d)[:, :, ri, ci].reshape(b, h, rows, ncb, nk, d)
    vb = v.reshape(b, h, rows, GRID_W, d)[:, :, ri, ci].reshape(b, h, rows, ncb, nk, d)
    scale = HEAD_DIM ** -0.5
    s_loc = jnp.einsum('bhrcqd,bhrckd->bhrcqk', qb, kb).astype(jnp.float32) * scale + bias
    s_ctx = jnp.einsum('bhrcqd,bhkd->bhrcqk', qb, k_ctx).astype(jnp.float32) * scale
    p = jax.nn.softmax(jnp.concatenate([s_loc, s_ctx], axis=-1), axis=-1).astype(v.dtype)
    out = (jnp.einsum('bhrcqk,bhrckd->bhrcqd', p[..., :nk], vb)
           + jnp.einsum('bhrcqk,bhkd->bhrcqd', p[..., nk:], v_ctx))
    return out.reshape(b, h, s, d)


def spatial_gating(p_uz, w_s, b_s, g_z):
    uz = jax.nn.gelu(p_uz)
    u, z = uz[..., :GROUP_WIDTH], uz[..., GROUP_WIDTH:]
    z = rms_norm(z, g_z)
    b, t, _ = z.shape
    zc = z.reshape(b, t // GMLP_CHUNK, GMLP_CHUNK, GROUP_HEADS, HEAD_DIM)
    mixed = jnp.einsum('hpq,bnqhd->bnphd', w_s, zc) + b_s.T[:, :, None]
    return u * mixed.reshape(b, t, GROUP_WIDTH)


def short_conv(x, w):
    pad = CONV_W // 2
    return lax.conv_general_dilated(x, w[:, None, :].astype(x.dtype), (1,), [(pad, pad)],
                                    dimension_numbers=('NWC', 'WIO', 'NWC'),
                                    feature_group_count=x.shape[-1])


def rope_axis(x, pos):
    m = x.shape[-1] // 2
    inv = ROPE_THETA ** (-jnp.arange(m, dtype=jnp.float32) / m)
    ang = pos.astype(jnp.float32)[:, None] * inv[None, :]
    cos, sin = jnp.cos(ang), jnp.sin(ang)
    x1, x2 = x[..., :m].astype(jnp.float32), x[..., m:].astype(jnp.float32)
    return jnp.concatenate([x1 * cos - x2 * sin, x1 * sin + x2 * cos], axis=-1).astype(x.dtype)


def rope_2d(x, rows, cols):
    half = x.shape[-1] // 2
    return jnp.concatenate([rope_axis(x[..., :half], rows), rope_axis(x[..., half:], cols)], axis=-1)


def mlstm_scan(q, k, v, ig, lf, state, emit):
    b, h, t, d = q.shape
    n_chunks = t // MLSTM_CHUNK

    def to_chunks(a):
        a = a.astype(jnp.float32)
        return jnp.moveaxis(a.reshape(a.shape[:2] + (n_chunks, MLSTM_CHUNK) + a.shape[3:]), 2, 0)

    lower = jnp.tril(jnp.ones((MLSTM_CHUNK, MLSTM_CHUNK), dtype=bool))

    def step(carry, inp):
        c_mem, n_mem, m_mem = carry
        qc, kc, vc, ic, fc = inp
        bcum = jnp.cumsum(fc, axis=-1)
        b_end = bcum[..., -1]
        log_src = b_end[..., None] - bcum + ic
        m_new = jnp.maximum(b_end + m_mem, jnp.max(log_src, axis=-1))
        w_src = jnp.exp(log_src - m_new[..., None])
        decay = jnp.exp(b_end + m_mem - m_new)
        c_new = decay[..., None, None] * c_mem + jnp.einsum('bhs,bhsd,bhse->bhde', w_src, kc, vc)
        n_new = decay[..., None] * n_mem + jnp.einsum('bhs,bhsd->bhd', w_src, kc)
        if not emit:
            return (c_new, n_new, m_new), None
        log_w = jnp.where(lower, bcum[..., :, None] - bcum[..., None, :] + ic[..., None, :], -jnp.inf)
        log_inter = bcum + m_mem[..., None]
        m_t = jnp.maximum(jnp.max(log_w, axis=-1), log_inter)
        w_intra = jnp.einsum('bhtd,bhsd->bhts', qc, kc) * jnp.exp(log_w - m_t[..., None])
        w_inter = jnp.exp(log_inter - m_t)
        num = (jnp.einsum('bhts,bhsd->bhtd', w_intra, vc)
               + w_inter[..., None] * jnp.einsum('bhtd,bhde->bhte', qc, c_mem))
        den = jnp.sum(w_intra, axis=-1) + w_inter * jnp.einsum('bhtd,bhd->bht', qc, n_mem)
        h_out = num / jnp.maximum(jnp.abs(den), jnp.exp(-m_t))[..., None]
        return (c_new, n_new, m_new), h_out

    xs = (to_chunks(q), to_chunks(k), to_chunks(v), to_chunks(ig), to_chunks(lf))
    state, hs = lax.scan(step, state, xs)
    if not emit:
        return None, state
    return jnp.moveaxis(hs, 0, 2).reshape(b, h, t, d).astype(q.dtype), state


def mlstm_prep(p, w_conv, b_gate, pos):
    gw = GROUP_WIDTH
    qk = jax.nn.silu(short_conv(p[..., OFF_C:OFF_C + 2 * gw], w_conv))
    q = split_heads(qk[..., :gw])
    k = split_heads(qk[..., gw:])
    if pos is not None:
        q = rope_2d(q, pos[0], pos[1])
        k = rope_2d(k, pos[0], pos[1])
    k = k * HEAD_DIM ** -0.5
    v = split_heads(p[..., OFF_C + 2 * gw:OFF_C + 3 * gw])
    g = (p[..., OFF_G:OFF_G + N_GATES].astype(jnp.float32) + b_gate.astype(jnp.float32)).transpose(0, 2, 1)
    return q, k, v, g


def mlstm_bidirectional(ctx_in, lat_in, emit_ctx):
    qc, kc, vc, gc = ctx_in
    ql, kl, vl, gl = lat_in
    b = ql.shape[0]
    zero = (jnp.zeros((b, GROUP_HEADS, HEAD_DIM, HEAD_DIM), jnp.float32),
            jnp.zeros((b, GROUP_HEADS, HEAD_DIM), jnp.float32),
            jnp.zeros((b, GROUP_HEADS), jnp.float32))

    def run(rev):
        o = 2 * GROUP_HEADS * int(rev)
        flip = (lambda a: jnp.flip(a, axis=2)) if rev else (lambda a: a)
        ig_c, lf_c = gc[:, o:o + GROUP_HEADS], jax.nn.log_sigmoid(gc[:, o + GROUP_HEADS:o + 2 * GROUP_HEADS])
        ig_l, lf_l = gl[:, o:o + GROUP_HEADS], jax.nn.log_sigmoid(gl[:, o + GROUP_HEADS:o + 2 * GROUP_HEADS])
        h_c, st = mlstm_scan(flip(qc), flip(kc), flip(vc), flip(ig_c), flip(lf_c), zero, emit_ctx)
        h_l, _ = mlstm_scan(flip(ql), flip(kl), flip(vl), flip(ig_l), flip(lf_l), st, True)
        return (flip(h_c) if emit_ctx else None), flip(h_l)

    hc_f, hl_f = run(False)
    hc_b, hl_b = run(True)
    h_ctx = hc_f + hc_b if emit_ctx else None
    return h_ctx, hl_f + hl_b


def head_layer_norm(x, g):
    b, t, _ = x.shape
    xh = x.astype(jnp.float32).reshape(b, t, GROUP_HEADS, HEAD_DIM)
    mu = jnp.mean(xh, axis=-1, keepdims=True)
    var = jnp.mean(jnp.square(xh - mu), axis=-1, keepdims=True)
    y = ((xh - mu) * lax.rsqrt(var + EPS)).reshape(b, t, GROUP_WIDTH) * g.astype(jnp.float32)
    return y.astype(x.dtype)


def mlstm_output(h, p, g):
    o = jax.nn.sigmoid(p[..., OFF_C + 3 * GROUP_WIDTH:OFF_C + 4 * GROUP_WIDTH])
    return o * head_layer_norm(merge_heads(h), g)


def fourier_mix(f, w_fnet):
    b, t, _ = f.shape
    fg = f.astype(jnp.float32).reshape(b, t, FNET_GROUPS, GROUP_WIDTH // FNET_GROUPS)
    spec = jnp.fft.fft2(fg, axes=(1, 3), norm='ortho').real
    return spec.reshape(b, t, GROUP_WIDTH).astype(f.dtype) @ w_fnet


def token_mixers(h_lat, h_ctx, pos, w_in, b_gate, w_conv_qk, rpb, w_spatial, b_spatial,
                 g_gmlp, g_mlstm, w_fnet, w_out, with_ctx_out):
    gw = GROUP_WIDTH
    p_lat = h_lat @ w_in
    p_ctx = h_ctx @ w_in
    qa_l, ka_l, va_l = [split_heads(p_lat[..., OFF_A + i * gw:OFF_A + (i + 1) * gw]) for i in range(3)]
    qa_c, ka_c, va_c = [split_heads(p_ctx[..., OFF_A + i * gw:OFF_A + (i + 1) * gw]) for i in range(3)]
    a_lat = merge_heads(neighbourhood_attention(qa_l, ka_l, va_l, ka_c, va_c, rpb))
    b_lat = spatial_gating(p_lat[..., OFF_B:OFF_B + 2 * gw], w_spatial, b_spatial, g_gmlp)
    c_in_ctx = mlstm_prep(p_ctx, w_conv_qk, b_gate, None)
    c_in_lat = mlstm_prep(p_lat, w_conv_qk, b_gate, pos)
    hm_ctx, hm_lat = mlstm_bidirectional(c_in_ctx, c_in_lat, with_ctx_out)
    c_lat = mlstm_output(hm_lat, p_lat, g_mlstm)
    d_lat = fourier_mix(p_lat[..., OFF_D:OFF_D + gw], w_fnet)
    y_lat = jnp.concatenate([a_lat, b_lat, c_lat, d_lat], axis=-1) @ w_out
    if not with_ctx_out:
        return y_lat, None
    a_ctx = merge_heads(dense_attention(qa_c, ka_c, va_c))
    b_ctx = spatial_gating(p_ctx[..., OFF_B:OFF_B + 2 * gw], w_spatial, b_spatial, g_gmlp)
    c_ctx_out = mlstm_output(hm_ctx, p_ctx, g_mlstm)
    d_ctx = fourier_mix(p_ctx[..., OFF_D:OFF_D + gw], w_fnet)
    y_ctx = jnp.concatenate([a_ctx, b_ctx, c_ctx_out, d_ctx], axis=-1) @ w_out
    return y_lat, y_ctx


def setup_inputs(seed: int = 0) -> dict:
    key = jax.random.key(seed)
    ks = jax.random.split(key, 21)
    nrm = jax.random.normal
    gate_base = jnp.tile(jnp.concatenate([jnp.zeros((GROUP_HEADS,), jnp.float32),
                                          jnp.linspace(3.0, 6.0, GROUP_HEADS, dtype=jnp.float32)]), 2)
    return {
        'x': nrm(ks[0], (BATCH, SEQ, D_MODEL), jnp.float32),
        'c': nrm(ks[1], (BATCH, D_MODEL), jnp.float32),
        'ctx': nrm(ks[2], (BATCH, CTX_LEN, D_MODEL), jnp.float32),
        'c_ctx': nrm(ks[3], (D_MODEL,), jnp.float32),
        'w_ada': nrm(ks[4], (DEPTH, D_MODEL, N_MOD * D_MODEL), jnp.float32) * (0.5 * D_MODEL ** -0.5),
        'b_ada': 0.02 * nrm(ks[5], (DEPTH, N_MOD * D_MODEL), jnp.float32),
        'g_norm_mix': 1.0 + 0.05 * nrm(ks[6], (DEPTH, D_MODEL), jnp.float32),
        'g_norm_ffn': 1.0 + 0.05 * nrm(ks[7], (DEPTH, D_MODEL), jnp.float32),
        'w_in': nrm(ks[8], (DEPTH, D_MODEL, D_IN), jnp.float32) * D_MODEL ** -0.5,
        'b_gate': gate_base[None, :] + 0.1 * nrm(ks[9], (DEPTH, N_GATES), jnp.float32),
        'w_conv_qk': nrm(ks[10], (DEPTH, CONV_W, 2 * GROUP_WIDTH), jnp.float32) * CONV_W ** -0.5,
        'rpb': 0.2 * nrm(ks[11], (DEPTH, GROUP_HEADS, 2 * WIN_ROWS - 1, 2 * WIN_COLS - 1), jnp.float32),
        'w_spatial': nrm(ks[12], (DEPTH, GROUP_HEADS, GMLP_CHUNK, GMLP_CHUNK), jnp.float32) * GMLP_CHUNK ** -0.5,
        'b_spatial': 1.0 + 0.1 * nrm(ks[13], (DEPTH, GROUP_HEADS, GMLP_CHUNK), jnp.float32),
        'g_gmlp': 1.0 + 0.05 * nrm(ks[14], (DEPTH, GROUP_WIDTH), jnp.float32),
        'g_mlstm': 1.0 + 0.05 * nrm(ks[15], (DEPTH, GROUP_WIDTH), jnp.float32),
        'w_fnet': nrm(ks[16], (DEPTH, GROUP_WIDTH, GROUP_WIDTH), jnp.float32) * GROUP_WIDTH ** -0.5,
        'w_out': nrm(ks[17], (DEPTH, D_MODEL, D_MODEL), jnp.float32) * D_MODEL ** -0.5,
        'w_ff1': nrm(ks[18], (DEPTH, D_MODEL, D_FF), jnp.float32) * D_MODEL ** -0.5,
        'w_ff2': nrm(ks[19], (DEPTH, D_FF, D_MODEL), jnp.float32) * D_FF ** -0.5,
        'g_final': 1.0 + 0.05 * nrm(ks[20], (D_MODEL,), jnp.float32),
    }


def reference(x, c, ctx, c_ctx, w_ada, b_ada, g_norm_mix, g_norm_ffn, w_in, b_gate, w_conv_qk, rpb,
              w_spatial, b_spatial, g_gmlp, g_mlstm, w_fnet, w_out, w_ff1, w_ff2, g_final):
    seq = x.shape[1]
    t = jnp.arange(seq)
    pos = ((t // GRID_W).astype(jnp.float32), (t % GRID_W).astype(jnp.float32))
    for l in range(DEPTH):
        last = l == DEPTH - 1
        mod_lat = (jax.nn.silu(c) @ w_ada[l] + b_ada[l]).reshape(c.shape[0], N_MOD, 1, D_MODEL)
        mod_ctx = (jax.nn.silu(c_ctx) @ w_ada[l] + b_ada[l]).reshape(N_MOD, D_MODEL)
        h_lat = rms_norm(x, g_norm_mix[l]) * (1.0 + mod_lat[:, 1]) + mod_lat[:, 0]
        h_ctx = rms_norm(ctx, g_norm_mix[l]) * (1.0 + mod_ctx[1]) + mod_ctx[0]
        y_lat, y_ctx = token_mixers(h_lat, h_ctx, pos, w_in[l], b_gate[l], w_conv_qk[l], rpb[l],
                                    w_spatial[l], b_spatial[l], g_gmlp[l], g_mlstm[l], w_fnet[l],
                                    w_out[l], not last)
        x = x + mod_lat[:, 2] * y_lat
        h_lat = rms_norm(x, g_norm_ffn[l]) * (1.0 + mod_lat[:, 4]) + mod_lat[:, 3]
        x = x + mod_lat[:, 5] * sq_relu_mlp(h_lat, w_ff1[l], w_ff2[l])
        if not last:
            ctx = ctx + mod_ctx[2] * y_ctx
            h_ctx = rms_norm(ctx, g_norm_ffn[l]) * (1.0 + mod_ctx[4]) + mod_ctx[3]
            ctx = ctx + mod_ctx[5] * sq_relu_mlp(h_ctx, w_ff1[l], w_ff2[l])
    return rms_norm(x, g_final)
```

```python
import functools

import jax
import jax.numpy as jnp
from jax import lax
from jax.experimental import pallas as pl
from jax.experimental.pallas import tpu as pltpu

D_MODEL = 1024
SEQ = 2048
DEPTH = 2
GRID_W = 64
GRID_ROWS = SEQ // GRID_W
CTX_LEN = 256
T_ALL = CTX_LEN + SEQ
HEAD_DIM = 64
GROUP_WIDTH = 256
GROUP_HEADS = 4
WIN_ROWS = 8
WIN_COLS = 16
CHUNK = 128
N_CHUNKS = T_ALL // CHUNK
CTX_CHUNKS = CTX_LEN // CHUNK
CONV_W = 3
FNET_GROUPS = 4
ROPE_THETA = 10000.0
D_FF = 4 * D_MODEL
N_MOD = 6
EPS = 1e-6
NEG_INF = -1e30
N_GATES = 4 * GROUP_HEADS
OFF_G = 10 * GROUP_WIDTH
D_IN = OFF_G + N_GATES
MOD_ROWS = 24
CTX_MOD_ROW = 16
LANES = 128
TOKEN_TILE = 256
N_TILES = T_ALL // TOKEN_TILE
P_WIDTH = 9 * GROUP_WIDTH
VMEM_LIMIT = 56 * 1024 * 1024

F32 = jnp.float32
BF16 = jnp.bfloat16


def _dot(a, b):
    return jnp.dot(a, b, preferred_element_type=F32)


def _dot_nt(a, b):
    return lax.dot_general(a, b, (((1,), (1,)), ((), ())), preferred_element_type=F32)


def _dot_f32(a, b):
    return jnp.dot(a, b, preferred_element_type=F32, precision=lax.Precision.HIGHEST)


def _resident(shape):
    nd = len(shape)
    return pl.BlockSpec(shape, lambda *_: (0,) * nd, pipeline_mode=pl.Buffered(1))


def _params(n_axes):
    return pltpu.CompilerParams(dimension_semantics=("arbitrary",) * n_axes,
                                vmem_limit_bytes=VMEM_LIMIT)


def _mod_body(c_ref, w_ref, b_ref, o_ref):
    s = c_ref[...]
    s = s * jax.nn.sigmoid(s)
    o_ref[0] = _dot(s.astype(BF16), w_ref[0].astype(BF16)) + b_ref[0]


def _ada_mod(cc, w_ada, b_ada):
    depth, d, n = w_ada.shape
    tn = 1536
    return pl.pallas_call(
        _mod_body,
        grid=(depth, n // tn),
        in_specs=[pl.BlockSpec((MOD_ROWS, d), lambda l, j: (0, 0)),
                  pl.BlockSpec((1, d, tn), lambda l, j: (l, 0, j)),
                  pl.BlockSpec((1, 1, tn), lambda l, j: (l, 0, j))],
        out_specs=pl.BlockSpec((1, MOD_ROWS, tn), lambda l, j: (l, 0, j)),
        out_shape=jax.ShapeDtypeStruct((depth, MOD_ROWS, n), F32),
        compiler_params=_params(2),
        name="ada_mod",
    )(cc, w_ada, b_ada.reshape(depth, 1, n))


def _mod_index(b, t):
    return (jnp.where(t == 0, CTX_MOD_ROW, b), 0, 0)


def _rms(x, g):
    return x * lax.rsqrt(jnp.mean(x * x, axis=-1, keepdims=True) + EPS) * g


def _inproj_body(x_ref, mod_ref, g_ref, w_ref, dft_ref, p_ref, gate_ref, zc_ref, zs_ref, h_scr):
    h = _rms(x_ref[0], g_ref[...]) * (1.0 + mod_ref[0, 1:2, :]) + mod_ref[0, 0:1, :]
    h_scr[...] = h.astype(BF16)
    step = 2 * GROUP_WIDTH
    for j in range(0, P_WIDTH - GROUP_WIDTH, step):
        p_ref[0, :, j:j + step] = _dot(h_scr[...], w_ref[:, j:j + step]).astype(BF16)
    j = P_WIDTH - GROUP_WIDTH
    p_ref[0, :, j:] = _dot(h_scr[...], w_ref[:, j:P_WIDTH]).astype(BF16)
    f = _dot(h_scr[...], w_ref[:, P_WIDTH:OFF_G]).astype(BF16)
    z = _dot(f, dft_ref[...])
    zc_ref[...] = z[:, :GROUP_WIDTH].astype(BF16)
    zs_ref[...] = z[:, GROUP_WIDTH:].astype(BF16)
    gate_ref[0] = _dot(h_scr[...], w_ref[:, OFF_G:])


def _z_index(b, t):
    return ((t + N_TILES - 1) % N_TILES, b)


def _in_proj(xx, mod_l, g, w_cat, dft_c):
    bsz = xx.shape[0]
    tm = TOKEN_TILE
    wn = w_cat.shape[1]
    return pl.pallas_call(
        _inproj_body,
        grid=(bsz, N_TILES),
        in_specs=[pl.BlockSpec((1, tm, D_MODEL), lambda b, t: (b, t, 0)),
                  pl.BlockSpec((1, N_MOD, D_MODEL), _mod_index),
                  _resident((1, D_MODEL)),
                  _resident((D_MODEL, wn)),
                  _resident((GROUP_WIDTH, 2 * GROUP_WIDTH))],
        out_specs=[pl.BlockSpec((1, tm, P_WIDTH), lambda b, t: (b, t, 0)),
                   pl.BlockSpec((1, tm, LANES), lambda b, t: (b, t, 0)),
                   pl.BlockSpec((tm, GROUP_WIDTH), _z_index),
                   pl.BlockSpec((tm, GROUP_WIDTH), _z_index)],
        out_shape=[jax.ShapeDtypeStruct((bsz, T_ALL, P_WIDTH), BF16),
                   jax.ShapeDtypeStruct((bsz, T_ALL, LANES), F32),
                   jax.ShapeDtypeStruct((T_ALL, bsz * GROUP_WIDTH), BF16),
                   jax.ShapeDtypeStruct((T_ALL, bsz * GROUP_WIDTH), BF16)],
        scratch_shapes=[pltpu.VMEM((tm, D_MODEL), BF16)],
        compiler_params=_params(2),
        name="in_proj",
    )(xx, mod_l, g.reshape(1, D_MODEL), w_cat, dft_c)


def _group_spec(col_block):
    return pl.BlockSpec((1, T_ALL, GROUP_WIDTH), lambda b: (b, 0, col_block))


def _softmax_pv(parts):
    m = functools.reduce(jnp.maximum, [jnp.max(s, axis=-1, keepdims=True) for s, _ in parts])
    es = [jnp.exp(s - m) for s, _ in parts]
    den = functools.reduce(jnp.add, [jnp.sum(e, axis=-1, keepdims=True) for e in es])
    num = functools.reduce(jnp.add, [_dot(e.astype(BF16), v) for e, (_, v) in zip(es, parts)])
    return num / den


def _attn_body(q_ref, k_ref, v_ref, bias_ref, o_ref):
    scale = HEAD_DIM ** -0.5
    for h in range(GROUP_HEADS):
        hs = slice(h * HEAD_DIM, (h + 1) * HEAD_DIM)
        kc = k_ref[0, 0:CTX_LEN, hs]
        vc = v_ref[0, 0:CTX_LEN, hs]
        qc = q_ref[0, 0:CTX_LEN, hs] * scale
        o_ref[0, 0:CTX_LEN, hs] = _softmax_pv([(_dot_nt(qc, kc), vc)]).astype(BF16)

        def row(r, carry):
            rs = jnp.clip(r - WIN_ROWS // 2, 0, GRID_ROWS - WIN_ROWS)
            q0 = pl.multiple_of(CTX_LEN + r * GRID_W, GRID_W)
            k0 = pl.multiple_of(CTX_LEN + rs * GRID_W, GRID_W)
            qr = q_ref[0, pl.ds(q0, GRID_W), hs] * scale
            kl = k_ref[0, pl.ds(k0, WIN_ROWS * GRID_W), hs]
            vl = v_ref[0, pl.ds(k0, WIN_ROWS * GRID_W), hs]
            s_loc = _dot_nt(qr, kl) + bias_ref[h, r - rs]
            s_ctx = _dot_nt(qr, kc)
            o_ref[0, pl.ds(q0, GRID_W), hs] = _softmax_pv([(s_loc, vl), (s_ctx, vc)]).astype(BF16)
            return carry

        lax.fori_loop(0, GRID_ROWS, row, 0)


def _attention(p, bias_tab):
    bsz = p.shape[0]
    return pl.pallas_call(
        _attn_body,
        grid=(bsz,),
        in_specs=[_group_spec(0), _group_spec(1), _group_spec(2), _resident(bias_tab.shape)],
        out_specs=pl.BlockSpec((1, T_ALL, GROUP_WIDTH), lambda b: (b, 0, 0)),
        out_shape=jax.ShapeDtypeStruct((bsz, T_ALL, GROUP_WIDTH), BF16),
        compiler_params=_params(1),
        name="nbr_attention",
    )(p, p, p, bias_tab)


def _attention_bias(rpb):
    c = jnp.arange(GRID_W)
    qstart = jnp.clip(c - WIN_COLS // 2, 0, GRID_W - WIN_COLS)
    in_win = (c[None, :] >= qstart[:, None]) & (c[None, :] < qstart[:, None] + WIN_COLS)
    col = jnp.clip(c[None, :] - c[:, None] + WIN_COLS - 1, 0, 2 * WIN_COLS - 2)
    var = jnp.arange(WIN_ROWS)
    drow = jnp.arange(WIN_ROWS)[None, :] - var[:, None] + WIN_ROWS - 1
    tab = rpb[:, drow[:, :, None, None], col[None, None, :, :]].astype(F32)
    tab = jnp.where(in_win[None, None, None], tab, NEG_INF)
    return tab.transpose(0, 1, 3, 2, 4).reshape(GROUP_HEADS, WIN_ROWS, GRID_W, WIN_ROWS * GRID_W)


def _gmlp_body(u_ref, z_ref, ws_ref, bs_ref, g_ref, o_ref):
    def chunk(n, carry):
        rows = pl.ds(pl.multiple_of(n * CHUNK, CHUNK), CHUNK)
        z = _rms(jax.nn.gelu(z_ref[0, rows, :].astype(F32)), g_ref[...]).astype(BF16)
        u = jax.nn.gelu(u_ref[0, rows, :].astype(F32))
        for h in range(GROUP_HEADS):
            hs = slice(h * HEAD_DIM, (h + 1) * HEAD_DIM)
            mixed = _dot(ws_ref[h], z[:, hs]) + bs_ref[:, hs]
            o_ref[0, rows, hs] = (u[:, hs] * mixed).astype(BF16)
        return carry

    lax.fori_loop(0, N_CHUNKS, chunk, 0)


def _gmlp(p, w_spatial, b_spatial, g_gmlp):
    bsz = p.shape[0]
    bias = jnp.repeat(b_spatial.T, HEAD_DIM, axis=1)
    return pl.pallas_call(
        _gmlp_body,
        grid=(bsz,),
        in_specs=[_group_spec(3), _group_spec(4),
                  _resident((GROUP_HEADS, CHUNK, CHUNK)),
                  _resident((CHUNK, GROUP_WIDTH)),
                  _resident((1, GROUP_WIDTH))],
        out_specs=pl.BlockSpec((1, T_ALL, GROUP_WIDTH), lambda b: (b, 0, 0)),
        out_shape=jax.ShapeDtypeStruct((bsz, T_ALL, GROUP_WIDTH), BF16),
        compiler_params=_params(1),
        name="gmlp",
    )(p, p, w_spatial.astype(BF16), bias, g_gmlp.reshape(1, GROUP_WIDTH))


def _mlstm_body(q_ref, k_ref, v_ref, og_ref, gate_ref, wc_ref, bg_ref, cos_ref, sina_ref, sinb_ref,
                gln_ref, sh_ref, tri_ref, o_ref, q_scr, kt_scr, va_scr, h_scr, c_scr, m_scr):
    half = HEAD_DIM // 4

    def conv_act(ref, n, w0, w1, w2, rope, post):
        rows = pl.ds(pl.multiple_of(n * CHUNK, CHUNK), CHUNK)
        cur = ref[0, rows, :]
        first = (n == 0) | (n == CTX_CHUNKS)
        last = (n == CTX_CHUNKS - 1) | (n == N_CHUNKS - 1)
        prev_rows = pl.ds(pl.multiple_of(jnp.maximum(n - 1, 0) * CHUNK, CHUNK), CHUNK)
        next_rows = pl.ds(pl.multiple_of(jnp.minimum(n + 1, N_CHUNKS - 1) * CHUNK, CHUNK), CHUNK)
        keep_prev = jnp.where(first, 0.0, 1.0)
        keep_next = jnp.where(last, 0.0, 1.0)
        x_prev = _dot(sh_ref[0], cur) + keep_prev * _dot(sh_ref[1], ref[0, prev_rows, :])
        x_next = _dot(sh_ref[2], cur) + keep_next * _dot(sh_ref[3], ref[0, next_rows, :])
        y = w0 * x_prev + w1 * cur.astype(F32) + w2 * x_next
        y = y * jax.nn.sigmoid(y)
        if rope:
            lat = pl.ds(pl.multiple_of((n - CTX_CHUNKS) * CHUNK, CHUNK), CHUNK)
            y = (y * cos_ref[lat, :] + pltpu.roll(y, GROUP_WIDTH - half, 1) * sina_ref[lat, :]
                 + pltpu.roll(y, half, 1) * sinb_ref[lat, :])
        return y * post

    def prep(n, rope):
        rows = pl.ds(pl.multiple_of(n * CHUNK, CHUNK), CHUNK)
        gw = GROUP_WIDTH
        q = conv_act(q_ref, n, wc_ref[0:1, :gw], wc_ref[1:2, :gw], wc_ref[2:3, :gw], rope, 1.0)
        k = conv_act(k_ref, n, wc_ref[0:1, gw:], wc_ref[1:2, gw:], wc_ref[2:3, gw:], rope, HEAD_DIM ** -0.5)
        q_scr[n] = q.astype(BF16)
        kt_scr[n] = k.T.astype(BF16)
        v = v_ref[0, rows, :]
        lane = lax.broadcasted_iota(jnp.int32, (CHUNK, HEAD_DIM), 1)
        one_col = jnp.where(lane == 0, 1.0, 0.0).astype(BF16)
        for h in range(GROUP_HEADS):
            va_scr[n, :, 2 * h * HEAD_DIM:(2 * h + 1) * HEAD_DIM] = v[:, h * HEAD_DIM:(h + 1) * HEAD_DIM]
            va_scr[n, :, (2 * h + 1) * HEAD_DIM:(2 * h + 2) * HEAD_DIM] = one_col
        h_scr[n] = jnp.zeros((CHUNK, GROUP_WIDTH), F32)

    def prep_ctx(n, carry):
        prep(n, False)
        return carry

    def prep_lat(n, carry):
        prep(n, True)
        return carry

    lax.fori_loop(0, CTX_CHUNKS, prep_ctx, 0)
    lax.fori_loop(CTX_CHUNKS, N_CHUNKS, prep_lat, 0)
    c_scr[...] = jnp.zeros(c_scr.shape, F32)
    m_scr[...] = jnp.zeros(m_scr.shape, F32)

    t_idx = lax.broadcasted_iota(jnp.int32, (CHUNK, CHUNK), 0)
    s_idx = lax.broadcasted_iota(jnp.int32, (CHUNK, CHUNK), 1)

    def scan_chunk(n, rev):
        rows = pl.ds(pl.multiple_of(n * CHUNK, CHUNK), CHUNK)
        g = gate_ref[0, rows, :] + bg_ref[...]
        lf = jax.nn.log_sigmoid(g)
        b_col = _dot_f32(tri_ref[rev], lf)
        g_row = g.T
        b_row = _dot_f32(lf.T, tri_ref[1 - rev])
        end = 0 if rev else CHUNK - 1
        causal = (s_idx >= t_idx) if rev else (s_idx <= t_idx)
        for h in range(GROUP_HEADS):
            chain = rev * GROUP_HEADS + h
            gi = 2 * GROUP_HEADS * rev + h
            gf = gi + GROUP_HEADS
            i_col = g[:, gi:gi + 1]
            i_row = g_row[gi:gi + 1, :]
            bc_col = b_col[:, gf:gf + 1]
            bc_row = b_row[gf:gf + 1, :]
            b_end = bc_row[:, end:end + 1]
            m_mem = m_scr[chain, 0:1, 0:1]
            c_mem = c_scr[chain]
            q = q_scr[n, :, h * HEAD_DIM:(h + 1) * HEAD_DIM]
            kt = kt_scr[n, h * HEAD_DIM:(h + 1) * HEAD_DIM, :]
            va = va_scr[n, :, 2 * h * HEAD_DIM:(2 * h + 2) * HEAD_DIM]

            log_w = jnp.where(causal, bc_col - bc_row + i_row, NEG_INF)
            log_inter = bc_col + m_mem
            m_t = jnp.maximum(jnp.max(log_w, axis=-1, keepdims=True), log_inter)
            w_intra = (_dot(q, kt) * jnp.exp(log_w - m_t)).astype(BF16)
            w_inter = jnp.exp(log_inter - m_t)
            both = _dot(w_intra, va) + w_inter * _dot(q, c_mem.astype(BF16))
            num = both[:, :HEAD_DIM]
            den = both[:, HEAD_DIM:HEAD_DIM + 1]
            h_out = num / jnp.maximum(jnp.abs(den), jnp.exp(-m_t))
            h_scr[n, :, h * HEAD_DIM:(h + 1) * HEAD_DIM] += h_out

            log_src = b_end - bc_col + i_col
            m_new = jnp.maximum(b_end + m_mem, jnp.max(log_src, axis=0, keepdims=True))
            w_src = jnp.exp(log_src - m_new)
            decay = jnp.exp(b_end + m_mem - m_new)
            c_scr[chain] = decay * c_mem + _dot(kt, (w_src * va.astype(F32)).astype(BF16))
            m_scr[chain] = jnp.broadcast_to(m_new, m_scr.shape[1:])

    def scan(i, carry):
        scan_chunk(i, 0)
        scan_chunk(jnp.where(i < CTX_CHUNKS, CTX_CHUNKS - 1 - i, N_CHUNKS + CTX_CHUNKS - 1 - i), 1)
        return carry

    lax.fori_loop(0, N_CHUNKS, scan, 0)

    def finish(n, carry):
        rows = pl.ds(pl.multiple_of(n * CHUNK, CHUNK), CHUNK)
        gate = jax.nn.sigmoid(og_ref[0, rows, :].astype(F32))
        for h in range(GROUP_HEADS):
            hs = slice(h * HEAD_DIM, (h + 1) * HEAD_DIM)
            x = h_scr[n, :, hs]
            mu = jnp.mean(x, axis=-1, keepdims=True)
            var = jnp.mean(jnp.square(x - mu), axis=-1, keepdims=True)
            y = (x - mu) * lax.rsqrt(var + EPS) * gln_ref[:, hs]
            o_ref[0, rows, hs] = (gate[:, hs] * y).astype(BF16)
        return carry

    lax.fori_loop(0, N_CHUNKS, finish, 0)


def _rope_tables():
    t = jnp.arange(SEQ)
    pos = jnp.stack([(t // GRID_W).astype(F32), (t % GRID_W).astype(F32)], axis=1)
    lane = jnp.arange(GROUP_WIDTH)
    m = HEAD_DIM // 4
    inv = ROPE_THETA ** (-(lane % m).astype(F32) / m)
    axis = (lane % HEAD_DIM) // (HEAD_DIM // 2)
    ang = jnp.where(axis[None, :] == 0, pos[:, 0:1], pos[:, 1:2]) * inv[None, :]
    low = ((lane % (2 * m)) < m)[None, :]
    cos, sin = jnp.cos(ang), jnp.sin(ang)
    return cos, jnp.where(low, -sin, 0.0), jnp.where(low, 0.0, sin)


def _mlstm(p, gates, w_conv, b_gate, g_mlstm, rope):
    bsz = p.shape[0]
    i = jnp.arange(CHUNK)
    shifts = jnp.stack([i[:, None] == i[None, :] + 1,
                        (i[:, None] == 0) & (i[None, :] == CHUNK - 1),
                        i[:, None] + 1 == i[None, :],
                        (i[:, None] == CHUNK - 1) & (i[None, :] == 0)]).astype(BF16)
    tri = jnp.stack([i[:, None] >= i[None, :], i[:, None] <= i[None, :]]).astype(F32)
    bg = jnp.pad(b_gate, (0, LANES - N_GATES)).reshape(1, LANES)
    cos, sina, sinb = rope
    return pl.pallas_call(
        _mlstm_body,
        grid=(bsz,),
        in_specs=[_group_spec(5), _group_spec(6), _group_spec(7), _group_spec(8),
                  pl.BlockSpec((1, T_ALL, LANES), lambda b: (b, 0, 0)),
                  _resident((CONV_W, 2 * GROUP_WIDTH)),
                  _resident((1, LANES)),
                  _resident((SEQ, GROUP_WIDTH)), _resident((SEQ, GROUP_WIDTH)), _resident((SEQ, GROUP_WIDTH)),
                  _resident((1, GROUP_WIDTH)),
                  _resident((4, CHUNK, CHUNK)),
                  _resident((2, CHUNK, CHUNK))],
        out_specs=pl.BlockSpec((1, T_ALL, GROUP_WIDTH), lambda b: (b, 0, 0)),
        out_shape=jax.ShapeDtypeStruct((bsz, T_ALL, GROUP_WIDTH), BF16),
        scratch_shapes=[pltpu.VMEM((N_CHUNKS, CHUNK, GROUP_WIDTH), BF16),
                        pltpu.VMEM((N_CHUNKS, GROUP_WIDTH, CHUNK), BF16),
                        pltpu.VMEM((N_CHUNKS, CHUNK, 2 * GROUP_WIDTH), BF16),
                        pltpu.VMEM((N_CHUNKS, CHUNK, GROUP_WIDTH), F32),
                        pltpu.VMEM((2 * GROUP_HEADS, HEAD_DIM, 2 * HEAD_DIM), F32),
                        pltpu.VMEM((2 * GROUP_HEADS, 8, LANES), F32)],
        compiler_params=_params(1),
        name="mlstm",
    )(p, p, p, p, gates, w_conv, bg, cos, sina, sinb, g_mlstm.reshape(1, GROUP_WIDTH), shifts, tri)


def _dft_tables(n, scale):
    j = jnp.arange(n, dtype=jnp.int32)
    ang = ((j[:, None] * j[None, :]) % n).astype(F32) * (2.0 * jnp.pi / n)
    return (jnp.cos(ang) * scale).astype(BF16), (-jnp.sin(ang) * scale).astype(BF16)


def _channel_dft():
    gc = GROUP_WIDTH // FNET_GROUPS
    j = jnp.arange(GROUP_WIDTH, dtype=jnp.int32)
    same = (j[:, None] // gc) == (j[None, :] // gc)
    ang = (((j[:, None] % gc) * (j[None, :] % gc)) % gc).astype(F32) * (2.0 * jnp.pi / gc)
    c = jnp.where(same, jnp.cos(ang), 0.0)
    s = jnp.where(same, jnp.sin(ang), 0.0)
    return jnp.concatenate([c, s], axis=1).astype(BF16)


def _fnet_body(zc_ref, zs_ref, cl_ref, sl_ref, cc_ref, sc_ref, w_ref, o_ref):
    tm = 512
    for i in range(SEQ // tm):
        r = slice(i * tm, (i + 1) * tm)
        y = _dot(cl_ref[r, :], zc_ref[0:SEQ, :]) + _dot(sl_ref[r, :], zs_ref[0:SEQ, :])
        o_ref[r, :] = _dot(y.astype(BF16), w_ref[...]).astype(BF16)
    y = _dot(cc_ref[...], zc_ref[SEQ:, :]) + _dot(sc_ref[...], zs_ref[SEQ:, :])
    o_ref[SEQ:, :] = _dot(y.astype(BF16), w_ref[...]).astype(BF16)


def _fnet(zc, zs, tabs, w_fnet):
    bsz = zc.shape[1] // GROUP_WIDTH
    cl, sl, cc, sc = tabs
    col = pl.BlockSpec((T_ALL, GROUP_WIDTH), lambda b: (0, b))
    return pl.pallas_call(
        _fnet_body,
        grid=(bsz,),
        in_specs=[col, col, _resident((SEQ, SEQ)), _resident((SEQ, SEQ)),
                  _resident((CTX_LEN, CTX_LEN)), _resident((CTX_LEN, CTX_LEN)),
                  _resident((GROUP_WIDTH, GROUP_WIDTH))],
        out_specs=col,
        out_shape=jax.ShapeDtypeStruct((T_ALL, bsz * GROUP_WIDTH), BF16),
        compiler_params=_params(1),
        name="fnet",
    )(zc, zs, cl, sl, cc, sc, w_fnet.astype(BF16))


def _post_body(a_ref, b_ref, c_ref, d_ref, x_ref, mod_ref, g_ref, wo_ref, w1_ref, w2_ref, gf_ref,
               o_ref, h_scr, acc_scr, *, final):
    gw = GROUP_WIDTH
    y = (_dot(a_ref[0], wo_ref[0:gw, :]) + _dot(b_ref[0], wo_ref[gw:2 * gw, :])
         + _dot(c_ref[0], wo_ref[2 * gw:3 * gw, :]) + _dot(d_ref[...], wo_ref[3 * gw:, :]))
    x1 = x_ref[0] + mod_ref[0, 2:3, :] * y
    h = _rms(x1, g_ref[...]) * (1.0 + mod_ref[0, 4:5, :]) + mod_ref[0, 3:4, :]
    h_scr[...] = h.astype(BF16)
    step = 512
    for j in range(0, D_FF, step):
        a = jnp.maximum(_dot(h_scr[...], w1_ref[:, j:j + step]), 0.0)
        part = _dot((a * a).astype(BF16), w2_ref[j:j + step, :])
        if j == 0:
            acc_scr[...] = part
        else:
            acc_scr[...] += part
    x2 = x1 + mod_ref[0, 5:6, :] * acc_scr[...]
    if final:
        x2 = _rms(x2, gf_ref[...])
    o_ref[0] = x2


def _post(a, b_, c_, d, xx, mod_l, g_ffn, w_out, w_ff1, w_ff2, g_final, final):
    bsz = xx.shape[0]
    tm = TOKEN_TILE
    skip = 1 if final else 0
    tok = lambda b, t: (b, t + skip, 0)
    grp = pl.BlockSpec((1, tm, GROUP_WIDTH), tok)
    out_len = T_ALL - skip * TOKEN_TILE
    return pl.pallas_call(
        functools.partial(_post_body, final=final),
        grid=(bsz, N_TILES - skip),
        in_specs=[grp, grp, grp,
                  pl.BlockSpec((tm, GROUP_WIDTH), lambda b, t: _z_index(b, t + skip)),
                  pl.BlockSpec((1, tm, D_MODEL), tok),
                  pl.BlockSpec((1, N_MOD, D_MODEL), lambda b, t: _mod_index(b, t + skip)),
                  _resident((1, D_MODEL)),
                  _resident((D_MODEL, D_MODEL)),
                  _resident((D_MODEL, D_FF)),
                  _resident((D_FF, D_MODEL)),
                  _resident((1, D_MODEL))],
        out_specs=pl.BlockSpec((1, tm, D_MODEL), lambda b, t: (b, t, 0)),
        out_shape=jax.ShapeDtypeStruct((bsz, out_len, D_MODEL), F32),
        scratch_shapes=[pltpu.VMEM((tm, D_MODEL), BF16), pltpu.VMEM((tm, D_MODEL), F32)],
        compiler_params=_params(2),
        name="out_proj_mlp",
    )(a, b_, c_, d, xx, mod_l, g_ffn.reshape(1, D_MODEL), w_out.astype(BF16), w_ff1.astype(BF16),
      w_ff2.astype(BF16), g_final.reshape(1, D_MODEL))


def kernel(x, c, ctx, c_ctx, w_ada, b_ada, g_norm_mix, g_norm_ffn, w_in, b_gate, w_conv_qk, rpb, w_spatial,
           b_spatial, g_gmlp, g_mlstm, w_fnet, w_out, w_ff1, w_ff2, g_final):
    bsz = x.shape[0]
    assert bsz <= CTX_MOD_ROW and x.shape[1:] == (SEQ, D_MODEL) and ctx.shape[1:] == (CTX_LEN, D_MODEL)
    depth = w_ada.shape[0]
    xx = jnp.concatenate([ctx, x], axis=1)
    cc = jnp.zeros((MOD_ROWS, D_MODEL), F32).at[:bsz].set(c).at[CTX_MOD_ROW].set(c_ctx)
    mod = _ada_mod(cc, w_ada, b_ada).reshape(depth, MOD_ROWS, N_MOD, D_MODEL)

    rope = _rope_tables()
    dft_c = _channel_dft()
    tabs = (_dft_tables(SEQ, (SEQ * GROUP_WIDTH // FNET_GROUPS) ** -0.5)
            + _dft_tables(CTX_LEN, (CTX_LEN * GROUP_WIDTH // FNET_GROUPS) ** -0.5))

    for l in range(depth):
        w_cat = jnp.pad(w_in[l], ((0, 0), (0, LANES - N_GATES))).astype(BF16)
        p, gates, zc, zs = _in_proj(xx, mod[l], g_norm_mix[l], w_cat, dft_c)
        a = _attention(p, _attention_bias(rpb[l]))
        b_ = _gmlp(p, w_spatial[l], b_spatial[l], g_gmlp[l])
        c_ = _mlstm(p, gates, w_conv_qk[l], b_gate[l], g_mlstm[l], rope)
        d = _fnet(zc, zs, tabs, w_fnet[l])
        xx = _post(a, b_, c_, d, xx, mod[l], g_norm_ffn[l], w_out[l], w_ff1[l], w_ff2[l], g_final,
                   final=(l == depth - 1))
    return xx
```

```python
import functools

import jax
import jax.numpy as jnp
from jax import lax
from jax.experimental import pallas as pl
from jax.experimental.pallas import tpu as pltpu

D_MODEL = 1024
SEQ = 2048
DEPTH = 2
GRID_W = 64
GRID_ROWS = SEQ // GRID_W
CTX_LEN = 256
T_ALL = CTX_LEN + SEQ
HEAD_DIM = 64
GROUP_WIDTH = 256
GROUP_HEADS = 4
WIN_ROWS = 8
WIN_COLS = 16
CHUNK = 128
N_CHUNKS = T_ALL // CHUNK
CTX_CHUNKS = CTX_LEN // CHUNK
CONV_W = 3
FNET_GROUPS = 4
ROPE_THETA = 10000.0
D_FF = 4 * D_MODEL
N_MOD = 6
EPS = 1e-6
NEG_INF = -1e30
N_GATES = 4 * GROUP_HEADS
OFF_G = 10 * GROUP_WIDTH
D_IN = OFF_G + N_GATES
MOD_ROWS = 24
CTX_MOD_ROW = 16
LANES = 128
TOKEN_TILE = 256
N_TILES = T_ALL // TOKEN_TILE
P_WIDTH = 9 * GROUP_WIDTH
VMEM_LIMIT = 56 * 1024 * 1024

F32 = jnp.float32
BF16 = jnp.bfloat16


def _dot(a, b):
    return jnp.dot(a, b, preferred_element_type=F32)


def _dot_nt(a, b):
    return lax.dot_general(a, b, (((1,), (1,)), ((), ())), preferred_element_type=F32)


def _dot_f32(a, b):
    return jnp.dot(a, b, preferred_element_type=F32, precision=lax.Precision.HIGHEST)


def _resident(shape):
    nd = len(shape)
    return pl.BlockSpec(shape, lambda *_: (0,) * nd, pipeline_mode=pl.Buffered(1))


def _params(n_axes):
    return pltpu.CompilerParams(dimension_semantics=("arbitrary",) * n_axes,
                                vmem_limit_bytes=VMEM_LIMIT)


def _mod_body(c_ref, w_ref, b_ref, o_ref):
    s = c_ref[...]
    s = s * jax.nn.sigmoid(s)
    o_ref[0] = _dot(s.astype(BF16), w_ref[0].astype(BF16)) + b_ref[0]


def _ada_mod(cc, w_ada, b_ada):
    depth, d, n = w_ada.shape
    tn = 1536
    return pl.pallas_call(
        _mod_body,
        grid=(depth, n // tn),
        in_specs=[pl.BlockSpec((MOD_ROWS, d), lambda l, j: (0, 0)),
                  pl.BlockSpec((1, d, tn), lambda l, j: (l, 0, j)),
                  pl.BlockSpec((1, 1, tn), lambda l, j: (l, 0, j))],
        out_specs=pl.BlockSpec((1, MOD_ROWS, tn), lambda l, j: (l, 0, j)),
        out_shape=jax.ShapeDtypeStruct((depth, MOD_ROWS, n), F32),
        compiler_params=_params(2),
        name="ada_mod",
    )(cc, w_ada, b_ada.reshape(depth, 1, n))


def _mod_index(b, t):
    return (jnp.where(t == 0, CTX_MOD_ROW, b), 0, 0)


def _rms(x, g):
    return x * lax.rsqrt(jnp.mean(x * x, axis=-1, keepdims=True) + EPS) * g


def _inproj_body(x_ref, mod_ref, g_ref, w_ref, dft_ref, p_ref, gate_ref, zc_ref, zs_ref, h_scr):
    h = _rms(x_ref[0], g_ref[...]) * (1.0 + mod_ref[0, 1:2, :]) + mod_ref[0, 0:1, :]
    h_scr[...] = h.astype(BF16)
    step = 2 * GROUP_WIDTH
    for j in range(0, P_WIDTH - GROUP_WIDTH, step):
        p_ref[0, :, j:j + step] = _dot(h_scr[...], w_ref[:, j:j + step]).astype(BF16)
    j = P_WIDTH - GROUP_WIDTH
    p_ref[0, :, j:] = _dot(h_scr[...], w_ref[:, j:P_WIDTH]).astype(BF16)
    f = _dot(h_scr[...], w_ref[:, P_WIDTH:OFF_G]).astype(BF16)
    z = _dot(f, dft_ref[...])
    zc_ref[...] = z[:, :GROUP_WIDTH].astype(BF16)
    zs_ref[...] = z[:, GROUP_WIDTH:].astype(BF16)
    gate_ref[0] = _dot(h_scr[...], w_ref[:, OFF_G:])


def _z_index(b, t):
    return ((t + N_TILES - 1) % N_TILES, b)


def _in_proj(xx, mod_l, g, w_cat, dft_c):
    bsz = xx.shape[0]
    tm = TOKEN_TILE
    wn = w_cat.shape[1]
    return pl.pallas_call(
        _inproj_body,
        grid=(bsz, N_TILES),
        in_specs=[pl.BlockSpec((1, tm, D_MODEL), lambda b, t: (b, t, 0)),
                  pl.BlockSpec((1, N_MOD, D_MODEL), _mod_index),
                  _resident((1, D_MODEL)),
                  _resident((D_MODEL, wn)),
                  _resident((GROUP_WIDTH, 2 * GROUP_WIDTH))],
        out_specs=[pl.BlockSpec((1, tm, P_WIDTH), lambda b, t: (b, t, 0)),
                   pl.BlockSpec((1, tm, LANES), lambda b, t: (b, t, 0)),
                   pl.BlockSpec((tm, GROUP_WIDTH), _z_index),
                   pl.BlockSpec((tm, GROUP_WIDTH), _z_index)],
        out_shape=[jax.ShapeDtypeStruct((bsz, T_ALL, P_WIDTH), BF16),
                   jax.ShapeDtypeStruct((bsz, T_ALL, LANES), F32),
                   jax.ShapeDtypeStruct((T_ALL, bsz * GROUP_WIDTH), BF16),
                   jax.ShapeDtypeStruct((T_ALL, bsz * GROUP_WIDTH), BF16)],
        scratch_shapes=[pltpu.VMEM((tm, D_MODEL), BF16)],
        compiler_params=_params(2),
        name="in_proj",
    )(xx, mod_l, g.reshape(1, D_MODEL), w_cat, dft_c)


def _group_spec(col_block):
    return pl.BlockSpec((1, T_ALL, GROUP_WIDTH), lambda b: (b, 0, col_block))


def _softmax_pv(parts):
    m = functools.reduce(jnp.maximum, [jnp.max(s, axis=-1, keepdims=True) for s, _ in parts])
    es = [jnp.exp(s - m) for s, _ in parts]
    den = functools.reduce(jnp.add, [jnp.sum(e, axis=-1, keepdims=True) for e in es])
    num = functools.reduce(jnp.add, [_dot(e.astype(BF16), v) for e, (_, v) in zip(es, parts)])
    return num / den


def _attn_body(q_ref, k_ref, v_ref, bias_ref, o_ref):
    scale = HEAD_DIM ** -0.5
    for h in range(GROUP_HEADS):
        hs = slice(h * HEAD_DIM, (h + 1) * HEAD_DIM)
        kc = k_ref[0, 0:CTX_LEN, hs]
        vc = v_ref[0, 0:CTX_LEN, hs]
        qc = q_ref[0, 0:CTX_LEN, hs] * scale
        o_ref[0, 0:CTX_LEN, hs] = _softmax_pv([(_dot_nt(qc, kc), vc)]).astype(BF16)

        def row(r, carry):
            rs = jnp.clip(r - WIN_ROWS // 2, 0, GRID_ROWS - WIN_ROWS)
            q0 = pl.multiple_of(CTX_LEN + r * GRID_W, GRID_W)
            k0 = pl.multiple_of(CTX_LEN + rs * GRID_W, GRID_W)
            qr = q_ref[0, pl.ds(q0, GRID_W), hs] * scale
            kl = k_ref[0, pl.ds(k0, WIN_ROWS * GRID_W), hs]
            vl = v_ref[0, pl.ds(k0, WIN_ROWS * GRID_W), hs]
            s_loc = _dot_nt(qr, kl) + bias_ref[h, r - rs]
            s_ctx = _dot_nt(qr, kc)
            o_ref[0, pl.ds(q0, GRID_W), hs] = _softmax_pv([(s_loc, vl), (s_ctx, vc)]).astype(BF16)
            return carry

        lax.fori_loop(0, GRID_ROWS, row, 0)


def _attention(p, bias_tab):
    bsz = p.shape[0]
    return pl.pallas_call(
        _attn_body,
        grid=(bsz,),
        in_specs=[_group_spec(0), _group_spec(1), _group_spec(2), _resident(bias_tab.shape)],
        out_specs=pl.BlockSpec((1, T_ALL, GROUP_WIDTH), lambda b: (b, 0, 0)),
        out_shape=jax.ShapeDtypeStruct((bsz, T_ALL, GROUP_WIDTH), BF16),
        compiler_params=_params(1),
        name="nbr_attention",
    )(p, p, p, bias_tab)


def _attention_bias(rpb):
    c = jnp.arange(GRID_W)
    qstart = jnp.clip(c - WIN_COLS // 2, 0, GRID_W - WIN_COLS)
    in_win = (c[None, :] >= qstart[:, None]) & (c[None, :] < qstart[:, None] + WIN_COLS)
    col = jnp.clip(c[None, :] - c[:, None] + WIN_COLS - 1, 0, 2 * WIN_COLS - 2)
    pick = (col[:, :, None] == jnp.arange(2 * WIN_COLS - 1)).astype(F32)
    by_drow = jnp.einsum('hde,qke->hdqk', rpb.astype(F32), pick, precision=lax.Precision.HIGHEST)
    by_drow = jnp.where(in_win, by_drow, NEG_INF)
    tab = jnp.stack([by_drow[:, WIN_ROWS - 1 - v:2 * WIN_ROWS - 1 - v] for v in range(WIN_ROWS)], axis=1)
    return tab.transpose(0, 1, 3, 2, 4).reshape(GROUP_HEADS, WIN_ROWS, GRID_W, WIN_ROWS * GRID_W)


def _gmlp_body(u_ref, z_ref, ws_ref, bs_ref, g_ref, o_ref):
    def chunk(n, carry):
        rows = pl.ds(pl.multiple_of(n * CHUNK, CHUNK), CHUNK)
        z = _rms(jax.nn.gelu(z_ref[0, rows, :].astype(F32)), g_ref[...]).astype(BF16)
        u = jax.nn.gelu(u_ref[0, rows, :].astype(F32))
        for h in range(GROUP_HEADS):
            hs = slice(h * HEAD_DIM, (h + 1) * HEAD_DIM)
            mixed = _dot(ws_ref[h], z[:, hs]) + bs_ref[:, hs]
            o_ref[0, rows, hs] = (u[:, hs] * mixed).astype(BF16)
        return carry

    lax.fori_loop(0, N_CHUNKS, chunk, 0)


def _gmlp(p, w_spatial, b_spatial, g_gmlp):
    bsz = p.shape[0]
    bias = jnp.repeat(b_spatial.T, HEAD_DIM, axis=1)
    return pl.pallas_call(
        _gmlp_body,
        grid=(bsz,),
        in_specs=[_group_spec(3), _group_spec(4),
                  _resident((GROUP_HEADS, CHUNK, CHUNK)),
                  _resident((CHUNK, GROUP_WIDTH)),
                  _resident((1, GROUP_WIDTH))],
        out_specs=pl.BlockSpec((1, T_ALL, GROUP_WIDTH), lambda b: (b, 0, 0)),
        out_shape=jax.ShapeDtypeStruct((bsz, T_ALL, GROUP_WIDTH), BF16),
        compiler_params=_params(1),
        name="gmlp",
    )(p, p, w_spatial.astype(BF16), bias, g_gmlp.reshape(1, GROUP_WIDTH))


def _mlstm_body(q_ref, k_ref, v_ref, og_ref, gate_ref, wc_ref, bg_ref, cos_ref, sina_ref, sinb_ref,
                gln_ref, sh_ref, tri_ref, o_ref, q_scr, kt_scr, va_scr, h_scr, c_scr, m_scr):
    half = HEAD_DIM // 4

    def conv_act(ref, n, w0, w1, w2, rope, post):
        rows = pl.ds(pl.multiple_of(n * CHUNK, CHUNK), CHUNK)
        cur = ref[0, rows, :]
        first = (n == 0) | (n == CTX_CHUNKS)
        last = (n == CTX_CHUNKS - 1) | (n == N_CHUNKS - 1)
        prev_rows = pl.ds(pl.multiple_of(jnp.maximum(n - 1, 0) * CHUNK, CHUNK), CHUNK)
        next_rows = pl.ds(pl.multiple_of(jnp.minimum(n + 1, N_CHUNKS - 1) * CHUNK, CHUNK), CHUNK)
        keep_prev = jnp.where(first, 0.0, 1.0)
        keep_next = jnp.where(last, 0.0, 1.0)
        x_prev = _dot(sh_ref[0], cur) + keep_prev * _dot(sh_ref[1], ref[0, prev_rows, :])
        x_next = _dot(sh_ref[2], cur) + keep_next * _dot(sh_ref[3], ref[0, next_rows, :])
        y = w0 * x_prev + w1 * cur.astype(F32) + w2 * x_next
        y = y * jax.nn.sigmoid(y)
        if rope:
            lat = pl.ds(pl.multiple_of((n - CTX_CHUNKS) * CHUNK, CHUNK), CHUNK)
            y = (y * cos_ref[lat, :] + pltpu.roll(y, GROUP_WIDTH - half, 1) * sina_ref[lat, :]
                 + pltpu.roll(y, half, 1) * sinb_ref[lat, :])
        return y * post

    def prep(n, rope):
        rows = pl.ds(pl.multiple_of(n * CHUNK, CHUNK), CHUNK)
        gw = GROUP_WIDTH
        q = conv_act(q_ref, n, wc_ref[0:1, :gw], wc_ref[1:2, :gw], wc_ref[2:3, :gw], rope, 1.0)
        k = conv_act(k_ref, n, wc_ref[0:1, gw:], wc_ref[1:2, gw:], wc_ref[2:3, gw:], rope, HEAD_DIM ** -0.5)
        q_scr[n] = q.astype(BF16)
        kt_scr[n] = k.T.astype(BF16)
        v = v_ref[0, rows, :]
        lane = lax.broadcasted_iota(jnp.int32, (CHUNK, HEAD_DIM), 1)
        one_col = jnp.where(lane == 0, 1.0, 0.0).astype(BF16)
        for h in range(GROUP_HEADS):
            va_scr[n, :, 2 * h * HEAD_DIM:(2 * h + 1) * HEAD_DIM] = v[:, h * HEAD_DIM:(h + 1) * HEAD_DIM]
            va_scr[n, :, (2 * h + 1) * HEAD_DIM:(2 * h + 2) * HEAD_DIM] = one_col
        h_scr[n] = jnp.zeros((CHUNK, GROUP_WIDTH), F32)

    def prep_ctx(n, carry):
        prep(n, False)
        return carry

    def prep_lat(n, carry):
        prep(n, True)
        return carry

    lax.fori_loop(0, CTX_CHUNKS, prep_ctx, 0)
    lax.fori_loop(CTX_CHUNKS, N_CHUNKS, prep_lat, 0)
    c_scr[...] = jnp.zeros(c_scr.shape, F32)
    m_scr[...] = jnp.zeros(m_scr.shape, F32)

    t_idx = lax.broadcasted_iota(jnp.int32, (CHUNK, CHUNK), 0)
    s_idx = lax.broadcasted_iota(jnp.int32, (CHUNK, CHUNK), 1)

    def scan_chunk(n, rev):
        rows = pl.ds(pl.multiple_of(n * CHUNK, CHUNK), CHUNK)
        g = gate_ref[0, rows, :] + bg_ref[...]
        lf = jax.nn.log_sigmoid(g)
        b_col = _dot_f32(tri_ref[rev], lf)
        g_row = g.T
        b_row = _dot_f32(lf.T, tri_ref[1 - rev])
        end = 0 if rev else CHUNK - 1
        causal = (s_idx >= t_idx) if rev else (s_idx <= t_idx)
        for h in range(GROUP_HEADS):
            chain = rev * GROUP_HEADS + h
            gi = 2 * GROUP_HEADS * rev + h
            gf = gi + GROUP_HEADS
            i_col = g[:, gi:gi + 1]
            i_row = g_row[gi:gi + 1, :]
            bc_col = b_col[:, gf:gf + 1]
            bc_row = b_row[gf:gf + 1, :]
            b_end = bc_row[:, end:end + 1]
            m_mem = m_scr[chain, 0:1, 0:1]
            c_mem = c_scr[chain]
            q = q_scr[n, :, h * HEAD_DIM:(h + 1) * HEAD_DIM]
            kt = kt_scr[n, h * HEAD_DIM:(h + 1) * HEAD_DIM, :]
            va = va_scr[n, :, 2 * h * HEAD_DIM:(2 * h + 2) * HEAD_DIM]

            log_w = jnp.where(causal, bc_col - bc_row + i_row, NEG_INF)
            log_inter = bc_col + m_mem
            m_t = jnp.maximum(jnp.max(log_w, axis=-1, keepdims=True), log_inter)
            w_intra = (_dot(q, kt) * jnp.exp(log_w - m_t)).astype(BF16)
            w_inter = jnp.exp(log_inter - m_t)
            both = _dot(w_intra, va) + w_inter * _dot(q, c_mem.astype(BF16))
            num = both[:, :HEAD_DIM]
            den = both[:, HEAD_DIM:HEAD_DIM + 1]
            h_out = num / jnp.maximum(jnp.abs(den), jnp.exp(-m_t))
            h_scr[n, :, h * HEAD_DIM:(h + 1) * HEAD_DIM] += h_out

            log_src = b_end - bc_col + i_col
            m_new = jnp.maximum(b_end + m_mem, jnp.max(log_src, axis=0, keepdims=True))
            w_src = jnp.exp(log_src - m_new)
            decay = jnp.exp(b_end + m_mem - m_new)
            c_scr[chain] = decay * c_mem + _dot(kt, (w_src * va.astype(F32)).astype(BF16))
            m_scr[chain] = jnp.broadcast_to(m_new, m_scr.shape[1:])

    def scan(i, carry):
        scan_chunk(i, 0)
        scan_chunk(jnp.where(i < CTX_CHUNKS, CTX_CHUNKS - 1 - i, N_CHUNKS + CTX_CHUNKS - 1 - i), 1)
        return carry

    lax.fori_loop(0, N_CHUNKS, scan, 0)

    def finish(n, carry):
        rows = pl.ds(pl.multiple_of(n * CHUNK, CHUNK), CHUNK)
        gate = jax.nn.sigmoid(og_ref[0, rows, :].astype(F32))
        for h in range(GROUP_HEADS):
            hs = slice(h * HEAD_DIM, (h + 1) * HEAD_DIM)
            x = h_scr[n, :, hs]
            mu = jnp.mean(x, axis=-1, keepdims=True)
            var = jnp.mean(jnp.square(x - mu), axis=-1, keepdims=True)
            y = (x - mu) * lax.rsqrt(var + EPS) * gln_ref[:, hs]
            o_ref[0, rows, hs] = (gate[:, hs] * y).astype(BF16)
        return carry

    lax.fori_loop(0, N_CHUNKS, finish, 0)


def _rope_tables():
    t = jnp.arange(SEQ)
    pos = jnp.stack([(t // GRID_W).astype(F32), (t % GRID_W).astype(F32)], axis=1)
    lane = jnp.arange(GROUP_WIDTH)
    m = HEAD_DIM // 4
    inv = ROPE_THETA ** (-(lane % m).astype(F32) / m)
    axis = (lane % HEAD_DIM) // (HEAD_DIM // 2)
    ang = jnp.where(axis[None, :] == 0, pos[:, 0:1], pos[:, 1:2]) * inv[None, :]
    low = ((lane % (2 * m)) < m)[None, :]
    cos, sin = jnp.cos(ang), jnp.sin(ang)
    return cos, jnp.where(low, -sin, 0.0), jnp.where(low, 0.0, sin)


def _mlstm(p, gates, w_conv, b_gate, g_mlstm, rope):
    bsz = p.shape[0]
    i = jnp.arange(CHUNK)
    shifts = jnp.stack([i[:, None] == i[None, :] + 1,
                        (i[:, None] == 0) & (i[None, :] == CHUNK - 1),
                        i[:, None] + 1 == i[None, :],
                        (i[:, None] == CHUNK - 1) & (i[None, :] == 0)]).astype(BF16)
    tri = jnp.stack([i[:, None] >= i[None, :], i[:, None] <= i[None, :]]).astype(F32)
    bg = jnp.pad(b_gate, (0, LANES - N_GATES)).reshape(1, LANES)
    cos, sina, sinb = rope
    return pl.pallas_call(
        _mlstm_body,
        grid=(bsz,),
        in_specs=[_group_spec(5), _group_spec(6), _group_spec(7), _group_spec(8),
                  pl.BlockSpec((1, T_ALL, LANES), lambda b: (b, 0, 0)),
                  _resident((CONV_W, 2 * GROUP_WIDTH)),
                  _resident((1, LANES)),
                  _resident((SEQ, GROUP_WIDTH)), _resident((SEQ, GROUP_WIDTH)), _resident((SEQ, GROUP_WIDTH)),
                  _resident((1, GROUP_WIDTH)),
                  _resident((4, CHUNK, CHUNK)),
                  _resident((2, CHUNK, CHUNK))],
        out_specs=pl.BlockSpec((1, T_ALL, GROUP_WIDTH), lambda b: (b, 0, 0)),
        out_shape=jax.ShapeDtypeStruct((bsz, T_ALL, GROUP_WIDTH), BF16),
        scratch_shapes=[pltpu.VMEM((N_CHUNKS, CHUNK, GROUP_WIDTH), BF16),
                        pltpu.VMEM((N_CHUNKS, GROUP_WIDTH, CHUNK), BF16),
                        pltpu.VMEM((N_CHUNKS, CHUNK, 2 * GROUP_WIDTH), BF16),
                        pltpu.VMEM((N_CHUNKS, CHUNK, GROUP_WIDTH), F32),
                        pltpu.VMEM((2 * GROUP_HEADS, HEAD_DIM, 2 * HEAD_DIM), F32),
                        pltpu.VMEM((2 * GROUP_HEADS, 8, LANES), F32)],
        compiler_params=_params(1),
        name="mlstm",
    )(p, p, p, p, gates, w_conv, bg, cos, sina, sinb, g_mlstm.reshape(1, GROUP_WIDTH), shifts, tri)


def _dft_tables(n, scale):
    j = jnp.arange(n, dtype=jnp.int32)
    ang = ((j[:, None] * j[None, :]) % n).astype(F32) * (2.0 * jnp.pi / n)
    return (jnp.cos(ang) * scale).astype(BF16), (-jnp.sin(ang) * scale).astype(BF16)


def _channel_dft():
    gc = GROUP_WIDTH // FNET_GROUPS
    j = jnp.arange(GROUP_WIDTH, dtype=jnp.int32)
    same = (j[:, None] // gc) == (j[None, :] // gc)
    ang = (((j[:, None] % gc) * (j[None, :] % gc)) % gc).astype(F32) * (2.0 * jnp.pi / gc)
    c = jnp.where(same, jnp.cos(ang), 0.0)
    s = jnp.where(same, jnp.sin(ang), 0.0)
    return jnp.concatenate([c, s], axis=1).astype(BF16)


def _fnet_body(zc_ref, zs_ref, cl_ref, sl_ref, cc_ref, sc_ref, w_ref, o_ref):
    tm = 512
    for i in range(SEQ // tm):
        r = slice(i * tm, (i + 1) * tm)
        y = _dot(cl_ref[r, :], zc_ref[0:SEQ, :]) + _dot(sl_ref[r, :], zs_ref[0:SEQ, :])
        o_ref[r, :] = _dot(y.astype(BF16), w_ref[...]).astype(BF16)
    y = _dot(cc_ref[...], zc_ref[SEQ:, :]) + _dot(sc_ref[...], zs_ref[SEQ:, :])
    o_ref[SEQ:, :] = _dot(y.astype(BF16), w_ref[...]).astype(BF16)


def _fnet(zc, zs, tabs, w_fnet):
    bsz = zc.shape[1] // GROUP_WIDTH
    cl, sl, cc, sc = tabs
    col = pl.BlockSpec((T_ALL, GROUP_WIDTH), lambda b: (0, b))
    return pl.pallas_call(
        _fnet_body,
        grid=(bsz,),
        in_specs=[col, col, _resident((SEQ, SEQ)), _resident((SEQ, SEQ)),
                  _resident((CTX_LEN, CTX_LEN)), _resident((CTX_LEN, CTX_LEN)),
                  _resident((GROUP_WIDTH, GROUP_WIDTH))],
        out_specs=col,
        out_shape=jax.ShapeDtypeStruct((T_ALL, bsz * GROUP_WIDTH), BF16),
        compiler_params=_params(1),
        name="fnet",
    )(zc, zs, cl, sl, cc, sc, w_fnet.astype(BF16))


def _post_body(a_ref, b_ref, c_ref, d_ref, x_ref, mod_ref, g_ref, wo_ref, w1_ref, w2_ref, gf_ref,
               o_ref, h_scr, acc_scr, *, final):
    gw = GROUP_WIDTH
    y = (_dot(a_ref[0], wo_ref[0:gw, :]) + _dot(b_ref[0], wo_ref[gw:2 * gw, :])
         + _dot(c_ref[0], wo_ref[2 * gw:3 * gw, :]) + _dot(d_ref[...], wo_ref[3 * gw:, :]))
    x1 = x_ref[0] + mod_ref[0, 2:3, :] * y
    h = _rms(x1, g_ref[...]) * (1.0 + mod_ref[0, 4:5, :]) + mod_ref[0, 3:4, :]
    h_scr[...] = h.astype(BF16)
    step = 512
    for j in range(0, D_FF, step):
        a = jnp.maximum(_dot(h_scr[...], w1_ref[:, j:j + step]), 0.0)
        part = _dot((a * a).astype(BF16), w2_ref[j:j + step, :])
        if j == 0:
            acc_scr[...] = part
        else:
            acc_scr[...] += part
    x2 = x1 + mod_ref[0, 5:6, :] * acc_scr[...]
    if final:
        x2 = _rms(x2, gf_ref[...])
    o_ref[0] = x2


def _post(a, b_, c_, d, xx, mod_l, g_ffn, w_out, w_ff1, w_ff2, g_final, final):
    bsz = xx.shape[0]
    tm = TOKEN_TILE
    skip = 1 if final else 0
    tok = lambda b, t: (b, t + skip, 0)
    grp = pl.BlockSpec((1, tm, GROUP_WIDTH), tok)
    out_len = T_ALL - skip * TOKEN_TILE
    return pl.pallas_call(
        functools.partial(_post_body, final=final),
        grid=(bsz, N_TILES - skip),
        in_specs=[grp, grp, grp,
                  pl.BlockSpec((tm, GROUP_WIDTH), lambda b, t: _z_index(b, t + skip)),
                  pl.BlockSpec((1, tm, D_MODEL), tok),
                  pl.BlockSpec((1, N_MOD, D_MODEL), lambda b, t: _mod_index(b, t + skip)),
                  _resident((1, D_MODEL)),
                  _resident((D_MODEL, D_MODEL)),
                  _resident((D_MODEL, D_FF)),
                  _resident((D_FF, D_MODEL)),
                  _resident((1, D_MODEL))],
        out_specs=pl.BlockSpec((1, tm, D_MODEL), lambda b, t: (b, t, 0)),
        out_shape=jax.ShapeDtypeStruct((bsz, out_len, D_MODEL), F32),
        scratch_shapes=[pltpu.VMEM((tm, D_MODEL), BF16), pltpu.VMEM((tm, D_MODEL), F32)],
        compiler_params=_params(2),
        name="out_proj_mlp",
    )(a, b_, c_, d, xx, mod_l, g_ffn.reshape(1, D_MODEL), w_out.astype(BF16), w_ff1.astype(BF16),
      w_ff2.astype(BF16), g_final.reshape(1, D_MODEL))


def kernel(x, c, ctx, c_ctx, w_ada, b_ada, g_norm_mix, g_norm_ffn, w_in, b_gate, w_conv_qk, rpb, w_spatial,
           b_spatial, g_gmlp, g_mlstm, w_fnet, w_out, w_ff1, w_ff2, g_final):
    bsz = x.shape[0]
    assert bsz <= CTX_MOD_ROW and x.shape[1:] == (SEQ, D_MODEL) and ctx.shape[1:] == (CTX_LEN, D_MODEL)
    depth = w_ada.shape[0]
    xx = jnp.concatenate([ctx, x], axis=1)
    cc = jnp.zeros((MOD_ROWS, D_MODEL), F32).at[:bsz].set(c).at[CTX_MOD_ROW].set(c_ctx)
    mod = _ada_mod(cc, w_ada, b_ada).reshape(depth, MOD_ROWS, N_MOD, D_MODEL)

    rope = _rope_tables()
    dft_c = _channel_dft()
    tabs = (_dft_tables(SEQ, (SEQ * GROUP_WIDTH // FNET_GROUPS) ** -0.5)
            + _dft_tables(CTX_LEN, (CTX_LEN * GROUP_WIDTH // FNET_GROUPS) ** -0.5))

    for l in range(depth):
        w_cat = jnp.pad(w_in[l], ((0, 0), (0, LANES - N_GATES))).astype(BF16)
        p, gates, zc, zs = _in_proj(xx, mod[l], g_norm_mix[l], w_cat, dft_c)
        a = _attention(p, _attention_bias(rpb[l]))
        b_ = _gmlp(p, w_spatial[l], b_spatial[l], g_gmlp[l])
        c_ = _mlstm(p, gates, w_conv_qk[l], b_gate[l], g_mlstm[l], rope)
        d = _fnet(zc, zs, tabs, w_fnet[l])
        xx = _post(a, b_, c_, d, xx, mod[l], g_norm_ffn[l], w_out[l], w_ff1[l], w_ff2[l], g_final,
                   final=(l == depth - 1))
    return xx
```

```python
import functools

import jax
import jax.numpy as jnp
from jax import lax
from jax.experimental import pallas as pl
from jax.experimental.pallas import tpu as pltpu

D_MODEL = 1024
SEQ = 2048
DEPTH = 2
GRID_W = 64
GRID_ROWS = SEQ // GRID_W
CTX_LEN = 256
T_ALL = CTX_LEN + SEQ
HEAD_DIM = 64
GROUP_WIDTH = 256
GROUP_HEADS = 4
WIN_ROWS = 8
WIN_COLS = 16
CHUNK = 128
N_CHUNKS = T_ALL // CHUNK
CTX_CHUNKS = CTX_LEN // CHUNK
CONV_W = 3
FNET_GROUPS = 4
ROPE_THETA = 10000.0
D_FF = 4 * D_MODEL
N_MOD = 6
EPS = 1e-6
NEG_INF = -1e30
N_GATES = 4 * GROUP_HEADS
OFF_G = 10 * GROUP_WIDTH
D_IN = OFF_G + N_GATES
MOD_ROWS = 24
CTX_MOD_ROW = 16
LANES = 128
GATE_I_COLS = (0, 1, 2, 3, 8, 9, 10, 11)
GATE_F_COLS = (4, 5, 6, 7, 12, 13, 14, 15)
TOKEN_TILE = 256
N_TILES = T_ALL // TOKEN_TILE
P_WIDTH = 9 * GROUP_WIDTH
VMEM_LIMIT = 56 * 1024 * 1024

F32 = jnp.float32
BF16 = jnp.bfloat16


def _dot(a, b):
    return jnp.dot(a, b, preferred_element_type=F32)


def _dot_nt(a, b):
    return lax.dot_general(a, b, (((1,), (1,)), ((), ())), preferred_element_type=F32)


def _dot_f32(a, b):
    return jnp.dot(a, b, preferred_element_type=F32, precision=lax.Precision.HIGHEST)


def _resident(shape):
    nd = len(shape)
    return pl.BlockSpec(shape, lambda *_: (0,) * nd, pipeline_mode=pl.Buffered(1))


def _params(n_axes):
    return pltpu.CompilerParams(dimension_semantics=("arbitrary",) * n_axes,
                                vmem_limit_bytes=VMEM_LIMIT)


def _mod_body(c_ref, w_ref, b_ref, o_ref):
    s = c_ref[...]
    s = s * jax.nn.sigmoid(s)
    o_ref[0] = _dot(s.astype(BF16), w_ref[0].astype(BF16)) + b_ref[0]


def _ada_mod(cc, w_ada, b_ada):
    depth, d, n = w_ada.shape
    tn = 1536
    return pl.pallas_call(
        _mod_body,
        grid=(depth, n // tn),
        in_specs=[pl.BlockSpec((MOD_ROWS, d), lambda l, j: (0, 0)),
                  pl.BlockSpec((1, d, tn), lambda l, j: (l, 0, j)),
                  pl.BlockSpec((1, 1, tn), lambda l, j: (l, 0, j))],
        out_specs=pl.BlockSpec((1, MOD_ROWS, tn), lambda l, j: (l, 0, j)),
        out_shape=jax.ShapeDtypeStruct((depth, MOD_ROWS, n), F32),
        compiler_params=_params(2),
        name="ada_mod",
    )(cc, w_ada, b_ada.reshape(depth, 1, n))


def _mod_index(b, t):
    return (jnp.where(t == 0, CTX_MOD_ROW, b), 0, 0)


def _rms(x, g):
    return x * lax.rsqrt(jnp.mean(x * x, axis=-1, keepdims=True) + EPS) * g


def _inproj_body(x_ref, mod_ref, g_ref, w_ref, dft_ref, p_ref, gate_ref, zc_ref, zs_ref, h_scr):
    h = _rms(x_ref[0], g_ref[...]) * (1.0 + mod_ref[0, 1:2, :]) + mod_ref[0, 0:1, :]
    h_scr[...] = h.astype(BF16)
    step = 2 * GROUP_WIDTH
    for j in range(0, P_WIDTH - GROUP_WIDTH, step):
        p_ref[0, :, j:j + step] = _dot(h_scr[...], w_ref[:, j:j + step]).astype(BF16)
    j = P_WIDTH - GROUP_WIDTH
    p_ref[0, :, j:] = _dot(h_scr[...], w_ref[:, j:P_WIDTH]).astype(BF16)
    f = _dot(h_scr[...], w_ref[:, P_WIDTH:OFF_G]).astype(BF16)
    z = _dot(f, dft_ref[...])
    zc_ref[...] = z[:, :GROUP_WIDTH].astype(BF16)
    zs_ref[...] = z[:, GROUP_WIDTH:].astype(BF16)
    gate_ref[0] = _dot(h_scr[...], w_ref[:, OFF_G:])


def _projection_weights(w_in):
    pad = jnp.zeros((w_in.shape[0], LANES - len(GATE_I_COLS)), w_in.dtype)
    w_g = w_in[:, OFF_G:]
    return jnp.concatenate([w_in[:, :OFF_G], w_g[:, jnp.array(GATE_I_COLS)], pad,
                            w_g[:, jnp.array(GATE_F_COLS)], pad], axis=1).astype(BF16)


def _z_index(b, t):
    return ((t + N_TILES - 1) % N_TILES, b)


def _in_proj(xx, mod_l, g, w_cat, dft_c):
    bsz = xx.shape[0]
    tm = TOKEN_TILE
    wn = w_cat.shape[1]
    return pl.pallas_call(
        _inproj_body,
        grid=(bsz, N_TILES),
        in_specs=[pl.BlockSpec((1, tm, D_MODEL), lambda b, t: (b, t, 0)),
                  pl.BlockSpec((1, N_MOD, D_MODEL), _mod_index),
                  _resident((1, D_MODEL)),
                  _resident((D_MODEL, wn)),
                  _resident((GROUP_WIDTH, 2 * GROUP_WIDTH))],
        out_specs=[pl.BlockSpec((1, tm, P_WIDTH), lambda b, t: (b, t, 0)),
                   pl.BlockSpec((1, tm, 2 * LANES), lambda b, t: (b, t, 0)),
                   pl.BlockSpec((tm, GROUP_WIDTH), _z_index),
                   pl.BlockSpec((tm, GROUP_WIDTH), _z_index)],
        out_shape=[jax.ShapeDtypeStruct((bsz, T_ALL, P_WIDTH), BF16),
                   jax.ShapeDtypeStruct((bsz, T_ALL, 2 * LANES), F32),
                   jax.ShapeDtypeStruct((T_ALL, bsz * GROUP_WIDTH), BF16),
                   jax.ShapeDtypeStruct((T_ALL, bsz * GROUP_WIDTH), BF16)],
        scratch_shapes=[pltpu.VMEM((tm, D_MODEL), BF16)],
        compiler_params=_params(2),
        name="in_proj",
    )(xx, mod_l, g.reshape(1, D_MODEL), w_cat, dft_c)


def _group_spec(col_block):
    return pl.BlockSpec((1, T_ALL, GROUP_WIDTH), lambda b: (b, 0, col_block))


def _softmax_pv(parts):
    m = functools.reduce(jnp.maximum, [jnp.max(s, axis=-1, keepdims=True) for s, _ in parts])
    es = [jnp.exp(s - m) for s, _ in parts]
    den = functools.reduce(jnp.add, [jnp.sum(e, axis=-1, keepdims=True) for e in es])
    num = functools.reduce(jnp.add, [_dot(e.astype(BF16), v) for e, (_, v) in zip(es, parts)])
    return num / den


def _attn_body(q_ref, k_ref, v_ref, bias_ref, o_ref):
    scale = HEAD_DIM ** -0.5
    pairs = [slice(i * LANES, (i + 1) * LANES) for i in range(GROUP_WIDTH // LANES)]

    def stack_pair(q2):
        low = lax.broadcasted_iota(jnp.int32, q2.shape, 1) < HEAD_DIM
        zero = jnp.zeros_like(q2)
        return low, jnp.concatenate([jnp.where(low, q2, zero), jnp.where(low, zero, q2)], axis=0)

    def unstack_pair(low, o):
        n = o.shape[0] // 2
        return jnp.where(low, o[:n], o[n:]).astype(BF16)

    for ps in pairs:
        kc, vc = k_ref[0, 0:CTX_LEN, ps], v_ref[0, 0:CTX_LEN, ps]
        low, qm = stack_pair(q_ref[0, 0:CTX_LEN, ps] * scale)
        o_ref[0, 0:CTX_LEN, ps] = unstack_pair(low, _softmax_pv([(_dot_nt(qm, kc), vc)]))

    def row(r, carry):
        rs = jnp.clip(r - WIN_ROWS // 2, 0, GRID_ROWS - WIN_ROWS)
        q0 = pl.multiple_of(CTX_LEN + r * GRID_W, GRID_W)
        k0 = pl.multiple_of(CTX_LEN + rs * GRID_W, GRID_W)
        for i, ps in enumerate(pairs):
            kc, vc = k_ref[0, 0:CTX_LEN, ps], v_ref[0, 0:CTX_LEN, ps]
            kl = k_ref[0, pl.ds(k0, WIN_ROWS * GRID_W), ps]
            vl = v_ref[0, pl.ds(k0, WIN_ROWS * GRID_W), ps]
            low, qm = stack_pair(q_ref[0, pl.ds(q0, GRID_W), ps] * scale)
            o = _softmax_pv([(_dot_nt(qm, kl) + bias_ref[i, r - rs], vl), (_dot_nt(qm, kc), vc)])
            o_ref[0, pl.ds(q0, GRID_W), ps] = unstack_pair(low, o)
        return carry

    lax.fori_loop(0, GRID_ROWS, row, 0, unroll=2)


def _attention(p, bias_tab):
    bsz = p.shape[0]
    return pl.pallas_call(
        _attn_body,
        grid=(bsz,),
        in_specs=[_group_spec(0), _group_spec(1), _group_spec(2), _resident(bias_tab.shape)],
        out_specs=pl.BlockSpec((1, T_ALL, GROUP_WIDTH), lambda b: (b, 0, 0)),
        out_shape=jax.ShapeDtypeStruct((bsz, T_ALL, GROUP_WIDTH), BF16),
        compiler_params=_params(1),
        name="nbr_attention",
    )(p, p, p, bias_tab)


def _attention_bias(rpb):
    c = jnp.arange(GRID_W)
    qstart = jnp.clip(c - WIN_COLS // 2, 0, GRID_W - WIN_COLS)
    in_win = (c[None, :] >= qstart[:, None]) & (c[None, :] < qstart[:, None] + WIN_COLS)
    col = jnp.clip(c[None, :] - c[:, None] + WIN_COLS - 1, 0, 2 * WIN_COLS - 2)
    pick = (col[:, :, None] == jnp.arange(2 * WIN_COLS - 1)).astype(F32)
    by_drow = jnp.einsum('hde,qke->hdqk', rpb.astype(F32), pick, precision=lax.Precision.HIGHEST)
    by_drow = jnp.where(in_win, by_drow, NEG_INF)
    tab = jnp.stack([by_drow[:, WIN_ROWS - 1 - v:2 * WIN_ROWS - 1 - v] for v in range(WIN_ROWS)], axis=1)
    tab = tab.transpose(0, 1, 3, 2, 4).reshape(GROUP_HEADS // 2, 2, WIN_ROWS, GRID_W, WIN_ROWS * GRID_W)
    return tab.transpose(0, 2, 1, 3, 4).reshape(GROUP_HEADS // 2, WIN_ROWS, 2 * GRID_W, WIN_ROWS * GRID_W)


def _gmlp_body(u_ref, z_ref, ws_ref, bs_ref, g_ref, o_ref):
    def chunk(n, carry):
        rows = pl.ds(pl.multiple_of(n * CHUNK, CHUNK), CHUNK)
        z = _rms(jax.nn.gelu(z_ref[0, rows, :].astype(F32)), g_ref[...]).astype(BF16)
        u = jax.nn.gelu(u_ref[0, rows, :].astype(F32))
        for h in range(GROUP_HEADS):
            hs = slice(h * HEAD_DIM, (h + 1) * HEAD_DIM)
            mixed = _dot(ws_ref[h], z[:, hs]) + bs_ref[:, hs]
            o_ref[0, rows, hs] = (u[:, hs] * mixed).astype(BF16)
        return carry

    lax.fori_loop(0, N_CHUNKS, chunk, 0)


def _gmlp(p, w_spatial, b_spatial, g_gmlp):
    bsz = p.shape[0]
    bias = jnp.repeat(b_spatial.T, HEAD_DIM, axis=1)
    return pl.pallas_call(
        _gmlp_body,
        grid=(bsz,),
        in_specs=[_group_spec(3), _group_spec(4),
                  _resident((GROUP_HEADS, CHUNK, CHUNK)),
                  _resident((CHUNK, GROUP_WIDTH)),
                  _resident((1, GROUP_WIDTH))],
        out_specs=pl.BlockSpec((1, T_ALL, GROUP_WIDTH), lambda b: (b, 0, 0)),
        out_shape=jax.ShapeDtypeStruct((bsz, T_ALL, GROUP_WIDTH), BF16),
        compiler_params=_params(1),
        name="gmlp",
    )(p, p, w_spatial.astype(BF16), bias, g_gmlp.reshape(1, GROUP_WIDTH))


def _mlstm_body(q_ref, k_ref, v_ref, og_ref, gi_ref, gf_ref, wc_ref, bgi_ref, bgf_ref, cos_ref, sina_ref,
                sinb_ref, gln_ref, sh_ref, tri_ref, o_ref,
                qt_scr, k_scr, vat_scr, acol_scr, rows_scr, ht_scr, c_scr, m_scr):
    half = HEAD_DIM // 4
    n_chains = 2 * GROUP_HEADS

    def conv_act(ref, n, w0, w1, w2, rope, post):
        rows = pl.ds(pl.multiple_of(n * CHUNK, CHUNK), CHUNK)
        cur = ref[0, rows, :]
        first = (n == 0) | (n == CTX_CHUNKS)
        last = (n == CTX_CHUNKS - 1) | (n == N_CHUNKS - 1)
        prev_rows = pl.ds(pl.multiple_of(jnp.maximum(n - 1, 0) * CHUNK, CHUNK), CHUNK)
        next_rows = pl.ds(pl.multiple_of(jnp.minimum(n + 1, N_CHUNKS - 1) * CHUNK, CHUNK), CHUNK)
        keep_prev = jnp.where(first, 0.0, 1.0)
        keep_next = jnp.where(last, 0.0, 1.0)
        x_prev = _dot(sh_ref[0], cur) + keep_prev * _dot(sh_ref[1], ref[0, prev_rows, :])
        x_next = _dot(sh_ref[2], cur) + keep_next * _dot(sh_ref[3], ref[0, next_rows, :])
        y = w0 * x_prev + w1 * cur.astype(F32) + w2 * x_next
        y = y * jax.nn.sigmoid(y)
        if rope:
            lat = pl.ds(pl.multiple_of((n - CTX_CHUNKS) * CHUNK, CHUNK), CHUNK)
            y = (y * cos_ref[lat, :] + pltpu.roll(y, GROUP_WIDTH - half, 1) * sina_ref[lat, :]
                 + pltpu.roll(y, half, 1) * sinb_ref[lat, :])
        return y * post

    def prep(n, rope):
        rows = pl.ds(pl.multiple_of(n * CHUNK, CHUNK), CHUNK)
        gw = GROUP_WIDTH
        q = conv_act(q_ref, n, wc_ref[0:1, :gw], wc_ref[1:2, :gw], wc_ref[2:3, :gw], rope, 1.0)
        k = conv_act(k_ref, n, wc_ref[0:1, gw:], wc_ref[1:2, gw:], wc_ref[2:3, gw:], rope, HEAD_DIM ** -0.5)
        q_t = q.T.astype(BF16)
        v_t = v_ref[0, rows, :].astype(F32).T.astype(BF16)
        k = k.astype(BF16)
        sub = lax.broadcasted_iota(jnp.int32, (HEAD_DIM, CHUNK), 0)
        ones_row = jnp.where(sub == 0, 1.0, 0.0).astype(BF16)
        for h in range(GROUP_HEADS):
            hs = slice(h * HEAD_DIM, (h + 1) * HEAD_DIM)
            qt_scr[n, h] = q_t[hs, :]
            k_scr[n, h] = k[:, hs]
            vat_scr[n, h, 0:HEAD_DIM, :] = v_t[hs, :]
            vat_scr[n, h, HEAD_DIM:, :] = ones_row
        ht_scr[n] = jnp.zeros((GROUP_WIDTH, CHUNK), F32)

        g_i = gi_ref[0, rows, :] + bgi_ref[...]
        lf = jax.nn.log_sigmoid(gf_ref[0, rows, :] + bgf_ref[...])
        pre = _dot_f32(tri_ref[...], lf)
        suf = pre[CHUNK - 1:CHUNK, :] - pre + lf
        lane = lax.broadcasted_iota(jnp.int32, (CHUNK, LANES), 1)
        b = jnp.where(lane < GROUP_HEADS, pre, suf)
        a = g_i - b
        acol_scr[n] = a
        a_t = a.T[0:n_chains, :]
        b_t = b.T[0:n_chains, :]
        chain = lax.broadcasted_iota(jnp.int32, (n_chains, CHUNK), 0)
        b_end = jnp.where(chain < GROUP_HEADS, b_t[:, CHUNK - 1:CHUNK], b_t[:, 0:1])
        rows_scr[n, 0] = a_t
        rows_scr[n, 1] = b_t
        rows_scr[n, 2] = jnp.broadcast_to(b_end, (n_chains, CHUNK))
        rows_scr[n, 3] = jnp.broadcast_to(jnp.max(a_t, axis=1, keepdims=True), (n_chains, CHUNK))

    def prep_ctx(n, carry):
        prep(n, False)
        return carry

    def prep_lat(n, carry):
        prep(n, True)
        return carry

    lax.fori_loop(0, CTX_CHUNKS, prep_ctx, 0)
    lax.fori_loop(CTX_CHUNKS, N_CHUNKS, prep_lat, 0)
    c_scr[...] = jnp.zeros(c_scr.shape, F32)
    m_scr[...] = jnp.zeros(m_scr.shape, F32)

    s_idx = lax.broadcasted_iota(jnp.int32, (CHUNK, CHUNK), 0)
    t_idx = lax.broadcasted_iota(jnp.int32, (CHUNK, CHUNK), 1)

    def scan_chunk(n, rev):
        causal = (s_idx >= t_idx) if rev else (s_idx <= t_idx)
        for h in range(GROUP_HEADS):
            c = rev * GROUP_HEADS + h
            hs = slice(h * HEAD_DIM, (h + 1) * HEAD_DIM)
            q_t, k, va_t = qt_scr[n, h], k_scr[n, h], vat_scr[n, h]

            arg = jnp.where(causal, acol_scr[n, :, c:c + 1], NEG_INF)
            cm = jnp.max(arg, axis=0, keepdims=True)
            g = (_dot(k, q_t) * jnp.exp(arg - cm)).astype(BF16)
            x1 = _dot(va_t, g)

            a_row = rows_scr[n, 0, c:c + 1, :]
            b_row = rows_scr[n, 1, c:c + 1, :]
            b_end = rows_scr[n, 2, c:c + 1, :]
            a_max = rows_scr[n, 3, c:c + 1, :]
            m_mem = m_scr[c:c + 1, :]
            c_mem = c_scr[c]
            mu = jnp.maximum(cm, m_mem)
            both = x1 * jnp.exp(cm - mu) + _dot(c_mem.astype(BF16), q_t) * jnp.exp(m_mem - mu)
            den = both[HEAD_DIM:HEAD_DIM + 1, :]
            inv = 1.0 / jnp.maximum(jnp.abs(den), jnp.exp(-(b_row + mu)))
            ht_scr[n, hs, :] += both[0:HEAD_DIM, :] * inv

            m_new = b_end + jnp.maximum(m_mem, a_max)
            w_src = jnp.exp(b_end + a_row - m_new)
            decay = jnp.exp(b_end + m_mem - m_new)
            c_scr[c] = decay[:, 0:HEAD_DIM] * c_mem + _dot((va_t.astype(F32) * w_src).astype(BF16), k)
            m_scr[c:c + 1, :] = m_new

    def scan(i, carry):
        scan_chunk(i, 0)
        scan_chunk(jnp.where(i < CTX_CHUNKS, CTX_CHUNKS - 1 - i, N_CHUNKS + CTX_CHUNKS - 1 - i), 1)
        return carry

    lax.fori_loop(0, N_CHUNKS, scan, 0)

    def finish(n, carry):
        rows = pl.ds(pl.multiple_of(n * CHUNK, CHUNK), CHUNK)
        gate = jax.nn.sigmoid(og_ref[0, rows, :].astype(F32))
        ys = []
        for h in range(GROUP_HEADS):
            x = ht_scr[n, h * HEAD_DIM:(h + 1) * HEAD_DIM, :]
            mu = jnp.mean(x, axis=0, keepdims=True)
            var = jnp.mean(jnp.square(x - mu), axis=0, keepdims=True)
            ys.append((x - mu) * lax.rsqrt(var + EPS))
        y = jnp.concatenate(ys, axis=0).T * gln_ref[...]
        o_ref[0, rows, :] = (gate * y).astype(BF16)
        return carry

    lax.fori_loop(0, N_CHUNKS, finish, 0)


def _rope_tables():
    t = jnp.arange(SEQ)
    pos = jnp.stack([(t // GRID_W).astype(F32), (t % GRID_W).astype(F32)], axis=1)
    lane = jnp.arange(GROUP_WIDTH)
    m = HEAD_DIM // 4
    inv = ROPE_THETA ** (-(lane % m).astype(F32) / m)
    axis = (lane % HEAD_DIM) // (HEAD_DIM // 2)
    ang = jnp.where(axis[None, :] == 0, pos[:, 0:1], pos[:, 1:2]) * inv[None, :]
    low = ((lane % (2 * m)) < m)[None, :]
    cos, sin = jnp.cos(ang), jnp.sin(ang)
    return cos, jnp.where(low, -sin, 0.0), jnp.where(low, 0.0, sin)


def _mlstm(p, gates, w_conv, b_gate, g_mlstm, rope):
    bsz = p.shape[0]
    i = jnp.arange(CHUNK)
    shifts = jnp.stack([i[:, None] == i[None, :] + 1,
                        (i[:, None] == 0) & (i[None, :] == CHUNK - 1),
                        i[:, None] + 1 == i[None, :],
                        (i[:, None] == CHUNK - 1) & (i[None, :] == 0)]).astype(BF16)
    tri = (i[:, None] >= i[None, :]).astype(F32)
    bgi = jnp.pad(b_gate[jnp.array(GATE_I_COLS)], (0, LANES - 2 * GROUP_HEADS)).reshape(1, LANES)
    bgf = jnp.pad(b_gate[jnp.array(GATE_F_COLS)], (0, LANES - 2 * GROUP_HEADS)).reshape(1, LANES)
    cos, sina, sinb = rope
    n_chains = 2 * GROUP_HEADS
    return pl.pallas_call(
        _mlstm_body,
        grid=(bsz,),
        in_specs=[_group_spec(5), _group_spec(6), _group_spec(7), _group_spec(8),
                  pl.BlockSpec((1, T_ALL, LANES), lambda b: (b, 0, 0)),
                  pl.BlockSpec((1, T_ALL, LANES), lambda b: (b, 0, 1)),
                  _resident((CONV_W, 2 * GROUP_WIDTH)),
                  _resident((1, LANES)), _resident((1, LANES)),
                  _resident((SEQ, GROUP_WIDTH)), _resident((SEQ, GROUP_WIDTH)), _resident((SEQ, GROUP_WIDTH)),
                  _resident((1, GROUP_WIDTH)),
                  _resident((4, CHUNK, CHUNK)),
                  _resident((CHUNK, CHUNK))],
        out_specs=pl.BlockSpec((1, T_ALL, GROUP_WIDTH), lambda b: (b, 0, 0)),
        out_shape=jax.ShapeDtypeStruct((bsz, T_ALL, GROUP_WIDTH), BF16),
        scratch_shapes=[pltpu.VMEM((N_CHUNKS, GROUP_HEADS, HEAD_DIM, CHUNK), BF16),
                        pltpu.VMEM((N_CHUNKS, GROUP_HEADS, CHUNK, HEAD_DIM), BF16),
                        pltpu.VMEM((N_CHUNKS, GROUP_HEADS, 2 * HEAD_DIM, CHUNK), BF16),
                        pltpu.VMEM((N_CHUNKS, CHUNK, LANES), F32),
                        pltpu.VMEM((N_CHUNKS, 4, n_chains, CHUNK), F32),
                        pltpu.VMEM((N_CHUNKS, GROUP_WIDTH, CHUNK), F32),
                        pltpu.VMEM((n_chains, 2 * HEAD_DIM, HEAD_DIM), F32),
                        pltpu.VMEM((n_chains, CHUNK), F32)],
        compiler_params=_params(1),
        name="mlstm",
    )(p, p, p, p, gates, gates, w_conv, bgi, bgf, cos, sina, sinb, g_mlstm.reshape(1, GROUP_WIDTH), shifts, tri)


def _dft_tables(n, scale):
    j = jnp.arange(n, dtype=jnp.int32)
    ang = ((j[:, None] * j[None, :]) % n).astype(F32) * (2.0 * jnp.pi / n)
    return (jnp.cos(ang) * scale).astype(BF16), (-jnp.sin(ang) * scale).astype(BF16)


def _channel_dft():
    gc = GROUP_WIDTH // FNET_GROUPS
    j = jnp.arange(GROUP_WIDTH, dtype=jnp.int32)
    same = (j[:, None] // gc) == (j[None, :] // gc)
    ang = (((j[:, None] % gc) * (j[None, :] % gc)) % gc).astype(F32) * (2.0 * jnp.pi / gc)
    c = jnp.where(same, jnp.cos(ang), 0.0)
    s = jnp.where(same, jnp.sin(ang), 0.0)
    return jnp.concatenate([c, s], axis=1).astype(BF16)


def _fnet_body(zc_ref, zs_ref, cl_ref, sl_ref, cc_ref, sc_ref, w_ref, o_ref):
    tm = 512
    for i in range(SEQ // tm):
        r = slice(i * tm, (i + 1) * tm)
        y = _dot(cl_ref[r, :], zc_ref[0:SEQ, :]) + _dot(sl_ref[r, :], zs_ref[0:SEQ, :])
        o_ref[r, :] = _dot(y.astype(BF16), w_ref[...]).astype(BF16)
    y = _dot(cc_ref[...], zc_ref[SEQ:, :]) + _dot(sc_ref[...], zs_ref[SEQ:, :])
    o_ref[SEQ:, :] = _dot(y.astype(BF16), w_ref[...]).astype(BF16)


def _fnet(zc, zs, tabs, w_fnet):
    bsz = zc.shape[1] // GROUP_WIDTH
    cl, sl, cc, sc = tabs
    col = pl.BlockSpec((T_ALL, GROUP_WIDTH), lambda b: (0, b))
    return pl.pallas_call(
        _fnet_body,
        grid=(bsz,),
        in_specs=[col, col, _resident((SEQ, SEQ)), _resident((SEQ, SEQ)),
                  _resident((CTX_LEN, CTX_LEN)), _resident((CTX_LEN, CTX_LEN)),
                  _resident((GROUP_WIDTH, GROUP_WIDTH))],
        out_specs=col,
        out_shape=jax.ShapeDtypeStruct((T_ALL, bsz * GROUP_WIDTH), BF16),
        compiler_params=_params(1),
        name="fnet",
    )(zc, zs, cl, sl, cc, sc, w_fnet.astype(BF16))


def _post_body(a_ref, b_ref, c_ref, d_ref, x_ref, mod_ref, g_ref, wo_ref, w1_ref, w2_ref, gf_ref,
               o_ref, h_scr, acc_scr, *, final):
    gw = GROUP_WIDTH
    y = (_dot(a_ref[0], wo_ref[0:gw, :]) + _dot(b_ref[0], wo_ref[gw:2 * gw, :])
         + _dot(c_ref[0], wo_ref[2 * gw:3 * gw, :]) + _dot(d_ref[...], wo_ref[3 * gw:, :]))
    x1 = x_ref[0] + mod_ref[0, 2:3, :] * y
    h = _rms(x1, g_ref[...]) * (1.0 + mod_ref[0, 4:5, :]) + mod_ref[0, 3:4, :]
    h_scr[...] = h.astype(BF16)
    step = 512
    for j in range(0, D_FF, step):
        a = jnp.maximum(_dot(h_scr[...], w1_ref[:, j:j + step]), 0.0)
        part = _dot((a * a).astype(BF16), w2_ref[j:j + step, :])
        if j == 0:
            acc_scr[...] = part
        else:
            acc_scr[...] += part
    x2 = x1 + mod_ref[0, 5:6, :] * acc_scr[...]
    if final:
        x2 = _rms(x2, gf_ref[...])
    o_ref[0] = x2


def _post(a, b_, c_, d, xx, mod_l, g_ffn, w_out, w_ff1, w_ff2, g_final, final):
    bsz = xx.shape[0]
    tm = TOKEN_TILE
    skip = 1 if final else 0
    tok = lambda b, t: (b, t + skip, 0)
    grp = pl.BlockSpec((1, tm, GROUP_WIDTH), tok)
    out_len = T_ALL - skip * TOKEN_TILE
    return pl.pallas_call(
        functools.partial(_post_body, final=final),
        grid=(bsz, N_TILES - skip),
        in_specs=[grp, grp, grp,
                  pl.BlockSpec((tm, GROUP_WIDTH), lambda b, t: _z_index(b, t + skip)),
                  pl.BlockSpec((1, tm, D_MODEL), tok),
                  pl.BlockSpec((1, N_MOD, D_MODEL), lambda b, t: _mod_index(b, t + skip)),
                  _resident((1, D_MODEL)),
                  _resident((D_MODEL, D_MODEL)),
                  _resident((D_MODEL, D_FF)),
                  _resident((D_FF, D_MODEL)),
                  _resident((1, D_MODEL))],
        out_specs=pl.BlockSpec((1, tm, D_MODEL), lambda b, t: (b, t, 0)),
        out_shape=jax.ShapeDtypeStruct((bsz, out_len, D_MODEL), F32),
        scratch_shapes=[pltpu.VMEM((tm, D_MODEL), BF16), pltpu.VMEM((tm, D_MODEL), F32)],
        compiler_params=_params(2),
        name="out_proj_mlp",
    )(a, b_, c_, d, xx, mod_l, g_ffn.reshape(1, D_MODEL), w_out.astype(BF16), w_ff1.astype(BF16),
      w_ff2.astype(BF16), g_final.reshape(1, D_MODEL))


def kernel(x, c, ctx, c_ctx, w_ada, b_ada, g_norm_mix, g_norm_ffn, w_in, b_gate, w_conv_qk, rpb, w_spatial,
           b_spatial, g_gmlp, g_mlstm, w_fnet, w_out, w_ff1, w_ff2, g_final):
    bsz = x.shape[0]
    assert bsz <= CTX_MOD_ROW and x.shape[1:] == (SEQ, D_MODEL) and ctx.shape[1:] == (CTX_LEN, D_MODEL)
    depth = w_ada.shape[0]
    xx = jnp.concatenate([ctx, x], axis=1)
    cc = jnp.zeros((MOD_ROWS, D_MODEL), F32).at[:bsz].set(c).at[CTX_MOD_ROW].set(c_ctx)
    mod = _ada_mod(cc, w_ada, b_ada).reshape(depth, MOD_ROWS, N_MOD, D_MODEL)

    rope = _rope_tables()
    dft_c = _channel_dft()
    tabs = (_dft_tables(SEQ, (SEQ * GROUP_WIDTH // FNET_GROUPS) ** -0.5)
            + _dft_tables(CTX_LEN, (CTX_LEN * GROUP_WIDTH // FNET_GROUPS) ** -0.5))

    for l in range(depth):
        w_cat = _projection_weights(w_in[l])
        p, gates, zc, zs = _in_proj(xx, mod[l], g_norm_mix[l], w_cat, dft_c)
        a = _attention(p, _attention_bias(rpb[l]))
        b_ = _gmlp(p, w_spatial[l], b_spatial[l], g_gmlp[l])
        c_ = _mlstm(p, gates, w_conv_qk[l], b_gate[l], g_mlstm[l], rope)
        d = _fnet(zc, zs, tabs, w_fnet[l])
        xx = _post(a, b_, c_, d, xx, mod[l], g_norm_ffn[l], w_out[l], w_ff1[l], w_ff2[l], g_final,
                   final=(l == depth - 1))
    return xx
```

```python
import functools

import jax
import jax.numpy as jnp
from jax import lax
from jax.experimental import pallas as pl
from jax.experimental.pallas import tpu as pltpu

D_MODEL = 1024
SEQ = 2048
DEPTH = 2
GRID_W = 64
GRID_ROWS = SEQ // GRID_W
CTX_LEN = 256
T_ALL = CTX_LEN + SEQ
HEAD_DIM = 64
GROUP_WIDTH = 256
GROUP_HEADS = 4
WIN_ROWS = 8
WIN_COLS = 16
CHUNK = 128
N_CHUNKS = T_ALL // CHUNK
CTX_CHUNKS = CTX_LEN // CHUNK
LAT_CHUNKS = SEQ // CHUNK
CONV_W = 3
FNET_GROUPS = 4
ROPE_THETA = 10000.0
D_FF = 4 * D_MODEL
N_MOD = 6
EPS = 1e-6
NEG_INF = -1e30
N_GATES = 4 * GROUP_HEADS
OFF_G = 10 * GROUP_WIDTH
D_IN = OFF_G + N_GATES
MOD_ROWS = 24
CTX_MOD_ROW = 16
LANES = 128
GATE_I_COLS = (0, 1, 2, 3, 8, 9, 10, 11)
GATE_F_COLS = (4, 5, 6, 7, 12, 13, 14, 15)
TOKEN_TILE = 768
LATENT_TILE = 512
P_WIDTH = 9 * GROUP_WIDTH
VMEM_LIMIT = 56 * 1024 * 1024

F32 = jnp.float32
BF16 = jnp.bfloat16


def _dot(a, b):
    return jnp.dot(a, b, preferred_element_type=F32)


def _dot_nt(a, b):
    return lax.dot_general(a, b, (((1,), (1,)), ((), ())), preferred_element_type=F32)


def _dot_f32(a, b):
    return jnp.dot(a, b, preferred_element_type=F32, precision=lax.Precision.HIGHEST)


def _resident(shape):
    nd = len(shape)
    return pl.BlockSpec(shape, lambda *_: (0,) * nd, pipeline_mode=pl.Buffered(1))


def _params(n_axes):
    return pltpu.CompilerParams(dimension_semantics=("arbitrary",) * n_axes,
                                vmem_limit_bytes=VMEM_LIMIT)


def _mod_body(c_ref, w_ref, b_ref, o_ref):
    s = c_ref[...]
    s = s * jax.nn.sigmoid(s)
    o_ref[0] = _dot(s.astype(BF16), w_ref[0].astype(BF16)) + b_ref[0]


def _ada_mod(cc, w_ada, b_ada):
    depth, d, n = w_ada.shape
    tn = 1536
    return pl.pallas_call(
        _mod_body,
        grid=(depth, n // tn),
        in_specs=[pl.BlockSpec((MOD_ROWS, d), lambda l, j: (0, 0)),
                  pl.BlockSpec((1, d, tn), lambda l, j: (l, 0, j)),
                  pl.BlockSpec((1, 1, tn), lambda l, j: (l, 0, j))],
        out_specs=pl.BlockSpec((1, MOD_ROWS, tn), lambda l, j: (l, 0, j)),
        out_shape=jax.ShapeDtypeStruct((depth, MOD_ROWS, n), F32),
        compiler_params=_params(2),
        name="ada_mod",
    )(cc, w_ada, b_ada.reshape(depth, 1, n))


_MOD_SPECS = [pl.BlockSpec((1, N_MOD, D_MODEL), lambda b, t: (b, 0, 0)),
              pl.BlockSpec((1, N_MOD, D_MODEL), lambda b, t: (CTX_MOD_ROW, 0, 0))]


def _mod_rows(mod_lat_ref, mod_ctx_ref, tile):
    tok = pl.program_id(1) * tile + lax.broadcasted_iota(jnp.int32, (tile, 1), 0)
    is_ctx = tok >= SEQ
    return lambda k: jnp.where(is_ctx, mod_ctx_ref[0, k:k + 1, :], mod_lat_ref[0, k:k + 1, :])


def _rms(x, g):
    return x * lax.rsqrt(jnp.mean(x * x, axis=-1, keepdims=True) + EPS) * g


def _inproj_body(x_ref, ml_ref, mc_ref, g_ref, w_ref, dft_ref, p_ref, gate_ref, zc_ref, zs_ref, h_scr):
    mod = _mod_rows(ml_ref, mc_ref, x_ref.shape[1])
    h = _rms(x_ref[0], g_ref[...]) * (1.0 + mod(1)) + mod(0)
    h_scr[...] = h.astype(BF16)
    f = _dot(h_scr[...], w_ref[:, P_WIDTH:OFF_G]).astype(BF16)
    z = _dot(f, dft_ref[...])
    zc_ref[...] = z[:, :GROUP_WIDTH].astype(BF16)
    zs_ref[...] = z[:, GROUP_WIDTH:].astype(BF16)
    step = 2 * GROUP_WIDTH
    for j in range(0, P_WIDTH - GROUP_WIDTH, step):
        p_ref[0, :, j:j + step] = _dot(h_scr[...], w_ref[:, j:j + step]).astype(BF16)
    j = P_WIDTH - GROUP_WIDTH
    p_ref[0, :, j:] = _dot(h_scr[...], w_ref[:, j:P_WIDTH]).astype(BF16)
    gate_ref[0] = _dot(h_scr[...], w_ref[:, OFF_G:])


def _projection_weights(w_in):
    pad = jnp.zeros((w_in.shape[0], LANES - len(GATE_I_COLS)), w_in.dtype)
    w_g = w_in[:, OFF_G:]
    return jnp.concatenate([w_in[:, :OFF_G], w_g[:, jnp.array(GATE_I_COLS)], pad,
                            w_g[:, jnp.array(GATE_F_COLS)], pad], axis=1).astype(BF16)


def _in_proj(xx, mod_l, g, w_cat, dft_c):
    bsz = xx.shape[0]
    tm = TOKEN_TILE
    wn = w_cat.shape[1]
    return pl.pallas_call(
        _inproj_body,
        grid=(bsz, T_ALL // tm),
        in_specs=[pl.BlockSpec((1, tm, D_MODEL), lambda b, t: (b, t, 0))] + _MOD_SPECS
                 + [_resident((1, D_MODEL)),
                    _resident((D_MODEL, wn)),
                    _resident((GROUP_WIDTH, 2 * GROUP_WIDTH))],
        out_specs=[pl.BlockSpec((1, tm, P_WIDTH), lambda b, t: (b, t, 0)),
                   pl.BlockSpec((1, tm, 2 * LANES), lambda b, t: (b, t, 0)),
                   pl.BlockSpec((tm, GROUP_WIDTH), lambda b, t: (t, b)),
                   pl.BlockSpec((tm, GROUP_WIDTH), lambda b, t: (t, b))],
        out_shape=[jax.ShapeDtypeStruct((bsz, T_ALL, P_WIDTH), BF16),
                   jax.ShapeDtypeStruct((bsz, T_ALL, 2 * LANES), F32),
                   jax.ShapeDtypeStruct((T_ALL, bsz * GROUP_WIDTH), BF16),
                   jax.ShapeDtypeStruct((T_ALL, bsz * GROUP_WIDTH), BF16)],
        scratch_shapes=[pltpu.VMEM((tm, D_MODEL), BF16)],
        compiler_params=_params(2),
        name="in_proj",
    )(xx, mod_l, mod_l, g.reshape(1, D_MODEL), w_cat, dft_c)


def _group_spec(col_block):
    return pl.BlockSpec((1, T_ALL, GROUP_WIDTH), lambda b: (b, 0, col_block))


def _softmax_pv(parts):
    m = functools.reduce(jnp.maximum, [jnp.max(s, axis=-1, keepdims=True) for s, _ in parts])
    es = [jnp.exp(s - m) for s, _ in parts]
    den = functools.reduce(jnp.add, [jnp.sum(e, axis=-1, keepdims=True) for e in es])
    num = functools.reduce(jnp.add, [_dot(e.astype(BF16), v) for e, (_, v) in zip(es, parts)])
    return num / den


def _attn_body(q_ref, k_ref, v_ref, bias_ref, o_ref):
    scale = HEAD_DIM ** -0.5
    pairs = [slice(i * LANES, (i + 1) * LANES) for i in range(GROUP_WIDTH // LANES)]

    def stack_pair(q2):
        low = lax.broadcasted_iota(jnp.int32, q2.shape, 1) < HEAD_DIM
        zero = jnp.zeros_like(q2)
        return low, jnp.concatenate([jnp.where(low, q2, zero), jnp.where(low, zero, q2)], axis=0)

    def unstack_pair(low, o):
        n = o.shape[0] // 2
        return jnp.where(low, o[:n], o[n:]).astype(BF16)

    ctx = slice(SEQ, T_ALL)
    for ps in pairs:
        kc, vc = k_ref[0, ctx, ps], v_ref[0, ctx, ps]
        low, qm = stack_pair(q_ref[0, ctx, ps] * scale)
        o_ref[0, ctx, ps] = unstack_pair(low, _softmax_pv([(_dot_nt(qm, kc), vc)]))

    def row(r, carry):
        rs = jnp.clip(r - WIN_ROWS // 2, 0, GRID_ROWS - WIN_ROWS)
        q0 = pl.multiple_of(r * GRID_W, GRID_W)
        k0 = pl.multiple_of(rs * GRID_W, GRID_W)
        for i, ps in enumerate(pairs):
            kc, vc = k_ref[0, ctx, ps], v_ref[0, ctx, ps]
            kl = k_ref[0, pl.ds(k0, WIN_ROWS * GRID_W), ps]
            vl = v_ref[0, pl.ds(k0, WIN_ROWS * GRID_W), ps]
            low, qm = stack_pair(q_ref[0, pl.ds(q0, GRID_W), ps] * scale)
            o = _softmax_pv([(_dot_nt(qm, kl) + bias_ref[i, r - rs], vl), (_dot_nt(qm, kc), vc)])
            o_ref[0, pl.ds(q0, GRID_W), ps] = unstack_pair(low, o)
        return carry

    lax.fori_loop(0, GRID_ROWS, row, 0, unroll=2)


def _attention(p, bias_tab):
    bsz = p.shape[0]
    return pl.pallas_call(
        _attn_body,
        grid=(bsz,),
        in_specs=[_group_spec(0), _group_spec(1), _group_spec(2), _resident(bias_tab.shape)],
        out_specs=pl.BlockSpec((1, T_ALL, GROUP_WIDTH), lambda b: (b, 0, 0)),
        out_shape=jax.ShapeDtypeStruct((bsz, T_ALL, GROUP_WIDTH), BF16),
        compiler_params=_params(1),
        name="nbr_attention",
    )(p, p, p, bias_tab)


def _attention_bias(rpb):
    c = jnp.arange(GRID_W)
    qstart = jnp.clip(c - WIN_COLS // 2, 0, GRID_W - WIN_COLS)
    in_win = (c[None, :] >= qstart[:, None]) & (c[None, :] < qstart[:, None] + WIN_COLS)
    col = jnp.clip(c[None, :] - c[:, None] + WIN_COLS - 1, 0, 2 * WIN_COLS - 2)
    pick = (col[:, :, None] == jnp.arange(2 * WIN_COLS - 1)).astype(F32)
    by_drow = jnp.einsum('hde,qke->hdqk', rpb.astype(F32), pick, precision=lax.Precision.HIGHEST)
    by_drow = jnp.where(in_win, by_drow, NEG_INF)
    tab = jnp.stack([by_drow[:, WIN_ROWS - 1 - v:2 * WIN_ROWS - 1 - v] for v in range(WIN_ROWS)], axis=1)
    tab = tab.transpose(0, 1, 3, 2, 4).reshape(GROUP_HEADS // 2, 2, WIN_ROWS, GRID_W, WIN_ROWS * GRID_W)
    return tab.transpose(0, 2, 1, 3, 4).reshape(GROUP_HEADS // 2, WIN_ROWS, 2 * GRID_W, WIN_ROWS * GRID_W)


def _gmlp_body(u_ref, z_ref, ws_ref, bs_ref, g_ref, o_ref):
    def chunk(n, carry):
        rows = pl.ds(pl.multiple_of(n * CHUNK, CHUNK), CHUNK)
        z = _rms(jax.nn.gelu(z_ref[0, rows, :].astype(F32)), g_ref[...]).astype(BF16)
        u = jax.nn.gelu(u_ref[0, rows, :].astype(F32))
        for h in range(GROUP_HEADS):
            hs = slice(h * HEAD_DIM, (h + 1) * HEAD_DIM)
            mixed = _dot(ws_ref[h], z[:, hs]) + bs_ref[:, hs]
            o_ref[0, rows, hs] = (u[:, hs] * mixed).astype(BF16)
        return carry

    lax.fori_loop(0, N_CHUNKS, chunk, 0, unroll=3)


def _gmlp(p, w_spatial, b_spatial, g_gmlp):
    bsz = p.shape[0]
    bias = jnp.repeat(b_spatial.T, HEAD_DIM, axis=1)
    return pl.pallas_call(
        _gmlp_body,
        grid=(bsz,),
        in_specs=[_group_spec(3), _group_spec(4),
                  _resident((GROUP_HEADS, CHUNK, CHUNK)),
                  _resident((CHUNK, GROUP_WIDTH)),
                  _resident((1, GROUP_WIDTH))],
        out_specs=pl.BlockSpec((1, T_ALL, GROUP_WIDTH), lambda b: (b, 0, 0)),
        out_shape=jax.ShapeDtypeStruct((bsz, T_ALL, GROUP_WIDTH), BF16),
        compiler_params=_params(1),
        name="gmlp",
    )(p, p, w_spatial.astype(BF16), bias, g_gmlp.reshape(1, GROUP_WIDTH))


def _mlstm_body(q_ref, k_ref, v_ref, og_ref, gi_ref, gf_ref, wc_ref, bgi_ref, bgf_ref, cos_ref, sina_ref,
                sinb_ref, gln_ref, sh_ref, tri_ref, o_ref,
                qt_scr, k_scr, vat_scr, acol_scr, rows_scr, ht_scr, c_scr, m_scr):
    half = HEAD_DIM // 4
    n_chains = 2 * GROUP_HEADS

    def conv_act(ref, n, w0, w1, w2, rope, post):
        rows = pl.ds(pl.multiple_of(n * CHUNK, CHUNK), CHUNK)
        cur = ref[0, rows, :]
        first = (n == 0) | (n == LAT_CHUNKS)
        last = (n == LAT_CHUNKS - 1) | (n == N_CHUNKS - 1)
        prev_rows = pl.ds(pl.multiple_of(jnp.maximum(n - 1, 0) * CHUNK, CHUNK), CHUNK)
        next_rows = pl.ds(pl.multiple_of(jnp.minimum(n + 1, N_CHUNKS - 1) * CHUNK, CHUNK), CHUNK)
        keep_prev = jnp.where(first, 0.0, 1.0)
        keep_next = jnp.where(last, 0.0, 1.0)
        x_prev = _dot(sh_ref[0], cur) + keep_prev * _dot(sh_ref[1], ref[0, prev_rows, :])
        x_next = _dot(sh_ref[2], cur) + keep_next * _dot(sh_ref[3], ref[0, next_rows, :])
        y = w0 * x_prev + w1 * cur.astype(F32) + w2 * x_next
        y = y * jax.nn.sigmoid(y)
        if rope:
            y = (y * cos_ref[rows, :] + pltpu.roll(y, GROUP_WIDTH - half, 1) * sina_ref[rows, :]
                 + pltpu.roll(y, half, 1) * sinb_ref[rows, :])
        return y * post

    def prep(n, rope):
        rows = pl.ds(pl.multiple_of(n * CHUNK, CHUNK), CHUNK)
        gw = GROUP_WIDTH
        q = conv_act(q_ref, n, wc_ref[0:1, :gw], wc_ref[1:2, :gw], wc_ref[2:3, :gw], rope, 1.0)
        k = conv_act(k_ref, n, wc_ref[0:1, gw:], wc_ref[1:2, gw:], wc_ref[2:3, gw:], rope, HEAD_DIM ** -0.5)
        q_t = q.T.astype(BF16)
        v_t = v_ref[0, rows, :].astype(F32).T.astype(BF16)
        k = k.astype(BF16)
        sub = lax.broadcasted_iota(jnp.int32, (HEAD_DIM, CHUNK), 0)
        ones_row = jnp.where(sub == 0, 1.0, 0.0).astype(BF16)
        for h in range(GROUP_HEADS):
            hs = slice(h * HEAD_DIM, (h + 1) * HEAD_DIM)
            qt_scr[n, h] = q_t[hs, :]
            k_scr[n, h] = k[:, hs]
            vat_scr[n, h, 0:HEAD_DIM, :] = v_t[hs, :]
            vat_scr[n, h, HEAD_DIM:, :] = ones_row
        ht_scr[n] = jnp.zeros((GROUP_WIDTH, CHUNK), F32)

        g_i = gi_ref[0, rows, :] + bgi_ref[...]
        lf = jax.nn.log_sigmoid(gf_ref[0, rows, :] + bgf_ref[...])
        pre = _dot_f32(tri_ref[...], lf)
        suf = pre[CHUNK - 1:CHUNK, :] - pre + lf
        lane = lax.broadcasted_iota(jnp.int32, (CHUNK, LANES), 1)
        b = jnp.where(lane < GROUP_HEADS, pre, suf)
        a = g_i - b
        acol_scr[n] = a
        a_t = a.T[0:n_chains, :]
        b_t = b.T[0:n_chains, :]
        chain = lax.broadcasted_iota(jnp.int32, (n_chains, CHUNK), 0)
        b_end = jnp.where(chain < GROUP_HEADS, b_t[:, CHUNK - 1:CHUNK], b_t[:, 0:1])
        rows_scr[n, 0] = a_t
        rows_scr[n, 1] = b_t
        rows_scr[n, 2] = jnp.broadcast_to(b_end, (n_chains, CHUNK))
        rows_scr[n, 3] = jnp.broadcast_to(jnp.max(a_t, axis=1, keepdims=True), (n_chains, CHUNK))

    def prep_ctx(n, carry):
        prep(n, False)
        return carry

    def prep_lat(n, carry):
        prep(n, True)
        return carry

    lax.fori_loop(0, LAT_CHUNKS, prep_lat, 0)
    lax.fori_loop(LAT_CHUNKS, N_CHUNKS, prep_ctx, 0)
    c_scr[...] = jnp.zeros(c_scr.shape, F32)
    m_scr[...] = jnp.zeros(m_scr.shape, F32)

    s_idx = lax.broadcasted_iota(jnp.int32, (CHUNK, CHUNK), 0)
    t_idx = lax.broadcasted_iota(jnp.int32, (CHUNK, CHUNK), 1)

    def scan_chunk(n, rev):
        causal = (s_idx >= t_idx) if rev else (s_idx <= t_idx)
        for h in range(GROUP_HEADS):
            c = rev * GROUP_HEADS + h
            hs = slice(h * HEAD_DIM, (h + 1) * HEAD_DIM)
            q_t, k, va_t = qt_scr[n, h], k_scr[n, h], vat_scr[n, h]

            arg = jnp.where(causal, acol_scr[n, :, c:c + 1], NEG_INF)
            cm = jnp.max(arg, axis=0, keepdims=True)
            g = (_dot(k, q_t) * jnp.exp(arg - cm)).astype(BF16)
            x1 = _dot(va_t, g)

            a_row = rows_scr[n, 0, c:c + 1, :]
            b_row = rows_scr[n, 1, c:c + 1, :]
            b_end = rows_scr[n, 2, c:c + 1, :]
            a_max = rows_scr[n, 3, c:c + 1, :]
            m_mem = m_scr[c:c + 1, :]
            c_mem = c_scr[c]
            mu = jnp.maximum(cm, m_mem)
            both = x1 * jnp.exp(cm - mu) + _dot(c_mem.astype(BF16), q_t) * jnp.exp(m_mem - mu)
            den = both[HEAD_DIM:HEAD_DIM + 1, :]
            inv = 1.0 / jnp.maximum(jnp.abs(den), jnp.exp(-(b_row + mu)))
            ht_scr[n, hs, :] += both[0:HEAD_DIM, :] * inv

            m_new = b_end + jnp.maximum(m_mem, a_max)
            w_src = jnp.exp(b_end + a_row - m_new)
            decay = jnp.exp(b_end + m_mem - m_new)
            c_scr[c] = decay[:, 0:HEAD_DIM] * c_mem + _dot((va_t.astype(F32) * w_src).astype(BF16), k)
            m_scr[c:c + 1, :] = m_new

    def scan(i, carry):
        scan_chunk(jnp.where(i < CTX_CHUNKS, LAT_CHUNKS + i, i - CTX_CHUNKS), 0)
        scan_chunk(N_CHUNKS - 1 - i, 1)
        return carry

    lax.fori_loop(0, N_CHUNKS, scan, 0)

    def finish(n, carry):
        rows = pl.ds(pl.multiple_of(n * CHUNK, CHUNK), CHUNK)
        gate = jax.nn.sigmoid(og_ref[0, rows, :].astype(F32))
        ys = []
        for h in range(GROUP_HEADS):
            x = ht_scr[n, h * HEAD_DIM:(h + 1) * HEAD_DIM, :]
            mu = jnp.mean(x, axis=0, keepdims=True)
            var = jnp.mean(jnp.square(x - mu), axis=0, keepdims=True)
            ys.append((x - mu) * lax.rsqrt(var + EPS))
        y = jnp.concatenate(ys, axis=0).T * gln_ref[...]
        o_ref[0, rows, :] = (gate * y).astype(BF16)
        return carry

    lax.fori_loop(0, N_CHUNKS, finish, 0)


def _rope_tables():
    t = jnp.arange(SEQ)
    pos = jnp.stack([(t // GRID_W).astype(F32), (t % GRID_W).astype(F32)], axis=1)
    lane = jnp.arange(GROUP_WIDTH)
    m = HEAD_DIM // 4
    inv = ROPE_THETA ** (-(lane % m).astype(F32) / m)
    axis = (lane % HEAD_DIM) // (HEAD_DIM // 2)
    ang = jnp.where(axis[None, :] == 0, pos[:, 0:1], pos[:, 1:2]) * inv[None, :]
    low = ((lane % (2 * m)) < m)[None, :]
    cos, sin = jnp.cos(ang), jnp.sin(ang)
    return cos, jnp.where(low, -sin, 0.0), jnp.where(low, 0.0, sin)


def _mlstm(p, gates, w_conv, b_gate, g_mlstm, rope):
    bsz = p.shape[0]
    i = jnp.arange(CHUNK)
    shifts = jnp.stack([i[:, None] == i[None, :] + 1,
                        (i[:, None] == 0) & (i[None, :] == CHUNK - 1),
                        i[:, None] + 1 == i[None, :],
                        (i[:, None] == CHUNK - 1) & (i[None, :] == 0)]).astype(BF16)
    tri = (i[:, None] >= i[None, :]).astype(F32)
    bgi = jnp.pad(b_gate[jnp.array(GATE_I_COLS)], (0, LANES - 2 * GROUP_HEADS)).reshape(1, LANES)
    bgf = jnp.pad(b_gate[jnp.array(GATE_F_COLS)], (0, LANES - 2 * GROUP_HEADS)).reshape(1, LANES)
    cos, sina, sinb = rope
    n_chains = 2 * GROUP_HEADS
    return pl.pallas_call(
        _mlstm_body,
        grid=(bsz,),
        in_specs=[_group_spec(5), _group_spec(6), _group_spec(7), _group_spec(8),
                  pl.BlockSpec((1, T_ALL, LANES), lambda b: (b, 0, 0)),
                  pl.BlockSpec((1, T_ALL, LANES), lambda b: (b, 0, 1)),
                  _resident((CONV_W, 2 * GROUP_WIDTH)),
                  _resident((1, LANES)), _resident((1, LANES)),
                  _resident((SEQ, GROUP_WIDTH)), _resident((SEQ, GROUP_WIDTH)), _resident((SEQ, GROUP_WIDTH)),
                  _resident((1, GROUP_WIDTH)),
                  _resident((4, CHUNK, CHUNK)),
                  _resident((CHUNK, CHUNK))],
        out_specs=pl.BlockSpec((1, T_ALL, GROUP_WIDTH), lambda b: (b, 0, 0)),
        out_shape=jax.ShapeDtypeStruct((bsz, T_ALL, GROUP_WIDTH), BF16),
        scratch_shapes=[pltpu.VMEM((N_CHUNKS, GROUP_HEADS, HEAD_DIM, CHUNK), BF16),
                        pltpu.VMEM((N_CHUNKS, GROUP_HEADS, CHUNK, HEAD_DIM), BF16),
                        pltpu.VMEM((N_CHUNKS, GROUP_HEADS, 2 * HEAD_DIM, CHUNK), BF16),
                        pltpu.VMEM((N_CHUNKS, CHUNK, LANES), F32),
                        pltpu.VMEM((N_CHUNKS, 4, n_chains, CHUNK), F32),
                        pltpu.VMEM((N_CHUNKS, GROUP_WIDTH, CHUNK), F32),
                        pltpu.VMEM((n_chains, 2 * HEAD_DIM, HEAD_DIM), F32),
                        pltpu.VMEM((n_chains, CHUNK), F32)],
        compiler_params=_params(1),
        name="mlstm",
    )(p, p, p, p, gates, gates, w_conv, bgi, bgf, cos, sina, sinb, g_mlstm.reshape(1, GROUP_WIDTH), shifts, tri)


def _dft_tables(n, scale):
    f = 1 << (n.bit_length() // 2)
    s = jnp.arange(n, dtype=jnp.int32)[None, :]
    ang_a = ((f * jnp.arange(n // f, dtype=jnp.int32)[:, None] * s) % n).astype(F32) * (2.0 * jnp.pi / n)
    ang_b = ((jnp.arange(f, dtype=jnp.int32)[:, None] * s) % n).astype(F32) * (2.0 * jnp.pi / n)
    ca, sa = jnp.cos(ang_a)[:, None, :], jnp.sin(ang_a)[:, None, :]
    cb, sb = jnp.cos(ang_b)[None, :, :], jnp.sin(ang_b)[None, :, :]
    cos = (ca * cb - sa * sb).reshape(n, n)
    sin = (sa * cb + ca * sb).reshape(n, n)
    return (cos * scale).astype(BF16), (-sin * scale).astype(BF16)


def _channel_dft():
    gc = GROUP_WIDTH // FNET_GROUPS
    j = jnp.arange(GROUP_WIDTH, dtype=jnp.int32)
    same = (j[:, None] // gc) == (j[None, :] // gc)
    ang = (((j[:, None] % gc) * (j[None, :] % gc)) % gc).astype(F32) * (2.0 * jnp.pi / gc)
    c = jnp.where(same, jnp.cos(ang), 0.0)
    s = jnp.where(same, jnp.sin(ang), 0.0)
    return jnp.concatenate([c, s], axis=1).astype(BF16)


def _fnet_body(zc_ref, zs_ref, cl_ref, sl_ref, cc_ref, sc_ref, w_ref, o_ref):
    y = _dot(cl_ref[...], zc_ref[0:SEQ, :]) + _dot(sl_ref[...], zs_ref[0:SEQ, :])
    o_ref[0:SEQ, :] = _dot(y.astype(BF16), w_ref[...]).astype(BF16)
    y = _dot(cc_ref[...], zc_ref[SEQ:, :]) + _dot(sc_ref[...], zs_ref[SEQ:, :])
    o_ref[SEQ:, :] = _dot(y.astype(BF16), w_ref[...]).astype(BF16)


def _fnet(zc, zs, tabs, w_fnet):
    bsz = zc.shape[1] // GROUP_WIDTH
    cl, sl, cc, sc = tabs
    col = pl.BlockSpec((T_ALL, GROUP_WIDTH), lambda b: (0, b))
    return pl.pallas_call(
        _fnet_body,
        grid=(bsz,),
        in_specs=[col, col, _resident((SEQ, SEQ)), _resident((SEQ, SEQ)),
                  _resident((CTX_LEN, CTX_LEN)), _resident((CTX_LEN, CTX_LEN)),
                  _resident((GROUP_WIDTH, GROUP_WIDTH))],
        out_specs=col,
        out_shape=jax.ShapeDtypeStruct((T_ALL, bsz * GROUP_WIDTH), BF16),
        compiler_params=_params(1),
        name="fnet",
    )(zc, zs, cl, sl, cc, sc, w_fnet.astype(BF16))


def _post_body(a_ref, b_ref, c_ref, d_ref, x_ref, ml_ref, mc_ref, g_ref, wo_ref, w1_ref, w2_ref, gf_ref,
               o_ref, h_scr, acc_scr, *, final):
    gw = GROUP_WIDTH
    mod = _mod_rows(ml_ref, mc_ref, x_ref.shape[1])
    y = (_dot(a_ref[0], wo_ref[0:gw, :]) + _dot(b_ref[0], wo_ref[gw:2 * gw, :])
         + _dot(c_ref[0], wo_ref[2 * gw:3 * gw, :]) + _dot(d_ref[...], wo_ref[3 * gw:, :]))
    x1 = x_ref[0] + mod(2) * y
    h = _rms(x1, g_ref[...]) * (1.0 + mod(4)) + mod(3)
    h_scr[...] = h.astype(BF16)
    step = 512
    for j in range(0, D_FF, step):
        a = jnp.maximum(_dot(h_scr[...], w1_ref[:, j:j + step]), 0.0)
        part = _dot((a * a).astype(BF16), w2_ref[j:j + step, :])
        if j == 0:
            acc_scr[...] = part
        else:
            acc_scr[...] += part
    x2 = x1 + mod(5) * acc_scr[...]
    if final:
        x2 = _rms(x2, gf_ref[...])
    o_ref[0] = x2


def _post(a, b_, c_, d, xx, mod_l, g_ffn, w_out, w_ff1, w_ff2, g_final, final):
    bsz = xx.shape[0]
    tm, out_len = (LATENT_TILE, SEQ) if final else (TOKEN_TILE, T_ALL)
    tok = pl.BlockSpec((1, tm, D_MODEL), lambda b, t: (b, t, 0))
    grp = pl.BlockSpec((1, tm, GROUP_WIDTH), lambda b, t: (b, t, 0))
    return pl.pallas_call(
        functools.partial(_post_body, final=final),
        grid=(bsz, out_len // tm),
        in_specs=[grp, grp, grp, pl.BlockSpec((tm, GROUP_WIDTH), lambda b, t: (t, b)), tok] + _MOD_SPECS
                 + [_resident((1, D_MODEL)),
                    _resident((D_MODEL, D_MODEL)),
                    _resident((D_MODEL, D_FF)),
                    _resident((D_FF, D_MODEL)),
                    _resident((1, D_MODEL))],
        out_specs=tok,
        out_shape=jax.ShapeDtypeStruct((bsz, out_len, D_MODEL), F32),
        scratch_shapes=[pltpu.VMEM((tm, D_MODEL), BF16), pltpu.VMEM((tm, D_MODEL), F32)],
        compiler_params=_params(2),
        name="out_proj_mlp",
    )(a, b_, c_, d, xx, mod_l, mod_l, g_ffn.reshape(1, D_MODEL), w_out.astype(BF16), w_ff1.astype(BF16),
      w_ff2.astype(BF16), g_final.reshape(1, D_MODEL))


def kernel(x, c, ctx, c_ctx, w_ada, b_ada, g_norm_mix, g_norm_ffn, w_in, b_gate, w_conv_qk, rpb, w_spatial,
           b_spatial, g_gmlp, g_mlstm, w_fnet, w_out, w_ff1, w_ff2, g_final):
    bsz = x.shape[0]
    assert bsz <= CTX_MOD_ROW and x.shape[1:] == (SEQ, D_MODEL) and ctx.shape[1:] == (CTX_LEN, D_MODEL)
    depth = w_ada.shape[0]
    xx = jnp.concatenate([x, ctx], axis=1)
    cc = jnp.zeros((MOD_ROWS, D_MODEL), F32).at[:bsz].set(c).at[CTX_MOD_ROW].set(c_ctx)
    mod = _ada_mod(cc, w_ada, b_ada).reshape(depth, MOD_ROWS, N_MOD, D_MODEL)

    rope = _rope_tables()
    dft_c = _channel_dft()
    tabs = (_dft_tables(SEQ, (SEQ * GROUP_WIDTH // FNET_GROUPS) ** -0.5)
            + _dft_tables(CTX_LEN, (CTX_LEN * GROUP_WIDTH // FNET_GROUPS) ** -0.5))

    for l in range(depth):
        w_cat = _projection_weights(w_in[l])
        p, gates, zc, zs = _in_proj(xx, mod[l], g_norm_mix[l], w_cat, dft_c)
        a = _attention(p, _attention_bias(rpb[l]))
        b_ = _gmlp(p, w_spatial[l], b_spatial[l], g_gmlp[l])
        c_ = _mlstm(p, gates, w_conv_qk[l], b_gate[l], g_mlstm[l], rope)
        d = _fnet(zc, zs, tabs, w_fnet[l])
        xx = _post(a, b_, c_, d, xx, mod[l], g_norm_ffn[l], w_out[l], w_ff1[l], w_ff2[l], g_final,
                   final=(l == depth - 1))
    return xx
```

```python
import functools

import jax
import jax.numpy as jnp
from jax import lax
from jax.experimental import pallas as pl
from jax.experimental.pallas import tpu as pltpu

D_MODEL = 1024
SEQ = 2048
DEPTH = 2
GRID_W = 64
GRID_ROWS = SEQ // GRID_W
CTX_LEN = 256
T_ALL = CTX_LEN + SEQ
HEAD_DIM = 64
GROUP_WIDTH = 256
GROUP_HEADS = 4
WIN_ROWS = 8
WIN_COLS = 16
CHUNK = 128
N_CHUNKS = T_ALL // CHUNK
CTX_CHUNKS = CTX_LEN // CHUNK
LAT_CHUNKS = SEQ // CHUNK
CONV_W = 3
FNET_GROUPS = 4
ROPE_THETA = 10000.0
D_FF = 4 * D_MODEL
N_MOD = 6
EPS = 1e-6
NEG_INF = -1e30
N_GATES = 4 * GROUP_HEADS
OFF_G = 10 * GROUP_WIDTH
D_IN = OFF_G + N_GATES
MOD_ROWS = 24
CTX_MOD_ROW = 16
LANES = 128
BF16_ROWS = 16
GATE_I_COLS = (0, 2, 8, 10, 1, 3, 9, 11)
GATE_F_COLS = (4, 6, 12, 14, 5, 7, 13, 15)
TOKEN_TILE = 768
LATENT_TILE = 512
P_WIDTH = 9 * GROUP_WIDTH
VMEM_LIMIT = 56 * 1024 * 1024

F32 = jnp.float32
BF16 = jnp.bfloat16


def _dot(a, b):
    return jnp.dot(a, b, preferred_element_type=F32)


def _dot_nt(a, b):
    return lax.dot_general(a, b, (((1,), (1,)), ((), ())), preferred_element_type=F32)


def _dot_f32(a, b):
    return jnp.dot(a, b, preferred_element_type=F32, precision=lax.Precision.HIGHEST)


def _resident(shape):
    nd = len(shape)
    return pl.BlockSpec(shape, lambda *_: (0,) * nd, pipeline_mode=pl.Buffered(1))


def _params(n_axes):
    return pltpu.CompilerParams(dimension_semantics=("arbitrary",) * n_axes,
                                vmem_limit_bytes=VMEM_LIMIT)


def _mod_body(c_ref, w_ref, b_ref, o_ref):
    s = c_ref[...]
    s = s * jax.nn.sigmoid(s)
    o_ref[0] = _dot(s.astype(BF16), w_ref[0].astype(BF16)) + b_ref[0]


def _ada_mod(cc, w_ada, b_ada):
    depth, d, n = w_ada.shape
    tn = 1536
    return pl.pallas_call(
        _mod_body,
        grid=(depth, n // tn),
        in_specs=[pl.BlockSpec((MOD_ROWS, d), lambda l, j: (0, 0)),
                  pl.BlockSpec((1, d, tn), lambda l, j: (l, 0, j)),
                  pl.BlockSpec((1, 1, tn), lambda l, j: (l, 0, j))],
        out_specs=pl.BlockSpec((1, MOD_ROWS, tn), lambda l, j: (l, 0, j)),
        out_shape=jax.ShapeDtypeStruct((depth, MOD_ROWS, n), F32),
        compiler_params=_params(2),
        name="ada_mod",
    )(cc, w_ada, b_ada.reshape(depth, 1, n))


_MOD_SPECS = [pl.BlockSpec((1, N_MOD, D_MODEL), lambda b, t: (b, 0, 0)),
              pl.BlockSpec((1, N_MOD, D_MODEL), lambda b, t: (CTX_MOD_ROW, 0, 0))]


def _mod_rows(mod_lat_ref, mod_ctx_ref, tile):
    tok = pl.program_id(1) * tile + lax.broadcasted_iota(jnp.int32, (tile, 1), 0)
    is_ctx = tok >= SEQ
    return lambda k: jnp.where(is_ctx, mod_ctx_ref[0, k:k + 1, :], mod_lat_ref[0, k:k + 1, :])


def _rms(x, g):
    return x * lax.rsqrt(jnp.mean(x * x, axis=-1, keepdims=True) + EPS) * g


def _inproj_body(x_ref, ml_ref, mc_ref, g_ref, w_ref, dft_ref, p_ref, gate_ref, zc_ref, zs_ref, h_scr):
    mod = _mod_rows(ml_ref, mc_ref, x_ref.shape[1])
    h = _rms(x_ref[0], g_ref[...]) * (1.0 + mod(1)) + mod(0)
    h_scr[...] = h.astype(BF16)
    f = _dot(h_scr[...], w_ref[:, P_WIDTH:OFF_G]).astype(BF16)
    z = _dot(f, dft_ref[...])
    zc_ref[...] = z[:, :GROUP_WIDTH].astype(BF16)
    zs_ref[...] = z[:, GROUP_WIDTH:].astype(BF16)
    step = 2 * GROUP_WIDTH
    for j in range(0, P_WIDTH - GROUP_WIDTH, step):
        p_ref[0, :, j:j + step] = _dot(h_scr[...], w_ref[:, j:j + step]).astype(BF16)
    j = P_WIDTH - GROUP_WIDTH
    p_ref[0, :, j:] = _dot(h_scr[...], w_ref[:, j:P_WIDTH]).astype(BF16)
    gate_ref[0] = _dot(h_scr[...], w_ref[:, OFF_G:])


def _projection_weights(w_in):
    pad = jnp.zeros((w_in.shape[0], LANES - N_GATES), w_in.dtype)
    w_g = w_in[:, OFF_G:]
    return jnp.concatenate([w_in[:, :OFF_G], w_g[:, jnp.array(GATE_I_COLS + GATE_F_COLS)], pad],
                           axis=1).astype(BF16)


def _in_proj(xx, mod_l, g, w_cat, dft_c):
    bsz = xx.shape[0]
    tm = TOKEN_TILE
    wn = w_cat.shape[1]
    return pl.pallas_call(
        _inproj_body,
        grid=(bsz, T_ALL // tm),
        in_specs=[pl.BlockSpec((1, tm, D_MODEL), lambda b, t: (b, t, 0))] + _MOD_SPECS
                 + [_resident((1, D_MODEL)),
                    _resident((D_MODEL, wn)),
                    _resident((GROUP_WIDTH, 2 * GROUP_WIDTH))],
        out_specs=[pl.BlockSpec((1, tm, P_WIDTH), lambda b, t: (b, t, 0)),
                   pl.BlockSpec((1, tm, LANES), lambda b, t: (b, t, 0)),
                   pl.BlockSpec((tm, GROUP_WIDTH), lambda b, t: (t, b)),
                   pl.BlockSpec((tm, GROUP_WIDTH), lambda b, t: (t, b))],
        out_shape=[jax.ShapeDtypeStruct((bsz, T_ALL, P_WIDTH), BF16),
                   jax.ShapeDtypeStruct((bsz, T_ALL, LANES), F32),
                   jax.ShapeDtypeStruct((T_ALL, bsz * GROUP_WIDTH), BF16),
                   jax.ShapeDtypeStruct((T_ALL, bsz * GROUP_WIDTH), BF16)],
        scratch_shapes=[pltpu.VMEM((tm, D_MODEL), BF16)],
        compiler_params=_params(2),
        name="in_proj",
    )(xx, mod_l, mod_l, g.reshape(1, D_MODEL), w_cat, dft_c)


def _group_spec(col_block):
    return pl.BlockSpec((1, T_ALL, GROUP_WIDTH), lambda b: (b, 0, col_block))


def _softmax_pv(parts):
    m = functools.reduce(jnp.maximum, [jnp.max(s, axis=-1, keepdims=True) for s, _ in parts])
    es = [jnp.exp(s - m) for s, _ in parts]
    den = functools.reduce(jnp.add, [jnp.sum(e, axis=-1, keepdims=True) for e in es])
    num = functools.reduce(jnp.add, [_dot(e.astype(BF16), v) for e, (_, v) in zip(es, parts)])
    return num / den


def _attn_body(q_ref, k_ref, v_ref, bias_ref, o_ref):
    scale = HEAD_DIM ** -0.5
    pairs = [slice(i * LANES, (i + 1) * LANES) for i in range(GROUP_WIDTH // LANES)]

    def stack_pair(q2):
        low = lax.broadcasted_iota(jnp.int32, q2.shape, 1) < HEAD_DIM
        zero = jnp.zeros_like(q2)
        return low, jnp.concatenate([jnp.where(low, q2, zero), jnp.where(low, zero, q2)], axis=0)

    def unstack_pair(low, o):
        n = o.shape[0] // 2
        return jnp.where(low, o[:n], o[n:]).astype(BF16)

    ctx = slice(SEQ, T_ALL)
    for ps in pairs:
        kc, vc = k_ref[0, ctx, ps], v_ref[0, ctx, ps]
        low, qm = stack_pair(q_ref[0, ctx, ps] * scale)
        o_ref[0, ctx, ps] = unstack_pair(low, _softmax_pv([(_dot_nt(qm, kc), vc)]))

    def row(r, carry):
        rs = jnp.clip(r - WIN_ROWS // 2, 0, GRID_ROWS - WIN_ROWS)
        q0 = pl.multiple_of(r * GRID_W, GRID_W)
        k0 = pl.multiple_of(rs * GRID_W, GRID_W)
        for i, ps in enumerate(pairs):
            kc, vc = k_ref[0, ctx, ps], v_ref[0, ctx, ps]
            kl = k_ref[0, pl.ds(k0, WIN_ROWS * GRID_W), ps]
            vl = v_ref[0, pl.ds(k0, WIN_ROWS * GRID_W), ps]
            low, qm = stack_pair(q_ref[0, pl.ds(q0, GRID_W), ps] * scale)
            o = _softmax_pv([(_dot_nt(qm, kl) + bias_ref[i, r - rs], vl), (_dot_nt(qm, kc), vc)])
            o_ref[0, pl.ds(q0, GRID_W), ps] = unstack_pair(low, o)
        return carry

    lax.fori_loop(0, GRID_ROWS, row, 0, unroll=4)


def _attention(p, bias_tab):
    bsz = p.shape[0]
    return pl.pallas_call(
        _attn_body,
        grid=(bsz,),
        in_specs=[_group_spec(0), _group_spec(1), _group_spec(2), _resident(bias_tab.shape)],
        out_specs=pl.BlockSpec((1, T_ALL, GROUP_WIDTH), lambda b: (b, 0, 0)),
        out_shape=jax.ShapeDtypeStruct((bsz, T_ALL, GROUP_WIDTH), BF16),
        compiler_params=_params(1),
        name="nbr_attention",
    )(p, p, p, bias_tab)


def _attention_bias(rpb):
    c = jnp.arange(GRID_W)
    qstart = jnp.clip(c - WIN_COLS // 2, 0, GRID_W - WIN_COLS)
    in_win = (c[None, :] >= qstart[:, None]) & (c[None, :] < qstart[:, None] + WIN_COLS)
    col = jnp.clip(c[None, :] - c[:, None] + WIN_COLS - 1, 0, 2 * WIN_COLS - 2)
    pick = (col[:, :, None] == jnp.arange(2 * WIN_COLS - 1)).astype(F32)
    by_drow = jnp.einsum('hde,qke->hdqk', rpb.astype(F32), pick, precision=lax.Precision.HIGHEST)
    by_drow = jnp.where(in_win, by_drow, NEG_INF)
    tab = jnp.stack([by_drow[:, WIN_ROWS - 1 - v:2 * WIN_ROWS - 1 - v] for v in range(WIN_ROWS)], axis=1)
    tab = tab.transpose(0, 1, 3, 2, 4).reshape(GROUP_HEADS // 2, 2, WIN_ROWS, GRID_W, WIN_ROWS * GRID_W)
    return tab.transpose(0, 2, 1, 3, 4).reshape(GROUP_HEADS // 2, WIN_ROWS, 2 * GRID_W, WIN_ROWS * GRID_W)


def _gmlp_body(u_ref, z_ref, ws_ref, bs_ref, g_ref, o_ref):
    def chunk(n, carry):
        rows = pl.ds(pl.multiple_of(n * CHUNK, CHUNK), CHUNK)
        z = _rms(jax.nn.gelu(z_ref[0, rows, :].astype(F32)), g_ref[...]).astype(BF16)
        u = jax.nn.gelu(u_ref[0, rows, :].astype(F32))
        for h in range(GROUP_HEADS):
            hs = slice(h * HEAD_DIM, (h + 1) * HEAD_DIM)
            mixed = _dot(ws_ref[h], z[:, hs]) + bs_ref[:, hs]
            o_ref[0, rows, hs] = (u[:, hs] * mixed).astype(BF16)
        return carry

    lax.fori_loop(0, N_CHUNKS, chunk, 0, unroll=3)


def _gmlp(p, w_spatial, b_spatial, g_gmlp):
    bsz = p.shape[0]
    bias = jnp.repeat(b_spatial.T, HEAD_DIM, axis=1)
    return pl.pallas_call(
        _gmlp_body,
        grid=(bsz,),
        in_specs=[_group_spec(3), _group_spec(4),
                  _resident((GROUP_HEADS, CHUNK, CHUNK)),
                  _resident((CHUNK, GROUP_WIDTH)),
                  _resident((1, GROUP_WIDTH))],
        out_specs=pl.BlockSpec((1, T_ALL, GROUP_WIDTH), lambda b: (b, 0, 0)),
        out_shape=jax.ShapeDtypeStruct((bsz, T_ALL, GROUP_WIDTH), BF16),
        compiler_params=_params(1),
        name="gmlp",
    )(p, p, w_spatial.astype(BF16), bias, g_gmlp.reshape(1, GROUP_WIDTH))


def _mlstm_body(q_ref, k_ref, v_ref, og_ref, gate_ref, wc_ref, bg_ref, cos_ref, sina_ref,
                sinb_ref, gln_ref, tri_ref, o_ref,
                qtz_scr, k_scr, kbd_scr, vat_scr, acol_scr, rows_scr, ht_scr, c_scr, m_scr):
    half = HEAD_DIM // 4
    n_chains = 2 * GROUP_HEADS

    def conv_act(ref, n, w0, w1, w2, rope, post):
        rows = pl.ds(pl.multiple_of(n * CHUNK, CHUNK), CHUNK)
        cur = ref[0, rows, :].astype(F32)
        first = (n == 0) | (n == LAT_CHUNKS)
        last = (n == LAT_CHUNKS - 1) | (n == N_CHUNKS - 1)
        before = pl.ds(pl.multiple_of(jnp.maximum(n * CHUNK - BF16_ROWS, 0), BF16_ROWS), BF16_ROWS)
        after = pl.ds(pl.multiple_of(jnp.minimum((n + 1) * CHUNK, T_ALL - BF16_ROWS), BF16_ROWS), BF16_ROWS)
        tail = jnp.where(first, 0.0, 1.0) * ref[0, before, :][BF16_ROWS - 1:BF16_ROWS, :].astype(F32)
        head = jnp.where(last, 0.0, 1.0) * ref[0, after, :][0:1, :].astype(F32)
        sub = lax.broadcasted_iota(jnp.int32, (CHUNK, 1), 0)
        x_prev = jnp.where(sub == 0, tail, pltpu.roll(cur, 1, 0))
        x_next = jnp.where(sub == CHUNK - 1, head, pltpu.roll(cur, CHUNK - 1, 0))
        y = w0 * x_prev + w1 * cur + w2 * x_next
        y = y * jax.nn.sigmoid(y)
        if rope:
            y = (y * cos_ref[rows, :] + pltpu.roll(y, GROUP_WIDTH - half, 1) * sina_ref[rows, :]
                 + pltpu.roll(y, half, 1) * sinb_ref[rows, :])
        return y * post

    def prep(n, rope):
        rows = pl.ds(pl.multiple_of(n * CHUNK, CHUNK), CHUNK)
        gw = GROUP_WIDTH
        q = conv_act(q_ref, n, wc_ref[0:1, :gw], wc_ref[1:2, :gw], wc_ref[2:3, :gw], rope, 1.0)
        k = conv_act(k_ref, n, wc_ref[0:1, gw:], wc_ref[1:2, gw:], wc_ref[2:3, gw:], rope, HEAD_DIM ** -0.5)
        q_t = q.T.astype(BF16)
        v_t = v_ref[0, rows, :].astype(F32).T.astype(BF16)
        k = k.astype(BF16)
        k_scr[n] = k
        sub = lax.broadcasted_iota(jnp.int32, (HEAD_DIM, CHUNK), 0)
        ones_row = jnp.where(sub == 0, 1.0, 0.0).astype(BF16)
        zeros = jnp.zeros((HEAD_DIM, CHUNK), BF16)
        low = lax.broadcasted_iota(jnp.int32, (CHUNK, LANES), 1) < HEAD_DIM
        for pair in range(GROUP_HEADS // 2):
            h0 = slice(2 * pair * HEAD_DIM, (2 * pair + 1) * HEAD_DIM)
            h1 = slice((2 * pair + 1) * HEAD_DIM, (2 * pair + 2) * HEAD_DIM)
            qtz_scr[n, pair] = jnp.concatenate([jnp.concatenate([q_t[h0, :], zeros], axis=1),
                                                jnp.concatenate([zeros, q_t[h1, :]], axis=1)], axis=0)
            k_pair = k[:, pair * LANES:(pair + 1) * LANES]
            kbd_scr[n, pair] = jnp.concatenate([jnp.where(low, k_pair, jnp.zeros_like(k_pair)),
                                                jnp.where(low, jnp.zeros_like(k_pair), k_pair)], axis=0)
            vat_scr[n, pair] = jnp.concatenate([jnp.concatenate([v_t[h0, :], ones_row], axis=0),
                                                jnp.concatenate([v_t[h1, :], ones_row], axis=0)], axis=1)
        ht_scr[n] = jnp.zeros((GROUP_WIDTH, CHUNK), F32)

        g_i = gate_ref[0, rows, :] + bg_ref[...]
        lf = pltpu.roll(jax.nn.log_sigmoid(g_i), LANES - n_chains, 1)
        pre = _dot_f32(tri_ref[...], lf)
        suf = pre[CHUNK - 1:CHUNK, :] - pre + lf
        lane = lax.broadcasted_iota(jnp.int32, (CHUNK, LANES), 1)
        b = jnp.where(lane % 4 < 2, pre, suf)
        a = g_i - b
        acol_scr[n] = a
        a_t = a.T[0:n_chains, :]
        b_t = b.T[0:n_chains, :]
        chain = lax.broadcasted_iota(jnp.int32, (n_chains, CHUNK), 0)
        b_end = jnp.where(chain % 4 < 2, b_t[:, CHUNK - 1:CHUNK], b_t[:, 0:1])
        b_end = jnp.broadcast_to(b_end, (n_chains, CHUNK))
        a_max = jnp.broadcast_to(jnp.max(a_t, axis=1, keepdims=True), (n_chains, CHUNK))
        for kind, rows8 in enumerate((a_t, b_t, b_end, a_max)):
            rows_scr[n, kind] = jnp.concatenate([rows8[0:4, :], rows8[4:8, :]], axis=1)

    def prep_ctx(n, carry):
        prep(n, False)
        return carry

    def prep_lat(n, carry):
        prep(n, True)
        return carry

    lax.fori_loop(0, LAT_CHUNKS, prep_lat, 0, unroll=2)
    lax.fori_loop(LAT_CHUNKS, N_CHUNKS, prep_ctx, 0)
    c_scr[...] = jnp.zeros(c_scr.shape, F32)
    m_scr[...] = jnp.zeros(m_scr.shape, F32)

    s_idx = lax.broadcasted_iota(jnp.int32, (CHUNK, CHUNK), 0)
    t_idx = lax.broadcasted_iota(jnp.int32, (CHUNK, CHUNK), 1)

    def scan_chunk(n, rev):
        causal = (s_idx >= t_idx) if rev else (s_idx <= t_idx)
        zeros = jnp.zeros((CHUNK, CHUNK), BF16)
        low = lax.broadcasted_iota(jnp.int32, (1, LANES), 1) < HEAD_DIM
        for pair in range(GROUP_HEADS // 2):
            dp = 2 * rev + pair
            qtz, va_t = qtz_scr[n, pair], vat_scr[n, pair]

            arg = jnp.concatenate([jnp.where(causal, acol_scr[n, :, dp:dp + 1], NEG_INF),
                                   jnp.where(causal, acol_scr[n, :, 4 + dp:5 + dp], NEG_INF)], axis=1)
            cm = jnp.max(arg, axis=0, keepdims=True)
            s = _dot(k_scr[n, :, pair * LANES:(pair + 1) * LANES], qtz)
            g = (s * jnp.exp(arg - cm)).astype(BF16)
            g_bd = jnp.concatenate([jnp.concatenate([g[:, :CHUNK], zeros], axis=1),
                                    jnp.concatenate([zeros, g[:, CHUNK:]], axis=1)], axis=0)
            x1 = _dot(va_t, g_bd)

            a_row = rows_scr[n, 0, dp:dp + 1, :]
            b_row = rows_scr[n, 1, dp:dp + 1, :]
            b_end = rows_scr[n, 2, dp:dp + 1, :]
            a_max = rows_scr[n, 3, dp:dp + 1, :]
            m_mem = m_scr[dp:dp + 1, :]
            c_mem = c_scr[dp]
            mu = jnp.maximum(cm, m_mem)
            both = x1 * jnp.exp(cm - mu) + _dot(c_mem.astype(BF16), qtz) * jnp.exp(m_mem - mu)
            den = both[HEAD_DIM:HEAD_DIM + 1, :]
            inv = 1.0 / jnp.maximum(jnp.abs(den), jnp.exp(-(b_row + mu)))
            h_t = both[0:HEAD_DIM, :] * inv
            ht_scr[n, pair * LANES:(pair + 1) * LANES, :] += jnp.concatenate([h_t[:, :CHUNK], h_t[:, CHUNK:]],
                                                                             axis=0)

            m_new = b_end + jnp.maximum(m_mem, a_max)
            w_src = jnp.exp(b_end + a_row - m_new)
            decay = jnp.exp(b_end + m_mem - m_new)
            decay = jnp.where(low, decay[:, :CHUNK], decay[:, CHUNK:])
            c_scr[dp] = decay * c_mem + _dot((va_t.astype(F32) * w_src).astype(BF16), kbd_scr[n, pair])
            m_scr[dp:dp + 1, :] = m_new

    def scan(i, carry):
        scan_chunk(jnp.where(i < CTX_CHUNKS, LAT_CHUNKS + i, i - CTX_CHUNKS), 0)
        scan_chunk(N_CHUNKS - 1 - i, 1)
        return carry

    lax.fori_loop(0, N_CHUNKS, scan, 0, unroll=3)

    def finish(n, carry):
        rows = pl.ds(pl.multiple_of(n * CHUNK, CHUNK), CHUNK)
        gate = jax.nn.sigmoid(og_ref[0, rows, :].astype(F32))
        ys = []
        for h in range(GROUP_HEADS):
            x = ht_scr[n, h * HEAD_DIM:(h + 1) * HEAD_DIM, :]
            mu = jnp.mean(x, axis=0, keepdims=True)
            var = jnp.mean(jnp.square(x - mu), axis=0, keepdims=True)
            ys.append((x - mu) * lax.rsqrt(var + EPS))
        y = jnp.concatenate(ys, axis=0).T * gln_ref[...]
        o_ref[0, rows, :] = (gate * y).astype(BF16)
        return carry

    lax.fori_loop(0, N_CHUNKS, finish, 0)


def _rope_tables():
    t = jnp.arange(SEQ)
    pos = jnp.stack([(t // GRID_W).astype(F32), (t % GRID_W).astype(F32)], axis=1)
    lane = jnp.arange(GROUP_WIDTH)
    m = HEAD_DIM // 4
    inv = ROPE_THETA ** (-(lane % m).astype(F32) / m)
    axis = (lane % HEAD_DIM) // (HEAD_DIM // 2)
    ang = jnp.where(axis[None, :] == 0, pos[:, 0:1], pos[:, 1:2]) * inv[None, :]
    low = ((lane % (2 * m)) < m)[None, :]
    cos, sin = jnp.cos(ang), jnp.sin(ang)
    return cos, jnp.where(low, -sin, 0.0), jnp.where(low, 0.0, sin)


def _mlstm(p, gates, w_conv, b_gate, g_mlstm, rope):
    bsz = p.shape[0]
    i = jnp.arange(CHUNK)
    tri = (i[:, None] >= i[None, :]).astype(F32)
    bg = jnp.pad(b_gate[jnp.array(GATE_I_COLS + GATE_F_COLS)], (0, LANES - N_GATES)).reshape(1, LANES)
    cos, sina, sinb = rope
    n_pairs = GROUP_HEADS // 2
    return pl.pallas_call(
        _mlstm_body,
        grid=(bsz,),
        in_specs=[_group_spec(5), _group_spec(6), _group_spec(7), _group_spec(8),
                  pl.BlockSpec((1, T_ALL, LANES), lambda b: (b, 0, 0)),
                  _resident((CONV_W, 2 * GROUP_WIDTH)),
                  _resident((1, LANES)),
                  _resident((SEQ, GROUP_WIDTH)), _resident((SEQ, GROUP_WIDTH)), _resident((SEQ, GROUP_WIDTH)),
                  _resident((1, GROUP_WIDTH)),
                  _resident((CHUNK, CHUNK))],
        out_specs=pl.BlockSpec((1, T_ALL, GROUP_WIDTH), lambda b: (b, 0, 0)),
        out_shape=jax.ShapeDtypeStruct((bsz, T_ALL, GROUP_WIDTH), BF16),
        scratch_shapes=[pltpu.VMEM((N_CHUNKS, n_pairs, LANES, 2 * CHUNK), BF16),
                        pltpu.VMEM((N_CHUNKS, CHUNK, GROUP_WIDTH), BF16),
                        pltpu.VMEM((N_CHUNKS, n_pairs, 2 * CHUNK, LANES), BF16),
                        pltpu.VMEM((N_CHUNKS, n_pairs, LANES, 2 * CHUNK), BF16),
                        pltpu.VMEM((N_CHUNKS, CHUNK, LANES), F32),
                        pltpu.VMEM((N_CHUNKS, 4, 2 * n_pairs, 2 * CHUNK), F32),
                        pltpu.VMEM((N_CHUNKS, GROUP_WIDTH, CHUNK), F32),
                        pltpu.VMEM((2 * n_pairs, LANES, LANES), F32),
                        pltpu.VMEM((2 * n_pairs, 2 * CHUNK), F32)],
        compiler_params=_params(1),
        name="mlstm",
    )(p, p, p, p, gates, w_conv, bg, cos, sina, sinb, g_mlstm.reshape(1, GROUP_WIDTH), tri)


def _dft_tables(n, scale):
    f = 1 << (n.bit_length() // 2)
    s = jnp.arange(n, dtype=jnp.int32)[None, :]
    ang_a = ((f * jnp.arange(n // f, dtype=jnp.int32)[:, None] * s) % n).astype(F32) * (2.0 * jnp.pi / n)
    ang_b = ((jnp.arange(f, dtype=jnp.int32)[:, None] * s) % n).astype(F32) * (2.0 * jnp.pi / n)
    ca, sa = jnp.cos(ang_a)[:, None, :], jnp.sin(ang_a)[:, None, :]
    cb, sb = jnp.cos(ang_b)[None, :, :], jnp.sin(ang_b)[None, :, :]
    cos = (ca * cb - sa * sb).reshape(n, n)
    sin = (sa * cb + ca * sb).reshape(n, n)
    return (cos * scale).astype(BF16), (-sin * scale).astype(BF16)


def _channel_dft():
    gc = GROUP_WIDTH // FNET_GROUPS
    j = jnp.arange(GROUP_WIDTH, dtype=jnp.int32)
    same = (j[:, None] // gc) == (j[None, :] // gc)
    ang = (((j[:, None] % gc) * (j[None, :] % gc)) % gc).astype(F32) * (2.0 * jnp.pi / gc)
    c = jnp.where(same, jnp.cos(ang), 0.0)
    s = jnp.where(same, jnp.sin(ang), 0.0)
    return jnp.concatenate([c, s], axis=1).astype(BF16)


def _fnet_body(zc_ref, zs_ref, cl_ref, sl_ref, cc_ref, sc_ref, w_ref, o_ref):
    y = _dot(cl_ref[...], zc_ref[0:SEQ, :]) + _dot(sl_ref[...], zs_ref[0:SEQ, :])
    o_ref[0:SEQ, :] = _dot(y.astype(BF16), w_ref[...]).astype(BF16)
    y = _dot(cc_ref[...], zc_ref[SEQ:, :]) + _dot(sc_ref[...], zs_ref[SEQ:, :])
    o_ref[SEQ:, :] = _dot(y.astype(BF16), w_ref[...]).astype(BF16)


def _fnet(zc, zs, tabs, w_fnet):
    bsz = zc.shape[1] // GROUP_WIDTH
    cl, sl, cc, sc = tabs
    col = pl.BlockSpec((T_ALL, GROUP_WIDTH), lambda b: (0, b))
    return pl.pallas_call(
        _fnet_body,
        grid=(bsz,),
        in_specs=[col, col, _resident((SEQ, SEQ)), _resident((SEQ, SEQ)),
                  _resident((CTX_LEN, CTX_LEN)), _resident((CTX_LEN, CTX_LEN)),
                  _resident((GROUP_WIDTH, GROUP_WIDTH))],
        out_specs=col,
        out_shape=jax.ShapeDtypeStruct((T_ALL, bsz * GROUP_WIDTH), BF16),
        compiler_params=_params(1),
        name="fnet",
    )(zc, zs, cl, sl, cc, sc, w_fnet.astype(BF16))


def _post_body(a_ref, b_ref, c_ref, d_ref, x_ref, ml_ref, mc_ref, g_ref, wo_ref, w1_ref, w2_ref, gf_ref,
               o_ref, h_scr, acc_scr, *, final):
    gw = GROUP_WIDTH
    mod = _mod_rows(ml_ref, mc_ref, x_ref.shape[1])
    y = (_dot(a_ref[0], wo_ref[0:gw, :]) + _dot(b_ref[0], wo_ref[gw:2 * gw, :])
         + _dot(c_ref[0], wo_ref[2 * gw:3 * gw, :]) + _dot(d_ref[...], wo_ref[3 * gw:, :]))
    x1 = x_ref[0] + mod(2) * y
    h = _rms(x1, g_ref[...]) * (1.0 + mod(4)) + mod(3)
    h_scr[...] = h.astype(BF16)
    step = 512
    for j in range(0, D_FF, step):
        a = jnp.maximum(_dot(h_scr[...], w1_ref[:, j:j + step]), 0.0)
        part = _dot((a * a).astype(BF16), w2_ref[j:j + step, :])
        if j == 0:
            acc_scr[...] = part
        else:
            acc_scr[...] += part
    x2 = x1 + mod(5) * acc_scr[...]
    if final:
        x2 = _rms(x2, gf_ref[...])
    o_ref[0] = x2


def _post(a, b_, c_, d, xx, mod_l, g_ffn, w_out, w_ff1, w_ff2, g_final, final):
    bsz = xx.shape[0]
    tm, out_len = (LATENT_TILE, SEQ) if final else (TOKEN_TILE, T_ALL)
    tok = pl.BlockSpec((1, tm, D_MODEL), lambda b, t: (b, t, 0))
    grp = pl.BlockSpec((1, tm, GROUP_WIDTH), lambda b, t: (b, t, 0))
    return pl.pallas_call(
        functools.partial(_post_body, final=final),
        grid=(bsz, out_len // tm),
        in_specs=[grp, grp, grp, pl.BlockSpec((tm, GROUP_WIDTH), lambda b, t: (t, b)), tok] + _MOD_SPECS
                 + [_resident((1, D_MODEL)),
                    _resident((D_MODEL, D_MODEL)),
                    _resident((D_MODEL, D_FF)),
                    _resident((D_FF, D_MODEL)),
                    _resident((1, D_MODEL))],
        out_specs=tok,
        out_shape=jax.ShapeDtypeStruct((bsz, out_len, D_MODEL), F32),
        scratch_shapes=[pltpu.VMEM((tm, D_MODEL), BF16), pltpu.VMEM((tm, D_MODEL), F32)],
        compiler_params=_params(2),
        name="out_proj_mlp",
    )(a, b_, c_, d, xx, mod_l, mod_l, g_ffn.reshape(1, D_MODEL), w_out.astype(BF16), w_ff1.astype(BF16),
      w_ff2.astype(BF16), g_final.reshape(1, D_MODEL))


def kernel(x, c, ctx, c_ctx, w_ada, b_ada, g_norm_mix, g_norm_ffn, w_in, b_gate, w_conv_qk, rpb, w_spatial,
           b_spatial, g_gmlp, g_mlstm, w_fnet, w_out, w_ff1, w_ff2, g_final):
    bsz = x.shape[0]
    assert bsz <= CTX_MOD_ROW and x.shape[1:] == (SEQ, D_MODEL) and ctx.shape[1:] == (CTX_LEN, D_MODEL)
    depth = w_ada.shape[0]
    xx = jnp.concatenate([x, ctx], axis=1)
    cc = jnp.zeros((MOD_ROWS, D_MODEL), F32).at[:bsz].set(c).at[CTX_MOD_ROW].set(c_ctx)
    mod = _ada_mod(cc, w_ada, b_ada).reshape(depth, MOD_ROWS, N_MOD, D_MODEL)

    rope = _rope_tables()
    dft_c = _channel_dft()
    tabs = (_dft_tables(SEQ, (SEQ * GROUP_WIDTH // FNET_GROUPS) ** -0.5)
            + _dft_tables(CTX_LEN, (CTX_LEN * GROUP_WIDTH // FNET_GROUPS) ** -0.5))

    for l in range(depth):
        w_cat = _projection_weights(w_in[l])
        p, gates, zc, zs = _in_proj(xx, mod[l], g_norm_mix[l], w_cat, dft_c)
        a = _attention(p, _attention_bias(rpb[l]))
        b_ = _gmlp(p, w_spatial[l], b_spatial[l], g_gmlp[l])
        c_ = _mlstm(p, gates, w_conv_qk[l], b_gate[l], g_mlstm[l], rope)
        d = _fnet(zc, zs, tabs, w_fnet[l])
        xx = _post(a, b_, c_, d, xx, mod[l], g_norm_ffn[l], w_out[l], w_ff1[l], w_ff2[l], g_final,
                   final=(l == depth - 1))
    return xx
```

```python
import functools

import jax
import jax.numpy as jnp
from jax import lax
from jax.experimental import pallas as pl
from jax.experimental.pallas import tpu as pltpu

D_MODEL = 1024
SEQ = 2048
DEPTH = 2
GRID_W = 64
GRID_ROWS = SEQ // GRID_W
CTX_LEN = 256
T_ALL = CTX_LEN + SEQ
HEAD_DIM = 64
GROUP_WIDTH = 256
GROUP_HEADS = 4
WIN_ROWS = 8
WIN_COLS = 16
CHUNK = 128
N_CHUNKS = T_ALL // CHUNK
CTX_CHUNKS = CTX_LEN // CHUNK
LAT_CHUNKS = SEQ // CHUNK
CONV_W = 3
FNET_GROUPS = 4
ROPE_THETA = 10000.0
D_FF = 4 * D_MODEL
N_MOD = 6
EPS = 1e-6
NEG_INF = -1e30
N_GATES = 4 * GROUP_HEADS
OFF_G = 10 * GROUP_WIDTH
D_IN = OFF_G + N_GATES
MOD_ROWS = 24
CTX_MOD_ROW = 16
LANES = 128
BF16_ROWS = 16
GATE_I_COLS = (0, 2, 8, 10, 1, 3, 9, 11)
GATE_F_COLS = (4, 6, 12, 14, 5, 7, 13, 15)
TOKEN_TILE = 768
LATENT_TILE = 512
P_WIDTH = 9 * GROUP_WIDTH
VMEM_LIMIT = 56 * 1024 * 1024

F32 = jnp.float32
BF16 = jnp.bfloat16


def _dot(a, b):
    return jnp.dot(a, b, preferred_element_type=F32)


def _dot_nt(a, b):
    return lax.dot_general(a, b, (((1,), (1,)), ((), ())), preferred_element_type=F32)


def _dot_f32(a, b):
    return jnp.dot(a, b, preferred_element_type=F32, precision=lax.Precision.HIGHEST)


def _sigmoid(x):
    return 0.5 * (1.0 + jnp.tanh(0.5 * x))


def _resident(shape):
    nd = len(shape)
    return pl.BlockSpec(shape, lambda *_: (0,) * nd, pipeline_mode=pl.Buffered(1))


def _params(n_axes):
    return pltpu.CompilerParams(dimension_semantics=("arbitrary",) * n_axes,
                                vmem_limit_bytes=VMEM_LIMIT)


def _mod_body(c_ref, w_ref, b_ref, o_ref):
    s = c_ref[...]
    s = s * jax.nn.sigmoid(s)
    o_ref[0] = _dot(s.astype(BF16), w_ref[0].astype(BF16)) + b_ref[0]


def _ada_mod(cc, w_ada, b_ada):
    depth, d, n = w_ada.shape
    tn = 1536
    return pl.pallas_call(
        _mod_body,
        grid=(depth, n // tn),
        in_specs=[pl.BlockSpec((MOD_ROWS, d), lambda l, j: (0, 0)),
                  pl.BlockSpec((1, d, tn), lambda l, j: (l, 0, j)),
                  pl.BlockSpec((1, 1, tn), lambda l, j: (l, 0, j))],
        out_specs=pl.BlockSpec((1, MOD_ROWS, tn), lambda l, j: (l, 0, j)),
        out_shape=jax.ShapeDtypeStruct((depth, MOD_ROWS, n), F32),
        compiler_params=_params(2),
        name="ada_mod",
    )(cc, w_ada, b_ada.reshape(depth, 1, n))


_MOD_SPECS = [pl.BlockSpec((1, N_MOD, D_MODEL), lambda b, t: (b, 0, 0)),
              pl.BlockSpec((1, N_MOD, D_MODEL), lambda b, t: (CTX_MOD_ROW, 0, 0))]


def _mod_rows(mod_lat_ref, mod_ctx_ref, tile):
    tok = pl.program_id(1) * tile + lax.broadcasted_iota(jnp.int32, (tile, 1), 0)
    is_ctx = tok >= SEQ
    return lambda k: jnp.where(is_ctx, mod_ctx_ref[0, k:k + 1, :], mod_lat_ref[0, k:k + 1, :])


def _rms(x, g):
    return x * lax.rsqrt(jnp.mean(x * x, axis=-1, keepdims=True) + EPS) * g


def _inproj_body(x_ref, ml_ref, mc_ref, g_ref, w_ref, dft_ref, p_ref, gate_ref, zc_ref, zs_ref, h_scr):
    mod = _mod_rows(ml_ref, mc_ref, x_ref.shape[1])
    h = _rms(x_ref[0], g_ref[...]) * (1.0 + mod(1)) + mod(0)
    h_scr[...] = h.astype(BF16)
    step = 2 * GROUP_WIDTH
    f = _dot(h_scr[...], w_ref[:, P_WIDTH:OFF_G]).astype(BF16)
    p_ref[0, :, 0:step] = _dot(h_scr[...], w_ref[:, 0:step]).astype(BF16)
    z = _dot(f, dft_ref[...])
    zc_ref[...] = z[:, :GROUP_WIDTH].astype(BF16)
    zs_ref[...] = z[:, GROUP_WIDTH:].astype(BF16)
    for j in range(step, P_WIDTH - GROUP_WIDTH, step):
        p_ref[0, :, j:j + step] = _dot(h_scr[...], w_ref[:, j:j + step]).astype(BF16)
    j = P_WIDTH - GROUP_WIDTH
    p_ref[0, :, j:] = _dot(h_scr[...], w_ref[:, j:P_WIDTH]).astype(BF16)
    gate_ref[0] = _dot(h_scr[...], w_ref[:, OFF_G:])


def _projection_weights(w_in):
    pad = jnp.zeros((w_in.shape[0], LANES - N_GATES), w_in.dtype)
    w_g = w_in[:, OFF_G:]
    return jnp.concatenate([w_in[:, :OFF_G], w_g[:, jnp.array(GATE_I_COLS + GATE_F_COLS)], pad],
                           axis=1).astype(BF16)


def _in_proj(xx, mod_l, g, w_cat, dft_c):
    bsz = xx.shape[0]
    tm = TOKEN_TILE
    wn = w_cat.shape[1]
    return pl.pallas_call(
        _inproj_body,
        grid=(bsz, T_ALL // tm),
        in_specs=[pl.BlockSpec((1, tm, D_MODEL), lambda b, t: (b, t, 0))] + _MOD_SPECS
                 + [_resident((1, D_MODEL)),
                    _resident((D_MODEL, wn)),
                    _resident((GROUP_WIDTH, 2 * GROUP_WIDTH))],
        out_specs=[pl.BlockSpec((1, tm, P_WIDTH), lambda b, t: (b, t, 0)),
                   pl.BlockSpec((1, tm, LANES), lambda b, t: (b, t, 0)),
                   pl.BlockSpec((tm, GROUP_WIDTH), lambda b, t: (t, b)),
                   pl.BlockSpec((tm, GROUP_WIDTH), lambda b, t: (t, b))],
        out_shape=[jax.ShapeDtypeStruct((bsz, T_ALL, P_WIDTH), BF16),
                   jax.ShapeDtypeStruct((bsz, T_ALL, LANES), F32),
                   jax.ShapeDtypeStruct((T_ALL, bsz * GROUP_WIDTH), BF16),
                   jax.ShapeDtypeStruct((T_ALL, bsz * GROUP_WIDTH), BF16)],
        scratch_shapes=[pltpu.VMEM((tm, D_MODEL), BF16)],
        compiler_params=_params(2),
        name="in_proj",
    )(xx, mod_l, mod_l, g.reshape(1, D_MODEL), w_cat, dft_c)


def _group_spec(col_block):
    return pl.BlockSpec((1, T_ALL, GROUP_WIDTH), lambda b: (b, 0, col_block))


def _softmax_pv(parts):
    m = functools.reduce(jnp.maximum, [jnp.max(s, axis=-1, keepdims=True) for s, _ in parts])
    es = [jnp.exp(s - m) for s, _ in parts]
    den = functools.reduce(jnp.add, [jnp.sum(e, axis=-1, keepdims=True) for e in es])
    num = functools.reduce(jnp.add, [_dot(e.astype(BF16), v) for e, (_, v) in zip(es, parts)])
    return num / den


def _attn_body(q_ref, k_ref, v_ref, bias_ref, o_ref, *, ctx_out):
    scale = HEAD_DIM ** -0.5
    pairs = [slice(i * LANES, (i + 1) * LANES) for i in range(GROUP_WIDTH // LANES)]

    def stack_pair(q2):
        low = lax.broadcasted_iota(jnp.int32, q2.shape, 1) < HEAD_DIM
        zero = jnp.zeros_like(q2)
        return low, jnp.concatenate([jnp.where(low, q2, zero), jnp.where(low, zero, q2)], axis=0)

    def unstack_pair(low, o):
        n = o.shape[0] // 2
        return jnp.where(low, o[:n], o[n:]).astype(BF16)

    ctx = slice(SEQ, T_ALL)
    if ctx_out:
        ctx_tiles = []
        for ps in pairs:
            low, qm = stack_pair(q_ref[0, ctx, ps] * scale)
            ctx_tiles.append((low, _dot_nt(qm, k_ref[0, ctx, ps])))
        for ps, (low, s) in zip(pairs, ctx_tiles):
            o_ref[0, ctx, ps] = unstack_pair(low, _softmax_pv([(s, v_ref[0, ctx, ps])]))
    else:
        o_ref[0, ctx, :] = jnp.zeros((CTX_LEN, GROUP_WIDTH), BF16)

    def scores(r, i):
        ps = pairs[i]
        rs = jnp.clip(r - WIN_ROWS // 2, 0, GRID_ROWS - WIN_ROWS)
        q0 = pl.multiple_of(r * GRID_W, GRID_W)
        k0 = pl.multiple_of(rs * GRID_W, GRID_W)
        low, qm = stack_pair(q_ref[0, pl.ds(q0, GRID_W), ps] * scale)
        s_loc = _dot_nt(qm, k_ref[0, pl.ds(k0, WIN_ROWS * GRID_W), ps]) + bias_ref[i, r - rs]
        s_ctx = _dot_nt(qm, k_ref[0, ctx, ps])
        return s_loc, s_ctx, low, q0, k0

    def finish(i, tile):
        ps = pairs[i]
        s_loc, s_ctx, low, q0, k0 = tile
        o = _softmax_pv([(s_loc, v_ref[0, pl.ds(k0, WIN_ROWS * GRID_W), ps]), (s_ctx, v_ref[0, ctx, ps])])
        o_ref[0, pl.ds(q0, GRID_W), ps] = unstack_pair(low, o)

    rows_per_trip = 4

    def trip(g, carry):
        tiles = [(g * rows_per_trip + dr, i) for dr in range(rows_per_trip) for i in range(len(pairs))]
        cur = scores(*tiles[0])
        for j, (_, i) in enumerate(tiles):
            nxt = scores(*tiles[j + 1]) if j + 1 < len(tiles) else None
            finish(i, cur)
            cur = nxt
        return carry

    lax.fori_loop(0, GRID_ROWS // rows_per_trip, trip, 0)


def _attention(p, bias_tab, ctx_out):
    bsz = p.shape[0]
    return pl.pallas_call(
        functools.partial(_attn_body, ctx_out=ctx_out),
        grid=(bsz,),
        in_specs=[_group_spec(0), _group_spec(1), _group_spec(2), _resident(bias_tab.shape)],
        out_specs=pl.BlockSpec((1, T_ALL, GROUP_WIDTH), lambda b: (b, 0, 0)),
        out_shape=jax.ShapeDtypeStruct((bsz, T_ALL, GROUP_WIDTH), BF16),
        compiler_params=_params(1),
        name="nbr_attention",
    )(p, p, p, bias_tab)


def _attention_bias(rpb):
    c = jnp.arange(GRID_W)
    qstart = jnp.clip(c - WIN_COLS // 2, 0, GRID_W - WIN_COLS)
    in_win = (c[None, :] >= qstart[:, None]) & (c[None, :] < qstart[:, None] + WIN_COLS)
    col = jnp.clip(c[None, :] - c[:, None] + WIN_COLS - 1, 0, 2 * WIN_COLS - 2)
    pick = (col[:, :, None] == jnp.arange(2 * WIN_COLS - 1)).astype(F32)
    by_drow = jnp.einsum('hde,qke->hdqk', rpb.astype(F32), pick, precision=lax.Precision.HIGHEST)
    by_drow = jnp.where(in_win, by_drow, NEG_INF)
    tab = jnp.stack([by_drow[:, WIN_ROWS - 1 - v:2 * WIN_ROWS - 1 - v] for v in range(WIN_ROWS)], axis=1)
    tab = tab.transpose(0, 1, 3, 2, 4).reshape(GROUP_HEADS // 2, 2, WIN_ROWS, GRID_W, WIN_ROWS * GRID_W)
    return tab.transpose(0, 2, 1, 3, 4).reshape(GROUP_HEADS // 2, WIN_ROWS, 2 * GRID_W, WIN_ROWS * GRID_W)


def _gmlp_body(u_ref, z_ref, ws_ref, bs_ref, g_ref, o_ref):
    def chunk(n, carry):
        rows = pl.ds(pl.multiple_of(n * CHUNK, CHUNK), CHUNK)
        z = _rms(jax.nn.gelu(z_ref[0, rows, :].astype(F32)), g_ref[...]).astype(BF16)
        u = jax.nn.gelu(u_ref[0, rows, :].astype(F32))
        for h in range(GROUP_HEADS):
            hs = slice(h * HEAD_DIM, (h + 1) * HEAD_DIM)
            mixed = _dot(ws_ref[h], z[:, hs]) + bs_ref[:, hs]
            o_ref[0, rows, hs] = (u[:, hs] * mixed).astype(BF16)
        return carry

    lax.fori_loop(0, N_CHUNKS, chunk, 0, unroll=3)


def _gmlp(p, w_spatial, b_spatial, g_gmlp):
    bsz = p.shape[0]
    bias = jnp.repeat(b_spatial.T, HEAD_DIM, axis=1)
    return pl.pallas_call(
        _gmlp_body,
        grid=(bsz,),
        in_specs=[_group_spec(3), _group_spec(4),
                  _resident((GROUP_HEADS, CHUNK, CHUNK)),
                  _resident((CHUNK, GROUP_WIDTH)),
                  _resident((1, GROUP_WIDTH))],
        out_specs=pl.BlockSpec((1, T_ALL, GROUP_WIDTH), lambda b: (b, 0, 0)),
        out_shape=jax.ShapeDtypeStruct((bsz, T_ALL, GROUP_WIDTH), BF16),
        compiler_params=_params(1),
        name="gmlp",
    )(p, p, w_spatial.astype(BF16), bias, g_gmlp.reshape(1, GROUP_WIDTH))


def _mlstm_body(q_ref, k_ref, v_ref, og_ref, gate_ref, wc_ref, bg_ref, cos_ref, sina_ref,
                sinb_ref, gln_ref, tri_ref, o_ref,
                qtz_scr, k_scr, kbd_scr, vat_scr, acol_scr, rows_scr, ht_scr, c_scr, m_scr):
    half = HEAD_DIM // 4
    n_chains = 2 * GROUP_HEADS

    def conv_act(ref, n, w0, w1, w2, rope, post):
        rows = pl.ds(pl.multiple_of(n * CHUNK, CHUNK), CHUNK)
        cur = ref[0, rows, :].astype(F32)
        first = (n == 0) | (n == LAT_CHUNKS)
        last = (n == LAT_CHUNKS - 1) | (n == N_CHUNKS - 1)
        before = pl.ds(pl.multiple_of(jnp.maximum(n * CHUNK - BF16_ROWS, 0), BF16_ROWS), BF16_ROWS)
        after = pl.ds(pl.multiple_of(jnp.minimum((n + 1) * CHUNK, T_ALL - BF16_ROWS), BF16_ROWS), BF16_ROWS)
        tail = jnp.where(first, 0.0, 1.0) * ref[0, before, :][BF16_ROWS - 1:BF16_ROWS, :].astype(F32)
        head = jnp.where(last, 0.0, 1.0) * ref[0, after, :][0:1, :].astype(F32)
        sub = lax.broadcasted_iota(jnp.int32, (CHUNK, 1), 0)
        x_prev = jnp.where(sub == 0, tail, pltpu.roll(cur, 1, 0))
        x_next = jnp.where(sub == CHUNK - 1, head, pltpu.roll(cur, CHUNK - 1, 0))
        y = w0 * x_prev + w1 * cur + w2 * x_next
        y = y * _sigmoid(y)
        if rope:
            y = (y * cos_ref[rows, :] + pltpu.roll(y, GROUP_WIDTH - half, 1) * sina_ref[rows, :]
                 + pltpu.roll(y, half, 1) * sinb_ref[rows, :])
        return y * post

    def prep(n, rope):
        rows = pl.ds(pl.multiple_of(n * CHUNK, CHUNK), CHUNK)
        gw = GROUP_WIDTH
        q = conv_act(q_ref, n, wc_ref[0:1, :gw], wc_ref[1:2, :gw], wc_ref[2:3, :gw], rope, 1.0)
        k = conv_act(k_ref, n, wc_ref[0:1, gw:], wc_ref[1:2, gw:], wc_ref[2:3, gw:], rope, HEAD_DIM ** -0.5)
        q_t = q.T.astype(BF16)
        v_t = v_ref[0, rows, :].astype(F32).T.astype(BF16)
        k = k.astype(BF16)
        k_scr[n] = k
        sub = lax.broadcasted_iota(jnp.int32, (HEAD_DIM, CHUNK), 0)
        ones_row = jnp.where(sub == 0, 1.0, 0.0).astype(BF16)
        zeros = jnp.zeros((HEAD_DIM, CHUNK), BF16)
        low = lax.broadcasted_iota(jnp.int32, (CHUNK, LANES), 1) < HEAD_DIM
        for pair in range(GROUP_HEADS // 2):
            h0 = slice(2 * pair * HEAD_DIM, (2 * pair + 1) * HEAD_DIM)
            h1 = slice((2 * pair + 1) * HEAD_DIM, (2 * pair + 2) * HEAD_DIM)
            qtz_scr[n, pair] = jnp.concatenate([jnp.concatenate([q_t[h0, :], zeros], axis=1),
                                                jnp.concatenate([zeros, q_t[h1, :]], axis=1)], axis=0)
            k_pair = k[:, pair * LANES:(pair + 1) * LANES]
            kbd_scr[n, pair] = jnp.concatenate([jnp.where(low, k_pair, jnp.zeros_like(k_pair)),
                                                jnp.where(low, jnp.zeros_like(k_pair), k_pair)], axis=0)
            vat_scr[n, pair] = jnp.concatenate([jnp.concatenate([v_t[h0, :], ones_row], axis=0),
                                                jnp.concatenate([v_t[h1, :], ones_row], axis=0)], axis=1)
        ht_scr[n] = jnp.zeros((GROUP_WIDTH, CHUNK), F32)

        g_i = gate_ref[0, rows, :] + bg_ref[...]
        lf = pltpu.roll(jax.nn.log_sigmoid(g_i), LANES - n_chains, 1)
        pre = _dot_f32(tri_ref[...], lf)
        suf = pre[CHUNK - 1:CHUNK, :] - pre + lf
        lane = lax.broadcasted_iota(jnp.int32, (CHUNK, LANES), 1)
        b = jnp.where(lane % 4 < 2, pre, suf)
        a = g_i - b
        acol_scr[n] = a
        a_t = a.T[0:n_chains, :]
        b_t = b.T[0:n_chains, :]
        chain = lax.broadcasted_iota(jnp.int32, (n_chains, CHUNK), 0)
        b_end = jnp.where(chain % 4 < 2, b_t[:, CHUNK - 1:CHUNK], b_t[:, 0:1])
        b_end = jnp.broadcast_to(b_end, (n_chains, CHUNK))
        a_max = jnp.broadcast_to(jnp.max(a_t, axis=1, keepdims=True), (n_chains, CHUNK))
        for kind, rows8 in enumerate((a_t, b_t, b_end, a_max)):
            rows_scr[n, kind] = jnp.concatenate([rows8[0:4, :], rows8[4:8, :]], axis=1)

    def prep_ctx(n, carry):
        prep(n, False)
        return carry

    def prep_lat(n, carry):
        prep(n, True)
        return carry

    lax.fori_loop(0, LAT_CHUNKS, prep_lat, 0, unroll=2)
    lax.fori_loop(LAT_CHUNKS, N_CHUNKS, prep_ctx, 0)
    c_scr[...] = jnp.zeros(c_scr.shape, F32)
    m_scr[...] = jnp.zeros(m_scr.shape, F32)

    s_idx = lax.broadcasted_iota(jnp.int32, (CHUNK, CHUNK), 0)
    t_idx = lax.broadcasted_iota(jnp.int32, (CHUNK, CHUNK), 1)

    zeros = jnp.zeros((CHUNK, CHUNK), BF16)
    low = lax.broadcasted_iota(jnp.int32, (1, LANES), 1) < HEAD_DIM

    def scan(i, carry):
        chunk_of = (jnp.where(i < CTX_CHUNKS, LAT_CHUNKS + i, i - CTX_CHUNKS), N_CHUNKS - 1 - i)
        steps = [(rev, pair, chunk_of[rev]) for rev in range(2) for pair in range(GROUP_HEADS // 2)]

        scores = []
        for rev, pair, n in steps:
            dp = 2 * rev + pair
            causal = (s_idx >= t_idx) if rev else (s_idx <= t_idx)
            arg = jnp.concatenate([jnp.where(causal, acol_scr[n, :, dp:dp + 1], NEG_INF),
                                   jnp.where(causal, acol_scr[n, :, 4 + dp:5 + dp], NEG_INF)], axis=1)
            s = _dot(k_scr[n, :, pair * LANES:(pair + 1) * LANES], qtz_scr[n, pair])
            scores.append((arg, jnp.max(arg, axis=0, keepdims=True), s))

        inter = []
        for rev, pair, n in steps:
            dp = 2 * rev + pair
            a_row = rows_scr[n, 0, dp:dp + 1, :]
            b_end = rows_scr[n, 2, dp:dp + 1, :]
            a_max = rows_scr[n, 3, dp:dp + 1, :]
            m_mem = m_scr[dp:dp + 1, :]
            c_mem = c_scr[dp]
            inter.append((_dot(c_mem.astype(BF16), qtz_scr[n, pair]), m_mem))
            m_new = b_end + jnp.maximum(m_mem, a_max)
            w_src = jnp.exp(b_end + a_row - m_new)
            decay = jnp.exp(b_end + m_mem - m_new)
            decay = jnp.where(low, decay[:, :CHUNK], decay[:, CHUNK:])
            c_scr[dp] = decay * c_mem + _dot((vat_scr[n, pair].astype(F32) * w_src).astype(BF16),
                                             kbd_scr[n, pair])
            m_scr[dp:dp + 1, :] = m_new

        for (rev, pair, n), (arg, cm, s), (x2, m_mem) in zip(steps, scores, inter):
            dp = 2 * rev + pair
            g = (s * jnp.exp(arg - cm)).astype(BF16)
            g_bd = jnp.concatenate([jnp.concatenate([g[:, :CHUNK], zeros], axis=1),
                                    jnp.concatenate([zeros, g[:, CHUNK:]], axis=1)], axis=0)
            x1 = _dot(vat_scr[n, pair], g_bd)
            b_row = rows_scr[n, 1, dp:dp + 1, :]
            mu = jnp.maximum(cm, m_mem)
            both = x1 * jnp.exp(cm - mu) + x2 * jnp.exp(m_mem - mu)
            den = both[HEAD_DIM:HEAD_DIM + 1, :]
            inv = 1.0 / jnp.maximum(jnp.abs(den), jnp.exp(-(b_row + mu)))
            h_t = both[0:HEAD_DIM, :] * inv
            ht_scr[n, pair * LANES:(pair + 1) * LANES, :] += jnp.concatenate([h_t[:, :CHUNK], h_t[:, CHUNK:]],
                                                                             axis=0)
        return carry

    lax.fori_loop(0, N_CHUNKS, scan, 0, unroll=3)

    def finish(n, carry):
        rows = pl.ds(pl.multiple_of(n * CHUNK, CHUNK), CHUNK)
        gate = _sigmoid(og_ref[0, rows, :].astype(F32))
        ys = []
        for h in range(GROUP_HEADS):
            x = ht_scr[n, h * HEAD_DIM:(h + 1) * HEAD_DIM, :]
            mu = jnp.mean(x, axis=0, keepdims=True)
            var = jnp.mean(jnp.square(x - mu), axis=0, keepdims=True)
            ys.append((x - mu) * lax.rsqrt(var + EPS))
        y = jnp.concatenate(ys, axis=0).T * gln_ref[...]
        o_ref[0, rows, :] = (gate * y).astype(BF16)
        return carry

    lax.fori_loop(0, N_CHUNKS, finish, 0)


def _rope_tables():
    t = jnp.arange(SEQ)
    pos = jnp.stack([(t // GRID_W).astype(F32), (t % GRID_W).astype(F32)], axis=1)
    lane = jnp.arange(GROUP_WIDTH)
    m = HEAD_DIM // 4
    inv = ROPE_THETA ** (-(lane % m).astype(F32) / m)
    axis = (lane % HEAD_DIM) // (HEAD_DIM // 2)
    ang = jnp.where(axis[None, :] == 0, pos[:, 0:1], pos[:, 1:2]) * inv[None, :]
    low = ((lane % (2 * m)) < m)[None, :]
    cos, sin = jnp.cos(ang), jnp.sin(ang)
    return cos, jnp.where(low, -sin, 0.0), jnp.where(low, 0.0, sin)


def _mlstm(p, gates, w_conv, b_gate, g_mlstm, rope):
    bsz = p.shape[0]
    i = jnp.arange(CHUNK)
    tri = (i[:, None] >= i[None, :]).astype(F32)
    bg = jnp.pad(b_gate[jnp.array(GATE_I_COLS + GATE_F_COLS)], (0, LANES - N_GATES)).reshape(1, LANES)
    cos, sina, sinb = rope
    n_pairs = GROUP_HEADS // 2
    return pl.pallas_call(
        _mlstm_body,
        grid=(bsz,),
        in_specs=[_group_spec(5), _group_spec(6), _group_spec(7), _group_spec(8),
                  pl.BlockSpec((1, T_ALL, LANES), lambda b: (b, 0, 0)),
                  _resident((CONV_W, 2 * GROUP_WIDTH)),
                  _resident((1, LANES)),
                  _resident((SEQ, GROUP_WIDTH)), _resident((SEQ, GROUP_WIDTH)), _resident((SEQ, GROUP_WIDTH)),
                  _resident((1, GROUP_WIDTH)),
                  _resident((CHUNK, CHUNK))],
        out_specs=pl.BlockSpec((1, T_ALL, GROUP_WIDTH), lambda b: (b, 0, 0)),
        out_shape=jax.ShapeDtypeStruct((bsz, T_ALL, GROUP_WIDTH), BF16),
        scratch_shapes=[pltpu.VMEM((N_CHUNKS, n_pairs, LANES, 2 * CHUNK), BF16),
                        pltpu.VMEM((N_CHUNKS, CHUNK, GROUP_WIDTH), BF16),
                        pltpu.VMEM((N_CHUNKS, n_pairs, 2 * CHUNK, LANES), BF16),
                        pltpu.VMEM((N_CHUNKS, n_pairs, LANES, 2 * CHUNK), BF16),
                        pltpu.VMEM((N_CHUNKS, CHUNK, LANES), F32),
                        pltpu.VMEM((N_CHUNKS, 4, 2 * n_pairs, 2 * CHUNK), F32),
                        pltpu.VMEM((N_CHUNKS, GROUP_WIDTH, CHUNK), F32),
                        pltpu.VMEM((2 * n_pairs, LANES, LANES), F32),
                        pltpu.VMEM((2 * n_pairs, 2 * CHUNK), F32)],
        compiler_params=_params(1),
        name="mlstm",
    )(p, p, p, p, gates, w_conv, bg, cos, sina, sinb, g_mlstm.reshape(1, GROUP_WIDTH), tri)


def _dft_tables(n, scale):
    f = 1 << (n.bit_length() // 2)
    s = jnp.arange(n, dtype=jnp.int32)[None, :]
    ang_a = ((f * jnp.arange(n // f, dtype=jnp.int32)[:, None] * s) % n).astype(F32) * (2.0 * jnp.pi / n)
    ang_b = ((jnp.arange(f, dtype=jnp.int32)[:, None] * s) % n).astype(F32) * (2.0 * jnp.pi / n)
    ca, sa = jnp.cos(ang_a)[:, None, :], jnp.sin(ang_a)[:, None, :]
    cb, sb = jnp.cos(ang_b)[None, :, :], jnp.sin(ang_b)[None, :, :]
    cos = (ca * cb - sa * sb).reshape(n, n)
    sin = (sa * cb + ca * sb).reshape(n, n)
    return (cos * scale).astype(BF16), (-sin * scale).astype(BF16)


def _channel_dft():
    gc = GROUP_WIDTH // FNET_GROUPS
    j = jnp.arange(GROUP_WIDTH, dtype=jnp.int32)
    same = (j[:, None] // gc) == (j[None, :] // gc)
    ang = (((j[:, None] % gc) * (j[None, :] % gc)) % gc).astype(F32) * (2.0 * jnp.pi / gc)
    c = jnp.where(same, jnp.cos(ang), 0.0)
    s = jnp.where(same, jnp.sin(ang), 0.0)
    return jnp.concatenate([c, s], axis=1).astype(BF16)


def _fnet_body(zc_ref, zs_ref, cl_ref, sl_ref, cc_ref, sc_ref, w_ref, o_ref):
    y = _dot(cl_ref[...], zc_ref[0:SEQ, :]) + _dot(sl_ref[...], zs_ref[0:SEQ, :])
    o_ref[0:SEQ, :] = _dot(y.astype(BF16), w_ref[...]).astype(BF16)
    y = _dot(cc_ref[...], zc_ref[SEQ:, :]) + _dot(sc_ref[...], zs_ref[SEQ:, :])
    o_ref[SEQ:, :] = _dot(y.astype(BF16), w_ref[...]).astype(BF16)


def _fnet(zc, zs, tabs, w_fnet):
    bsz = zc.shape[1] // GROUP_WIDTH
    cl, sl, cc, sc = tabs
    col = pl.BlockSpec((T_ALL, GROUP_WIDTH), lambda b: (0, b))
    return pl.pallas_call(
        _fnet_body,
        grid=(bsz,),
        in_specs=[col, col, _resident((SEQ, SEQ)), _resident((SEQ, SEQ)),
                  _resident((CTX_LEN, CTX_LEN)), _resident((CTX_LEN, CTX_LEN)),
                  _resident((GROUP_WIDTH, GROUP_WIDTH))],
        out_specs=col,
        out_shape=jax.ShapeDtypeStruct((T_ALL, bsz * GROUP_WIDTH), BF16),
        compiler_params=_params(1),
        name="fnet",
    )(zc, zs, cl, sl, cc, sc, w_fnet.astype(BF16))


def _post_body(a_ref, b_ref, c_ref, d_ref, x_ref, ml_ref, mc_ref, g_ref, wo_ref, w1_ref, w2_ref, gf_ref,
               o_ref, h_scr, acc_scr, *, final):
    gw = GROUP_WIDTH
    mod = _mod_rows(ml_ref, mc_ref, x_ref.shape[1])
    y = (_dot(a_ref[0], wo_ref[0:gw, :]) + _dot(b_ref[0], wo_ref[gw:2 * gw, :])
         + _dot(c_ref[0], wo_ref[2 * gw:3 * gw, :]) + _dot(d_ref[...], wo_ref[3 * gw:, :]))
    x1 = x_ref[0] + mod(2) * y
    h = _rms(x1, g_ref[...]) * (1.0 + mod(4)) + mod(3)
    h_scr[...] = h.astype(BF16)
    step = 512

    def up(j):
        return jnp.maximum(_dot(h_scr[...], w1_ref[:, j:j + step]), 0.0)

    a = up(0)
    for j in range(0, D_FF, step):
        nxt = up(j + step) if j + step < D_FF else None
        part = _dot((a * a).astype(BF16), w2_ref[j:j + step, :])
        if j == 0:
            acc_scr[...] = part
        else:
            acc_scr[...] += part
        a = nxt
    x2 = x1 + mod(5) * acc_scr[...]
    if final:
        x2 = _rms(x2, gf_ref[...])
    o_ref[0] = x2


def _post(a, b_, c_, d, xx, mod_l, g_ffn, w_out, w_ff1, w_ff2, g_final, final):
    bsz = xx.shape[0]
    tm, out_len = (LATENT_TILE, SEQ) if final else (TOKEN_TILE, T_ALL)
    tok = pl.BlockSpec((1, tm, D_MODEL), lambda b, t: (b, t, 0))
    grp = pl.BlockSpec((1, tm, GROUP_WIDTH), lambda b, t: (b, t, 0))
    return pl.pallas_call(
        functools.partial(_post_body, final=final),
        grid=(bsz, out_len // tm),
        in_specs=[grp, grp, grp, pl.BlockSpec((tm, GROUP_WIDTH), lambda b, t: (t, b)), tok] + _MOD_SPECS
                 + [_resident((1, D_MODEL)),
                    _resident((D_MODEL, D_MODEL)),
                    _resident((D_MODEL, D_FF)),
                    _resident((D_FF, D_MODEL)),
                    _resident((1, D_MODEL))],
        out_specs=tok,
        out_shape=jax.ShapeDtypeStruct((bsz, out_len, D_MODEL), F32),
        scratch_shapes=[pltpu.VMEM((tm, D_MODEL), BF16), pltpu.VMEM((tm, D_MODEL), F32)],
        compiler_params=_params(2),
        name="out_proj_mlp",
    )(a, b_, c_, d, xx, mod_l, mod_l, g_ffn.reshape(1, D_MODEL), w_out.astype(BF16), w_ff1.astype(BF16),
      w_ff2.astype(BF16), g_final.reshape(1, D_MODEL))


def kernel(x, c, ctx, c_ctx, w_ada, b_ada, g_norm_mix, g_norm_ffn, w_in, b_gate, w_conv_qk, rpb, w_spatial,
           b_spatial, g_gmlp, g_mlstm, w_fnet, w_out, w_ff1, w_ff2, g_final):
    bsz = x.shape[0]
    assert bsz <= CTX_MOD_ROW and x.shape[1:] == (SEQ, D_MODEL) and ctx.shape[1:] == (CTX_LEN, D_MODEL)
    depth = w_ada.shape[0]
    xx = jnp.concatenate([x, ctx], axis=1)
    cc = jnp.zeros((MOD_ROWS, D_MODEL), F32).at[:bsz].set(c).at[CTX_MOD_ROW].set(c_ctx)
    mod = _ada_mod(cc, w_ada, b_ada).reshape(depth, MOD_ROWS, N_MOD, D_MODEL)

    rope = _rope_tables()
    dft_c = _channel_dft()
    tabs = (_dft_tables(SEQ, (SEQ * GROUP_WIDTH // FNET_GROUPS) ** -0.5)
            + _dft_tables(CTX_LEN, (CTX_LEN * GROUP_WIDTH // FNET_GROUPS) ** -0.5))

    for l in range(depth):
        w_cat = _projection_weights(w_in[l])
        p, gates, zc, zs = _in_proj(xx, mod[l], g_norm_mix[l], w_cat, dft_c)
        a = _attention(p, _attention_bias(rpb[l]), ctx_out=(l < depth - 1))
        b_ = _gmlp(p, w_spatial[l], b_spatial[l], g_gmlp[l])
        c_ = _mlstm(p, gates, w_conv_qk[l], b_gate[l], g_mlstm[l], rope)
        d = _fnet(zc, zs, tabs, w_fnet[l])
        xx = _post(a, b_, c_, d, xx, mod[l], g_norm_ffn[l], w_out[l], w_ff1[l], w_ff2[l], g_final,
                   final=(l == depth - 1))
    return xx
```

```python
import functools

import jax
import jax.numpy as jnp
from jax import lax
from jax.experimental import pallas as pl
from jax.experimental.pallas import tpu as pltpu

D_MODEL = 1024
SEQ = 2048
DEPTH = 2
GRID_W = 64
GRID_ROWS = SEQ // GRID_W
CTX_LEN = 256
T_ALL = CTX_LEN + SEQ
HEAD_DIM = 64
GROUP_WIDTH = 256
GROUP_HEADS = 4
WIN_ROWS = 8
WIN_COLS = 16
CHUNK = 128
N_CHUNKS = T_ALL // CHUNK
CTX_CHUNKS = CTX_LEN // CHUNK
LAT_CHUNKS = SEQ // CHUNK
CONV_W = 3
FNET_GROUPS = 4
ROPE_THETA = 10000.0
D_FF = 4 * D_MODEL
N_MOD = 6
EPS = 1e-6
NEG_INF = -1e30
N_GATES = 4 * GROUP_HEADS
OFF_G = 10 * GROUP_WIDTH
D_IN = OFF_G + N_GATES
MOD_ROWS = 24
CTX_MOD_ROW = 16
LANES = 128
BF16_ROWS = 16
GATE_I_COLS = (0, 2, 8, 10, 1, 3, 9, 11)
GATE_F_COLS = (4, 6, 12, 14, 5, 7, 13, 15)
TOKEN_TILE = 768
PROJ_TILE = 1152
LATENT_TILE = 512
P_WIDTH = 9 * GROUP_WIDTH
VMEM_LIMIT = 56 * 1024 * 1024

F32 = jnp.float32
BF16 = jnp.bfloat16


def _dot(a, b):
    return jnp.dot(a, b, preferred_element_type=F32)


def _dot_nt(a, b):
    return lax.dot_general(a, b, (((1,), (1,)), ((), ())), preferred_element_type=F32)


def _dot_f32(a, b):
    return jnp.dot(a, b, preferred_element_type=F32, precision=lax.Precision.HIGHEST)


def _sigmoid(x):
    return 0.5 * (1.0 + jnp.tanh(0.5 * x))


def _resident(shape):
    nd = len(shape)
    return pl.BlockSpec(shape, lambda *_: (0,) * nd, pipeline_mode=pl.Buffered(1))


def _params(n_axes):
    return pltpu.CompilerParams(dimension_semantics=("arbitrary",) * n_axes,
                                vmem_limit_bytes=VMEM_LIMIT)


def _mod_body(c_ref, w_ref, b_ref, o_ref):
    s = c_ref[...]
    s = s * jax.nn.sigmoid(s)
    o_ref[0] = _dot(s.astype(BF16), w_ref[0].astype(BF16)) + b_ref[0]


def _ada_mod(cc, w_ada, b_ada):
    depth, d, n = w_ada.shape
    tn = 1536
    return pl.pallas_call(
        _mod_body,
        grid=(depth, n // tn),
        in_specs=[pl.BlockSpec((MOD_ROWS, d), lambda l, j: (0, 0)),
                  pl.BlockSpec((1, d, tn), lambda l, j: (l, 0, j)),
                  pl.BlockSpec((1, 1, tn), lambda l, j: (l, 0, j))],
        out_specs=pl.BlockSpec((1, MOD_ROWS, tn), lambda l, j: (l, 0, j)),
        out_shape=jax.ShapeDtypeStruct((depth, MOD_ROWS, n), F32),
        compiler_params=_params(2),
        name="ada_mod",
    )(cc, w_ada, b_ada.reshape(depth, 1, n))


_MOD_SPECS = [pl.BlockSpec((1, N_MOD, D_MODEL), lambda b, t: (b, 0, 0)),
              pl.BlockSpec((1, N_MOD, D_MODEL), lambda b, t: (CTX_MOD_ROW, 0, 0))]


def _mod_rows(mod_lat_ref, mod_ctx_ref, tile):
    tok = pl.program_id(1) * tile + lax.broadcasted_iota(jnp.int32, (tile, 1), 0)
    is_ctx = tok >= SEQ
    return lambda k: jnp.where(is_ctx, mod_ctx_ref[0, k:k + 1, :], mod_lat_ref[0, k:k + 1, :])


def _rms(x, g):
    return x * lax.rsqrt(jnp.mean(x * x, axis=-1, keepdims=True) + EPS) * g


def _inproj_body(x_ref, ml_ref, mc_ref, g_ref, w_ref, dft_ref, p_ref, gate_ref, zc_ref, zs_ref, lhs_scr,
                 inv_scr):
    tm = x_ref.shape[1]
    head = tm - CTX_LEN
    ctx_tail = pl.program_id(1) == pl.num_programs(1) - 1
    shift_tail = jnp.where(ctx_tail, mc_ref[0, 0:1, :], ml_ref[0, 0:1, :])
    scale_tail = jnp.where(ctx_tail, mc_ref[0, 1:2, :], ml_ref[0, 1:2, :])
    lead = BF16_ROWS
    lhs_scr[0:lead, :] = jnp.concatenate([ml_ref[0, 0:1, :], shift_tail,
                                          jnp.zeros((lead - 2, D_MODEL), F32)], axis=0).astype(BF16)
    lhs_scr[lead:lead + head, :] = (x_ref[0, 0:head, :] * (g_ref[...] * (1.0 + ml_ref[0, 1:2, :]))).astype(BF16)
    lhs_scr[lead + head:, :] = (x_ref[0, head:, :] * (g_ref[...] * (1.0 + scale_tail))).astype(BF16)
    x = x_ref[0]
    inv = lax.rsqrt(jnp.mean(x * x, axis=-1, keepdims=True) + EPS)
    inv_scr[...] = jnp.broadcast_to(inv, (tm, LANES))

    def project(cols):
        n = (cols.stop - cols.start) // LANES
        r = _dot(lhs_scr[...], w_ref[:, cols])
        inv_head = jnp.concatenate([inv_scr[0:head, :]] * n, axis=1)
        inv_tail = jnp.concatenate([inv_scr[head:, :]] * n, axis=1)
        return jnp.concatenate([inv_head * r[lead:lead + head, :] + r[0:1, :],
                                inv_tail * r[lead + head:, :] + r[1:2, :]], axis=0)

    step = 2 * GROUP_WIDTH
    f = project(slice(P_WIDTH, OFF_G)).astype(BF16)
    p_ref[0, :, 0:step] = project(slice(0, step)).astype(BF16)
    z = _dot(f, dft_ref[...])
    zc_ref[...] = z[:, :GROUP_WIDTH].astype(BF16)
    zs_ref[...] = z[:, GROUP_WIDTH:].astype(BF16)
    for j in range(step, P_WIDTH - GROUP_WIDTH, step):
        p_ref[0, :, j:j + step] = project(slice(j, j + step)).astype(BF16)
    j = P_WIDTH - GROUP_WIDTH
    p_ref[0, :, j:] = project(slice(j, P_WIDTH)).astype(BF16)
    gate_ref[0] = project(slice(OFF_G, OFF_G + LANES))


def _projection_weights(w_in):
    pad = jnp.zeros((w_in.shape[0], LANES - N_GATES), w_in.dtype)
    w_g = w_in[:, OFF_G:]
    return jnp.concatenate([w_in[:, :OFF_G], w_g[:, jnp.array(GATE_I_COLS + GATE_F_COLS)], pad],
                           axis=1).astype(BF16)


def _in_proj(xx, mod_l, g, w_cat, dft_c):
    bsz = xx.shape[0]
    tm = PROJ_TILE
    wn = w_cat.shape[1]
    return pl.pallas_call(
        _inproj_body,
        grid=(bsz, T_ALL // tm),
        in_specs=[pl.BlockSpec((1, tm, D_MODEL), lambda b, t: (b, t, 0))] + _MOD_SPECS
                 + [_resident((1, D_MODEL)),
                    _resident((D_MODEL, wn)),
                    _resident((GROUP_WIDTH, 2 * GROUP_WIDTH))],
        out_specs=[pl.BlockSpec((1, tm, P_WIDTH), lambda b, t: (b, t, 0)),
                   pl.BlockSpec((1, tm, LANES), lambda b, t: (b, t, 0)),
                   pl.BlockSpec((tm, GROUP_WIDTH), lambda b, t: (t, b)),
                   pl.BlockSpec((tm, GROUP_WIDTH), lambda b, t: (t, b))],
        out_shape=[jax.ShapeDtypeStruct((bsz, T_ALL, P_WIDTH), BF16),
                   jax.ShapeDtypeStruct((bsz, T_ALL, LANES), F32),
                   jax.ShapeDtypeStruct((T_ALL, bsz * GROUP_WIDTH), BF16),
                   jax.ShapeDtypeStruct((T_ALL, bsz * GROUP_WIDTH), BF16)],
        scratch_shapes=[pltpu.VMEM((tm + BF16_ROWS, D_MODEL), BF16), pltpu.VMEM((tm, LANES), F32)],
        compiler_params=_params(2),
        name="in_proj",
    )(xx, mod_l, mod_l, g.reshape(1, D_MODEL), w_cat, dft_c)


def _group_spec(col_block):
    return pl.BlockSpec((1, T_ALL, GROUP_WIDTH), lambda b: (b, 0, col_block))


def _softmax_pv(parts):
    m = functools.reduce(jnp.maximum, [jnp.max(s, axis=-1, keepdims=True) for s, _ in parts])
    es = [jnp.exp(s - m) for s, _ in parts]
    den = functools.reduce(jnp.add, [jnp.sum(e, axis=-1, keepdims=True) for e in es])
    num = functools.reduce(jnp.add, [_dot(e.astype(BF16), v) for e, (_, v) in zip(es, parts)])
    return num / den


def _attn_body(q_ref, k_ref, v_ref, bias_ref, o_ref, *, ctx_out):
    scale = HEAD_DIM ** -0.5
    pairs = [slice(i * LANES, (i + 1) * LANES) for i in range(GROUP_WIDTH // LANES)]

    def stack_pair(q2):
        low = lax.broadcasted_iota(jnp.int32, q2.shape, 1) < HEAD_DIM
        zero = jnp.zeros_like(q2)
        return low, jnp.concatenate([jnp.where(low, q2, zero), jnp.where(low, zero, q2)], axis=0)

    def unstack_pair(low, o):
        n = o.shape[0] // 2
        return jnp.where(low, o[:n], o[n:]).astype(BF16)

    ctx = slice(SEQ, T_ALL)
    if ctx_out:
        ctx_tiles = []
        for ps in pairs:
            low, qm = stack_pair(q_ref[0, ctx, ps] * scale)
            ctx_tiles.append((low, _dot_nt(qm, k_ref[0, ctx, ps])))
        for ps, (low, s) in zip(pairs, ctx_tiles):
            o_ref[0, ctx, ps] = unstack_pair(low, _softmax_pv([(s, v_ref[0, ctx, ps])]))
    else:
        o_ref[0, ctx, :] = jnp.zeros((CTX_LEN, GROUP_WIDTH), BF16)

    rows_per_trip = 4
    tile_rows = 2 * GRID_W

    def trip(g, carry):
        tiles = [(dr, i) for dr in range(rows_per_trip) for i in range(len(pairs))]
        q0s = [pl.multiple_of((g * rows_per_trip + dr) * GRID_W, GRID_W) for dr in range(rows_per_trip)]
        stacked = {(dr, i): stack_pair(q_ref[0, pl.ds(q0s[dr], GRID_W), pairs[i]] * scale) for dr, i in tiles}
        s_ctx = [_dot_nt(jnp.concatenate([stacked[dr, i][1] for dr in range(rows_per_trip)], axis=0),
                         k_ref[0, ctx, pairs[i]]) for i in range(len(pairs))]

        def window_scores(dr, i):
            r = g * rows_per_trip + dr
            rs = jnp.clip(r - WIN_ROWS // 2, 0, GRID_ROWS - WIN_ROWS)
            k0 = pl.multiple_of(rs * GRID_W, GRID_W)
            s_loc = _dot_nt(stacked[dr, i][1], k_ref[0, pl.ds(k0, WIN_ROWS * GRID_W), pairs[i]])
            return s_loc + bias_ref[i, r - rs], k0

        partial = {}
        cur = window_scores(*tiles[0])
        for j, (dr, i) in enumerate(tiles):
            nxt = window_scores(*tiles[j + 1]) if j + 1 < len(tiles) else None
            s_loc, k0 = cur
            s_c = s_ctx[i][dr * tile_rows:(dr + 1) * tile_rows, :]
            m = jnp.maximum(jnp.max(s_loc, axis=-1, keepdims=True), jnp.max(s_c, axis=-1, keepdims=True))
            e_loc, e_ctx = jnp.exp(s_loc - m), jnp.exp(s_c - m)
            den = jnp.sum(e_loc, axis=-1, keepdims=True) + jnp.sum(e_ctx, axis=-1, keepdims=True)
            num = _dot(e_loc.astype(BF16), v_ref[0, pl.ds(k0, WIN_ROWS * GRID_W), pairs[i]])
            partial[dr, i] = (num, e_ctx.astype(BF16), den)
            cur = nxt

        for i, ps in enumerate(pairs):
            num_ctx = _dot(jnp.concatenate([partial[dr, i][1] for dr in range(rows_per_trip)], axis=0),
                           v_ref[0, ctx, ps])
            for dr in range(rows_per_trip):
                num, _, den = partial[dr, i]
                o = (num + num_ctx[dr * tile_rows:(dr + 1) * tile_rows, :]) / den
                o_ref[0, pl.ds(q0s[dr], GRID_W), ps] = unstack_pair(stacked[dr, i][0], o)
        return carry

    lax.fori_loop(0, GRID_ROWS // rows_per_trip, trip, 0)


def _attention(p, bias_tab, ctx_out):
    bsz = p.shape[0]
    return pl.pallas_call(
        functools.partial(_attn_body, ctx_out=ctx_out),
        grid=(bsz,),
        in_specs=[_group_spec(0), _group_spec(1), _group_spec(2), _resident(bias_tab.shape)],
        out_specs=pl.BlockSpec((1, T_ALL, GROUP_WIDTH), lambda b: (b, 0, 0)),
        out_shape=jax.ShapeDtypeStruct((bsz, T_ALL, GROUP_WIDTH), BF16),
        compiler_params=_params(1),
        name="nbr_attention",
    )(p, p, p, bias_tab)


def _attention_bias(rpb):
    c = jnp.arange(GRID_W)
    qstart = jnp.clip(c - WIN_COLS // 2, 0, GRID_W - WIN_COLS)
    in_win = (c[None, :] >= qstart[:, None]) & (c[None, :] < qstart[:, None] + WIN_COLS)
    col = jnp.clip(c[None, :] - c[:, None] + WIN_COLS - 1, 0, 2 * WIN_COLS - 2)
    pick = (col[:, :, None] == jnp.arange(2 * WIN_COLS - 1)).astype(F32)
    by_drow = jnp.einsum('hde,qke->hdqk', rpb.astype(F32), pick, precision=lax.Precision.HIGHEST)
    by_drow = jnp.where(in_win, by_drow, NEG_INF)
    tab = jnp.stack([by_drow[:, WIN_ROWS - 1 - v:2 * WIN_ROWS - 1 - v] for v in range(WIN_ROWS)], axis=1)
    tab = tab.transpose(0, 1, 3, 2, 4).reshape(GROUP_HEADS // 2, 2, WIN_ROWS, GRID_W, WIN_ROWS * GRID_W)
    return tab.transpose(0, 2, 1, 3, 4).reshape(GROUP_HEADS // 2, WIN_ROWS, 2 * GRID_W, WIN_ROWS * GRID_W)


def _gmlp_body(u_ref, z_ref, ws_ref, bs_ref, g_ref, o_ref):
    def chunk(n, carry):
        rows = pl.ds(pl.multiple_of(n * CHUNK, CHUNK), CHUNK)
        z = _rms(jax.nn.gelu(z_ref[0, rows, :].astype(F32)), g_ref[...]).astype(BF16)
        u = jax.nn.gelu(u_ref[0, rows, :].astype(F32))
        for h in range(GROUP_HEADS):
            hs = slice(h * HEAD_DIM, (h + 1) * HEAD_DIM)
            mixed = _dot(ws_ref[h], z[:, hs]) + bs_ref[:, hs]
            o_ref[0, rows, hs] = (u[:, hs] * mixed).astype(BF16)
        return carry

    lax.fori_loop(0, N_CHUNKS, chunk, 0, unroll=3)


def _gmlp(p, w_spatial, b_spatial, g_gmlp):
    bsz = p.shape[0]
    bias = jnp.repeat(b_spatial.T, HEAD_DIM, axis=1)
    return pl.pallas_call(
        _gmlp_body,
        grid=(bsz,),
        in_specs=[_group_spec(3), _group_spec(4),
                  _resident((GROUP_HEADS, CHUNK, CHUNK)),
                  _resident((CHUNK, GROUP_WIDTH)),
                  _resident((1, GROUP_WIDTH))],
        out_specs=pl.BlockSpec((1, T_ALL, GROUP_WIDTH), lambda b: (b, 0, 0)),
        out_shape=jax.ShapeDtypeStruct((bsz, T_ALL, GROUP_WIDTH), BF16),
        compiler_params=_params(1),
        name="gmlp",
    )(p, p, w_spatial.astype(BF16), bias, g_gmlp.reshape(1, GROUP_WIDTH))


def _mlstm_body(q_ref, k_ref, v_ref, og_ref, gate_ref, wc_ref, bg_ref, cos_ref, sina_ref,
                sinb_ref, gln_ref, tri_ref, o_ref,
                qtz_scr, k_scr, kbd_scr, vat_scr, acol_scr, rows_scr, ht_scr, c_scr, m_scr):
    half = HEAD_DIM // 4
    n_chains = 2 * GROUP_HEADS

    def conv_act(ref, n, w0, w1, w2, rope, post):
        rows = pl.ds(pl.multiple_of(n * CHUNK, CHUNK), CHUNK)
        cur = ref[0, rows, :].astype(F32)
        first = (n == 0) | (n == LAT_CHUNKS)
        last = (n == LAT_CHUNKS - 1) | (n == N_CHUNKS - 1)
        before = pl.ds(pl.multiple_of(jnp.maximum(n * CHUNK - BF16_ROWS, 0), BF16_ROWS), BF16_ROWS)
        after = pl.ds(pl.multiple_of(jnp.minimum((n + 1) * CHUNK, T_ALL - BF16_ROWS), BF16_ROWS), BF16_ROWS)
        tail = jnp.where(first, 0.0, 1.0) * ref[0, before, :][BF16_ROWS - 1:BF16_ROWS, :].astype(F32)
        head = jnp.where(last, 0.0, 1.0) * ref[0, after, :][0:1, :].astype(F32)
        sub = lax.broadcasted_iota(jnp.int32, (CHUNK, 1), 0)
        x_prev = jnp.where(sub == 0, tail, pltpu.roll(cur, 1, 0))
        x_next = jnp.where(sub == CHUNK - 1, head, pltpu.roll(cur, CHUNK - 1, 0))
        y = w0 * x_prev + w1 * cur + w2 * x_next
        y = y * _sigmoid(y)
        if rope:
            y = (y * cos_ref[rows, :] + pltpu.roll(y, GROUP_WIDTH - half, 1) * sina_ref[rows, :]
                 + pltpu.roll(y, half, 1) * sinb_ref[rows, :])
        return y * post

    def prep(n, rope):
        rows = pl.ds(pl.multiple_of(n * CHUNK, CHUNK), CHUNK)
        gw = GROUP_WIDTH
        q = conv_act(q_ref, n, wc_ref[0:1, :gw], wc_ref[1:2, :gw], wc_ref[2:3, :gw], rope, 1.0)
        k = conv_act(k_ref, n, wc_ref[0:1, gw:], wc_ref[1:2, gw:], wc_ref[2:3, gw:], rope, HEAD_DIM ** -0.5)
        q_t = q.T.astype(BF16)
        v_t = v_ref[0, rows, :].astype(F32).T.astype(BF16)
        k = k.astype(BF16)
        k_scr[n] = k
        sub = lax.broadcasted_iota(jnp.int32, (HEAD_DIM, CHUNK), 0)
        ones_row = jnp.where(sub == 0, 1.0, 0.0).astype(BF16)
        zeros = jnp.zeros((HEAD_DIM, CHUNK), BF16)
        low = lax.broadcasted_iota(jnp.int32, (CHUNK, LANES), 1) < HEAD_DIM
        for pair in range(GROUP_HEADS // 2):
            h0 = slice(2 * pair * HEAD_DIM, (2 * pair + 1) * HEAD_DIM)
            h1 = slice((2 * pair + 1) * HEAD_DIM, (2 * pair + 2) * HEAD_DIM)
            qtz_scr[n, pair] = jnp.concatenate([jnp.concatenate([q_t[h0, :], zeros], axis=1),
                                                jnp.concatenate([zeros, q_t[h1, :]], axis=1)], axis=0)
            k_pair = k[:, pair * LANES:(pair + 1) * LANES]
            kbd_scr[n, pair] = jnp.concatenate([jnp.where(low, k_pair, jnp.zeros_like(k_pair)),
                                                jnp.where(low, jnp.zeros_like(k_pair), k_pair)], axis=0)
            vat_scr[n, pair] = jnp.concatenate([jnp.concatenate([v_t[h0, :], ones_row], axis=0),
                                                jnp.concatenate([v_t[h1, :], ones_row], axis=0)], axis=1)
        ht_scr[n] = jnp.zeros((GROUP_WIDTH, CHUNK), F32)

        g_i = gate_ref[0, rows, :] + bg_ref[...]
        lf = pltpu.roll(jax.nn.log_sigmoid(g_i), LANES - n_chains, 1)
        pre = _dot_f32(tri_ref[...], lf)
        suf = pre[CHUNK - 1:CHUNK, :] - pre + lf
        lane = lax.broadcasted_iota(jnp.int32, (CHUNK, LANES), 1)
        b = jnp.where(lane % 4 < 2, pre, suf)
        a = g_i - b
        acol_scr[n] = a
        a_t = a.T[0:n_chains, :]
        b_t = b.T[0:n_chains, :]
        chain = lax.broadcasted_iota(jnp.int32, (n_chains, CHUNK), 0)
        b_end = jnp.where(chain % 4 < 2, b_t[:, CHUNK - 1:CHUNK], b_t[:, 0:1])
        b_end = jnp.broadcast_to(b_end, (n_chains, CHUNK))
        a_max = jnp.broadcast_to(jnp.max(a_t, axis=1, keepdims=True), (n_chains, CHUNK))
        for kind, rows8 in enumerate((a_t, b_t, b_end, a_max)):
            rows_scr[n, kind] = jnp.concatenate([rows8[0:4, :], rows8[4:8, :]], axis=1)

    def prep_ctx(n, carry):
        prep(n, False)
        return carry

    def prep_lat(n, carry):
        prep(n, True)
        return carry

    lax.fori_loop(0, LAT_CHUNKS, prep_lat, 0, unroll=2)
    lax.fori_loop(LAT_CHUNKS, N_CHUNKS, prep_ctx, 0)
    c_scr[...] = jnp.zeros(c_scr.shape, F32)
    m_scr[...] = jnp.zeros(m_scr.shape, F32)

    s_idx = lax.broadcasted_iota(jnp.int32, (CHUNK, CHUNK), 0)
    t_idx = lax.broadcasted_iota(jnp.int32, (CHUNK, CHUNK), 1)

    zeros = jnp.zeros((CHUNK, CHUNK), BF16)
    low = lax.broadcasted_iota(jnp.int32, (1, LANES), 1) < HEAD_DIM

    def scan(i, carry):
        chunk_of = (jnp.where(i < CTX_CHUNKS, LAT_CHUNKS + i, i - CTX_CHUNKS), N_CHUNKS - 1 - i)
        steps = [(rev, pair, chunk_of[rev]) for rev in range(2) for pair in range(GROUP_HEADS // 2)]

        scores = []
        for rev, pair, n in steps:
            dp = 2 * rev + pair
            causal = (s_idx >= t_idx) if rev else (s_idx <= t_idx)
            arg = jnp.concatenate([jnp.where(causal, acol_scr[n, :, dp:dp + 1], NEG_INF),
                                   jnp.where(causal, acol_scr[n, :, 4 + dp:5 + dp], NEG_INF)], axis=1)
            s = _dot(k_scr[n, :, pair * LANES:(pair + 1) * LANES], qtz_scr[n, pair])
            scores.append((arg, jnp.max(arg, axis=0, keepdims=True), s))

        inter = []
        for rev, pair, n in steps:
            dp = 2 * rev + pair
            a_row = rows_scr[n, 0, dp:dp + 1, :]
            b_end = rows_scr[n, 2, dp:dp + 1, :]
            a_max = rows_scr[n, 3, dp:dp + 1, :]
            m_mem = m_scr[dp:dp + 1, :]
            c_mem = c_scr[dp]
            inter.append((_dot(c_mem.astype(BF16), qtz_scr[n, pair]), m_mem))
            m_new = b_end + jnp.maximum(m_mem, a_max)
            w_src = jnp.exp(b_end + a_row - m_new)
            decay = jnp.exp(b_end + m_mem - m_new)
            decay = jnp.where(low, decay[:, :CHUNK], decay[:, CHUNK:])
            c_scr[dp] = decay * c_mem + _dot((vat_scr[n, pair].astype(F32) * w_src).astype(BF16),
                                             kbd_scr[n, pair])
            m_scr[dp:dp + 1, :] = m_new

        for (rev, pair, n), (arg, cm, s), (x2, m_mem) in zip(steps, scores, inter):
            dp = 2 * rev + pair
            g = (s * jnp.exp(arg - cm)).astype(BF16)
            g_bd = jnp.concatenate([jnp.concatenate([g[:, :CHUNK], zeros], axis=1),
                                    jnp.concatenate([zeros, g[:, CHUNK:]], axis=1)], axis=0)
            x1 = _dot(vat_scr[n, pair], g_bd)
            b_row = rows_scr[n, 1, dp:dp + 1, :]
            mu = jnp.maximum(cm, m_mem)
            both = x1 * jnp.exp(cm - mu) + x2 * jnp.exp(m_mem - mu)
            den = both[HEAD_DIM:HEAD_DIM + 1, :]
            inv = 1.0 / jnp.maximum(jnp.abs(den), jnp.exp(-(b_row + mu)))
            h_t = both[0:HEAD_DIM, :] * inv
            ht_scr[n, pair * LANES:(pair + 1) * LANES, :] += jnp.concatenate([h_t[:, :CHUNK], h_t[:, CHUNK:]],
                                                                             axis=0)
        return carry

    lax.fori_loop(0, N_CHUNKS, scan, 0, unroll=3)

    def finish(n, carry):
        rows = pl.ds(pl.multiple_of(n * CHUNK, CHUNK), CHUNK)
        gate = _sigmoid(og_ref[0, rows, :].astype(F32))
        ys = []
        for h in range(GROUP_HEADS):
            x = ht_scr[n, h * HEAD_DIM:(h + 1) * HEAD_DIM, :]
            mu = jnp.mean(x, axis=0, keepdims=True)
            var = jnp.mean(jnp.square(x - mu), axis=0, keepdims=True)
            ys.append((x - mu) * lax.rsqrt(var + EPS))
        y = jnp.concatenate(ys, axis=0).T * gln_ref[...]
        o_ref[0, rows, :] = (gate * y).astype(BF16)
        return carry

    lax.fori_loop(0, N_CHUNKS, finish, 0, unroll=3)


def _rope_tables():
    t = jnp.arange(SEQ)
    pos = jnp.stack([(t // GRID_W).astype(F32), (t % GRID_W).astype(F32)], axis=1)
    lane = jnp.arange(GROUP_WIDTH)
    m = HEAD_DIM // 4
    inv = ROPE_THETA ** (-(lane % m).astype(F32) / m)
    axis = (lane % HEAD_DIM) // (HEAD_DIM // 2)
    ang = jnp.where(axis[None, :] == 0, pos[:, 0:1], pos[:, 1:2]) * inv[None, :]
    low = ((lane % (2 * m)) < m)[None, :]
    cos, sin = jnp.cos(ang), jnp.sin(ang)
    return cos, jnp.where(low, -sin, 0.0), jnp.where(low, 0.0, sin)


def _mlstm(p, gates, w_conv, b_gate, g_mlstm, rope):
    bsz = p.shape[0]
    i = jnp.arange(CHUNK)
    tri = (i[:, None] >= i[None, :]).astype(F32)
    bg = jnp.pad(b_gate[jnp.array(GATE_I_COLS + GATE_F_COLS)], (0, LANES - N_GATES)).reshape(1, LANES)
    cos, sina, sinb = rope
    n_pairs = GROUP_HEADS // 2
    return pl.pallas_call(
        _mlstm_body,
        grid=(bsz,),
        in_specs=[_group_spec(5), _group_spec(6), _group_spec(7), _group_spec(8),
                  pl.BlockSpec((1, T_ALL, LANES), lambda b: (b, 0, 0)),
                  _resident((CONV_W, 2 * GROUP_WIDTH)),
                  _resident((1, LANES)),
                  _resident((SEQ, GROUP_WIDTH)), _resident((SEQ, GROUP_WIDTH)), _resident((SEQ, GROUP_WIDTH)),
                  _resident((1, GROUP_WIDTH)),
                  _resident((CHUNK, CHUNK))],
        out_specs=pl.BlockSpec((1, T_ALL, GROUP_WIDTH), lambda b: (b, 0, 0)),
        out_shape=jax.ShapeDtypeStruct((bsz, T_ALL, GROUP_WIDTH), BF16),
        scratch_shapes=[pltpu.VMEM((N_CHUNKS, n_pairs, LANES, 2 * CHUNK), BF16),
                        pltpu.VMEM((N_CHUNKS, CHUNK, GROUP_WIDTH), BF16),
                        pltpu.VMEM((N_CHUNKS, n_pairs, 2 * CHUNK, LANES), BF16),
                        pltpu.VMEM((N_CHUNKS, n_pairs, LANES, 2 * CHUNK), BF16),
                        pltpu.VMEM((N_CHUNKS, CHUNK, LANES), F32),
                        pltpu.VMEM((N_CHUNKS, 4, 2 * n_pairs, 2 * CHUNK), F32),
                        pltpu.VMEM((N_CHUNKS, GROUP_WIDTH, CHUNK), F32),
                        pltpu.VMEM((2 * n_pairs, LANES, LANES), F32),
                        pltpu.VMEM((2 * n_pairs, 2 * CHUNK), F32)],
        compiler_params=_params(1),
        name="mlstm",
    )(p, p, p, p, gates, w_conv, bg, cos, sina, sinb, g_mlstm.reshape(1, GROUP_WIDTH), tri)


def _dft_tables(n, scale):
    f = 1 << (n.bit_length() // 2)
    s = jnp.arange(n, dtype=jnp.int32)[None, :]
    ang_a = ((f * jnp.arange(n // f, dtype=jnp.int32)[:, None] * s) % n).astype(F32) * (2.0 * jnp.pi / n)
    ang_b = ((jnp.arange(f, dtype=jnp.int32)[:, None] * s) % n).astype(F32) * (2.0 * jnp.pi / n)
    ca, sa = jnp.cos(ang_a)[:, None, :], jnp.sin(ang_a)[:, None, :]
    cb, sb = jnp.cos(ang_b)[None, :, :], jnp.sin(ang_b)[None, :, :]
    cos = (ca * cb - sa * sb).reshape(n, n)
    sin = (sa * cb + ca * sb).reshape(n, n)
    return (cos * scale).astype(BF16), (-sin * scale).astype(BF16)


def _channel_dft():
    gc = GROUP_WIDTH // FNET_GROUPS
    j = jnp.arange(GROUP_WIDTH, dtype=jnp.int32)
    same = (j[:, None] // gc) == (j[None, :] // gc)
    ang = (((j[:, None] % gc) * (j[None, :] % gc)) % gc).astype(F32) * (2.0 * jnp.pi / gc)
    c = jnp.where(same, jnp.cos(ang), 0.0)
    s = jnp.where(same, jnp.sin(ang), 0.0)
    return jnp.concatenate([c, s], axis=1).astype(BF16)


def _fnet_body(zc_ref, zs_ref, cl_ref, sl_ref, cc_ref, sc_ref, w_ref, o_ref):
    y = _dot(cl_ref[...], zc_ref[0:SEQ, :]) + _dot(sl_ref[...], zs_ref[0:SEQ, :])
    o_ref[0:SEQ, :] = _dot(y.astype(BF16), w_ref[...]).astype(BF16)
    y = _dot(cc_ref[...], zc_ref[SEQ:, :]) + _dot(sc_ref[...], zs_ref[SEQ:, :])
    o_ref[SEQ:, :] = _dot(y.astype(BF16), w_ref[...]).astype(BF16)


def _fnet(zc, zs, tabs, w_fnet):
    bsz = zc.shape[1] // GROUP_WIDTH
    cl, sl, cc, sc = tabs
    col = pl.BlockSpec((T_ALL, GROUP_WIDTH), lambda b: (0, b))
    return pl.pallas_call(
        _fnet_body,
        grid=(bsz,),
        in_specs=[col, col, _resident((SEQ, SEQ)), _resident((SEQ, SEQ)),
                  _resident((CTX_LEN, CTX_LEN)), _resident((CTX_LEN, CTX_LEN)),
                  _resident((GROUP_WIDTH, GROUP_WIDTH))],
        out_specs=col,
        out_shape=jax.ShapeDtypeStruct((T_ALL, bsz * GROUP_WIDTH), BF16),
        compiler_params=_params(1),
        name="fnet",
    )(zc, zs, cl, sl, cc, sc, w_fnet.astype(BF16))


def _post_body(a_ref, b_ref, c_ref, d_ref, x_ref, ml_ref, mc_ref, g_ref, wo_ref, w1_ref, w2_ref, gf_ref,
               o_ref, h_scr, acc_scr, *, final):
    gw = GROUP_WIDTH
    mod = _mod_rows(ml_ref, mc_ref, x_ref.shape[1])
    y = (_dot(a_ref[0], wo_ref[0:gw, :]) + _dot(b_ref[0], wo_ref[gw:2 * gw, :])
         + _dot(c_ref[0], wo_ref[2 * gw:3 * gw, :]) + _dot(d_ref[...], wo_ref[3 * gw:, :]))
    x1 = x_ref[0] + mod(2) * y
    h = _rms(x1, g_ref[...]) * (1.0 + mod(4)) + mod(3)
    h_scr[...] = h.astype(BF16)
    step = 512

    def up(j):
        return jnp.maximum(_dot(h_scr[...], w1_ref[:, j:j + step]), 0.0)

    a = up(0)
    for j in range(0, D_FF, step):
        nxt = up(j + step) if j + step < D_FF else None
        part = _dot((a * a).astype(BF16), w2_ref[j:j + step, :])
        if j == 0:
            acc_scr[...] = part
        else:
            acc_scr[...] += part
        a = nxt
    x2 = x1 + mod(5) * acc_scr[...]
    if final:
        x2 = _rms(x2, gf_ref[...])
    o_ref[0] = x2


def _post(a, b_, c_, d, xx, mod_l, g_ffn, w_out, w_ff1, w_ff2, g_final, final):
    bsz = xx.shape[0]
    tm, out_len = (LATENT_TILE, SEQ) if final else (TOKEN_TILE, T_ALL)
    tok = pl.BlockSpec((1, tm, D_MODEL), lambda b, t: (b, t, 0))
    grp = pl.BlockSpec((1, tm, GROUP_WIDTH), lambda b, t: (b, t, 0))
    return pl.pallas_call(
        functools.partial(_post_body, final=final),
        grid=(bsz, out_len // tm),
        in_specs=[grp, grp, grp, pl.BlockSpec((tm, GROUP_WIDTH), lambda b, t: (t, b)), tok] + _MOD_SPECS
                 + [_resident((1, D_MODEL)),
                    _resident((D_MODEL, D_MODEL)),
                    _resident((D_MODEL, D_FF)),
                    _resident((D_FF, D_MODEL)),
                    _resident((1, D_MODEL))],
        out_specs=tok,
        out_shape=jax.ShapeDtypeStruct((bsz, out_len, D_MODEL), F32),
        scratch_shapes=[pltpu.VMEM((tm, D_MODEL), BF16), pltpu.VMEM((tm, D_MODEL), F32)],
        compiler_params=_params(2),
        name="out_proj_mlp",
    )(a, b_, c_, d, xx, mod_l, mod_l, g_ffn.reshape(1, D_MODEL), w_out.astype(BF16), w_ff1.astype(BF16),
      w_ff2.astype(BF16), g_final.reshape(1, D_MODEL))


def kernel(x, c, ctx, c_ctx, w_ada, b_ada, g_norm_mix, g_norm_ffn, w_in, b_gate, w_conv_qk, rpb, w_spatial,
           b_spatial, g_gmlp, g_mlstm, w_fnet, w_out, w_ff1, w_ff2, g_final):
    bsz = x.shape[0]
    assert bsz <= CTX_MOD_ROW and x.shape[1:] == (SEQ, D_MODEL) and ctx.shape[1:] == (CTX_LEN, D_MODEL)
    depth = w_ada.shape[0]
    xx = jnp.concatenate([x, ctx], axis=1)
    cc = jnp.zeros((MOD_ROWS, D_MODEL), F32).at[:bsz].set(c).at[CTX_MOD_ROW].set(c_ctx)
    mod = _ada_mod(cc, w_ada, b_ada).reshape(depth, MOD_ROWS, N_MOD, D_MODEL)

    rope = _rope_tables()
    dft_c = _channel_dft()
    tabs = (_dft_tables(SEQ, (SEQ * GROUP_WIDTH // FNET_GROUPS) ** -0.5)
            + _dft_tables(CTX_LEN, (CTX_LEN * GROUP_WIDTH // FNET_GROUPS) ** -0.5))

    for l in range(depth):
        w_cat = _projection_weights(w_in[l])
        p, gates, zc, zs = _in_proj(xx, mod[l], g_norm_mix[l], w_cat, dft_c)
        a = _attention(p, _attention_bias(rpb[l]), ctx_out=(l < depth - 1))
        b_ = _gmlp(p, w_spatial[l], b_spatial[l], g_gmlp[l])
        c_ = _mlstm(p, gates, w_conv_qk[l], b_gate[l], g_mlstm[l], rope)
        d = _fnet(zc, zs, tabs, w_fnet[l])
        xx = _post(a, b_, c_, d, xx, mod[l], g_norm_ffn[l], w_out[l], w_ff1[l], w_ff2[l], g_final,
                   final=(l == depth - 1))
    return xx
```

```python
import functools

import jax
import jax.numpy as jnp
from jax import lax
from jax.experimental import pallas as pl
from jax.experimental.pallas import tpu as pltpu

D_MODEL = 1024
SEQ = 2048
DEPTH = 2
GRID_W = 64
GRID_ROWS = SEQ // GRID_W
CTX_LEN = 256
T_ALL = CTX_LEN + SEQ
HEAD_DIM = 64
GROUP_WIDTH = 256
GROUP_HEADS = 4
WIN_ROWS = 8
WIN_COLS = 16
CHUNK = 128
N_CHUNKS = T_ALL // CHUNK
CTX_CHUNKS = CTX_LEN // CHUNK
LAT_CHUNKS = SEQ // CHUNK
CONV_W = 3
FNET_GROUPS = 4
ROPE_THETA = 10000.0
D_FF = 4 * D_MODEL
N_MOD = 6
EPS = 1e-6
NEG_INF = -1e30
N_GATES = 4 * GROUP_HEADS
OFF_G = 10 * GROUP_WIDTH
D_IN = OFF_G + N_GATES
MOD_ROWS = 24
CTX_MOD_ROW = 16
LANES = 128
BF16_ROWS = 16
GATE_I_COLS = (0, 2, 8, 10, 1, 3, 9, 11)
GATE_F_COLS = (4, 6, 12, 14, 5, 7, 13, 15)
TOKEN_TILE = 768
PROJ_TILE = 1152
LATENT_TILE = 512
P_WIDTH = 9 * GROUP_WIDTH
VMEM_LIMIT = 56 * 1024 * 1024

F32 = jnp.float32
BF16 = jnp.bfloat16


def _dot(a, b):
    return jnp.dot(a, b, preferred_element_type=F32)


def _dot_nt(a, b):
    return lax.dot_general(a, b, (((1,), (1,)), ((), ())), preferred_element_type=F32)


def _dot_f32(a, b):
    return jnp.dot(a, b, preferred_element_type=F32, precision=lax.Precision.HIGHEST)


def _sigmoid(x):
    return 0.5 * (1.0 + jnp.tanh(0.5 * x))


def _resident(shape):
    nd = len(shape)
    return pl.BlockSpec(shape, lambda *_: (0,) * nd, pipeline_mode=pl.Buffered(1))


def _params(n_axes):
    return pltpu.CompilerParams(dimension_semantics=("arbitrary",) * n_axes,
                                vmem_limit_bytes=VMEM_LIMIT)


def _mod_body(c_ref, w_ref, b_ref, o_ref):
    s = c_ref[...]
    s = s * jax.nn.sigmoid(s)
    o_ref[0] = _dot(s.astype(BF16), w_ref[0].astype(BF16)) + b_ref[0]


def _ada_mod(cc, w_ada, b_ada):
    depth, d, n = w_ada.shape
    tn = 1536
    return pl.pallas_call(
        _mod_body,
        grid=(depth, n // tn),
        in_specs=[pl.BlockSpec((MOD_ROWS, d), lambda l, j: (0, 0)),
                  pl.BlockSpec((1, d, tn), lambda l, j: (l, 0, j)),
                  pl.BlockSpec((1, 1, tn), lambda l, j: (l, 0, j))],
        out_specs=pl.BlockSpec((1, MOD_ROWS, tn), lambda l, j: (l, 0, j)),
        out_shape=jax.ShapeDtypeStruct((depth, MOD_ROWS, n), F32),
        compiler_params=_params(2),
        name="ada_mod",
    )(cc, w_ada, b_ada.reshape(depth, 1, n))


_MOD_SPECS = [pl.BlockSpec((1, N_MOD, D_MODEL), lambda b, t: (b, 0, 0)),
              pl.BlockSpec((1, N_MOD, D_MODEL), lambda b, t: (CTX_MOD_ROW, 0, 0))]


def _mod_rows(mod_lat_ref, mod_ctx_ref, tile):
    tok = pl.program_id(1) * tile + lax.broadcasted_iota(jnp.int32, (tile, 1), 0)
    is_ctx = tok >= SEQ
    return lambda k: jnp.where(is_ctx, mod_ctx_ref[0, k:k + 1, :], mod_lat_ref[0, k:k + 1, :])


def _rms(x, g):
    return x * lax.rsqrt(jnp.mean(x * x, axis=-1, keepdims=True) + EPS) * g


def _tail_rows(x_ref, c_ref, ctx_tail):
    head = x_ref.shape[1] - CTX_LEN
    return jnp.where(ctx_tail, c_ref[0], x_ref[0, head:, :])


def _inproj_body(x_ref, c_ref, ml_ref, mc_ref, g_ref, w_ref, dft_ref, p_ref, gate_ref, zc_ref, zs_ref,
                 lhs_scr, inv_scr):
    tm = x_ref.shape[1]
    head = tm - CTX_LEN
    ctx_tail = pl.program_id(1) == pl.num_programs(1) - 1
    shift_tail = jnp.where(ctx_tail, mc_ref[0, 0:1, :], ml_ref[0, 0:1, :])
    scale_tail = jnp.where(ctx_tail, mc_ref[0, 1:2, :], ml_ref[0, 1:2, :])
    lead = BF16_ROWS
    lhs_scr[0:lead, :] = jnp.concatenate([ml_ref[0, 0:1, :], shift_tail,
                                          jnp.zeros((lead - 2, D_MODEL), F32)], axis=0).astype(BF16)
    x_head = x_ref[0, 0:head, :]
    x_tail = _tail_rows(x_ref, c_ref, ctx_tail)
    lhs_scr[lead:lead + head, :] = (x_head * (g_ref[...] * (1.0 + ml_ref[0, 1:2, :]))).astype(BF16)
    lhs_scr[lead + head:, :] = (x_tail * (g_ref[...] * (1.0 + scale_tail))).astype(BF16)
    for rows, x in ((slice(0, head), x_head), (slice(head, tm), x_tail)):
        inv = lax.rsqrt(jnp.mean(x * x, axis=-1, keepdims=True) + EPS)
        inv_scr[rows, :] = jnp.broadcast_to(inv, (x.shape[0], LANES))

    def project(cols):
        n = (cols.stop - cols.start) // LANES
        r = _dot(lhs_scr[...], w_ref[:, cols])
        inv_head = jnp.concatenate([inv_scr[0:head, :]] * n, axis=1)
        inv_tail = jnp.concatenate([inv_scr[head:, :]] * n, axis=1)
        return jnp.concatenate([inv_head * r[lead:lead + head, :] + r[0:1, :],
                                inv_tail * r[lead + head:, :] + r[1:2, :]], axis=0)

    step = 2 * GROUP_WIDTH
    f = project(slice(P_WIDTH, OFF_G)).astype(BF16)
    p_ref[0, :, 0:step] = project(slice(0, step)).astype(BF16)
    z = _dot(f, dft_ref[...])
    zc_ref[...] = z[:, :GROUP_WIDTH].astype(BF16)
    zs_ref[...] = z[:, GROUP_WIDTH:].astype(BF16)
    for j in range(step, P_WIDTH - GROUP_WIDTH, step):
        p_ref[0, :, j:j + step] = project(slice(j, j + step)).astype(BF16)
    j = P_WIDTH - GROUP_WIDTH
    p_ref[0, :, j:] = project(slice(j, P_WIDTH)).astype(BF16)
    gate_ref[0] = project(slice(OFF_G, OFF_G + LANES))


def _projection_weights(w_in):
    pad = jnp.zeros((w_in.shape[0], LANES - N_GATES), w_in.dtype)
    w_g = w_in[:, OFF_G:]
    return jnp.concatenate([w_in[:, :OFF_G], w_g[:, jnp.array(GATE_I_COLS + GATE_F_COLS)], pad],
                           axis=1).astype(BF16)


def _token_operands(tokens, ctx, tm):
    tile = pl.BlockSpec((1, tm, D_MODEL), lambda b, t: (b, t, 0))
    if ctx is None:
        return (tokens, tokens), [tile, pl.BlockSpec((1, CTX_LEN, D_MODEL), lambda b, t: (b, SEQ // CTX_LEN, 0))]
    return (tokens, ctx), [tile, pl.BlockSpec((1, CTX_LEN, D_MODEL), lambda b, t: (b, 0, 0))]


def _in_proj(tokens, ctx, mod_l, g, w_cat, dft_c):
    bsz = tokens.shape[0]
    tm = PROJ_TILE
    wn = w_cat.shape[1]
    operands, token_specs = _token_operands(tokens, ctx, tm)
    return pl.pallas_call(
        _inproj_body,
        grid=(bsz, T_ALL // tm),
        in_specs=token_specs + _MOD_SPECS
                 + [_resident((1, D_MODEL)),
                    _resident((D_MODEL, wn)),
                    _resident((GROUP_WIDTH, 2 * GROUP_WIDTH))],
        out_specs=[pl.BlockSpec((1, tm, P_WIDTH), lambda b, t: (b, t, 0)),
                   pl.BlockSpec((1, tm, LANES), lambda b, t: (b, t, 0)),
                   pl.BlockSpec((tm, GROUP_WIDTH), lambda b, t: (t, b)),
                   pl.BlockSpec((tm, GROUP_WIDTH), lambda b, t: (t, b))],
        out_shape=[jax.ShapeDtypeStruct((bsz, T_ALL, P_WIDTH), BF16),
                   jax.ShapeDtypeStruct((bsz, T_ALL, LANES), F32),
                   jax.ShapeDtypeStruct((T_ALL, bsz * GROUP_WIDTH), BF16),
                   jax.ShapeDtypeStruct((T_ALL, bsz * GROUP_WIDTH), BF16)],
        scratch_shapes=[pltpu.VMEM((tm + BF16_ROWS, D_MODEL), BF16), pltpu.VMEM((tm, LANES), F32)],
        compiler_params=_params(2),
        name="in_proj",
    )(*operands, mod_l, mod_l, g.reshape(1, D_MODEL), w_cat, dft_c)


def _group_spec(col_block):
    return pl.BlockSpec((1, T_ALL, GROUP_WIDTH), lambda b: (b, 0, col_block))


def _softmax_pv(parts):
    m = functools.reduce(jnp.maximum, [jnp.max(s, axis=-1, keepdims=True) for s, _ in parts])
    es = [jnp.exp(s - m) for s, _ in parts]
    den = functools.reduce(jnp.add, [jnp.sum(e, axis=-1, keepdims=True) for e in es])
    num = functools.reduce(jnp.add, [_dot(e.astype(BF16), v) for e, (_, v) in zip(es, parts)])
    return num / den


def _attn_body(q_ref, k_ref, v_ref, bias_ref, o_ref, *, ctx_out):
    scale = HEAD_DIM ** -0.5
    pairs = [slice(i * LANES, (i + 1) * LANES) for i in range(GROUP_WIDTH // LANES)]

    def stack_pair(q2):
        low = lax.broadcasted_iota(jnp.int32, q2.shape, 1) < HEAD_DIM
        zero = jnp.zeros_like(q2)
        return low, jnp.concatenate([jnp.where(low, q2, zero), jnp.where(low, zero, q2)], axis=0)

    def unstack_pair(low, o):
        n = o.shape[0] // 2
        return jnp.where(low, o[:n], o[n:]).astype(BF16)

    ctx = slice(SEQ, T_ALL)
    if ctx_out:
        ctx_tiles = []
        for ps in pairs:
            low, qm = stack_pair(q_ref[0, ctx, ps] * scale)
            ctx_tiles.append((low, _dot_nt(qm, k_ref[0, ctx, ps])))
        for ps, (low, s) in zip(pairs, ctx_tiles):
            o_ref[0, ctx, ps] = unstack_pair(low, _softmax_pv([(s, v_ref[0, ctx, ps])]))
    else:
        o_ref[0, ctx, :] = jnp.zeros((CTX_LEN, GROUP_WIDTH), BF16)

    rows_per_trip = 4
    tile_rows = 2 * GRID_W

    def trip(g, carry):
        tiles = [(dr, i) for dr in range(rows_per_trip) for i in range(len(pairs))]
        q0s = [pl.multiple_of((g * rows_per_trip + dr) * GRID_W, GRID_W) for dr in range(rows_per_trip)]
        stacked = {(dr, i): stack_pair(q_ref[0, pl.ds(q0s[dr], GRID_W), pairs[i]] * scale) for dr, i in tiles}
        s_ctx = [_dot_nt(jnp.concatenate([stacked[dr, i][1] for dr in range(rows_per_trip)], axis=0),
                         k_ref[0, ctx, pairs[i]]) for i in range(len(pairs))]

        def window_scores(dr, i):
            r = g * rows_per_trip + dr
            rs = jnp.clip(r - WIN_ROWS // 2, 0, GRID_ROWS - WIN_ROWS)
            k0 = pl.multiple_of(rs * GRID_W, GRID_W)
            s_loc = _dot_nt(stacked[dr, i][1], k_ref[0, pl.ds(k0, WIN_ROWS * GRID_W), pairs[i]])
            return s_loc + bias_ref[i, r - rs], k0

        partial = {}
        cur = window_scores(*tiles[0])
        for j, (dr, i) in enumerate(tiles):
            nxt = window_scores(*tiles[j + 1]) if j + 1 < len(tiles) else None
            s_loc, k0 = cur
            s_c = s_ctx[i][dr * tile_rows:(dr + 1) * tile_rows, :]
            m = jnp.maximum(jnp.max(s_loc, axis=-1, keepdims=True), jnp.max(s_c, axis=-1, keepdims=True))
            e_loc, e_ctx = jnp.exp(s_loc - m), jnp.exp(s_c - m)
            den = jnp.sum(e_loc, axis=-1, keepdims=True) + jnp.sum(e_ctx, axis=-1, keepdims=True)
            num = _dot(e_loc.astype(BF16), v_ref[0, pl.ds(k0, WIN_ROWS * GRID_W), pairs[i]])
            partial[dr, i] = (num, e_ctx.astype(BF16), den)
            cur = nxt

        for i, ps in enumerate(pairs):
            num_ctx = _dot(jnp.concatenate([partial[dr, i][1] for dr in range(rows_per_trip)], axis=0),
                           v_ref[0, ctx, ps])
            for dr in range(rows_per_trip):
                num, _, den = partial[dr, i]
                o = (num + num_ctx[dr * tile_rows:(dr + 1) * tile_rows, :]) / den
                o_ref[0, pl.ds(q0s[dr], GRID_W), ps] = unstack_pair(stacked[dr, i][0], o)
        return carry

    lax.fori_loop(0, GRID_ROWS // rows_per_trip, trip, 0)


def _attention(p, bias_tab, ctx_out):
    bsz = p.shape[0]
    return pl.pallas_call(
        functools.partial(_attn_body, ctx_out=ctx_out),
        grid=(bsz,),
        in_specs=[_group_spec(0), _group_spec(1), _group_spec(2), _resident(bias_tab.shape)],
        out_specs=pl.BlockSpec((1, T_ALL, GROUP_WIDTH), lambda b: (b, 0, 0)),
        out_shape=jax.ShapeDtypeStruct((bsz, T_ALL, GROUP_WIDTH), BF16),
        compiler_params=_params(1),
        name="nbr_attention",
    )(p, p, p, bias_tab)


def _attention_bias(rpb):
    c = jnp.arange(GRID_W)
    qstart = jnp.clip(c - WIN_COLS // 2, 0, GRID_W - WIN_COLS)
    in_win = (c[None, :] >= qstart[:, None]) & (c[None, :] < qstart[:, None] + WIN_COLS)
    col = jnp.clip(c[None, :] - c[:, None] + WIN_COLS - 1, 0, 2 * WIN_COLS - 2)
    pick = (col[:, :, None] == jnp.arange(2 * WIN_COLS - 1)).astype(F32)
    by_drow = jnp.einsum('hde,qke->hdqk', rpb.astype(F32), pick, precision=lax.Precision.HIGHEST)
    by_drow = jnp.where(in_win, by_drow, NEG_INF)
    tab = jnp.stack([by_drow[:, WIN_ROWS - 1 - v:2 * WIN_ROWS - 1 - v] for v in range(WIN_ROWS)], axis=1)
    tab = tab.transpose(0, 1, 3, 2, 4).reshape(GROUP_HEADS // 2, 2, WIN_ROWS, GRID_W, WIN_ROWS * GRID_W)
    return tab.transpose(0, 2, 1, 3, 4).reshape(GROUP_HEADS // 2, WIN_ROWS, 2 * GRID_W, WIN_ROWS * GRID_W)


def _gmlp_body(u_ref, z_ref, ws_ref, bs_ref, g_ref, o_ref):
    def chunk(n, carry):
        rows = pl.ds(pl.multiple_of(n * CHUNK, CHUNK), CHUNK)
        z = _rms(jax.nn.gelu(z_ref[0, rows, :].astype(F32)), g_ref[...]).astype(BF16)
        u = jax.nn.gelu(u_ref[0, rows, :].astype(F32))
        for h in range(GROUP_HEADS):
            hs = slice(h * HEAD_DIM, (h + 1) * HEAD_DIM)
            mixed = _dot(ws_ref[h], z[:, hs]) + bs_ref[:, hs]
            o_ref[0, rows, hs] = (u[:, hs] * mixed).astype(BF16)
        return carry

    lax.fori_loop(0, N_CHUNKS, chunk, 0, unroll=3)


def _gmlp(p, w_spatial, b_spatial, g_gmlp):
    bsz = p.shape[0]
    bias = jnp.repeat(b_spatial.T, HEAD_DIM, axis=1)
    return pl.pallas_call(
        _gmlp_body,
        grid=(bsz,),
        in_specs=[_group_spec(3), _group_spec(4),
                  _resident((GROUP_HEADS, CHUNK, CHUNK)),
                  _resident((CHUNK, GROUP_WIDTH)),
                  _resident((1, GROUP_WIDTH))],
        out_specs=pl.BlockSpec((1, T_ALL, GROUP_WIDTH), lambda b: (b, 0, 0)),
        out_shape=jax.ShapeDtypeStruct((bsz, T_ALL, GROUP_WIDTH), BF16),
        compiler_params=_params(1),
        name="gmlp",
    )(p, p, w_spatial.astype(BF16), bias, g_gmlp.reshape(1, GROUP_WIDTH))


def _mlstm_body(q_ref, k_ref, v_ref, og_ref, gate_ref, wc_ref, bg_ref, cos_ref, sina_ref,
                sinb_ref, gln_ref, tri_ref, o_ref,
                qtz_scr, k_scr, kbd_scr, vat_scr, acol_scr, rows_scr, ht_scr, c_scr, m_scr):
    half = HEAD_DIM // 4
    n_chains = 2 * GROUP_HEADS

    def conv_act(ref, n, w0, w1, w2, rope, post):
        rows = pl.ds(pl.multiple_of(n * CHUNK, CHUNK), CHUNK)
        cur = ref[0, rows, :].astype(F32)
        first = (n == 0) | (n == LAT_CHUNKS)
        last = (n == LAT_CHUNKS - 1) | (n == N_CHUNKS - 1)
        before = pl.ds(pl.multiple_of(jnp.maximum(n * CHUNK - BF16_ROWS, 0), BF16_ROWS), BF16_ROWS)
        after = pl.ds(pl.multiple_of(jnp.minimum((n + 1) * CHUNK, T_ALL - BF16_ROWS), BF16_ROWS), BF16_ROWS)
        tail = jnp.where(first, 0.0, 1.0) * ref[0, before, :][BF16_ROWS - 1:BF16_ROWS, :].astype(F32)
        head = jnp.where(last, 0.0, 1.0) * ref[0, after, :][0:1, :].astype(F32)
        sub = lax.broadcasted_iota(jnp.int32, (CHUNK, 1), 0)
        x_prev = jnp.where(sub == 0, tail, pltpu.roll(cur, 1, 0))
        x_next = jnp.where(sub == CHUNK - 1, head, pltpu.roll(cur, CHUNK - 1, 0))
        y = w0 * x_prev + w1 * cur + w2 * x_next
        y = y * _sigmoid(y)
        if rope:
            y = (y * cos_ref[rows, :] + pltpu.roll(y, GROUP_WIDTH - half, 1) * sina_ref[rows, :]
                 + pltpu.roll(y, half, 1) * sinb_ref[rows, :])
        return y * post

    def prep(n, rope):
        rows = pl.ds(pl.multiple_of(n * CHUNK, CHUNK), CHUNK)
        gw = GROUP_WIDTH
        q = conv_act(q_ref, n, wc_ref[0:1, :gw], wc_ref[1:2, :gw], wc_ref[2:3, :gw], rope, 1.0)
        k = conv_act(k_ref, n, wc_ref[0:1, gw:], wc_ref[1:2, gw:], wc_ref[2:3, gw:], rope, HEAD_DIM ** -0.5)
        q_t = q.T.astype(BF16)
        v_t = v_ref[0, rows, :].astype(F32).T.astype(BF16)
        k = k.astype(BF16)
        k_scr[n] = k
        sub = lax.broadcasted_iota(jnp.int32, (HEAD_DIM, CHUNK), 0)
        ones_row = jnp.where(sub == 0, 1.0, 0.0).astype(BF16)
        zeros = jnp.zeros((HEAD_DIM, CHUNK), BF16)
        low = lax.broadcasted_iota(jnp.int32, (CHUNK, LANES), 1) < HEAD_DIM
        for pair in range(GROUP_HEADS // 2):
            h0 = slice(2 * pair * HEAD_DIM, (2 * pair + 1) * HEAD_DIM)
            h1 = slice((2 * pair + 1) * HEAD_DIM, (2 * pair + 2) * HEAD_DIM)
            qtz_scr[n, pair] = jnp.concatenate([jnp.concatenate([q_t[h0, :], zeros], axis=1),
                                                jnp.concatenate([zeros, q_t[h1, :]], axis=1)], axis=0)
            k_pair = k[:, pair * LANES:(pair + 1) * LANES]
            kbd_scr[n, pair] = jnp.concatenate([jnp.where(low, k_pair, jnp.zeros_like(k_pair)),
                                                jnp.where(low, jnp.zeros_like(k_pair), k_pair)], axis=0)
            vat_scr[n, pair] = jnp.concatenate([jnp.concatenate([v_t[h0, :], ones_row], axis=0),
                                                jnp.concatenate([v_t[h1, :], ones_row], axis=0)], axis=1)
        ht_scr[n] = jnp.zeros((GROUP_WIDTH, CHUNK), F32)

        g_i = gate_ref[0, rows, :] + bg_ref[...]
        lf = pltpu.roll(jax.nn.log_sigmoid(g_i), LANES - n_chains, 1)
        pre = _dot_f32(tri_ref[...], lf)
        suf = pre[CHUNK - 1:CHUNK, :] - pre + lf
        lane = lax.broadcasted_iota(jnp.int32, (CHUNK, LANES), 1)
        b = jnp.where(lane % 4 < 2, pre, suf)
        a = g_i - b
        acol_scr[n] = a
        a_t = a.T[0:n_chains, :]
        b_t = b.T[0:n_chains, :]
        chain = lax.broadcasted_iota(jnp.int32, (n_chains, CHUNK), 0)
        b_end = jnp.where(chain % 4 < 2, b_t[:, CHUNK - 1:CHUNK], b_t[:, 0:1])
        b_end = jnp.broadcast_to(b_end, (n_chains, CHUNK))
        a_max = jnp.broadcast_to(jnp.max(a_t, axis=1, keepdims=True), (n_chains, CHUNK))
        for kind, rows8 in enumerate((a_t, b_t, b_end, a_max)):
            rows_scr[n, kind] = jnp.concatenate([rows8[0:4, :], rows8[4:8, :]], axis=1)

    def prep_ctx(n, carry):
        prep(n, False)
        return carry

    def prep_lat(n, carry):
        prep(n, True)
        return carry

    lax.fori_loop(0, LAT_CHUNKS, prep_lat, 0, unroll=2)
    lax.fori_loop(LAT_CHUNKS, N_CHUNKS, prep_ctx, 0)
    c_scr[...] = jnp.zeros(c_scr.shape, F32)
    m_scr[...] = jnp.zeros(m_scr.shape, F32)

    s_idx = lax.broadcasted_iota(jnp.int32, (CHUNK, CHUNK), 0)
    t_idx = lax.broadcasted_iota(jnp.int32, (CHUNK, CHUNK), 1)

    zeros = jnp.zeros((CHUNK, CHUNK), BF16)
    low = lax.broadcasted_iota(jnp.int32, (1, LANES), 1) < HEAD_DIM

    def scan(i, carry):
        chunk_of = (jnp.where(i < CTX_CHUNKS, LAT_CHUNKS + i, i - CTX_CHUNKS), N_CHUNKS - 1 - i)
        steps = [(rev, pair, chunk_of[rev]) for rev in range(2) for pair in range(GROUP_HEADS // 2)]

        scores = []
        for rev, pair, n in steps:
            dp = 2 * rev + pair
            causal = (s_idx >= t_idx) if rev else (s_idx <= t_idx)
            arg = jnp.concatenate([jnp.where(causal, acol_scr[n, :, dp:dp + 1], NEG_INF),
                                   jnp.where(causal, acol_scr[n, :, 4 + dp:5 + dp], NEG_INF)], axis=1)
            s = _dot(k_scr[n, :, pair * LANES:(pair + 1) * LANES], qtz_scr[n, pair])
            scores.append((arg, jnp.max(arg, axis=0, keepdims=True), s))

        inter = []
        for rev, pair, n in steps:
            dp = 2 * rev + pair
            a_row = rows_scr[n, 0, dp:dp + 1, :]
            b_end = rows_scr[n, 2, dp:dp + 1, :]
            a_max = rows_scr[n, 3, dp:dp + 1, :]
            m_mem = m_scr[dp:dp + 1, :]
            c_mem = c_scr[dp]
            inter.append((_dot(c_mem.astype(BF16), qtz_scr[n, pair]), m_mem))
            m_new = b_end + jnp.maximum(m_mem, a_max)
            w_src = jnp.exp(b_end + a_row - m_new)
            decay = jnp.exp(b_end + m_mem - m_new)
            decay = jnp.where(low, decay[:, :CHUNK], decay[:, CHUNK:])
            c_scr[dp] = decay * c_mem + _dot((vat_scr[n, pair].astype(F32) * w_src).astype(BF16),
                                             kbd_scr[n, pair])
            m_scr[dp:dp + 1, :] = m_new

        for (rev, pair, n), (arg, cm, s), (x2, m_mem) in zip(steps, scores, inter):
            dp = 2 * rev + pair
            g = (s * jnp.exp(arg - cm)).astype(BF16)
            g_bd = jnp.concatenate([jnp.concatenate([g[:, :CHUNK], zeros], axis=1),
                                    jnp.concatenate([zeros, g[:, CHUNK:]], axis=1)], axis=0)
            x1 = _dot(vat_scr[n, pair], g_bd)
            b_row = rows_scr[n, 1, dp:dp + 1, :]
            mu = jnp.maximum(cm, m_mem)
            both = x1 * jnp.exp(cm - mu) + x2 * jnp.exp(m_mem - mu)
            den = both[HEAD_DIM:HEAD_DIM + 1, :]
            inv = 1.0 / jnp.maximum(jnp.abs(den), jnp.exp(-(b_row + mu)))
            h_t = both[0:HEAD_DIM, :] * inv
            ht_scr[n, pair * LANES:(pair + 1) * LANES, :] += jnp.concatenate([h_t[:, :CHUNK], h_t[:, CHUNK:]],
                                                                             axis=0)
        return carry

    lax.fori_loop(0, N_CHUNKS, scan, 0, unroll=3)

    def finish(n, carry):
        rows = pl.ds(pl.multiple_of(n * CHUNK, CHUNK), CHUNK)
        gate = _sigmoid(og_ref[0, rows, :].astype(F32))
        ys = []
        for h in range(GROUP_HEADS):
            x = ht_scr[n, h * HEAD_DIM:(h + 1) * HEAD_DIM, :]
            mu = jnp.mean(x, axis=0, keepdims=True)
            var = jnp.mean(jnp.square(x - mu), axis=0, keepdims=True)
            ys.append((x - mu) * lax.rsqrt(var + EPS))
        y = jnp.concatenate(ys, axis=0).T * gln_ref[...]
        o_ref[0, rows, :] = (gate * y).astype(BF16)
        return carry

    lax.fori_loop(0, N_CHUNKS, finish, 0, unroll=3)


def _rope_tables():
    t = jnp.arange(SEQ)
    pos = jnp.stack([(t // GRID_W).astype(F32), (t % GRID_W).astype(F32)], axis=1)
    lane = jnp.arange(GROUP_WIDTH)
    m = HEAD_DIM // 4
    inv = ROPE_THETA ** (-(lane % m).astype(F32) / m)
    axis = (lane % HEAD_DIM) // (HEAD_DIM // 2)
    ang = jnp.where(axis[None, :] == 0, pos[:, 0:1], pos[:, 1:2]) * inv[None, :]
    low = ((lane % (2 * m)) < m)[None, :]
    cos, sin = jnp.cos(ang), jnp.sin(ang)
    return cos, jnp.where(low, -sin, 0.0), jnp.where(low, 0.0, sin)


def _mlstm(p, gates, w_conv, b_gate, g_mlstm, rope):
    bsz = p.shape[0]
    i = jnp.arange(CHUNK)
    tri = (i[:, None] >= i[None, :]).astype(F32)
    bg = jnp.pad(b_gate[jnp.array(GATE_I_COLS + GATE_F_COLS)], (0, LANES - N_GATES)).reshape(1, LANES)
    cos, sina, sinb = rope
    n_pairs = GROUP_HEADS // 2
    return pl.pallas_call(
        _mlstm_body,
        grid=(bsz,),
        in_specs=[_group_spec(5), _group_spec(6), _group_spec(7), _group_spec(8),
                  pl.BlockSpec((1, T_ALL, LANES), lambda b: (b, 0, 0)),
                  _resident((CONV_W, 2 * GROUP_WIDTH)),
                  _resident((1, LANES)),
                  _resident((SEQ, GROUP_WIDTH)), _resident((SEQ, GROUP_WIDTH)), _resident((SEQ, GROUP_WIDTH)),
                  _resident((1, GROUP_WIDTH)),
                  _resident((CHUNK, CHUNK))],
        out_specs=pl.BlockSpec((1, T_ALL, GROUP_WIDTH), lambda b: (b, 0, 0)),
        out_shape=jax.ShapeDtypeStruct((bsz, T_ALL, GROUP_WIDTH), BF16),
        scratch_shapes=[pltpu.VMEM((N_CHUNKS, n_pairs, LANES, 2 * CHUNK), BF16),
                        pltpu.VMEM((N_CHUNKS, CHUNK, GROUP_WIDTH), BF16),
                        pltpu.VMEM((N_CHUNKS, n_pairs, 2 * CHUNK, LANES), BF16),
                        pltpu.VMEM((N_CHUNKS, n_pairs, LANES, 2 * CHUNK), BF16),
                        pltpu.VMEM((N_CHUNKS, CHUNK, LANES), F32),
                        pltpu.VMEM((N_CHUNKS, 4, 2 * n_pairs, 2 * CHUNK), F32),
                        pltpu.VMEM((N_CHUNKS, GROUP_WIDTH, CHUNK), F32),
                        pltpu.VMEM((2 * n_pairs, LANES, LANES), F32),
                        pltpu.VMEM((2 * n_pairs, 2 * CHUNK), F32)],
        compiler_params=_params(1),
        name="mlstm",
    )(p, p, p, p, gates, w_conv, bg, cos, sina, sinb, g_mlstm.reshape(1, GROUP_WIDTH), tri)


def _dft_tables(n, scale):
    f = 1 << (n.bit_length() // 2)
    s = jnp.arange(n, dtype=jnp.int32)[None, :]
    ang_a = ((f * jnp.arange(n // f, dtype=jnp.int32)[:, None] * s) % n).astype(F32) * (2.0 * jnp.pi / n)
    ang_b = ((jnp.arange(f, dtype=jnp.int32)[:, None] * s) % n).astype(F32) * (2.0 * jnp.pi / n)
    ca, sa = jnp.cos(ang_a)[:, None, :], jnp.sin(ang_a)[:, None, :]
    cb, sb = jnp.cos(ang_b)[None, :, :], jnp.sin(ang_b)[None, :, :]
    cos = (ca * cb - sa * sb).reshape(n, n)
    sin = (sa * cb + ca * sb).reshape(n, n)
    return (cos * scale).astype(BF16), (-sin * scale).astype(BF16)


def _channel_dft():
    gc = GROUP_WIDTH // FNET_GROUPS
    j = jnp.arange(GROUP_WIDTH, dtype=jnp.int32)
    same = (j[:, None] // gc) == (j[None, :] // gc)
    ang = (((j[:, None] % gc) * (j[None, :] % gc)) % gc).astype(F32) * (2.0 * jnp.pi / gc)
    c = jnp.where(same, jnp.cos(ang), 0.0)
    s = jnp.where(same, jnp.sin(ang), 0.0)
    return jnp.concatenate([c, s], axis=1).astype(BF16)


def _fnet_body(zc_ref, zs_ref, cl_ref, sl_ref, cc_ref, sc_ref, w_ref, o_ref):
    y = _dot(cl_ref[...], zc_ref[0:SEQ, :]) + _dot(sl_ref[...], zs_ref[0:SEQ, :])
    o_ref[0:SEQ, :] = _dot(y.astype(BF16), w_ref[...]).astype(BF16)
    y = _dot(cc_ref[...], zc_ref[SEQ:, :]) + _dot(sc_ref[...], zs_ref[SEQ:, :])
    o_ref[SEQ:, :] = _dot(y.astype(BF16), w_ref[...]).astype(BF16)


def _fnet(zc, zs, tabs, w_fnet):
    bsz = zc.shape[1] // GROUP_WIDTH
    cl, sl, cc, sc = tabs
    col = pl.BlockSpec((T_ALL, GROUP_WIDTH), lambda b: (0, b))
    return pl.pallas_call(
        _fnet_body,
        grid=(bsz,),
        in_specs=[col, col, _resident((SEQ, SEQ)), _resident((SEQ, SEQ)),
                  _resident((CTX_LEN, CTX_LEN)), _resident((CTX_LEN, CTX_LEN)),
                  _resident((GROUP_WIDTH, GROUP_WIDTH))],
        out_specs=col,
        out_shape=jax.ShapeDtypeStruct((T_ALL, bsz * GROUP_WIDTH), BF16),
        compiler_params=_params(1),
        name="fnet",
    )(zc, zs, cl, sl, cc, sc, w_fnet.astype(BF16))


def _post_body(a_ref, b_ref, c_ref, d_ref, x_ref, cx_ref, ml_ref, mc_ref, g_ref, wo_ref, w1_ref, w2_ref, gf_ref,
               o_ref, h_scr, acc_scr, *, final):
    gw = GROUP_WIDTH
    tm = x_ref.shape[1]
    mod = _mod_rows(ml_ref, mc_ref, tm)
    y = (_dot(a_ref[0], wo_ref[0:gw, :]) + _dot(b_ref[0], wo_ref[gw:2 * gw, :])
         + _dot(c_ref[0], wo_ref[2 * gw:3 * gw, :]) + _dot(d_ref[...], wo_ref[3 * gw:, :]))
    if final:
        x = x_ref[0]
    else:
        ctx_tail = pl.program_id(1) == pl.num_programs(1) - 1
        x = jnp.concatenate([x_ref[0, 0:tm - CTX_LEN, :], _tail_rows(x_ref, cx_ref, ctx_tail)], axis=0)
    x1 = x + mod(2) * y
    h = _rms(x1, g_ref[...]) * (1.0 + mod(4)) + mod(3)
    h_scr[...] = h.astype(BF16)
    step = 512

    def up(j):
        return jnp.maximum(_dot(h_scr[...], w1_ref[:, j:j + step]), 0.0)

    a = up(0)
    for j in range(0, D_FF, step):
        nxt = up(j + step) if j + step < D_FF else None
        part = _dot((a * a).astype(BF16), w2_ref[j:j + step, :])
        if j == 0:
            acc_scr[...] = part
        else:
            acc_scr[...] += part
        a = nxt
    x2 = x1 + mod(5) * acc_scr[...]
    if final:
        x2 = _rms(x2, gf_ref[...])
    o_ref[0] = x2


def _post(a, b_, c_, d, tokens, ctx, mod_l, g_ffn, w_out, w_ff1, w_ff2, g_final, final):
    bsz = tokens.shape[0]
    tm, out_len = (LATENT_TILE, SEQ) if final else (TOKEN_TILE, T_ALL)
    tok = pl.BlockSpec((1, tm, D_MODEL), lambda b, t: (b, t, 0))
    grp = pl.BlockSpec((1, tm, GROUP_WIDTH), lambda b, t: (b, t, 0))
    operands, token_specs = _token_operands(tokens, ctx, tm)
    return pl.pallas_call(
        functools.partial(_post_body, final=final),
        grid=(bsz, out_len // tm),
        in_specs=[grp, grp, grp, pl.BlockSpec((tm, GROUP_WIDTH), lambda b, t: (t, b))] + token_specs + _MOD_SPECS
                 + [_resident((1, D_MODEL)),
                    _resident((D_MODEL, D_MODEL)),
                    _resident((D_MODEL, D_FF)),
                    _resident((D_FF, D_MODEL)),
                    _resident((1, D_MODEL))],
        out_specs=tok,
        out_shape=jax.ShapeDtypeStruct((bsz, out_len, D_MODEL), F32),
        scratch_shapes=[pltpu.VMEM((tm, D_MODEL), BF16), pltpu.VMEM((tm, D_MODEL), F32)],
        compiler_params=_params(2),
        name="out_proj_mlp",
    )(a, b_, c_, d, *operands, mod_l, mod_l, g_ffn.reshape(1, D_MODEL), w_out.astype(BF16), w_ff1.astype(BF16),
      w_ff2.astype(BF16), g_final.reshape(1, D_MODEL))


def kernel(x, c, ctx, c_ctx, w_ada, b_ada, g_norm_mix, g_norm_ffn, w_in, b_gate, w_conv_qk, rpb, w_spatial,
           b_spatial, g_gmlp, g_mlstm, w_fnet, w_out, w_ff1, w_ff2, g_final):
    bsz = x.shape[0]
    assert bsz <= CTX_MOD_ROW and x.shape[1:] == (SEQ, D_MODEL) and ctx.shape[1:] == (CTX_LEN, D_MODEL)
    depth = w_ada.shape[0]
    tokens, ctx_rows = x, ctx
    cc = jnp.zeros((MOD_ROWS, D_MODEL), F32).at[:bsz].set(c).at[CTX_MOD_ROW].set(c_ctx)
    mod = _ada_mod(cc, w_ada, b_ada).reshape(depth, MOD_ROWS, N_MOD, D_MODEL)

    rope = _rope_tables()
    dft_c = _channel_dft()
    tabs = (_dft_tables(SEQ, (SEQ * GROUP_WIDTH // FNET_GROUPS) ** -0.5)
            + _dft_tables(CTX_LEN, (CTX_LEN * GROUP_WIDTH // FNET_GROUPS) ** -0.5))

    for l in range(depth):
        w_cat = _projection_weights(w_in[l])
        p, gates, zc, zs = _in_proj(tokens, ctx_rows, mod[l], g_norm_mix[l], w_cat, dft_c)
        a = _attention(p, _attention_bias(rpb[l]), ctx_out=(l < depth - 1))
        b_ = _gmlp(p, w_spatial[l], b_spatial[l], g_gmlp[l])
        c_ = _mlstm(p, gates, w_conv_qk[l], b_gate[l], g_mlstm[l], rope)
        d = _fnet(zc, zs, tabs, w_fnet[l])
        tokens = _post(a, b_, c_, d, tokens, ctx_rows, mod[l], g_norm_ffn[l], w_out[l], w_ff1[l], w_ff2[l],
                       g_final, final=(l == depth - 1))
        ctx_rows = None
    return tokens
```

```python
import functools

import jax
import jax.numpy as jnp
from jax import lax
from jax.experimental import pallas as pl
from jax.experimental.pallas import tpu as pltpu

D_MODEL = 1024
SEQ = 2048
DEPTH = 2
GRID_W = 64
GRID_ROWS = SEQ // GRID_W
CTX_LEN = 256
T_ALL = CTX_LEN + SEQ
HEAD_DIM = 64
GROUP_WIDTH = 256
GROUP_HEADS = 4
WIN_ROWS = 8
WIN_COLS = 16
CHUNK = 128
N_CHUNKS = T_ALL // CHUNK
CTX_CHUNKS = CTX_LEN // CHUNK
LAT_CHUNKS = SEQ // CHUNK
CONV_W = 3
FNET_GROUPS = 4
ROPE_THETA = 10000.0
D_FF = 4 * D_MODEL
N_MOD = 6
EPS = 1e-6
NEG_INF = -1e30
N_GATES = 4 * GROUP_HEADS
OFF_G = 10 * GROUP_WIDTH
D_IN = OFF_G + N_GATES
MOD_ROWS = 24
CTX_MOD_ROW = 16
LANES = 128
BF16_ROWS = 16
GATE_I_COLS = (0, 2, 8, 10, 1, 3, 9, 11)
GATE_F_COLS = (4, 6, 12, 14, 5, 7, 13, 15)
TOKEN_TILE = 768
PROJ_TILE = 1152
LATENT_TILE = 512
P_WIDTH = 9 * GROUP_WIDTH
VMEM_LIMIT = 56 * 1024 * 1024

F32 = jnp.float32
BF16 = jnp.bfloat16


def _dot(a, b):
    return jnp.dot(a, b, preferred_element_type=F32)


def _dot_nt(a, b):
    return lax.dot_general(a, b, (((1,), (1,)), ((), ())), preferred_element_type=F32)


def _dot_f32(a, b):
    return jnp.dot(a, b, preferred_element_type=F32, precision=lax.Precision.HIGHEST)


def _sigmoid(x):
    return 0.5 * (1.0 + jnp.tanh(0.5 * x))


def _resident(shape):
    nd = len(shape)
    return pl.BlockSpec(shape, lambda *_: (0,) * nd, pipeline_mode=pl.Buffered(1))


def _params(n_axes):
    return pltpu.CompilerParams(dimension_semantics=("arbitrary",) * n_axes,
                                vmem_limit_bytes=VMEM_LIMIT)


def _mod_body(c_ref, w_ref, b_ref, o_ref):
    s = c_ref[...]
    s = s * jax.nn.sigmoid(s)
    o_ref[0] = _dot(s.astype(BF16), w_ref[0].astype(BF16)) + b_ref[0]


def _ada_mod(cc, w_ada, b_ada):
    depth, d, n = w_ada.shape
    tn = 1536
    return pl.pallas_call(
        _mod_body,
        grid=(depth, n // tn),
        in_specs=[pl.BlockSpec((MOD_ROWS, d), lambda l, j: (0, 0)),
                  pl.BlockSpec((1, d, tn), lambda l, j: (l, 0, j)),
                  pl.BlockSpec((1, 1, tn), lambda l, j: (l, 0, j))],
        out_specs=pl.BlockSpec((1, MOD_ROWS, tn), lambda l, j: (l, 0, j)),
        out_shape=jax.ShapeDtypeStruct((depth, MOD_ROWS, n), F32),
        compiler_params=_params(2),
        name="ada_mod",
    )(cc, w_ada, b_ada.reshape(depth, 1, n))


_MOD_SPECS = [pl.BlockSpec((1, N_MOD, D_MODEL), lambda b, t: (b, 0, 0)),
              pl.BlockSpec((1, N_MOD, D_MODEL), lambda b, t: (CTX_MOD_ROW, 0, 0))]


def _mod_rows(mod_lat_ref, mod_ctx_ref, tile):
    tok = pl.program_id(1) * tile + lax.broadcasted_iota(jnp.int32, (tile, 1), 0)
    is_ctx = tok >= SEQ
    return lambda k: jnp.where(is_ctx, mod_ctx_ref[0, k:k + 1, :], mod_lat_ref[0, k:k + 1, :])


def _rms(x, g):
    return x * lax.rsqrt(jnp.mean(x * x, axis=-1, keepdims=True) + EPS) * g


def _tail_rows(x_ref, c_ref, ctx_tail):
    head = x_ref.shape[1] - CTX_LEN
    return jnp.where(ctx_tail, c_ref[0], x_ref[0, head:, :])


def _inproj_body(x_ref, c_ref, ml_ref, mc_ref, g_ref, w_ref, dft_ref, p_ref, gate_ref, zc_ref, zs_ref,
                 lhs_scr, inv_scr):
    tm = x_ref.shape[1]
    head = tm - CTX_LEN
    ctx_tail = pl.program_id(1) == pl.num_programs(1) - 1
    shift_tail = jnp.where(ctx_tail, mc_ref[0, 0:1, :], ml_ref[0, 0:1, :])
    scale_tail = jnp.where(ctx_tail, mc_ref[0, 1:2, :], ml_ref[0, 1:2, :])
    lead = BF16_ROWS
    lhs_scr[0:lead, :] = jnp.concatenate([ml_ref[0, 0:1, :], shift_tail,
                                          jnp.zeros((lead - 2, D_MODEL), F32)], axis=0).astype(BF16)
    x_head = x_ref[0, 0:head, :]
    x_tail = _tail_rows(x_ref, c_ref, ctx_tail)
    lhs_scr[lead:lead + head, :] = (x_head * (g_ref[...] * (1.0 + ml_ref[0, 1:2, :]))).astype(BF16)
    lhs_scr[lead + head:, :] = (x_tail * (g_ref[...] * (1.0 + scale_tail))).astype(BF16)
    for rows, x in ((slice(0, head), x_head), (slice(head, tm), x_tail)):
        inv = lax.rsqrt(jnp.mean(x * x, axis=-1, keepdims=True) + EPS)
        inv_scr[rows, :] = jnp.broadcast_to(inv, (x.shape[0], LANES))

    def project(cols):
        n = (cols.stop - cols.start) // LANES
        r = _dot(lhs_scr[...], w_ref[:, cols])
        inv_head = jnp.concatenate([inv_scr[0:head, :]] * n, axis=1)
        inv_tail = jnp.concatenate([inv_scr[head:, :]] * n, axis=1)
        return jnp.concatenate([inv_head * r[lead:lead + head, :] + r[0:1, :],
                                inv_tail * r[lead + head:, :] + r[1:2, :]], axis=0)

    step = 2 * GROUP_WIDTH
    f = project(slice(P_WIDTH, OFF_G)).astype(BF16)
    p_ref[0, :, 0:step] = project(slice(0, step)).astype(BF16)
    z = _dot(f, dft_ref[...])
    zc_ref[...] = z[:, :GROUP_WIDTH].astype(BF16)
    zs_ref[...] = z[:, GROUP_WIDTH:].astype(BF16)
    for j in range(step, P_WIDTH - GROUP_WIDTH, step):
        p_ref[0, :, j:j + step] = project(slice(j, j + step)).astype(BF16)
    j = P_WIDTH - GROUP_WIDTH
    p_ref[0, :, j:] = project(slice(j, P_WIDTH)).astype(BF16)
    gate_ref[0] = project(slice(OFF_G, OFF_G + LANES))


def _projection_weights(w_in):
    pad = jnp.zeros((w_in.shape[0], LANES - N_GATES), w_in.dtype)
    w_g = w_in[:, OFF_G:]
    return jnp.concatenate([w_in[:, :OFF_G], w_g[:, jnp.array(GATE_I_COLS + GATE_F_COLS)], pad],
                           axis=1).astype(BF16)


def _token_operands(tokens, ctx, tm):
    tile = pl.BlockSpec((1, tm, D_MODEL), lambda b, t: (b, t, 0))
    if ctx is None:
        return (tokens, tokens), [tile, pl.BlockSpec((1, CTX_LEN, D_MODEL), lambda b, t: (b, SEQ // CTX_LEN, 0))]
    return (tokens, ctx), [tile, pl.BlockSpec((1, CTX_LEN, D_MODEL), lambda b, t: (b, 0, 0))]


def _in_proj(tokens, ctx, mod_l, g, w_cat, dft_c):
    bsz = tokens.shape[0]
    tm = PROJ_TILE
    wn = w_cat.shape[1]
    operands, token_specs = _token_operands(tokens, ctx, tm)
    return pl.pallas_call(
        _inproj_body,
        grid=(bsz, T_ALL // tm),
        in_specs=token_specs + _MOD_SPECS
                 + [_resident((1, D_MODEL)),
                    _resident((D_MODEL, wn)),
                    _resident((GROUP_WIDTH, 2 * GROUP_WIDTH))],
        out_specs=[pl.BlockSpec((1, tm, P_WIDTH), lambda b, t: (b, t, 0)),
                   pl.BlockSpec((1, tm, LANES), lambda b, t: (b, t, 0)),
                   pl.BlockSpec((tm, GROUP_WIDTH), lambda b, t: (t, b)),
                   pl.BlockSpec((tm, GROUP_WIDTH), lambda b, t: (t, b))],
        out_shape=[jax.ShapeDtypeStruct((bsz, T_ALL, P_WIDTH), BF16),
                   jax.ShapeDtypeStruct((bsz, T_ALL, LANES), F32),
                   jax.ShapeDtypeStruct((T_ALL, bsz * GROUP_WIDTH), BF16),
                   jax.ShapeDtypeStruct((T_ALL, bsz * GROUP_WIDTH), BF16)],
        scratch_shapes=[pltpu.VMEM((tm + BF16_ROWS, D_MODEL), BF16), pltpu.VMEM((tm, LANES), F32)],
        compiler_params=_params(2),
        name="in_proj",
    )(*operands, mod_l, mod_l, g.reshape(1, D_MODEL), w_cat, dft_c)


def _group_spec(col_block):
    return pl.BlockSpec((1, T_ALL, GROUP_WIDTH), lambda b: (b, 0, col_block))


def _softmax_pv(parts):
    m = functools.reduce(jnp.maximum, [jnp.max(s, axis=-1, keepdims=True) for s, _ in parts])
    es = [jnp.exp(s - m) for s, _ in parts]
    den = functools.reduce(jnp.add, [jnp.sum(e, axis=-1, keepdims=True) for e in es])
    num = functools.reduce(jnp.add, [_dot(e.astype(BF16), v) for e, (_, v) in zip(es, parts)])
    return num / den


def _attn_body(q_ref, k_ref, v_ref, bias_ref, o_ref, *, ctx_out):
    scale = HEAD_DIM ** -0.5
    pairs = [slice(i * LANES, (i + 1) * LANES) for i in range(GROUP_WIDTH // LANES)]

    def stack_pair(q2):
        low = lax.broadcasted_iota(jnp.int32, q2.shape, 1) < HEAD_DIM
        zero = jnp.zeros_like(q2)
        return low, jnp.concatenate([jnp.where(low, q2, zero), jnp.where(low, zero, q2)], axis=0)

    def unstack_pair(low, o):
        n = o.shape[0] // 2
        return jnp.where(low, o[:n], o[n:]).astype(BF16)

    ctx = slice(SEQ, T_ALL)
    if ctx_out:
        ctx_tiles = []
        for ps in pairs:
            low, qm = stack_pair(q_ref[0, ctx, ps] * scale)
            ctx_tiles.append((low, _dot_nt(qm, k_ref[0, ctx, ps])))
        for ps, (low, s) in zip(pairs, ctx_tiles):
            o_ref[0, ctx, ps] = unstack_pair(low, _softmax_pv([(s, v_ref[0, ctx, ps])]))
    else:
        o_ref[0, ctx, :] = jnp.zeros((CTX_LEN, GROUP_WIDTH), BF16)

    rows_per_trip = 4
    tile_rows = 2 * GRID_W

    def trip(g, carry):
        tiles = [(dr, i) for dr in range(rows_per_trip) for i in range(len(pairs))]
        q0s = [pl.multiple_of((g * rows_per_trip + dr) * GRID_W, GRID_W) for dr in range(rows_per_trip)]
        stacked = {(dr, i): stack_pair(q_ref[0, pl.ds(q0s[dr], GRID_W), pairs[i]] * scale) for dr, i in tiles}
        s_ctx = [_dot_nt(jnp.concatenate([stacked[dr, i][1] for dr in range(rows_per_trip)], axis=0),
                         k_ref[0, ctx, pairs[i]]) for i in range(len(pairs))]

        def window_scores(dr, i):
            r = g * rows_per_trip + dr
            rs = jnp.clip(r - WIN_ROWS // 2, 0, GRID_ROWS - WIN_ROWS)
            k0 = pl.multiple_of(rs * GRID_W, GRID_W)
            s_loc = _dot_nt(stacked[dr, i][1], k_ref[0, pl.ds(k0, WIN_ROWS * GRID_W), pairs[i]])
            return s_loc + bias_ref[i, r - rs], k0

        partial = {}
        cur = window_scores(*tiles[0])
        for j, (dr, i) in enumerate(tiles):
            nxt = window_scores(*tiles[j + 1]) if j + 1 < len(tiles) else None
            s_loc, k0 = cur
            s_c = s_ctx[i][dr * tile_rows:(dr + 1) * tile_rows, :]
            m = jnp.maximum(jnp.max(s_loc, axis=-1, keepdims=True), jnp.max(s_c, axis=-1, keepdims=True))
            e_loc, e_ctx = jnp.exp(s_loc - m), jnp.exp(s_c - m)
            den = jnp.sum(e_loc, axis=-1, keepdims=True) + jnp.sum(e_ctx, axis=-1, keepdims=True)
            num = _dot(e_loc.astype(BF16), v_ref[0, pl.ds(k0, WIN_ROWS * GRID_W), pairs[i]])
            partial[dr, i] = (num, e_ctx.astype(BF16), den)
            cur = nxt

        for i, ps in enumerate(pairs):
            num_ctx = _dot(jnp.concatenate([partial[dr, i][1] for dr in range(rows_per_trip)], axis=0),
                           v_ref[0, ctx, ps])
            for dr in range(rows_per_trip):
                num, _, den = partial[dr, i]
                o = (num + num_ctx[dr * tile_rows:(dr + 1) * tile_rows, :]) / den
                o_ref[0, pl.ds(q0s[dr], GRID_W), ps] = unstack_pair(stacked[dr, i][0], o)
        return carry

    lax.fori_loop(0, GRID_ROWS // rows_per_trip, trip, 0)


def _attention(p, bias_tab, ctx_out):
    bsz = p.shape[0]
    return pl.pallas_call(
        functools.partial(_attn_body, ctx_out=ctx_out),
        grid=(bsz,),
        in_specs=[_group_spec(0), _group_spec(1), _group_spec(2), _resident(bias_tab.shape)],
        out_specs=pl.BlockSpec((1, T_ALL, GROUP_WIDTH), lambda b: (b, 0, 0)),
        out_shape=jax.ShapeDtypeStruct((bsz, T_ALL, GROUP_WIDTH), BF16),
        compiler_params=_params(1),
        name="nbr_attention",
    )(p, p, p, bias_tab)


def _attention_bias(rpb):
    c = jnp.arange(GRID_W)
    qstart = jnp.clip(c - WIN_COLS // 2, 0, GRID_W - WIN_COLS)
    in_win = (c[None, :] >= qstart[:, None]) & (c[None, :] < qstart[:, None] + WIN_COLS)
    col = jnp.clip(c[None, :] - c[:, None] + WIN_COLS - 1, 0, 2 * WIN_COLS - 2)
    pick_col = (col[:, :, None] == jnp.arange(2 * WIN_COLS - 1)).astype(F32)
    w = jnp.arange(WIN_ROWS)
    pick_row = ((w[None, :] - w[:, None] + WIN_ROWS - 1)[:, :, None]
                == jnp.arange(2 * WIN_ROWS - 1)).astype(F32)
    rpb_pairs = rpb.astype(F32).reshape(GROUP_HEADS // 2, 2, 2 * WIN_ROWS - 1, 2 * WIN_COLS - 1)
    tab = jnp.einsum('phde,vjd,qke->pvhqjk', rpb_pairs, pick_row, pick_col, precision=lax.Precision.HIGHEST)
    tab = jnp.where(in_win[None, None, None, :, None, :], tab, NEG_INF)
    return tab.reshape(GROUP_HEADS // 2, WIN_ROWS, 2 * GRID_W, WIN_ROWS * GRID_W)


def _gmlp_chunk(n, u_ref, z_ref, ws_ref, bs_ref, g_ref, o_ref):
    rows = slice(n * CHUNK, (n + 1) * CHUNK)
    z = _rms(jax.nn.gelu(z_ref[0, rows, :].astype(F32)), g_ref[...]).astype(BF16)
    u = jax.nn.gelu(u_ref[0, rows, :].astype(F32))
    for h in range(GROUP_HEADS):
        hs = slice(h * HEAD_DIM, (h + 1) * HEAD_DIM)
        mixed = _dot(ws_ref[h], z[:, hs]) + bs_ref[:, hs]
        o_ref[0, rows, hs] = (u[:, hs] * mixed).astype(BF16)


def _mlstm_body(q_ref, k_ref, v_ref, og_ref, gate_ref, wc_ref, bg_ref, cos_ref, sina_ref,
                sinb_ref, gln_ref, tri_ref, o_ref,
                qtz_scr, k_scr, kbd_scr, vat_scr, acol_scr, rows_scr, ht_scr, c_scr, m_scr):
    half = HEAD_DIM // 4
    n_chains = 2 * GROUP_HEADS

    def conv_act(ref, n, w0, w1, w2, rope, post):
        rows = pl.ds(pl.multiple_of(n * CHUNK, CHUNK), CHUNK)
        cur = ref[0, rows, :].astype(F32)
        first = (n == 0) | (n == LAT_CHUNKS)
        last = (n == LAT_CHUNKS - 1) | (n == N_CHUNKS - 1)
        before = pl.ds(pl.multiple_of(jnp.maximum(n * CHUNK - BF16_ROWS, 0), BF16_ROWS), BF16_ROWS)
        after = pl.ds(pl.multiple_of(jnp.minimum((n + 1) * CHUNK, T_ALL - BF16_ROWS), BF16_ROWS), BF16_ROWS)
        tail = jnp.where(first, 0.0, 1.0) * ref[0, before, :][BF16_ROWS - 1:BF16_ROWS, :].astype(F32)
        head = jnp.where(last, 0.0, 1.0) * ref[0, after, :][0:1, :].astype(F32)
        sub = lax.broadcasted_iota(jnp.int32, (CHUNK, 1), 0)
        x_prev = jnp.where(sub == 0, tail, pltpu.roll(cur, 1, 0))
        x_next = jnp.where(sub == CHUNK - 1, head, pltpu.roll(cur, CHUNK - 1, 0))
        y = w0 * x_prev + w1 * cur + w2 * x_next
        y = y * _sigmoid(y)
        if rope:
            y = (y * cos_ref[rows, :] + pltpu.roll(y, GROUP_WIDTH - half, 1) * sina_ref[rows, :]
                 + pltpu.roll(y, half, 1) * sinb_ref[rows, :])
        return y * post

    def prep(n, rope):
        rows = pl.ds(pl.multiple_of(n * CHUNK, CHUNK), CHUNK)
        gw = GROUP_WIDTH
        q = conv_act(q_ref, n, wc_ref[0:1, :gw], wc_ref[1:2, :gw], wc_ref[2:3, :gw], rope, 1.0)
        k = conv_act(k_ref, n, wc_ref[0:1, gw:], wc_ref[1:2, gw:], wc_ref[2:3, gw:], rope, HEAD_DIM ** -0.5)
        q_t = q.T.astype(BF16)
        v_t = v_ref[0, rows, :].astype(F32).T.astype(BF16)
        k = k.astype(BF16)
        k_scr[n] = k
        sub = lax.broadcasted_iota(jnp.int32, (HEAD_DIM, CHUNK), 0)
        ones_row = jnp.where(sub == 0, 1.0, 0.0).astype(BF16)
        zeros = jnp.zeros((HEAD_DIM, CHUNK), BF16)
        low = lax.broadcasted_iota(jnp.int32, (CHUNK, LANES), 1) < HEAD_DIM
        for pair in range(GROUP_HEADS // 2):
            h0 = slice(2 * pair * HEAD_DIM, (2 * pair + 1) * HEAD_DIM)
            h1 = slice((2 * pair + 1) * HEAD_DIM, (2 * pair + 2) * HEAD_DIM)
            qtz_scr[n, pair] = jnp.concatenate([jnp.concatenate([q_t[h0, :], zeros], axis=1),
                                                jnp.concatenate([zeros, q_t[h1, :]], axis=1)], axis=0)
            k_pair = k[:, pair * LANES:(pair + 1) * LANES]
            kbd_scr[n, pair] = jnp.concatenate([jnp.where(low, k_pair, jnp.zeros_like(k_pair)),
                                                jnp.where(low, jnp.zeros_like(k_pair), k_pair)], axis=0)
            vat_scr[n, pair] = jnp.concatenate([jnp.concatenate([v_t[h0, :], ones_row], axis=0),
                                                jnp.concatenate([v_t[h1, :], ones_row], axis=0)], axis=1)
        ht_scr[n] = jnp.zeros((GROUP_WIDTH, CHUNK), F32)

        g_i = gate_ref[0, rows, :] + bg_ref[...]
        lf = pltpu.roll(jax.nn.log_sigmoid(g_i), LANES - n_chains, 1)
        pre = _dot_f32(tri_ref[...], lf)
        suf = pre[CHUNK - 1:CHUNK, :] - pre + lf
        lane = lax.broadcasted_iota(jnp.int32, (CHUNK, LANES), 1)
        b = jnp.where(lane % 4 < 2, pre, suf)
        a = g_i - b
        acol_scr[n] = a
        a_t = a.T[0:n_chains, :]
        b_t = b.T[0:n_chains, :]
        chain = lax.broadcasted_iota(jnp.int32, (n_chains, CHUNK), 0)
        b_end = jnp.where(chain % 4 < 2, b_t[:, CHUNK - 1:CHUNK], b_t[:, 0:1])
        b_end = jnp.broadcast_to(b_end, (n_chains, CHUNK))
        a_max = jnp.broadcast_to(jnp.max(a_t, axis=1, keepdims=True), (n_chains, CHUNK))
        for kind, rows8 in enumerate((a_t, b_t, b_end, a_max)):
            rows_scr[n, kind] = jnp.concatenate([rows8[0:4, :], rows8[4:8, :]], axis=1)

    def prep_ctx(n, carry):
        prep(n, False)
        return carry

    def prep_lat(n, carry):
        prep(n, True)
        return carry

    lax.fori_loop(0, LAT_CHUNKS, prep_lat, 0, unroll=2)
    lax.fori_loop(LAT_CHUNKS, N_CHUNKS, prep_ctx, 0)
    c_scr[...] = jnp.zeros(c_scr.shape, F32)
    m_scr[...] = jnp.zeros(m_scr.shape, F32)

    s_idx = lax.broadcasted_iota(jnp.int32, (CHUNK, CHUNK), 0)
    t_idx = lax.broadcasted_iota(jnp.int32, (CHUNK, CHUNK), 1)

    zeros = jnp.zeros((CHUNK, CHUNK), BF16)
    low = lax.broadcasted_iota(jnp.int32, (1, LANES), 1) < HEAD_DIM

    def scan(i, carry):
        chunk_of = (jnp.where(i < CTX_CHUNKS, LAT_CHUNKS + i, i - CTX_CHUNKS), N_CHUNKS - 1 - i)
        steps = [(rev, pair, chunk_of[rev]) for rev in range(2) for pair in range(GROUP_HEADS // 2)]

        scores = []
        for rev, pair, n in steps:
            dp = 2 * rev + pair
            causal = (s_idx >= t_idx) if rev else (s_idx <= t_idx)
            arg = jnp.concatenate([jnp.where(causal, acol_scr[n, :, dp:dp + 1], NEG_INF),
                                   jnp.where(causal, acol_scr[n, :, 4 + dp:5 + dp], NEG_INF)], axis=1)
            s = _dot(k_scr[n, :, pair * LANES:(pair + 1) * LANES], qtz_scr[n, pair])
            scores.append((arg, jnp.max(arg, axis=0, keepdims=True), s))

        inter = []
        for rev, pair, n in steps:
            dp = 2 * rev + pair
            a_row = rows_scr[n, 0, dp:dp + 1, :]
            b_end = rows_scr[n, 2, dp:dp + 1, :]
            a_max = rows_scr[n, 3, dp:dp + 1, :]
            m_mem = m_scr[dp:dp + 1, :]
            c_mem = c_scr[dp]
            inter.append((_dot(c_mem.astype(BF16), qtz_scr[n, pair]), m_mem))
            m_new = b_end + jnp.maximum(m_mem, a_max)
            w_src = jnp.exp(b_end + a_row - m_new)
            decay = jnp.exp(b_end + m_mem - m_new)
            decay = jnp.where(low, decay[:, :CHUNK], decay[:, CHUNK:])
            c_scr[dp] = decay * c_mem + _dot((vat_scr[n, pair].astype(F32) * w_src).astype(BF16),
                                             kbd_scr[n, pair])
            m_scr[dp:dp + 1, :] = m_new

        for (rev, pair, n), (arg, cm, s), (x2, m_mem) in zip(steps, scores, inter):
            dp = 2 * rev + pair
            g = (s * jnp.exp(arg - cm)).astype(BF16)
            g_bd = jnp.concatenate([jnp.concatenate([g[:, :CHUNK], zeros], axis=1),
                                    jnp.concatenate([zeros, g[:, CHUNK:]], axis=1)], axis=0)
            x1 = _dot(vat_scr[n, pair], g_bd)
            b_row = rows_scr[n, 1, dp:dp + 1, :]
            mu = jnp.maximum(cm, m_mem)
            both = x1 * jnp.exp(cm - mu) + x2 * jnp.exp(m_mem - mu)
            den = both[HEAD_DIM:HEAD_DIM + 1, :]
            inv = 1.0 / jnp.maximum(jnp.abs(den), jnp.exp(-(b_row + mu)))
            h_t = both[0:HEAD_DIM, :] * inv
            ht_scr[n, pair * LANES:(pair + 1) * LANES, :] += jnp.concatenate([h_t[:, :CHUNK], h_t[:, CHUNK:]],
                                                                             axis=0)
        return carry

    lax.fori_loop(0, N_CHUNKS, scan, 0, unroll=3)

    def finish(n, carry):
        rows = pl.ds(pl.multiple_of(n * CHUNK, CHUNK), CHUNK)
        gate = _sigmoid(og_ref[0, rows, :].astype(F32))
        ys = []
        for h in range(GROUP_HEADS):
            x = ht_scr[n, h * HEAD_DIM:(h + 1) * HEAD_DIM, :]
            mu = jnp.mean(x, axis=0, keepdims=True)
            var = jnp.mean(jnp.square(x - mu), axis=0, keepdims=True)
            ys.append((x - mu) * lax.rsqrt(var + EPS))
        y = jnp.concatenate(ys, axis=0).T * gln_ref[...]
        o_ref[0, rows, :] = (gate * y).astype(BF16)
        return carry

    lax.fori_loop(0, N_CHUNKS, finish, 0, unroll=3)


def _rope_tables():
    t = jnp.arange(SEQ)
    pos = jnp.stack([(t // GRID_W).astype(F32), (t % GRID_W).astype(F32)], axis=1)
    lane = jnp.arange(GROUP_WIDTH)
    m = HEAD_DIM // 4
    inv = ROPE_THETA ** (-(lane % m).astype(F32) / m)
    axis = (lane % HEAD_DIM) // (HEAD_DIM // 2)
    ang = jnp.where(axis[None, :] == 0, pos[:, 0:1], pos[:, 1:2]) * inv[None, :]
    low = ((lane % (2 * m)) < m)[None, :]
    cos, sin = jnp.cos(ang), jnp.sin(ang)
    return cos, jnp.where(low, -sin, 0.0), jnp.where(low, 0.0, sin)


def _mlstm(p, gates, w_conv, b_gate, g_mlstm, rope):
    bsz = p.shape[0]
    i = jnp.arange(CHUNK)
    tri = (i[:, None] >= i[None, :]).astype(F32)
    bg = jnp.pad(b_gate[jnp.array(GATE_I_COLS + GATE_F_COLS)], (0, LANES - N_GATES)).reshape(1, LANES)
    cos, sina, sinb = rope
    n_pairs = GROUP_HEADS // 2
    return pl.pallas_call(
        _mlstm_body,
        grid=(bsz,),
        in_specs=[_group_spec(5), _group_spec(6), _group_spec(7), _group_spec(8),
                  pl.BlockSpec((1, T_ALL, LANES), lambda b: (b, 0, 0)),
                  _resident((CONV_W, 2 * GROUP_WIDTH)),
                  _resident((1, LANES)),
                  _resident((SEQ, GROUP_WIDTH)), _resident((SEQ, GROUP_WIDTH)), _resident((SEQ, GROUP_WIDTH)),
                  _resident((1, GROUP_WIDTH)),
                  _resident((CHUNK, CHUNK))],
        out_specs=pl.BlockSpec((1, T_ALL, GROUP_WIDTH), lambda b: (b, 0, 0)),
        out_shape=jax.ShapeDtypeStruct((bsz, T_ALL, GROUP_WIDTH), BF16),
        scratch_shapes=[pltpu.VMEM((N_CHUNKS, n_pairs, LANES, 2 * CHUNK), BF16),
                        pltpu.VMEM((N_CHUNKS, CHUNK, GROUP_WIDTH), BF16),
                        pltpu.VMEM((N_CHUNKS, n_pairs, 2 * CHUNK, LANES), BF16),
                        pltpu.VMEM((N_CHUNKS, n_pairs, LANES, 2 * CHUNK), BF16),
                        pltpu.VMEM((N_CHUNKS, CHUNK, LANES), F32),
                        pltpu.VMEM((N_CHUNKS, 4, 2 * n_pairs, 2 * CHUNK), F32),
                        pltpu.VMEM((N_CHUNKS, GROUP_WIDTH, CHUNK), F32),
                        pltpu.VMEM((2 * n_pairs, LANES, LANES), F32),
                        pltpu.VMEM((2 * n_pairs, 2 * CHUNK), F32)],
        compiler_params=_params(1),
        name="mlstm",
    )(p, p, p, p, gates, w_conv, bg, cos, sina, sinb, g_mlstm.reshape(1, GROUP_WIDTH), tri)


def _dft_tables(n, scale):
    f = 1 << (n.bit_length() // 2)
    s = jnp.arange(n, dtype=jnp.int32)[None, :]
    ang_a = ((f * jnp.arange(n // f, dtype=jnp.int32)[:, None] * s) % n).astype(F32) * (2.0 * jnp.pi / n)
    ang_b = ((jnp.arange(f, dtype=jnp.int32)[:, None] * s) % n).astype(F32) * (2.0 * jnp.pi / n)
    ca, sa = jnp.cos(ang_a)[:, None, :], jnp.sin(ang_a)[:, None, :]
    cb, sb = jnp.cos(ang_b)[None, :, :], jnp.sin(ang_b)[None, :, :]
    cos = (ca * cb - sa * sb).reshape(n, n)
    sin = (sa * cb + ca * sb).reshape(n, n)
    return (cos * scale).astype(BF16), (-sin * scale).astype(BF16)


def _channel_dft():
    gc = GROUP_WIDTH // FNET_GROUPS
    j = jnp.arange(GROUP_WIDTH, dtype=jnp.int32)
    same = (j[:, None] // gc) == (j[None, :] // gc)
    ang = (((j[:, None] % gc) * (j[None, :] % gc)) % gc).astype(F32) * (2.0 * jnp.pi / gc)
    c = jnp.where(same, jnp.cos(ang), 0.0)
    s = jnp.where(same, jnp.sin(ang), 0.0)
    return jnp.concatenate([c, s], axis=1).astype(BF16)


def _fnet_gmlp_body(zc_ref, zs_ref, cl_ref, sl_ref, cc_ref, sc_ref, w_ref, u_ref, z_ref, ws_ref, bs_ref, g_ref,
                    o_ref, gm_ref):
    gmlp = (u_ref, z_ref, ws_ref, bs_ref, g_ref, gm_ref)
    half = N_CHUNKS // 2
    y = _dot(cl_ref[...], zc_ref[0:SEQ, :])
    for n in range(half):
        _gmlp_chunk(n, *gmlp)
    y = y + _dot(sl_ref[...], zs_ref[0:SEQ, :])
    for n in range(half, N_CHUNKS):
        _gmlp_chunk(n, *gmlp)
    o_ref[0:SEQ, :] = _dot(y.astype(BF16), w_ref[...]).astype(BF16)
    y = _dot(cc_ref[...], zc_ref[SEQ:, :]) + _dot(sc_ref[...], zs_ref[SEQ:, :])
    o_ref[SEQ:, :] = _dot(y.astype(BF16), w_ref[...]).astype(BF16)


def _fnet_gmlp(zc, zs, tabs, w_fnet, p, w_spatial, b_spatial, g_gmlp):
    bsz = p.shape[0]
    cl, sl, cc, sc = tabs
    col = pl.BlockSpec((T_ALL, GROUP_WIDTH), lambda b: (0, b))
    bias = jnp.repeat(b_spatial.T, HEAD_DIM, axis=1)
    return pl.pallas_call(
        _fnet_gmlp_body,
        grid=(bsz,),
        in_specs=[col, col, _resident((SEQ, SEQ)), _resident((SEQ, SEQ)),
                  _resident((CTX_LEN, CTX_LEN)), _resident((CTX_LEN, CTX_LEN)),
                  _resident((GROUP_WIDTH, GROUP_WIDTH)),
                  _group_spec(3), _group_spec(4),
                  _resident((GROUP_HEADS, CHUNK, CHUNK)),
                  _resident((CHUNK, GROUP_WIDTH)),
                  _resident((1, GROUP_WIDTH))],
        out_specs=[col, pl.BlockSpec((1, T_ALL, GROUP_WIDTH), lambda b: (b, 0, 0))],
        out_shape=[jax.ShapeDtypeStruct((T_ALL, bsz * GROUP_WIDTH), BF16),
                   jax.ShapeDtypeStruct((bsz, T_ALL, GROUP_WIDTH), BF16)],
        compiler_params=_params(1),
        name="fnet_gmlp",
    )(zc, zs, cl, sl, cc, sc, w_fnet.astype(BF16), p, p, w_spatial.astype(BF16), bias,
      g_gmlp.reshape(1, GROUP_WIDTH))


def _post_body(a_ref, b_ref, c_ref, d_ref, x_ref, cx_ref, ml_ref, mc_ref, g_ref, wo_ref, w1_ref, w2_ref, gf_ref,
               o_ref, h_scr, acc_scr, *, final):
    gw = GROUP_WIDTH
    tm = x_ref.shape[1]
    mod = _mod_rows(ml_ref, mc_ref, tm)
    y = (_dot(a_ref[0], wo_ref[0:gw, :]) + _dot(b_ref[0], wo_ref[gw:2 * gw, :])
         + _dot(c_ref[0], wo_ref[2 * gw:3 * gw, :]) + _dot(d_ref[...], wo_ref[3 * gw:, :]))
    if final:
        x = x_ref[0]
    else:
        ctx_tail = pl.program_id(1) == pl.num_programs(1) - 1
        x = jnp.concatenate([x_ref[0, 0:tm - CTX_LEN, :], _tail_rows(x_ref, cx_ref, ctx_tail)], axis=0)
    x1 = x + mod(2) * y
    h = _rms(x1, g_ref[...]) * (1.0 + mod(4)) + mod(3)
    h_scr[...] = h.astype(BF16)
    step = 512

    def up(j):
        return jnp.maximum(_dot(h_scr[...], w1_ref[:, j:j + step]), 0.0)

    a = up(0)
    for j in range(0, D_FF, step):
        nxt = up(j + step) if j + step < D_FF else None
        part = _dot((a * a).astype(BF16), w2_ref[j:j + step, :])
        if j == 0:
            acc_scr[...] = part
        else:
            acc_scr[...] += part
        a = nxt
    x2 = x1 + mod(5) * acc_scr[...]
    if final:
        x2 = _rms(x2, gf_ref[...])
    o_ref[0] = x2


def _post(a, b_, c_, d, tokens, ctx, mod_l, g_ffn, w_out, w_ff1, w_ff2, g_final, final):
    bsz = tokens.shape[0]
    tm, out_len = (LATENT_TILE, SEQ) if final else (TOKEN_TILE, T_ALL)
    tok = pl.BlockSpec((1, tm, D_MODEL), lambda b, t: (b, t, 0))
    grp = pl.BlockSpec((1, tm, GROUP_WIDTH), lambda b, t: (b, t, 0))
    operands, token_specs = _token_operands(tokens, ctx, tm)
    return pl.pallas_call(
        functools.partial(_post_body, final=final),
        grid=(bsz, out_len // tm),
        in_specs=[grp, grp, grp, pl.BlockSpec((tm, GROUP_WIDTH), lambda b, t: (t, b))] + token_specs + _MOD_SPECS
                 + [_resident((1, D_MODEL)),
                    _resident((D_MODEL, D_MODEL)),
                    _resident((D_MODEL, D_FF)),
                    _resident((D_FF, D_MODEL)),
                    _resident((1, D_MODEL))],
        out_specs=tok,
        out_shape=jax.ShapeDtypeStruct((bsz, out_len, D_MODEL), F32),
        scratch_shapes=[pltpu.VMEM((tm, D_MODEL), BF16), pltpu.VMEM((tm, D_MODEL), F32)],
        compiler_params=_params(2),
        name="out_proj_mlp",
    )(a, b_, c_, d, *operands, mod_l, mod_l, g_ffn.reshape(1, D_MODEL), w_out.astype(BF16), w_ff1.astype(BF16),
      w_ff2.astype(BF16), g_final.reshape(1, D_MODEL))


def kernel(x, c, ctx, c_ctx, w_ada, b_ada, g_norm_mix, g_norm_ffn, w_in, b_gate, w_conv_qk, rpb, w_spatial,
           b_spatial, g_gmlp, g_mlstm, w_fnet, w_out, w_ff1, w_ff2, g_final):
    bsz = x.shape[0]
    assert bsz <= CTX_MOD_ROW and x.shape[1:] == (SEQ, D_MODEL) and ctx.shape[1:] == (CTX_LEN, D_MODEL)
    depth = w_ada.shape[0]
    tokens, ctx_rows = x, ctx
    cc = jnp.zeros((MOD_ROWS, D_MODEL), F32).at[:bsz].set(c).at[CTX_MOD_ROW].set(c_ctx)
    mod = _ada_mod(cc, w_ada, b_ada).reshape(depth, MOD_ROWS, N_MOD, D_MODEL)

    rope = _rope_tables()
    dft_c = _channel_dft()
    tabs = (_dft_tables(SEQ, (SEQ * GROUP_WIDTH // FNET_GROUPS) ** -0.5)
            + _dft_tables(CTX_LEN, (CTX_LEN * GROUP_WIDTH // FNET_GROUPS) ** -0.5))

    for l in range(depth):
        w_cat = _projection_weights(w_in[l])
        p, gates, zc, zs = _in_proj(tokens, ctx_rows, mod[l], g_norm_mix[l], w_cat, dft_c)
        a = _attention(p, _attention_bias(rpb[l]), ctx_out=(l < depth - 1))
        c_ = _mlstm(p, gates, w_conv_qk[l], b_gate[l], g_mlstm[l], rope)
        d, b_ = _fnet_gmlp(zc, zs, tabs, w_fnet[l], p, w_spatial[l], b_spatial[l], g_gmlp[l])
        tokens = _post(a, b_, c_, d, tokens, ctx_rows, mod[l], g_norm_ffn[l], w_out[l], w_ff1[l], w_ff2[l],
                       g_final, final=(l == depth - 1))
        ctx_rows = None
    return tokens
```

```python
import functools

import jax
import jax.numpy as jnp
from jax import lax
from jax.experimental import pallas as pl
from jax.experimental.pallas import tpu as pltpu

D_MODEL = 1024
SEQ = 2048
DEPTH = 2
GRID_W = 64
GRID_ROWS = SEQ // GRID_W
CTX_LEN = 256
T_ALL = CTX_LEN + SEQ
HEAD_DIM = 64
GROUP_WIDTH = 256
GROUP_HEADS = 4
WIN_ROWS = 8
WIN_COLS = 16
CHUNK = 128
N_CHUNKS = T_ALL // CHUNK
CTX_CHUNKS = CTX_LEN // CHUNK
LAT_CHUNKS = SEQ // CHUNK
CONV_W = 3
FNET_GROUPS = 4
ROPE_THETA = 10000.0
D_FF = 4 * D_MODEL
N_MOD = 6
EPS = 1e-6
NEG_INF = -1e30
N_GATES = 4 * GROUP_HEADS
OFF_G = 10 * GROUP_WIDTH
D_IN = OFF_G + N_GATES
MOD_ROWS = 24
CTX_MOD_ROW = 16
LANES = 128
BF16_ROWS = 16
GATE_I_COLS = (0, 2, 8, 10, 1, 3, 9, 11)
GATE_F_COLS = (4, 6, 12, 14, 5, 7, 13, 15)
TOKEN_TILE = 768
PROJ_TILE = 1152
LATENT_TILE = 512
P_WIDTH = 9 * GROUP_WIDTH
VMEM_LIMIT = 56 * 1024 * 1024

F32 = jnp.float32
BF16 = jnp.bfloat16


def _dot(a, b):
    return jnp.dot(a, b, preferred_element_type=F32)


def _dot_nt(a, b):
    return lax.dot_general(a, b, (((1,), (1,)), ((), ())), preferred_element_type=F32)


def _dot_f32(a, b):
    return jnp.dot(a, b, preferred_element_type=F32, precision=lax.Precision.HIGHEST)


def _sigmoid(x):
    return 0.5 * (1.0 + jnp.tanh(0.5 * x))


def _resident(shape):
    nd = len(shape)
    return pl.BlockSpec(shape, lambda *_: (0,) * nd, pipeline_mode=pl.Buffered(1))


def _params(n_axes):
    return pltpu.CompilerParams(dimension_semantics=("arbitrary",) * n_axes,
                                vmem_limit_bytes=VMEM_LIMIT)


def _mod_body(c_ref, w_ref, b_ref, o_ref):
    s = c_ref[...]
    s = s * jax.nn.sigmoid(s)
    o_ref[0] = _dot(s.astype(BF16), w_ref[0].astype(BF16)) + b_ref[0]


def _ada_mod(cc, w_ada, b_ada):
    depth, d, n = w_ada.shape
    tn = 1536
    return pl.pallas_call(
        _mod_body,
        grid=(depth, n // tn),
        in_specs=[pl.BlockSpec((MOD_ROWS, d), lambda l, j: (0, 0)),
                  pl.BlockSpec((1, d, tn), lambda l, j: (l, 0, j)),
                  pl.BlockSpec((1, 1, tn), lambda l, j: (l, 0, j))],
        out_specs=pl.BlockSpec((1, MOD_ROWS, tn), lambda l, j: (l, 0, j)),
        out_shape=jax.ShapeDtypeStruct((depth, MOD_ROWS, n), F32),
        compiler_params=_params(2),
        name="ada_mod",
    )(cc, w_ada, b_ada.reshape(depth, 1, n))


_MOD_SPECS = [pl.BlockSpec((1, N_MOD, D_MODEL), lambda b, t: (b, 0, 0)),
              pl.BlockSpec((1, N_MOD, D_MODEL), lambda b, t: (CTX_MOD_ROW, 0, 0))]


def _mod_rows(mod_lat_ref, mod_ctx_ref, tile):
    tok = pl.program_id(1) * tile + lax.broadcasted_iota(jnp.int32, (tile, 1), 0)
    is_ctx = tok >= SEQ
    return lambda k: jnp.where(is_ctx, mod_ctx_ref[0, k:k + 1, :], mod_lat_ref[0, k:k + 1, :])


def _rms(x, g):
    return x * lax.rsqrt(jnp.mean(x * x, axis=-1, keepdims=True) + EPS) * g


def _tail_rows(x_ref, c_ref, ctx_tail):
    head = x_ref.shape[1] - CTX_LEN
    return jnp.where(ctx_tail, c_ref[0], x_ref[0, head:, :])


def _inproj_body(x_ref, c_ref, ml_ref, mc_ref, g_ref, w_ref, dft_ref, p_ref, gate_ref, zc_ref, zs_ref,
                 lhs_scr, inv_scr):
    tm = x_ref.shape[1]
    head = tm - CTX_LEN
    ctx_tail = pl.program_id(1) == pl.num_programs(1) - 1
    shift_tail = jnp.where(ctx_tail, mc_ref[0, 0:1, :], ml_ref[0, 0:1, :])
    scale_tail = jnp.where(ctx_tail, mc_ref[0, 1:2, :], ml_ref[0, 1:2, :])
    lead = BF16_ROWS
    lhs_scr[0:lead, :] = jnp.concatenate([ml_ref[0, 0:1, :], shift_tail,
                                          jnp.zeros((lead - 2, D_MODEL), F32)], axis=0).astype(BF16)
    x_head = x_ref[0, 0:head, :]
    x_tail = _tail_rows(x_ref, c_ref, ctx_tail)
    lhs_scr[lead:lead + head, :] = (x_head * (g_ref[...] * (1.0 + ml_ref[0, 1:2, :]))).astype(BF16)
    lhs_scr[lead + head:, :] = (x_tail * (g_ref[...] * (1.0 + scale_tail))).astype(BF16)
    for rows, x in ((slice(0, head), x_head), (slice(head, tm), x_tail)):
        inv = lax.rsqrt(jnp.mean(x * x, axis=-1, keepdims=True) + EPS)
        inv_scr[rows, :] = jnp.broadcast_to(inv, (x.shape[0], LANES))

    def project(cols):
        n = (cols.stop - cols.start) // LANES
        r = _dot(lhs_scr[...], w_ref[:, cols])
        inv_head = jnp.concatenate([inv_scr[0:head, :]] * n, axis=1)
        inv_tail = jnp.concatenate([inv_scr[head:, :]] * n, axis=1)
        return jnp.concatenate([inv_head * r[lead:lead + head, :] + r[0:1, :],
                                inv_tail * r[lead + head:, :] + r[1:2, :]], axis=0)

    step = 2 * GROUP_WIDTH
    f = project(slice(P_WIDTH, OFF_G)).astype(BF16)
    p_ref[0, :, 0:step] = project(slice(0, step)).astype(BF16)
    z = _dot(f, dft_ref[...])
    zc_ref[...] = z[:, :GROUP_WIDTH].astype(BF16)
    zs_ref[...] = z[:, GROUP_WIDTH:].astype(BF16)
    for j in range(step, P_WIDTH - GROUP_WIDTH, step):
        p_ref[0, :, j:j + step] = project(slice(j, j + step)).astype(BF16)
    j = P_WIDTH - GROUP_WIDTH
    p_ref[0, :, j:] = project(slice(j, P_WIDTH)).astype(BF16)
    gate_ref[0] = project(slice(OFF_G, OFF_G + LANES))


def _projection_weights(w_in):
    pad = jnp.zeros((w_in.shape[0], LANES - N_GATES), w_in.dtype)
    w_g = w_in[:, OFF_G:]
    return jnp.concatenate([w_in[:, :OFF_G], w_g[:, jnp.array(GATE_I_COLS + GATE_F_COLS)], pad],
                           axis=1).astype(BF16)


def _token_operands(tokens, ctx, tm):
    tile = pl.BlockSpec((1, tm, D_MODEL), lambda b, t: (b, t, 0))
    if ctx is None:
        return (tokens, tokens), [tile, pl.BlockSpec((1, CTX_LEN, D_MODEL), lambda b, t: (b, SEQ // CTX_LEN, 0))]
    return (tokens, ctx), [tile, pl.BlockSpec((1, CTX_LEN, D_MODEL), lambda b, t: (b, 0, 0))]


def _in_proj(tokens, ctx, mod_l, g, w_cat, dft_c):
    bsz = tokens.shape[0]
    tm = PROJ_TILE
    wn = w_cat.shape[1]
    operands, token_specs = _token_operands(tokens, ctx, tm)
    return pl.pallas_call(
        _inproj_body,
        grid=(bsz, T_ALL // tm),
        in_specs=token_specs + _MOD_SPECS
                 + [_resident((1, D_MODEL)),
                    _resident((D_MODEL, wn)),
                    _resident((GROUP_WIDTH, 2 * GROUP_WIDTH))],
        out_specs=[pl.BlockSpec((1, tm, P_WIDTH), lambda b, t: (b, t, 0)),
                   pl.BlockSpec((1, tm, LANES), lambda b, t: (b, t, 0)),
                   pl.BlockSpec((tm, GROUP_WIDTH), lambda b, t: (t, b)),
                   pl.BlockSpec((tm, GROUP_WIDTH), lambda b, t: (t, b))],
        out_shape=[jax.ShapeDtypeStruct((bsz, T_ALL, P_WIDTH), BF16),
                   jax.ShapeDtypeStruct((bsz, T_ALL, LANES), F32),
                   jax.ShapeDtypeStruct((T_ALL, bsz * GROUP_WIDTH), BF16),
                   jax.ShapeDtypeStruct((T_ALL, bsz * GROUP_WIDTH), BF16)],
        scratch_shapes=[pltpu.VMEM((tm + BF16_ROWS, D_MODEL), BF16), pltpu.VMEM((tm, LANES), F32)],
        compiler_params=_params(2),
        name="in_proj",
    )(*operands, mod_l, mod_l, g.reshape(1, D_MODEL), w_cat, dft_c)


def _group_spec(col_block):
    return pl.BlockSpec((1, T_ALL, GROUP_WIDTH), lambda b: (b, 0, col_block))


def _softmax_pv(parts):
    m = functools.reduce(jnp.maximum, [jnp.max(s, axis=-1, keepdims=True) for s, _ in parts])
    es = [jnp.exp(s - m) for s, _ in parts]
    den = functools.reduce(jnp.add, [jnp.sum(e, axis=-1, keepdims=True) for e in es])
    num = functools.reduce(jnp.add, [_dot(e.astype(BF16), v) for e, (_, v) in zip(es, parts)])
    return num / den


def _attn_parts(q_ref, k_ref, v_ref, bias_ref, o_ref, ctx_out):
    scale = HEAD_DIM ** -0.5
    pairs = [slice(i * LANES, (i + 1) * LANES) for i in range(GROUP_WIDTH // LANES)]

    def stack_pair(q2):
        low = lax.broadcasted_iota(jnp.int32, q2.shape, 1) < HEAD_DIM
        zero = jnp.zeros_like(q2)
        return low, jnp.concatenate([jnp.where(low, q2, zero), jnp.where(low, zero, q2)], axis=0)

    def unstack_pair(low, o):
        n = o.shape[0] // 2
        return jnp.where(low, o[:n], o[n:]).astype(BF16)

    ctx = slice(SEQ, T_ALL)

    def context_rows():
        if ctx_out:
            ctx_tiles = []
            for ps in pairs:
                low, qm = stack_pair(q_ref[0, ctx, ps] * scale)
                ctx_tiles.append((low, _dot_nt(qm, k_ref[0, ctx, ps])))
            for ps, (low, s) in zip(pairs, ctx_tiles):
                o_ref[0, ctx, ps] = unstack_pair(low, _softmax_pv([(s, v_ref[0, ctx, ps])]))
        else:
            o_ref[0, ctx, :] = jnp.zeros((CTX_LEN, GROUP_WIDTH), BF16)

    rows_per_trip = 4
    tile_rows = 2 * GRID_W

    def trip(g, carry):
        tiles = [(dr, i) for dr in range(rows_per_trip) for i in range(len(pairs))]
        q0s = [pl.multiple_of((g * rows_per_trip + dr) * GRID_W, GRID_W) for dr in range(rows_per_trip)]
        stacked = {(dr, i): stack_pair(q_ref[0, pl.ds(q0s[dr], GRID_W), pairs[i]] * scale) for dr, i in tiles}
        s_ctx = [_dot_nt(jnp.concatenate([stacked[dr, i][1] for dr in range(rows_per_trip)], axis=0),
                         k_ref[0, ctx, pairs[i]]) for i in range(len(pairs))]

        def window_scores(dr, i):
            r = g * rows_per_trip + dr
            rs = jnp.clip(r - WIN_ROWS // 2, 0, GRID_ROWS - WIN_ROWS)
            k0 = pl.multiple_of(rs * GRID_W, GRID_W)
            s_loc = _dot_nt(stacked[dr, i][1], k_ref[0, pl.ds(k0, WIN_ROWS * GRID_W), pairs[i]])
            return s_loc + bias_ref[i, r - rs], k0

        partial = {}
        cur = window_scores(*tiles[0])
        for j, (dr, i) in enumerate(tiles):
            nxt = window_scores(*tiles[j + 1]) if j + 1 < len(tiles) else None
            s_loc, k0 = cur
            s_c = s_ctx[i][dr * tile_rows:(dr + 1) * tile_rows, :]
            m = jnp.maximum(jnp.max(s_loc, axis=-1, keepdims=True), jnp.max(s_c, axis=-1, keepdims=True))
            e_loc, e_ctx = jnp.exp(s_loc - m), jnp.exp(s_c - m)
            den = jnp.sum(e_loc, axis=-1, keepdims=True) + jnp.sum(e_ctx, axis=-1, keepdims=True)
            num = _dot(e_loc.astype(BF16), v_ref[0, pl.ds(k0, WIN_ROWS * GRID_W), pairs[i]])
            partial[dr, i] = (num, e_ctx.astype(BF16), den)
            cur = nxt

        for i, ps in enumerate(pairs):
            num_ctx = _dot(jnp.concatenate([partial[dr, i][1] for dr in range(rows_per_trip)], axis=0),
                           v_ref[0, ctx, ps])
            for dr in range(rows_per_trip):
                num, _, den = partial[dr, i]
                o = (num + num_ctx[dr * tile_rows:(dr + 1) * tile_rows, :]) / den
                o_ref[0, pl.ds(q0s[dr], GRID_W), ps] = unstack_pair(stacked[dr, i][0], o)
        return carry

    return context_rows, trip, GRID_ROWS // rows_per_trip


def _attention_bias(rpb):
    c = jnp.arange(GRID_W)
    qstart = jnp.clip(c - WIN_COLS // 2, 0, GRID_W - WIN_COLS)
    in_win = (c[None, :] >= qstart[:, None]) & (c[None, :] < qstart[:, None] + WIN_COLS)
    col = jnp.clip(c[None, :] - c[:, None] + WIN_COLS - 1, 0, 2 * WIN_COLS - 2)
    pick_col = (col[:, :, None] == jnp.arange(2 * WIN_COLS - 1)).astype(F32)
    w = jnp.arange(WIN_ROWS)
    pick_row = ((w[None, :] - w[:, None] + WIN_ROWS - 1)[:, :, None]
                == jnp.arange(2 * WIN_ROWS - 1)).astype(F32)
    rpb_pairs = rpb.astype(F32).reshape(GROUP_HEADS // 2, 2, 2 * WIN_ROWS - 1, 2 * WIN_COLS - 1)
    tab = jnp.einsum('phde,vjd,qke->pvhqjk', rpb_pairs, pick_row, pick_col, precision=lax.Precision.HIGHEST)
    tab = jnp.where(in_win[None, None, None, :, None, :], tab, NEG_INF)
    return tab.reshape(GROUP_HEADS // 2, WIN_ROWS, 2 * GRID_W, WIN_ROWS * GRID_W)


def _gmlp_chunk(n, u_ref, z_ref, ws_ref, bs_ref, g_ref, o_ref):
    rows = slice(n * CHUNK, (n + 1) * CHUNK)
    z = _rms(jax.nn.gelu(z_ref[0, rows, :].astype(F32)), g_ref[...]).astype(BF16)
    u = jax.nn.gelu(u_ref[0, rows, :].astype(F32))
    for h in range(GROUP_HEADS):
        hs = slice(h * HEAD_DIM, (h + 1) * HEAD_DIM)
        mixed = _dot(ws_ref[h], z[:, hs]) + bs_ref[:, hs]
        o_ref[0, rows, hs] = (u[:, hs] * mixed).astype(BF16)


def _mlstm_parts(q_ref, k_ref, v_ref, og_ref, gate_ref, wc_ref, bg_ref, cos_ref, sina_ref,
                 sinb_ref, gln_ref, tri_ref, o_ref,
                 qtz_scr, k_scr, kbd_scr, vat_scr, acol_scr, rows_scr, ht_scr, c_scr, m_scr):
    half = HEAD_DIM // 4
    n_chains = 2 * GROUP_HEADS

    def conv_act(ref, n, w0, w1, w2, rope, post):
        rows = pl.ds(pl.multiple_of(n * CHUNK, CHUNK), CHUNK)
        cur = ref[0, rows, :].astype(F32)
        first = (n == 0) | (n == LAT_CHUNKS)
        last = (n == LAT_CHUNKS - 1) | (n == N_CHUNKS - 1)
        before = pl.ds(pl.multiple_of(jnp.maximum(n * CHUNK - BF16_ROWS, 0), BF16_ROWS), BF16_ROWS)
        after = pl.ds(pl.multiple_of(jnp.minimum((n + 1) * CHUNK, T_ALL - BF16_ROWS), BF16_ROWS), BF16_ROWS)
        tail = jnp.where(first, 0.0, 1.0) * ref[0, before, :][BF16_ROWS - 1:BF16_ROWS, :].astype(F32)
        head = jnp.where(last, 0.0, 1.0) * ref[0, after, :][0:1, :].astype(F32)
        sub = lax.broadcasted_iota(jnp.int32, (CHUNK, 1), 0)
        x_prev = jnp.where(sub == 0, tail, pltpu.roll(cur, 1, 0))
        x_next = jnp.where(sub == CHUNK - 1, head, pltpu.roll(cur, CHUNK - 1, 0))
        y = w0 * x_prev + w1 * cur + w2 * x_next
        y = y * _sigmoid(y)
        if rope:
            y = (y * cos_ref[rows, :] + pltpu.roll(y, GROUP_WIDTH - half, 1) * sina_ref[rows, :]
                 + pltpu.roll(y, half, 1) * sinb_ref[rows, :])
        return y * post

    def prep(n, rope):
        rows = pl.ds(pl.multiple_of(n * CHUNK, CHUNK), CHUNK)
        gw = GROUP_WIDTH
        q = conv_act(q_ref, n, wc_ref[0:1, :gw], wc_ref[1:2, :gw], wc_ref[2:3, :gw], rope, 1.0)
        k = conv_act(k_ref, n, wc_ref[0:1, gw:], wc_ref[1:2, gw:], wc_ref[2:3, gw:], rope, HEAD_DIM ** -0.5)
        q_t = q.T.astype(BF16)
        v_t = v_ref[0, rows, :].astype(F32).T.astype(BF16)
        k = k.astype(BF16)
        k_scr[n] = k
        sub = lax.broadcasted_iota(jnp.int32, (HEAD_DIM, CHUNK), 0)
        ones_row = jnp.where(sub == 0, 1.0, 0.0).astype(BF16)
        zeros = jnp.zeros((HEAD_DIM, CHUNK), BF16)
        low = lax.broadcasted_iota(jnp.int32, (CHUNK, LANES), 1) < HEAD_DIM
        for pair in range(GROUP_HEADS // 2):
            h0 = slice(2 * pair * HEAD_DIM, (2 * pair + 1) * HEAD_DIM)
            h1 = slice((2 * pair + 1) * HEAD_DIM, (2 * pair + 2) * HEAD_DIM)
            qtz_scr[n, pair] = jnp.concatenate([jnp.concatenate([q_t[h0, :], zeros], axis=1),
                                                jnp.concatenate([zeros, q_t[h1, :]], axis=1)], axis=0)
            k_pair = k[:, pair * LANES:(pair + 1) * LANES]
            kbd_scr[n, pair] = jnp.concatenate([jnp.where(low, k_pair, jnp.zeros_like(k_pair)),
                                                jnp.where(low, jnp.zeros_like(k_pair), k_pair)], axis=0)
            vat_scr[n, pair] = jnp.concatenate([jnp.concatenate([v_t[h0, :], ones_row], axis=0),
                                                jnp.concatenate([v_t[h1, :], ones_row], axis=0)], axis=1)
        ht_scr[n] = jnp.zeros((GROUP_WIDTH, CHUNK), F32)

        g_i = gate_ref[0, rows, :] + bg_ref[...]
        lf = pltpu.roll(jax.nn.log_sigmoid(g_i), LANES - n_chains, 1)
        pre = _dot_f32(tri_ref[...], lf)
        suf = pre[CHUNK - 1:CHUNK, :] - pre + lf
        lane = lax.broadcasted_iota(jnp.int32, (CHUNK, LANES), 1)
        b = jnp.where(lane % 4 < 2, pre, suf)
        a = g_i - b
        acol_scr[n] = a
        a_t = a.T[0:n_chains, :]
        b_t = b.T[0:n_chains, :]
        chain = lax.broadcasted_iota(jnp.int32, (n_chains, CHUNK), 0)
        b_end = jnp.where(chain % 4 < 2, b_t[:, CHUNK - 1:CHUNK], b_t[:, 0:1])
        b_end = jnp.broadcast_to(b_end, (n_chains, CHUNK))
        a_max = jnp.broadcast_to(jnp.max(a_t, axis=1, keepdims=True), (n_chains, CHUNK))
        for kind, rows8 in enumerate((a_t, b_t, b_end, a_max)):
            rows_scr[n, kind] = jnp.concatenate([rows8[0:4, :], rows8[4:8, :]], axis=1)

    def reset_state():
        c_scr[...] = jnp.zeros(c_scr.shape, F32)
        m_scr[...] = jnp.zeros(m_scr.shape, F32)

    s_idx = lax.broadcasted_iota(jnp.int32, (CHUNK, CHUNK), 0)
    t_idx = lax.broadcasted_iota(jnp.int32, (CHUNK, CHUNK), 1)

    zeros = jnp.zeros((CHUNK, CHUNK), BF16)
    low = lax.broadcasted_iota(jnp.int32, (1, LANES), 1) < HEAD_DIM

    def scan(i, carry):
        chunk_of = (jnp.where(i < CTX_CHUNKS, LAT_CHUNKS + i, i - CTX_CHUNKS), N_CHUNKS - 1 - i)
        steps = [(rev, pair, chunk_of[rev]) for rev in range(2) for pair in range(GROUP_HEADS // 2)]

        scores = []
        for rev, pair, n in steps:
            dp = 2 * rev + pair
            causal = (s_idx >= t_idx) if rev else (s_idx <= t_idx)
            arg = jnp.concatenate([jnp.where(causal, acol_scr[n, :, dp:dp + 1], NEG_INF),
                                   jnp.where(causal, acol_scr[n, :, 4 + dp:5 + dp], NEG_INF)], axis=1)
            s = _dot(k_scr[n, :, pair * LANES:(pair + 1) * LANES], qtz_scr[n, pair])
            scores.append((arg, jnp.max(arg, axis=0, keepdims=True), s))

        inter = []
        for rev, pair, n in steps:
            dp = 2 * rev + pair
            a_row = rows_scr[n, 0, dp:dp + 1, :]
            b_end = rows_scr[n, 2, dp:dp + 1, :]
            a_max = rows_scr[n, 3, dp:dp + 1, :]
            m_mem = m_scr[dp:dp + 1, :]
            c_mem = c_scr[dp]
            inter.append((_dot(c_mem.astype(BF16), qtz_scr[n, pair]), m_mem))
            m_new = b_end + jnp.maximum(m_mem, a_max)
            w_src = jnp.exp(b_end + a_row - m_new)
            decay = jnp.exp(b_end + m_mem - m_new)
            decay = jnp.where(low, decay[:, :CHUNK], decay[:, CHUNK:])
            c_scr[dp] = decay * c_mem + _dot((vat_scr[n, pair].astype(F32) * w_src).astype(BF16),
                                             kbd_scr[n, pair])
            m_scr[dp:dp + 1, :] = m_new

        for (rev, pair, n), (arg, cm, s), (x2, m_mem) in zip(steps, scores, inter):
            dp = 2 * rev + pair
            g = (s * jnp.exp(arg - cm)).astype(BF16)
            g_bd = jnp.concatenate([jnp.concatenate([g[:, :CHUNK], zeros], axis=1),
                                    jnp.concatenate([zeros, g[:, CHUNK:]], axis=1)], axis=0)
            x1 = _dot(vat_scr[n, pair], g_bd)
            b_row = rows_scr[n, 1, dp:dp + 1, :]
            mu = jnp.maximum(cm, m_mem)
            both = x1 * jnp.exp(cm - mu) + x2 * jnp.exp(m_mem - mu)
            den = both[HEAD_DIM:HEAD_DIM + 1, :]
            inv = 1.0 / jnp.maximum(jnp.abs(den), jnp.exp(-(b_row + mu)))
            h_t = both[0:HEAD_DIM, :] * inv
            ht_scr[n, pair * LANES:(pair + 1) * LANES, :] += jnp.concatenate([h_t[:, :CHUNK], h_t[:, CHUNK:]],
                                                                             axis=0)
        return carry

    def finish(n, carry):
        rows = pl.ds(pl.multiple_of(n * CHUNK, CHUNK), CHUNK)
        gate = _sigmoid(og_ref[0, rows, :].astype(F32))
        ys = []
        for h in range(GROUP_HEADS):
            x = ht_scr[n, h * HEAD_DIM:(h + 1) * HEAD_DIM, :]
            mu = jnp.mean(x, axis=0, keepdims=True)
            var = jnp.mean(jnp.square(x - mu), axis=0, keepdims=True)
            ys.append((x - mu) * lax.rsqrt(var + EPS))
        y = jnp.concatenate(ys, axis=0).T * gln_ref[...]
        o_ref[0, rows, :] = (gate * y).astype(BF16)
        return carry

    return prep, reset_state, scan, finish


def _attn_mlstm_body(aq_ref, ak_ref, av_ref, bias_ref, mq_ref, mk_ref, mv_ref, og_ref, gate_ref, wc_ref, bg_ref,
                     cos_ref, sina_ref, sinb_ref, gln_ref, tri_ref, a_ref, m_ref, *scratch, ctx_out):
    context_rows, trip, n_trips = _attn_parts(aq_ref, ak_ref, av_ref, bias_ref, a_ref, ctx_out)
    prep, reset_state, scan, finish = _mlstm_parts(mq_ref, mk_ref, mv_ref, og_ref, gate_ref, wc_ref, bg_ref,
                                                   cos_ref, sina_ref, sinb_ref, gln_ref, tri_ref, m_ref, *scratch)
    per_trip = LAT_CHUNKS // n_trips

    def prep_context(n, carry):
        prep(n, False)
        return carry

    def trip_and_prep(g, carry):
        for j in range(per_trip):
            prep(g * per_trip + j, True)
        trip(g, carry)
        return carry

    lax.fori_loop(LAT_CHUNKS, N_CHUNKS, prep_context, 0)
    context_rows()
    lax.fori_loop(0, n_trips, trip_and_prep, 0)
    reset_state()
    lax.fori_loop(0, N_CHUNKS, scan, 0, unroll=3)
    lax.fori_loop(0, N_CHUNKS, finish, 0, unroll=3)


def _rope_tables():
    t = jnp.arange(SEQ)
    pos = jnp.stack([(t // GRID_W).astype(F32), (t % GRID_W).astype(F32)], axis=1)
    lane = jnp.arange(GROUP_WIDTH)
    m = HEAD_DIM // 4
    inv = ROPE_THETA ** (-(lane % m).astype(F32) / m)
    axis = (lane % HEAD_DIM) // (HEAD_DIM // 2)
    ang = jnp.where(axis[None, :] == 0, pos[:, 0:1], pos[:, 1:2]) * inv[None, :]
    low = ((lane % (2 * m)) < m)[None, :]
    cos, sin = jnp.cos(ang), jnp.sin(ang)
    return cos, jnp.where(low, -sin, 0.0), jnp.where(low, 0.0, sin)


def _attn_mlstm(p, bias_tab, gates, w_conv, b_gate, g_mlstm, rope, ctx_out):
    bsz = p.shape[0]
    i = jnp.arange(CHUNK)
    tri = (i[:, None] >= i[None, :]).astype(F32)
    bg = jnp.pad(b_gate[jnp.array(GATE_I_COLS + GATE_F_COLS)], (0, LANES - N_GATES)).reshape(1, LANES)
    cos, sina, sinb = rope
    n_pairs = GROUP_HEADS // 2
    group_out = pl.BlockSpec((1, T_ALL, GROUP_WIDTH), lambda b: (b, 0, 0))
    return pl.pallas_call(
        functools.partial(_attn_mlstm_body, ctx_out=ctx_out),
        grid=(bsz,),
        in_specs=[_group_spec(0), _group_spec(1), _group_spec(2), _resident(bias_tab.shape),
                  _group_spec(5), _group_spec(6), _group_spec(7), _group_spec(8),
                  pl.BlockSpec((1, T_ALL, LANES), lambda b: (b, 0, 0)),
                  _resident((CONV_W, 2 * GROUP_WIDTH)),
                  _resident((1, LANES)),
                  _resident((SEQ, GROUP_WIDTH)), _resident((SEQ, GROUP_WIDTH)), _resident((SEQ, GROUP_WIDTH)),
                  _resident((1, GROUP_WIDTH)),
                  _resident((CHUNK, CHUNK))],
        out_specs=[group_out, group_out],
        out_shape=[jax.ShapeDtypeStruct((bsz, T_ALL, GROUP_WIDTH), BF16)] * 2,
        scratch_shapes=[pltpu.VMEM((N_CHUNKS, n_pairs, LANES, 2 * CHUNK), BF16),
                        pltpu.VMEM((N_CHUNKS, CHUNK, GROUP_WIDTH), BF16),
                        pltpu.VMEM((N_CHUNKS, n_pairs, 2 * CHUNK, LANES), BF16),
                        pltpu.VMEM((N_CHUNKS, n_pairs, LANES, 2 * CHUNK), BF16),
                        pltpu.VMEM((N_CHUNKS, CHUNK, LANES), F32),
                        pltpu.VMEM((N_CHUNKS, 4, 2 * n_pairs, 2 * CHUNK), F32),
                        pltpu.VMEM((N_CHUNKS, GROUP_WIDTH, CHUNK), F32),
                        pltpu.VMEM((2 * n_pairs, LANES, LANES), F32),
                        pltpu.VMEM((2 * n_pairs, 2 * CHUNK), F32)],
        compiler_params=_params(1),
        name="attn_mlstm",
    )(p, p, p, bias_tab, p, p, p, p, gates, w_conv, bg, cos, sina, sinb, g_mlstm.reshape(1, GROUP_WIDTH), tri)


def _dft_tables(n, scale):
    f = 1 << (n.bit_length() // 2)
    s = jnp.arange(n, dtype=jnp.int32)[None, :]
    ang_a = ((f * jnp.arange(n // f, dtype=jnp.int32)[:, None] * s) % n).astype(F32) * (2.0 * jnp.pi / n)
    ang_b = ((jnp.arange(f, dtype=jnp.int32)[:, None] * s) % n).astype(F32) * (2.0 * jnp.pi / n)
    ca, sa = jnp.cos(ang_a)[:, None, :], jnp.sin(ang_a)[:, None, :]
    cb, sb = jnp.cos(ang_b)[None, :, :], jnp.sin(ang_b)[None, :, :]
    cos = (ca * cb - sa * sb).reshape(n, n)
    sin = (sa * cb + ca * sb).reshape(n, n)
    return (cos * scale).astype(BF16), (-sin * scale).astype(BF16)


def _channel_dft():
    gc = GROUP_WIDTH // FNET_GROUPS
    j = jnp.arange(GROUP_WIDTH, dtype=jnp.int32)
    same = (j[:, None] // gc) == (j[None, :] // gc)
    ang = (((j[:, None] % gc) * (j[None, :] % gc)) % gc).astype(F32) * (2.0 * jnp.pi / gc)
    c = jnp.where(same, jnp.cos(ang), 0.0)
    s = jnp.where(same, jnp.sin(ang), 0.0)
    return jnp.concatenate([c, s], axis=1).astype(BF16)


def _fnet_gmlp_body(zc_ref, zs_ref, cl_ref, sl_ref, cc_ref, sc_ref, w_ref, u_ref, z_ref, ws_ref, bs_ref, g_ref,
                    o_ref, gm_ref):
    gmlp = (u_ref, z_ref, ws_ref, bs_ref, g_ref, gm_ref)
    half = N_CHUNKS // 2
    y = _dot(cl_ref[...], zc_ref[0:SEQ, :])
    for n in range(half):
        _gmlp_chunk(n, *gmlp)
    y = y + _dot(sl_ref[...], zs_ref[0:SEQ, :])
    for n in range(half, N_CHUNKS):
        _gmlp_chunk(n, *gmlp)
    o_ref[0:SEQ, :] = _dot(y.astype(BF16), w_ref[...]).astype(BF16)
    y = _dot(cc_ref[...], zc_ref[SEQ:, :]) + _dot(sc_ref[...], zs_ref[SEQ:, :])
    o_ref[SEQ:, :] = _dot(y.astype(BF16), w_ref[...]).astype(BF16)


def _fnet_gmlp(zc, zs, tabs, w_fnet, p, w_spatial, b_spatial, g_gmlp):
    bsz = p.shape[0]
    cl, sl, cc, sc = tabs
    col = pl.BlockSpec((T_ALL, GROUP_WIDTH), lambda b: (0, b))
    bias = jnp.repeat(b_spatial.T, HEAD_DIM, axis=1)
    return pl.pallas_call(
        _fnet_gmlp_body,
        grid=(bsz,),
        in_specs=[col, col, _resident((SEQ, SEQ)), _resident((SEQ, SEQ)),
                  _resident((CTX_LEN, CTX_LEN)), _resident((CTX_LEN, CTX_LEN)),
                  _resident((GROUP_WIDTH, GROUP_WIDTH)),
                  _group_spec(3), _group_spec(4),
                  _resident((GROUP_HEADS, CHUNK, CHUNK)),
                  _resident((CHUNK, GROUP_WIDTH)),
                  _resident((1, GROUP_WIDTH))],
        out_specs=[col, pl.BlockSpec((1, T_ALL, GROUP_WIDTH), lambda b: (b, 0, 0))],
        out_shape=[jax.ShapeDtypeStruct((T_ALL, bsz * GROUP_WIDTH), BF16),
                   jax.ShapeDtypeStruct((bsz, T_ALL, GROUP_WIDTH), BF16)],
        compiler_params=_params(1),
        name="fnet_gmlp",
    )(zc, zs, cl, sl, cc, sc, w_fnet.astype(BF16), p, p, w_spatial.astype(BF16), bias,
      g_gmlp.reshape(1, GROUP_WIDTH))


def _post_body(a_ref, b_ref, c_ref, d_ref, x_ref, cx_ref, ml_ref, mc_ref, g_ref, wo_ref, w1_ref, w2_ref, gf_ref,
               o_ref, h_scr, acc_scr, *, final):
    gw = GROUP_WIDTH
    tm = x_ref.shape[1]
    mod = _mod_rows(ml_ref, mc_ref, tm)
    y = (_dot(a_ref[0], wo_ref[0:gw, :]) + _dot(b_ref[0], wo_ref[gw:2 * gw, :])
         + _dot(c_ref[0], wo_ref[2 * gw:3 * gw, :]) + _dot(d_ref[...], wo_ref[3 * gw:, :]))
    if final:
        x = x_ref[0]
    else:
        ctx_tail = pl.program_id(1) == pl.num_programs(1) - 1
        x = jnp.concatenate([x_ref[0, 0:tm - CTX_LEN, :], _tail_rows(x_ref, cx_ref, ctx_tail)], axis=0)
    x1 = x + mod(2) * y
    h = _rms(x1, g_ref[...]) * (1.0 + mod(4)) + mod(3)
    h_scr[...] = h.astype(BF16)
    step = 512

    def up(j):
        return jnp.maximum(_dot(h_scr[...], w1_ref[:, j:j + step]), 0.0)

    a = up(0)
    for j in range(0, D_FF, step):
        nxt = up(j + step) if j + step < D_FF else None
        part = _dot((a * a).astype(BF16), w2_ref[j:j + step, :])
        if j == 0:
            acc_scr[...] = part
        else:
            acc_scr[...] += part
        a = nxt
    x2 = x1 + mod(5) * acc_scr[...]
    if final:
        x2 = _rms(x2, gf_ref[...])
    o_ref[0] = x2


def _post(a, b_, c_, d, tokens, ctx, mod_l, g_ffn, w_out, w_ff1, w_ff2, g_final, final):
    bsz = tokens.shape[0]
    tm, out_len = (LATENT_TILE, SEQ) if final else (TOKEN_TILE, T_ALL)
    tok = pl.BlockSpec((1, tm, D_MODEL), lambda b, t: (b, t, 0))
    grp = pl.BlockSpec((1, tm, GROUP_WIDTH), lambda b, t: (b, t, 0))
    operands, token_specs = _token_operands(tokens, ctx, tm)
    return pl.pallas_call(
        functools.partial(_post_body, final=final),
        grid=(bsz, out_len // tm),
        in_specs=[grp, grp, grp, pl.BlockSpec((tm, GROUP_WIDTH), lambda b, t: (t, b))] + token_specs + _MOD_SPECS
                 + [_resident((1, D_MODEL)),
                    _resident((D_MODEL, D_MODEL)),
                    _resident((D_MODEL, D_FF)),
                    _resident((D_FF, D_MODEL)),
                    _resident((1, D_MODEL))],
        out_specs=tok,
        out_shape=jax.ShapeDtypeStruct((bsz, out_len, D_MODEL), F32),
        scratch_shapes=[pltpu.VMEM((tm, D_MODEL), BF16), pltpu.VMEM((tm, D_MODEL), F32)],
        compiler_params=_params(2),
        name="out_proj_mlp",
    )(a, b_, c_, d, *operands, mod_l, mod_l, g_ffn.reshape(1, D_MODEL), w_out.astype(BF16), w_ff1.astype(BF16),
      w_ff2.astype(BF16), g_final.reshape(1, D_MODEL))


def kernel(x, c, ctx, c_ctx, w_ada, b_ada, g_norm_mix, g_norm_ffn, w_in, b_gate, w_conv_qk, rpb, w_spatial,
           b_spatial, g_gmlp, g_mlstm, w_fnet, w_out, w_ff1, w_ff2, g_final):
    bsz = x.shape[0]
    assert bsz <= CTX_MOD_ROW and x.shape[1:] == (SEQ, D_MODEL) and ctx.shape[1:] == (CTX_LEN, D_MODEL)
    depth = w_ada.shape[0]
    tokens, ctx_rows = x, ctx
    cc = jnp.zeros((MOD_ROWS, D_MODEL), F32).at[:bsz].set(c).at[CTX_MOD_ROW].set(c_ctx)
    mod = _ada_mod(cc, w_ada, b_ada).reshape(depth, MOD_ROWS, N_MOD, D_MODEL)

    rope = _rope_tables()
    dft_c = _channel_dft()
    tabs = (_dft_tables(SEQ, (SEQ * GROUP_WIDTH // FNET_GROUPS) ** -0.5)
            + _dft_tables(CTX_LEN, (CTX_LEN * GROUP_WIDTH // FNET_GROUPS) ** -0.5))

    for l in range(depth):
        w_cat = _projection_weights(w_in[l])
        p, gates, zc, zs = _in_proj(tokens, ctx_rows, mod[l], g_norm_mix[l], w_cat, dft_c)
        a, c_ = _attn_mlstm(p, _attention_bias(rpb[l]), gates, w_conv_qk[l], b_gate[l], g_mlstm[l], rope,
                            ctx_out=(l < depth - 1))
        d, b_ = _fnet_gmlp(zc, zs, tabs, w_fnet[l], p, w_spatial[l], b_spatial[l], g_gmlp[l])
        tokens = _post(a, b_, c_, d, tokens, ctx_rows, mod[l], g_norm_ffn[l], w_out[l], w_ff1[l], w_ff2[l],
                       g_final, final=(l == depth - 1))
        ctx_rows = None
    return tokens
```

```python
import functools

import jax
import jax.numpy as jnp
from jax import lax
from jax.experimental import pallas as pl
from jax.experimental.pallas import tpu as pltpu

D_MODEL = 1024
SEQ = 2048
DEPTH = 2
GRID_W = 64
GRID_ROWS = SEQ // GRID_W
CTX_LEN = 256
T_ALL = CTX_LEN + SEQ
HEAD_DIM = 64
GROUP_WIDTH = 256
GROUP_HEADS = 4
WIN_ROWS = 8
WIN_COLS = 16
CHUNK = 128
N_CHUNKS = T_ALL // CHUNK
CTX_CHUNKS = CTX_LEN // CHUNK
LAT_CHUNKS = SEQ // CHUNK
CONV_W = 3
FNET_GROUPS = 4
ROPE_THETA = 10000.0
D_FF = 4 * D_MODEL
N_MOD = 6
EPS = 1e-6
NEG_INF = -1e30
N_GATES = 4 * GROUP_HEADS
OFF_G = 10 * GROUP_WIDTH
D_IN = OFF_G + N_GATES
MOD_ROWS = 24
CTX_MOD_ROW = 16
LANES = 128
BF16_ROWS = 16
GATE_I_COLS = (0, 2, 8, 10, 1, 3, 9, 11)
GATE_F_COLS = (4, 6, 12, 14, 5, 7, 13, 15)
TOKEN_TILE = 768
PROJ_TILE = 1152
LATENT_TILE = 512
P_WIDTH = 9 * GROUP_WIDTH
VMEM_LIMIT = 56 * 1024 * 1024

F32 = jnp.float32
BF16 = jnp.bfloat16


def _dot(a, b):
    return jnp.dot(a, b, preferred_element_type=F32)


def _dot_nt(a, b):
    return lax.dot_general(a, b, (((1,), (1,)), ((), ())), preferred_element_type=F32)


def _dot_f32(a, b):
    return jnp.dot(a, b, preferred_element_type=F32, precision=lax.Precision.HIGHEST)


def _sigmoid(x):
    return 0.5 * (1.0 + jnp.tanh(0.5 * x))


def _resident(shape):
    nd = len(shape)
    return pl.BlockSpec(shape, lambda *_: (0,) * nd, pipeline_mode=pl.Buffered(1))


def _params(n_axes):
    return pltpu.CompilerParams(dimension_semantics=("arbitrary",) * n_axes,
                                vmem_limit_bytes=VMEM_LIMIT)


def _mod_body(c_ref, w_ref, b_ref, o_ref):
    s = c_ref[...]
    s = s * jax.nn.sigmoid(s)
    o_ref[0] = _dot(s.astype(BF16), w_ref[0].astype(BF16)) + b_ref[0]


def _ada_mod(cc, w_ada, b_ada):
    depth, d, n = w_ada.shape
    tn = 1536
    return pl.pallas_call(
        _mod_body,
        grid=(depth, n // tn),
        in_specs=[pl.BlockSpec((MOD_ROWS, d), lambda l, j: (0, 0)),
                  pl.BlockSpec((1, d, tn), lambda l, j: (l, 0, j)),
                  pl.BlockSpec((1, 1, tn), lambda l, j: (l, 0, j))],
        out_specs=pl.BlockSpec((1, MOD_ROWS, tn), lambda l, j: (l, 0, j)),
        out_shape=jax.ShapeDtypeStruct((depth, MOD_ROWS, n), F32),
        compiler_params=_params(2),
        name="ada_mod",
    )(cc, w_ada, b_ada.reshape(depth, 1, n))


_MOD_SPECS = [pl.BlockSpec((1, N_MOD, D_MODEL), lambda b, t: (b, 0, 0)),
              pl.BlockSpec((1, N_MOD, D_MODEL), lambda b, t: (CTX_MOD_ROW, 0, 0))]


def _mod_rows(mod_lat_ref, mod_ctx_ref, tile):
    tok = pl.program_id(1) * tile + lax.broadcasted_iota(jnp.int32, (tile, 1), 0)
    is_ctx = tok >= SEQ
    return lambda k: jnp.where(is_ctx, mod_ctx_ref[0, k:k + 1, :], mod_lat_ref[0, k:k + 1, :])


def _rms(x, g):
    return x * lax.rsqrt(jnp.mean(x * x, axis=-1, keepdims=True) + EPS) * g


def _tail_rows(x_ref, c_ref, ctx_tail):
    head = x_ref.shape[1] - CTX_LEN
    return jnp.where(ctx_tail, c_ref[0], x_ref[0, head:, :])


def _inproj_body(x_ref, c_ref, ml_ref, mc_ref, g_ref, w_ref, dft_ref, p_ref, gate_ref, zc_ref, zs_ref,
                 lhs_scr, inv_scr):
    tm = x_ref.shape[1]
    head = tm - CTX_LEN
    ctx_tail = pl.program_id(1) == pl.num_programs(1) - 1
    shift_tail = jnp.where(ctx_tail, mc_ref[0, 0:1, :], ml_ref[0, 0:1, :])
    scale_tail = jnp.where(ctx_tail, mc_ref[0, 1:2, :], ml_ref[0, 1:2, :])
    lead = BF16_ROWS
    lhs_scr[0:lead, :] = jnp.concatenate([ml_ref[0, 0:1, :], shift_tail,
                                          jnp.zeros((lead - 2, D_MODEL), F32)], axis=0).astype(BF16)
    x_head = x_ref[0, 0:head, :]
    x_tail = _tail_rows(x_ref, c_ref, ctx_tail)
    lhs_scr[lead:lead + head, :] = (x_head * (g_ref[...] * (1.0 + ml_ref[0, 1:2, :]))).astype(BF16)
    lhs_scr[lead + head:, :] = (x_tail * (g_ref[...] * (1.0 + scale_tail))).astype(BF16)
    for rows, x in ((slice(0, head), x_head), (slice(head, tm), x_tail)):
        inv = lax.rsqrt(jnp.mean(x * x, axis=-1, keepdims=True) + EPS)
        inv_scr[rows, :] = jnp.broadcast_to(inv, (x.shape[0], LANES))

    def project(cols):
        n = (cols.stop - cols.start) // LANES
        r = _dot(lhs_scr[...], w_ref[:, cols])
        inv_head = jnp.concatenate([inv_scr[0:head, :]] * n, axis=1)
        inv_tail = jnp.concatenate([inv_scr[head:, :]] * n, axis=1)
        return jnp.concatenate([inv_head * r[lead:lead + head, :] + r[0:1, :],
                                inv_tail * r[lead + head:, :] + r[1:2, :]], axis=0)

    step = 2 * GROUP_WIDTH
    f = project(slice(P_WIDTH, OFF_G)).astype(BF16)
    p_ref[0, :, 0:step] = project(slice(0, step)).astype(BF16)
    z = _dot(f, dft_ref[...])
    zc_ref[...] = z[:, :GROUP_WIDTH].astype(BF16)
    zs_ref[...] = z[:, GROUP_WIDTH:].astype(BF16)
    for j in range(step, P_WIDTH - GROUP_WIDTH, step):
        p_ref[0, :, j:j + step] = project(slice(j, j + step)).astype(BF16)
    j = P_WIDTH - GROUP_WIDTH
    p_ref[0, :, j:] = project(slice(j, P_WIDTH)).astype(BF16)
    gate_ref[0] = project(slice(OFF_G, OFF_G + LANES))


def _projection_weights(w_in):
    pad = jnp.zeros((w_in.shape[0], LANES - N_GATES), w_in.dtype)
    w_g = w_in[:, OFF_G:]
    return jnp.concatenate([w_in[:, :OFF_G], w_g[:, jnp.array(GATE_I_COLS + GATE_F_COLS)], pad],
                           axis=1).astype(BF16)


def _token_operands(tokens, ctx, tm):
    tile = pl.BlockSpec((1, tm, D_MODEL), lambda b, t: (b, t, 0))
    if ctx is None:
        return (tokens, tokens), [tile, pl.BlockSpec((1, CTX_LEN, D_MODEL), lambda b, t: (b, SEQ // CTX_LEN, 0))]
    return (tokens, ctx), [tile, pl.BlockSpec((1, CTX_LEN, D_MODEL), lambda b, t: (b, 0, 0))]


def _in_proj(tokens, ctx, mod_l, g, w_cat, dft_c):
    bsz = tokens.shape[0]
    tm = PROJ_TILE
    wn = w_cat.shape[1]
    operands, token_specs = _token_operands(tokens, ctx, tm)
    return pl.pallas_call(
        _inproj_body,
        grid=(bsz, T_ALL // tm),
        in_specs=token_specs + _MOD_SPECS
                 + [_resident((1, D_MODEL)),
                    _resident((D_MODEL, wn)),
                    _resident((GROUP_WIDTH, 2 * GROUP_WIDTH))],
        out_specs=[pl.BlockSpec((1, tm, P_WIDTH), lambda b, t: (b, t, 0)),
                   pl.BlockSpec((1, tm, LANES), lambda b, t: (b, t, 0)),
                   pl.BlockSpec((tm, GROUP_WIDTH), lambda b, t: (t, b)),
                   pl.BlockSpec((tm, GROUP_WIDTH), lambda b, t: (t, b))],
        out_shape=[jax.ShapeDtypeStruct((bsz, T_ALL, P_WIDTH), BF16),
                   jax.ShapeDtypeStruct((bsz, T_ALL, LANES), F32),
                   jax.ShapeDtypeStruct((T_ALL, bsz * GROUP_WIDTH), BF16),
                   jax.ShapeDtypeStruct((T_ALL, bsz * GROUP_WIDTH), BF16)],
        scratch_shapes=[pltpu.VMEM((tm + BF16_ROWS, D_MODEL), BF16), pltpu.VMEM((tm, LANES), F32)],
        compiler_params=_params(2),
        name="in_proj",
    )(*operands, mod_l, mod_l, g.reshape(1, D_MODEL), w_cat, dft_c)


def _group_spec(col_block):
    return pl.BlockSpec((1, T_ALL, GROUP_WIDTH), lambda b: (b, 0, col_block))


def _softmax_pv(parts):
    m = functools.reduce(jnp.maximum, [jnp.max(s, axis=-1, keepdims=True) for s, _ in parts])
    es = [jnp.exp(s - m) for s, _ in parts]
    den = functools.reduce(jnp.add, [jnp.sum(e, axis=-1, keepdims=True) for e in es])
    num = functools.reduce(jnp.add, [_dot(e.astype(BF16), v) for e, (_, v) in zip(es, parts)])
    return num / den


def _attn_parts(q_ref, k_ref, v_ref, bias_ref, o_ref, ctx_out):
    scale = HEAD_DIM ** -0.5
    pairs = [slice(i * LANES, (i + 1) * LANES) for i in range(GROUP_WIDTH // LANES)]

    def stack_pair(q2):
        low = lax.broadcasted_iota(jnp.int32, q2.shape, 1) < HEAD_DIM
        zero = jnp.zeros_like(q2)
        return low, jnp.concatenate([jnp.where(low, q2, zero), jnp.where(low, zero, q2)], axis=0)

    def unstack_pair(low, o):
        n = o.shape[0] // 2
        return jnp.where(low, o[:n], o[n:]).astype(BF16)

    ctx = slice(SEQ, T_ALL)

    def context_rows():
        if ctx_out:
            ctx_tiles = []
            for ps in pairs:
                low, qm = stack_pair(q_ref[0, ctx, ps] * scale)
                ctx_tiles.append((low, _dot_nt(qm, k_ref[0, ctx, ps])))
            for ps, (low, s) in zip(pairs, ctx_tiles):
                o_ref[0, ctx, ps] = unstack_pair(low, _softmax_pv([(s, v_ref[0, ctx, ps])]))
        else:
            o_ref[0, ctx, :] = jnp.zeros((CTX_LEN, GROUP_WIDTH), BF16)

    rows_per_trip = 4
    tile_rows = 2 * GRID_W

    def trip(g, carry):
        tiles = [(dr, i) for dr in range(rows_per_trip) for i in range(len(pairs))]
        q0s = [pl.multiple_of((g * rows_per_trip + dr) * GRID_W, GRID_W) for dr in range(rows_per_trip)]
        stacked = {(dr, i): stack_pair(q_ref[0, pl.ds(q0s[dr], GRID_W), pairs[i]] * scale) for dr, i in tiles}
        s_ctx = [_dot_nt(jnp.concatenate([stacked[dr, i][1] for dr in range(rows_per_trip)], axis=0),
                         k_ref[0, ctx, pairs[i]]) for i in range(len(pairs))]

        def window_scores(dr, i):
            r = g * rows_per_trip + dr
            rs = jnp.clip(r - WIN_ROWS // 2, 0, GRID_ROWS - WIN_ROWS)
            k0 = pl.multiple_of(rs * GRID_W, GRID_W)
            s_loc = _dot_nt(stacked[dr, i][1], k_ref[0, pl.ds(k0, WIN_ROWS * GRID_W), pairs[i]])
            return s_loc + bias_ref[i, r - rs], k0

        partial = {}
        cur = window_scores(*tiles[0])
        for j, (dr, i) in enumerate(tiles):
            nxt = window_scores(*tiles[j + 1]) if j + 1 < len(tiles) else None
            s_loc, k0 = cur
            s_c = s_ctx[i][dr * tile_rows:(dr + 1) * tile_rows, :]
            m = jnp.maximum(jnp.max(s_loc, axis=-1, keepdims=True), jnp.max(s_c, axis=-1, keepdims=True))
            e_loc, e_ctx = jnp.exp(s_loc - m), jnp.exp(s_c - m)
            den = jnp.sum(e_loc, axis=-1, keepdims=True) + jnp.sum(e_ctx, axis=-1, keepdims=True)
            num = _dot(e_loc.astype(BF16), v_ref[0, pl.ds(k0, WIN_ROWS * GRID_W), pairs[i]])
            partial[dr, i] = (num, e_ctx.astype(BF16), den)
            cur = nxt

        for i, ps in enumerate(pairs):
            num_ctx = _dot(jnp.concatenate([partial[dr, i][1] for dr in range(rows_per_trip)], axis=0),
                           v_ref[0, ctx, ps])
            for dr in range(rows_per_trip):
                num, _, den = partial[dr, i]
                o = (num + num_ctx[dr * tile_rows:(dr + 1) * tile_rows, :]) / den
                o_ref[0, pl.ds(q0s[dr], GRID_W), ps] = unstack_pair(stacked[dr, i][0], o)
        return carry

    return context_rows, trip, GRID_ROWS // rows_per_trip


def _attention_bias(rpb):
    c = jnp.arange(GRID_W)
    qstart = jnp.clip(c - WIN_COLS // 2, 0, GRID_W - WIN_COLS)
    in_win = (c[None, :] >= qstart[:, None]) & (c[None, :] < qstart[:, None] + WIN_COLS)
    col = jnp.clip(c[None, :] - c[:, None] + WIN_COLS - 1, 0, 2 * WIN_COLS - 2)
    pick_col = (col[:, :, None] == jnp.arange(2 * WIN_COLS - 1)).astype(F32)
    by_drow = jnp.einsum('hde,qke->hdqk', rpb.astype(F32), pick_col, precision=lax.Precision.HIGHEST)
    return jnp.where(in_win, by_drow, NEG_INF)


def _expand_bias(by_drow_ref, bias_scr):
    for pair in range(GROUP_HEADS // 2):
        for v in range(WIN_ROWS):
            for hh in range(2):
                for j in range(0, WIN_ROWS, 2):
                    two_rows = [by_drow_ref[2 * pair + hh, jj - v + WIN_ROWS - 1] for jj in (j, j + 1)]
                    bias_scr[pair, v, hh * GRID_W:(hh + 1) * GRID_W, j * GRID_W:(j + 2) * GRID_W] = (
                        jnp.concatenate(two_rows, axis=1))


def _gmlp_chunk(n, u_ref, z_ref, ws_ref, bs_ref, g_ref, o_ref):
    rows = slice(n * CHUNK, (n + 1) * CHUNK)
    z = _rms(jax.nn.gelu(z_ref[0, rows, :].astype(F32)), g_ref[...]).astype(BF16)
    u = jax.nn.gelu(u_ref[0, rows, :].astype(F32))
    for h in range(GROUP_HEADS):
        hs = slice(h * HEAD_DIM, (h + 1) * HEAD_DIM)
        mixed = _dot(ws_ref[h], z[:, hs]) + bs_ref[:, hs]
        o_ref[0, rows, hs] = (u[:, hs] * mixed).astype(BF16)


def _mlstm_parts(q_ref, k_ref, v_ref, og_ref, gate_ref, wc_ref, bg_ref, cos_ref, sina_ref,
                 sinb_ref, gln_ref, tri_ref, o_ref,
                 qtz_scr, k_scr, kbd_scr, vat_scr, acol_scr, rows_scr, ht_scr, c_scr, m_scr):
    half = HEAD_DIM // 4
    n_chains = 2 * GROUP_HEADS

    def conv_act(ref, n, w0, w1, w2, rope, post):
        rows = pl.ds(pl.multiple_of(n * CHUNK, CHUNK), CHUNK)
        cur = ref[0, rows, :].astype(F32)
        first = (n == 0) | (n == LAT_CHUNKS)
        last = (n == LAT_CHUNKS - 1) | (n == N_CHUNKS - 1)
        before = pl.ds(pl.multiple_of(jnp.maximum(n * CHUNK - BF16_ROWS, 0), BF16_ROWS), BF16_ROWS)
        after = pl.ds(pl.multiple_of(jnp.minimum((n + 1) * CHUNK, T_ALL - BF16_ROWS), BF16_ROWS), BF16_ROWS)
        tail = jnp.where(first, 0.0, 1.0) * ref[0, before, :][BF16_ROWS - 1:BF16_ROWS, :].astype(F32)
        head = jnp.where(last, 0.0, 1.0) * ref[0, after, :][0:1, :].astype(F32)
        sub = lax.broadcasted_iota(jnp.int32, (CHUNK, 1), 0)
        x_prev = jnp.where(sub == 0, tail, pltpu.roll(cur, 1, 0))
        x_next = jnp.where(sub == CHUNK - 1, head, pltpu.roll(cur, CHUNK - 1, 0))
        y = w0 * x_prev + w1 * cur + w2 * x_next
        y = y * _sigmoid(y)
        if rope:
            y = (y * cos_ref[rows, :] + pltpu.roll(y, GROUP_WIDTH - half, 1) * sina_ref[rows, :]
                 + pltpu.roll(y, half, 1) * sinb_ref[rows, :])
        return y * post

    def prep(n, rope):
        rows = pl.ds(pl.multiple_of(n * CHUNK, CHUNK), CHUNK)
        gw = GROUP_WIDTH
        q = conv_act(q_ref, n, wc_ref[0:1, :gw], wc_ref[1:2, :gw], wc_ref[2:3, :gw], rope, 1.0)
        k = conv_act(k_ref, n, wc_ref[0:1, gw:], wc_ref[1:2, gw:], wc_ref[2:3, gw:], rope, HEAD_DIM ** -0.5)
        q_t = q.T.astype(BF16)
        v_t = v_ref[0, rows, :].astype(F32).T.astype(BF16)
        k = k.astype(BF16)
        k_scr[n] = k
        sub = lax.broadcasted_iota(jnp.int32, (HEAD_DIM, CHUNK), 0)
        ones_row = jnp.where(sub == 0, 1.0, 0.0).astype(BF16)
        zeros = jnp.zeros((HEAD_DIM, CHUNK), BF16)
        low = lax.broadcasted_iota(jnp.int32, (CHUNK, LANES), 1) < HEAD_DIM
        for pair in range(GROUP_HEADS // 2):
            h0 = slice(2 * pair * HEAD_DIM, (2 * pair + 1) * HEAD_DIM)
            h1 = slice((2 * pair + 1) * HEAD_DIM, (2 * pair + 2) * HEAD_DIM)
            qtz_scr[n, pair] = jnp.concatenate([jnp.concatenate([q_t[h0, :], zeros], axis=1),
                                                jnp.concatenate([zeros, q_t[h1, :]], axis=1)], axis=0)
            k_pair = k[:, pair * LANES:(pair + 1) * LANES]
            kbd_scr[n, pair] = jnp.concatenate([jnp.where(low, k_pair, jnp.zeros_like(k_pair)),
                                                jnp.where(low, jnp.zeros_like(k_pair), k_pair)], axis=0)
            vat_scr[n, pair] = jnp.concatenate([jnp.concatenate([v_t[h0, :], ones_row], axis=0),
                                                jnp.concatenate([v_t[h1, :], ones_row], axis=0)], axis=1)
        ht_scr[n] = jnp.zeros((GROUP_WIDTH, CHUNK), F32)

        g_i = gate_ref[0, rows, :] + bg_ref[...]
        lf = pltpu.roll(jax.nn.log_sigmoid(g_i), LANES - n_chains, 1)
        pre = _dot_f32(tri_ref[...], lf)
        suf = pre[CHUNK - 1:CHUNK, :] - pre + lf
        lane = lax.broadcasted_iota(jnp.int32, (CHUNK, LANES), 1)
        b = jnp.where(lane % 4 < 2, pre, suf)
        a = g_i - b
        acol_scr[n] = a
        a_t = a.T[0:n_chains, :]
        b_t = b.T[0:n_chains, :]
        chain = lax.broadcasted_iota(jnp.int32, (n_chains, CHUNK), 0)
        b_end = jnp.where(chain % 4 < 2, b_t[:, CHUNK - 1:CHUNK], b_t[:, 0:1])
        b_end = jnp.broadcast_to(b_end, (n_chains, CHUNK))
        a_max = jnp.broadcast_to(jnp.max(a_t, axis=1, keepdims=True), (n_chains, CHUNK))
        for kind, rows8 in enumerate((a_t, b_t, b_end, a_max)):
            rows_scr[n, kind] = jnp.concatenate([rows8[0:4, :], rows8[4:8, :]], axis=1)

    def reset_state():
        c_scr[...] = jnp.zeros(c_scr.shape, F32)
        m_scr[...] = jnp.zeros(m_scr.shape, F32)

    s_idx = lax.broadcasted_iota(jnp.int32, (CHUNK, CHUNK), 0)
    t_idx = lax.broadcasted_iota(jnp.int32, (CHUNK, CHUNK), 1)

    zeros = jnp.zeros((CHUNK, CHUNK), BF16)
    low = lax.broadcasted_iota(jnp.int32, (1, LANES), 1) < HEAD_DIM

    def scan(i, carry):
        chunk_of = (jnp.where(i < CTX_CHUNKS, LAT_CHUNKS + i, i - CTX_CHUNKS), N_CHUNKS - 1 - i)
        steps = [(rev, pair, chunk_of[rev]) for rev in range(2) for pair in range(GROUP_HEADS // 2)]

        scores = []
        for rev, pair, n in steps:
            dp = 2 * rev + pair
            causal = (s_idx >= t_idx) if rev else (s_idx <= t_idx)
            arg = jnp.concatenate([jnp.where(causal, acol_scr[n, :, dp:dp + 1], NEG_INF),
                                   jnp.where(causal, acol_scr[n, :, 4 + dp:5 + dp], NEG_INF)], axis=1)
            s = _dot(k_scr[n, :, pair * LANES:(pair + 1) * LANES], qtz_scr[n, pair])
            scores.append((arg, jnp.max(arg, axis=0, keepdims=True), s))

        inter = []
        for rev, pair, n in steps:
            dp = 2 * rev + pair
            a_row = rows_scr[n, 0, dp:dp + 1, :]
            b_end = rows_scr[n, 2, dp:dp + 1, :]
            a_max = rows_scr[n, 3, dp:dp + 1, :]
            m_mem = m_scr[dp:dp + 1, :]
            c_mem = c_scr[dp]
            inter.append((_dot(c_mem.astype(BF16), qtz_scr[n, pair]), m_mem))
            m_new = b_end + jnp.maximum(m_mem, a_max)
            w_src = jnp.exp(b_end + a_row - m_new)
            decay = jnp.exp(b_end + m_mem - m_new)
            decay = jnp.where(low, decay[:, :CHUNK], decay[:, CHUNK:])
            c_scr[dp] = decay * c_mem + _dot((vat_scr[n, pair].astype(F32) * w_src).astype(BF16),
                                             kbd_scr[n, pair])
            m_scr[dp:dp + 1, :] = m_new

        for (rev, pair, n), (arg, cm, s), (x2, m_mem) in zip(steps, scores, inter):
            dp = 2 * rev + pair
            g = (s * jnp.exp(arg - cm)).astype(BF16)
            g_bd = jnp.concatenate([jnp.concatenate([g[:, :CHUNK], zeros], axis=1),
                                    jnp.concatenate([zeros, g[:, CHUNK:]], axis=1)], axis=0)
            x1 = _dot(vat_scr[n, pair], g_bd)
            b_row = rows_scr[n, 1, dp:dp + 1, :]
            mu = jnp.maximum(cm, m_mem)
            both = x1 * jnp.exp(cm - mu) + x2 * jnp.exp(m_mem - mu)
            den = both[HEAD_DIM:HEAD_DIM + 1, :]
            inv = 1.0 / jnp.maximum(jnp.abs(den), jnp.exp(-(b_row + mu)))
            h_t = both[0:HEAD_DIM, :] * inv
            ht_scr[n, pair * LANES:(pair + 1) * LANES, :] += jnp.concatenate([h_t[:, :CHUNK], h_t[:, CHUNK:]],
                                                                             axis=0)
        return carry

    def finish(n, carry):
        rows = pl.ds(pl.multiple_of(n * CHUNK, CHUNK), CHUNK)
        gate = _sigmoid(og_ref[0, rows, :].astype(F32))
        ys = []
        for h in range(GROUP_HEADS):
            x = ht_scr[n, h * HEAD_DIM:(h + 1) * HEAD_DIM, :]
            mu = jnp.mean(x, axis=0, keepdims=True)
            var = jnp.mean(jnp.square(x - mu), axis=0, keepdims=True)
            ys.append((x - mu) * lax.rsqrt(var + EPS))
        y = jnp.concatenate(ys, axis=0).T * gln_ref[...]
        o_ref[0, rows, :] = (gate * y).astype(BF16)
        return carry

    return prep, reset_state, scan, finish


def _attn_mlstm_body(aq_ref, ak_ref, av_ref, by_drow_ref, mq_ref, mk_ref, mv_ref, og_ref, gate_ref, wc_ref,
                     bg_ref, cos_ref, sina_ref, sinb_ref, gln_ref, tri_ref, a_ref, m_ref, bias_scr, *scratch,
                     ctx_out):
    @pl.when(pl.program_id(0) == 0)
    def _():
        _expand_bias(by_drow_ref, bias_scr)

    context_rows, trip, n_trips = _attn_parts(aq_ref, ak_ref, av_ref, bias_scr, a_ref, ctx_out)
    prep, reset_state, scan, finish = _mlstm_parts(mq_ref, mk_ref, mv_ref, og_ref, gate_ref, wc_ref, bg_ref,
                                                   cos_ref, sina_ref, sinb_ref, gln_ref, tri_ref, m_ref, *scratch)
    per_trip = LAT_CHUNKS // n_trips

    def prep_context(n, carry):
        prep(n, False)
        return carry

    def trip_and_prep(g, carry):
        for j in range(per_trip):
            prep(g * per_trip + j, True)
        trip(g, carry)
        return carry

    lax.fori_loop(LAT_CHUNKS, N_CHUNKS, prep_context, 0)
    context_rows()
    lax.fori_loop(0, n_trips, trip_and_prep, 0)
    reset_state()
    lax.fori_loop(0, N_CHUNKS, scan, 0, unroll=3)
    lax.fori_loop(0, N_CHUNKS, finish, 0, unroll=3)


def _rope_tables():
    t = jnp.arange(SEQ)
    pos = jnp.stack([(t // GRID_W).astype(F32), (t % GRID_W).astype(F32)], axis=1)
    lane = jnp.arange(GROUP_WIDTH)
    m = HEAD_DIM // 4
    inv = ROPE_THETA ** (-(lane % m).astype(F32) / m)
    axis = (lane % HEAD_DIM) // (HEAD_DIM // 2)
    ang = jnp.where(axis[None, :] == 0, pos[:, 0:1], pos[:, 1:2]) * inv[None, :]
    low = ((lane % (2 * m)) < m)[None, :]
    cos, sin = jnp.cos(ang), jnp.sin(ang)
    return cos, jnp.where(low, -sin, 0.0), jnp.where(low, 0.0, sin)


def _attn_mlstm(p, by_drow, gates, w_conv, b_gate, g_mlstm, rope, ctx_out):
    bsz = p.shape[0]
    i = jnp.arange(CHUNK)
    tri = (i[:, None] >= i[None, :]).astype(F32)
    bg = jnp.pad(b_gate[jnp.array(GATE_I_COLS + GATE_F_COLS)], (0, LANES - N_GATES)).reshape(1, LANES)
    cos, sina, sinb = rope
    n_pairs = GROUP_HEADS // 2
    group_out = pl.BlockSpec((1, T_ALL, GROUP_WIDTH), lambda b: (b, 0, 0))
    return pl.pallas_call(
        functools.partial(_attn_mlstm_body, ctx_out=ctx_out),
        grid=(bsz,),
        in_specs=[_group_spec(0), _group_spec(1), _group_spec(2), _resident(by_drow.shape),
                  _group_spec(5), _group_spec(6), _group_spec(7), _group_spec(8),
                  pl.BlockSpec((1, T_ALL, LANES), lambda b: (b, 0, 0)),
                  _resident((CONV_W, 2 * GROUP_WIDTH)),
                  _resident((1, LANES)),
                  _resident((SEQ, GROUP_WIDTH)), _resident((SEQ, GROUP_WIDTH)), _resident((SEQ, GROUP_WIDTH)),
                  _resident((1, GROUP_WIDTH)),
                  _resident((CHUNK, CHUNK))],
        out_specs=[group_out, group_out],
        out_shape=[jax.ShapeDtypeStruct((bsz, T_ALL, GROUP_WIDTH), BF16)] * 2,
        scratch_shapes=[pltpu.VMEM((n_pairs, WIN_ROWS, 2 * GRID_W, WIN_ROWS * GRID_W), F32),
                        pltpu.VMEM((N_CHUNKS, n_pairs, LANES, 2 * CHUNK), BF16),
                        pltpu.VMEM((N_CHUNKS, CHUNK, GROUP_WIDTH), BF16),
                        pltpu.VMEM((N_CHUNKS, n_pairs, 2 * CHUNK, LANES), BF16),
                        pltpu.VMEM((N_CHUNKS, n_pairs, LANES, 2 * CHUNK), BF16),
                        pltpu.VMEM((N_CHUNKS, CHUNK, LANES), F32),
                        pltpu.VMEM((N_CHUNKS, 4, 2 * n_pairs, 2 * CHUNK), F32),
                        pltpu.VMEM((N_CHUNKS, GROUP_WIDTH, CHUNK), F32),
                        pltpu.VMEM((2 * n_pairs, LANES, LANES), F32),
                        pltpu.VMEM((2 * n_pairs, 2 * CHUNK), F32)],
        compiler_params=_params(1),
        name="attn_mlstm",
    )(p, p, p, by_drow, p, p, p, p, gates, w_conv, bg, cos, sina, sinb, g_mlstm.reshape(1, GROUP_WIDTH), tri)


def _dft_tables(n, scale):
    f = 1 << (n.bit_length() // 2)
    s = jnp.arange(n, dtype=jnp.int32)[None, :]
    ang_a = ((f * jnp.arange(n // f, dtype=jnp.int32)[:, None] * s) % n).astype(F32) * (2.0 * jnp.pi / n)
    ang_b = ((jnp.arange(f, dtype=jnp.int32)[:, None] * s) % n).astype(F32) * (2.0 * jnp.pi / n)
    ca, sa = jnp.cos(ang_a)[:, None, :], jnp.sin(ang_a)[:, None, :]
    cb, sb = jnp.cos(ang_b)[None, :, :], jnp.sin(ang_b)[None, :, :]
    cos = (ca * cb - sa * sb).reshape(n, n)
    sin = (sa * cb + ca * sb).reshape(n, n)
    return (cos * scale).astype(BF16), (-sin * scale).astype(BF16)


def _channel_dft():
    gc = GROUP_WIDTH // FNET_GROUPS
    j = jnp.arange(GROUP_WIDTH, dtype=jnp.int32)
    same = (j[:, None] // gc) == (j[None, :] // gc)
    ang = (((j[:, None] % gc) * (j[None, :] % gc)) % gc).astype(F32) * (2.0 * jnp.pi / gc)
    c = jnp.where(same, jnp.cos(ang), 0.0)
    s = jnp.where(same, jnp.sin(ang), 0.0)
    return jnp.concatenate([c, s], axis=1).astype(BF16)


def _fnet_gmlp_body(zc_ref, zs_ref, cl_ref, sl_ref, cc_ref, sc_ref, w_ref, u_ref, z_ref, ws_ref, bs_ref, g_ref,
                    o_ref, gm_ref):
    gmlp = (u_ref, z_ref, ws_ref, bs_ref, g_ref, gm_ref)
    half = N_CHUNKS // 2
    y = _dot(cl_ref[...], zc_ref[0:SEQ, :])
    for n in range(half):
        _gmlp_chunk(n, *gmlp)
    y = y + _dot(sl_ref[...], zs_ref[0:SEQ, :])
    for n in range(half, N_CHUNKS):
        _gmlp_chunk(n, *gmlp)
    o_ref[0:SEQ, :] = _dot(y.astype(BF16), w_ref[...]).astype(BF16)
    y = _dot(cc_ref[...], zc_ref[SEQ:, :]) + _dot(sc_ref[...], zs_ref[SEQ:, :])
    o_ref[SEQ:, :] = _dot(y.astype(BF16), w_ref[...]).astype(BF16)


def _fnet_gmlp(zc, zs, tabs, w_fnet, p, w_spatial, b_spatial, g_gmlp):
    bsz = p.shape[0]
    cl, sl, cc, sc = tabs
    col = pl.BlockSpec((T_ALL, GROUP_WIDTH), lambda b: (0, b))
    bias = jnp.repeat(b_spatial.T, HEAD_DIM, axis=1)
    return pl.pallas_call(
        _fnet_gmlp_body,
        grid=(bsz,),
        in_specs=[col, col, _resident((SEQ, SEQ)), _resident((SEQ, SEQ)),
                  _resident((CTX_LEN, CTX_LEN)), _resident((CTX_LEN, CTX_LEN)),
                  _resident((GROUP_WIDTH, GROUP_WIDTH)),
                  _group_spec(3), _group_spec(4),
                  _resident((GROUP_HEADS, CHUNK, CHUNK)),
                  _resident((CHUNK, GROUP_WIDTH)),
                  _resident((1, GROUP_WIDTH))],
        out_specs=[col, pl.BlockSpec((1, T_ALL, GROUP_WIDTH), lambda b: (b, 0, 0))],
        out_shape=[jax.ShapeDtypeStruct((T_ALL, bsz * GROUP_WIDTH), BF16),
                   jax.ShapeDtypeStruct((bsz, T_ALL, GROUP_WIDTH), BF16)],
        compiler_params=_params(1),
        name="fnet_gmlp",
    )(zc, zs, cl, sl, cc, sc, w_fnet.astype(BF16), p, p, w_spatial.astype(BF16), bias,
      g_gmlp.reshape(1, GROUP_WIDTH))


def _post_body(a_ref, b_ref, c_ref, d_ref, x_ref, cx_ref, ml_ref, mc_ref, g_ref, wo_ref, w1_ref, w2_ref, gf_ref,
               o_ref, h_scr, acc_scr, *, final):
    gw = GROUP_WIDTH
    tm = x_ref.shape[1]
    mod = _mod_rows(ml_ref, mc_ref, tm)
    y = (_dot(a_ref[0], wo_ref[0:gw, :]) + _dot(b_ref[0], wo_ref[gw:2 * gw, :])
         + _dot(c_ref[0], wo_ref[2 * gw:3 * gw, :]) + _dot(d_ref[...], wo_ref[3 * gw:, :]))
    if final:
        x = x_ref[0]
    else:
        ctx_tail = pl.program_id(1) == pl.num_programs(1) - 1
        x = jnp.concatenate([x_ref[0, 0:tm - CTX_LEN, :], _tail_rows(x_ref, cx_ref, ctx_tail)], axis=0)
    x1 = x + mod(2) * y
    h = _rms(x1, g_ref[...]) * (1.0 + mod(4)) + mod(3)
    h_scr[...] = h.astype(BF16)
    step = 512

    def up(j):
        return jnp.maximum(_dot(h_scr[...], w1_ref[:, j:j + step]), 0.0)

    a = up(0)
    for j in range(0, D_FF, step):
        nxt = up(j + step) if j + step < D_FF else None
        part = _dot((a * a).astype(BF16), w2_ref[j:j + step, :])
        if j == 0:
            acc_scr[...] = part
        else:
            acc_scr[...] += part
        a = nxt
    x2 = x1 + mod(5) * acc_scr[...]
    if final:
        x2 = _rms(x2, gf_ref[...])
    o_ref[0] = x2


def _post(a, b_, c_, d, tokens, ctx, mod_l, g_ffn, w_out, w_ff1, w_ff2, g_final, final):
    bsz = tokens.shape[0]
    tm, out_len = (LATENT_TILE, SEQ) if final else (TOKEN_TILE, T_ALL)
    tok = pl.BlockSpec((1, tm, D_MODEL), lambda b, t: (b, t, 0))
    grp = pl.BlockSpec((1, tm, GROUP_WIDTH), lambda b, t: (b, t, 0))
    operands, token_specs = _token_operands(tokens, ctx, tm)
    return pl.pallas_call(
        functools.partial(_post_body, final=final),
        grid=(bsz, out_len // tm),
        in_specs=[grp, grp, grp, pl.BlockSpec((tm, GROUP_WIDTH), lambda b, t: (t, b))] + token_specs + _MOD_SPECS
                 + [_resident((1, D_MODEL)),
                    _resident((D_MODEL, D_MODEL)),
                    _resident((D_MODEL, D_FF)),
                    _resident((D_FF, D_MODEL)),
                    _resident((1, D_MODEL))],
        out_specs=tok,
        out_shape=jax.ShapeDtypeStruct((bsz, out_len, D_MODEL), F32),
        scratch_shapes=[pltpu.VMEM((tm, D_MODEL), BF16), pltpu.VMEM((tm, D_MODEL), F32)],
        compiler_params=_params(2),
        name="out_proj_mlp",
    )(a, b_, c_, d, *operands, mod_l, mod_l, g_ffn.reshape(1, D_MODEL), w_out.astype(BF16), w_ff1.astype(BF16),
      w_ff2.astype(BF16), g_final.reshape(1, D_MODEL))


def kernel(x, c, ctx, c_ctx, w_ada, b_ada, g_norm_mix, g_norm_ffn, w_in, b_gate, w_conv_qk, rpb, w_spatial,
           b_spatial, g_gmlp, g_mlstm, w_fnet, w_out, w_ff1, w_ff2, g_final):
    bsz = x.shape[0]
    assert bsz <= CTX_MOD_ROW and x.shape[1:] == (SEQ, D_MODEL) and ctx.shape[1:] == (CTX_LEN, D_MODEL)
    depth = w_ada.shape[0]
    tokens, ctx_rows = x, ctx
    cc = jnp.zeros((MOD_ROWS, D_MODEL), F32).at[:bsz].set(c).at[CTX_MOD_ROW].set(c_ctx)
    mod = _ada_mod(cc, w_ada, b_ada).reshape(depth, MOD_ROWS, N_MOD, D_MODEL)

    rope = _rope_tables()
    dft_c = _channel_dft()
    tabs = (_dft_tables(SEQ, (SEQ * GROUP_WIDTH // FNET_GROUPS) ** -0.5)
            + _dft_tables(CTX_LEN, (CTX_LEN * GROUP_WIDTH // FNET_GROUPS) ** -0.5))

    for l in range(depth):
        w_cat = _projection_weights(w_in[l])
        p, gates, zc, zs = _in_proj(tokens, ctx_rows, mod[l], g_norm_mix[l], w_cat, dft_c)
        a, c_ = _attn_mlstm(p, _attention_bias(rpb[l]), gates, w_conv_qk[l], b_gate[l], g_mlstm[l], rope,
                            ctx_out=(l < depth - 1))
        d, b_ = _fnet_gmlp(zc, zs, tabs, w_fnet[l], p, w_spatial[l], b_spatial[l], g_gmlp[l])
        tokens = _post(a, b_, c_, d, tokens, ctx_rows, mod[l], g_norm_ffn[l], w_out[l], w_ff1[l], w_ff2[l],
                       g_final, final=(l == depth - 1))
        ctx_rows = None
    return tokens
```

```python
import functools

import jax
import jax.numpy as jnp
from jax import lax
from jax.experimental import pallas as pl
from jax.experimental.pallas import tpu as pltpu

D_MODEL = 1024
SEQ = 2048
DEPTH = 2
GRID_W = 64
GRID_ROWS = SEQ // GRID_W
CTX_LEN = 256
T_ALL = CTX_LEN + SEQ
HEAD_DIM = 64
GROUP_WIDTH = 256
GROUP_HEADS = 4
WIN_ROWS = 8
WIN_COLS = 16
CHUNK = 128
N_CHUNKS = T_ALL // CHUNK
CTX_CHUNKS = CTX_LEN // CHUNK
LAT_CHUNKS = SEQ // CHUNK
CONV_W = 3
FNET_GROUPS = 4
ROPE_THETA = 10000.0
D_FF = 4 * D_MODEL
N_MOD = 6
EPS = 1e-6
NEG_INF = -1e30
N_GATES = 4 * GROUP_HEADS
OFF_G = 10 * GROUP_WIDTH
D_IN = OFF_G + N_GATES
MOD_ROWS = 24
CTX_MOD_ROW = 16
LANES = 128
BF16_ROWS = 16
GATE_I_COLS = (0, 2, 8, 10, 1, 3, 9, 11)
GATE_F_COLS = (4, 6, 12, 14, 5, 7, 13, 15)
TOKEN_TILE = 768
PROJ_TILE = 1152
LATENT_TILE = 512
P_WIDTH = 9 * GROUP_WIDTH
VMEM_LIMIT = 56 * 1024 * 1024

F32 = jnp.float32
BF16 = jnp.bfloat16


def _dot(a, b):
    return jnp.dot(a, b, preferred_element_type=F32)


def _dot_nt(a, b):
    return lax.dot_general(a, b, (((1,), (1,)), ((), ())), preferred_element_type=F32)


def _dot_f32(a, b):
    return jnp.dot(a, b, preferred_element_type=F32, precision=lax.Precision.HIGHEST)


def _sigmoid(x):
    return 0.5 * (1.0 + jnp.tanh(0.5 * x))


def _resident(shape):
    nd = len(shape)
    return pl.BlockSpec(shape, lambda *_: (0,) * nd, pipeline_mode=pl.Buffered(1))


def _layer_resident(shape, layer):
    nd = len(shape)
    return pl.BlockSpec((1,) + tuple(shape), lambda *_: (layer,) + (0,) * nd, pipeline_mode=pl.Buffered(1))


def _params(n_axes):
    return pltpu.CompilerParams(dimension_semantics=("arbitrary",) * n_axes,
                                vmem_limit_bytes=VMEM_LIMIT)


def _mod_body(c_ref, w_ref, b_ref, o_ref):
    s = c_ref[...]
    s = s * jax.nn.sigmoid(s)
    o_ref[0] = _dot(s.astype(BF16), w_ref[0].astype(BF16)) + b_ref[0]


def _ada_mod(cc, w_ada, b_ada):
    depth, d, n = w_ada.shape
    tn = 1536
    return pl.pallas_call(
        _mod_body,
        grid=(depth, n // tn),
        in_specs=[pl.BlockSpec((MOD_ROWS, d), lambda l, j: (0, 0)),
                  pl.BlockSpec((1, d, tn), lambda l, j: (l, 0, j)),
                  pl.BlockSpec((1, 1, tn), lambda l, j: (l, 0, j))],
        out_specs=pl.BlockSpec((1, MOD_ROWS, tn), lambda l, j: (l, 0, j)),
        out_shape=jax.ShapeDtypeStruct((depth, MOD_ROWS, n), F32),
        compiler_params=_params(2),
        name="ada_mod",
    )(cc, w_ada, b_ada.reshape(depth, 1, n))


_MOD_SPECS = [pl.BlockSpec((1, N_MOD, D_MODEL), lambda b, t: (b, 0, 0)),
              pl.BlockSpec((1, N_MOD, D_MODEL), lambda b, t: (CTX_MOD_ROW, 0, 0))]


def _mod_rows(mod_lat_ref, mod_ctx_ref, tile):
    tok = pl.program_id(1) * tile + lax.broadcasted_iota(jnp.int32, (tile, 1), 0)
    is_ctx = tok >= SEQ
    return lambda k: jnp.where(is_ctx, mod_ctx_ref[0, k:k + 1, :], mod_lat_ref[0, k:k + 1, :])


def _rms(x, g):
    return x * lax.rsqrt(jnp.mean(x * x, axis=-1, keepdims=True) + EPS) * g


def _tail_rows(x_ref, c_ref, ctx_tail):
    head = x_ref.shape[1] - CTX_LEN
    return jnp.where(ctx_tail, c_ref[0], x_ref[0, head:, :])


def _inproj_body(x_ref, c_ref, ml_ref, mc_ref, g_ref, w_ref, wg_ref, dft_ref, p_ref, gate_ref, zc_ref, zs_ref,
                 lhs_scr, inv_scr):
    tm = x_ref.shape[1]
    head = tm - CTX_LEN
    ctx_tail = pl.program_id(1) == pl.num_programs(1) - 1
    shift_tail = jnp.where(ctx_tail, mc_ref[0, 0:1, :], ml_ref[0, 0:1, :])
    scale_tail = jnp.where(ctx_tail, mc_ref[0, 1:2, :], ml_ref[0, 1:2, :])
    lead = BF16_ROWS
    lhs_scr[0:lead, :] = jnp.concatenate([ml_ref[0, 0:1, :], shift_tail,
                                          jnp.zeros((lead - 2, D_MODEL), F32)], axis=0).astype(BF16)
    x_head = x_ref[0, 0:head, :]
    x_tail = _tail_rows(x_ref, c_ref, ctx_tail)
    lhs_scr[lead:lead + head, :] = (x_head * (g_ref[...] * (1.0 + ml_ref[0, 1:2, :]))).astype(BF16)
    lhs_scr[lead + head:, :] = (x_tail * (g_ref[...] * (1.0 + scale_tail))).astype(BF16)
    for rows, x in ((slice(0, head), x_head), (slice(head, tm), x_tail)):
        inv = lax.rsqrt(jnp.mean(x * x, axis=-1, keepdims=True) + EPS)
        inv_scr[rows, :] = jnp.broadcast_to(inv, (x.shape[0], LANES))

    def project(cols=None):
        w = wg_ref[0] if cols is None else w_ref[0, :, cols]
        n = w.shape[1] // LANES
        r = _dot(lhs_scr[...], w)
        inv_head = jnp.concatenate([inv_scr[0:head, :]] * n, axis=1)
        inv_tail = jnp.concatenate([inv_scr[head:, :]] * n, axis=1)
        return jnp.concatenate([inv_head * r[lead:lead + head, :] + r[0:1, :],
                                inv_tail * r[lead + head:, :] + r[1:2, :]], axis=0)

    step = 2 * GROUP_WIDTH
    f = project(slice(P_WIDTH, OFF_G)).astype(BF16)
    p_ref[0, :, 0:step] = project(slice(0, step)).astype(BF16)
    z = _dot(f, dft_ref[...])
    zc_ref[...] = z[:, :GROUP_WIDTH].astype(BF16)
    zs_ref[...] = z[:, GROUP_WIDTH:].astype(BF16)
    for j in range(step, P_WIDTH - GROUP_WIDTH, step):
        p_ref[0, :, j:j + step] = project(slice(j, j + step)).astype(BF16)
    j = P_WIDTH - GROUP_WIDTH
    p_ref[0, :, j:] = project(slice(j, P_WIDTH)).astype(BF16)
    gate_ref[0] = project()


def _projection_weights(w_in):
    w_gate = jnp.pad(w_in[:, :, OFF_G:][:, :, jnp.array(GATE_I_COLS + GATE_F_COLS)],
                     ((0, 0), (0, 0), (0, LANES - N_GATES)))
    return w_in[:, :, :OFF_G].astype(BF16), w_gate.astype(BF16)


def _token_operands(tokens, ctx, tm):
    tile = pl.BlockSpec((1, tm, D_MODEL), lambda b, t: (b, t, 0))
    if ctx is None:
        return (tokens, tokens), [tile, pl.BlockSpec((1, CTX_LEN, D_MODEL), lambda b, t: (b, SEQ // CTX_LEN, 0))]
    return (tokens, ctx), [tile, pl.BlockSpec((1, CTX_LEN, D_MODEL), lambda b, t: (b, 0, 0))]


def _in_proj(tokens, ctx, mod_l, g, w_main, w_gate, layer, dft_c):
    bsz = tokens.shape[0]
    tm = PROJ_TILE
    operands, token_specs = _token_operands(tokens, ctx, tm)
    return pl.pallas_call(
        _inproj_body,
        grid=(bsz, T_ALL // tm),
        in_specs=token_specs + _MOD_SPECS
                 + [_resident((1, D_MODEL)),
                    _layer_resident(w_main.shape[1:], layer),
                    _layer_resident(w_gate.shape[1:], layer),
                    _resident((GROUP_WIDTH, 2 * GROUP_WIDTH))],
        out_specs=[pl.BlockSpec((1, tm, P_WIDTH), lambda b, t: (b, t, 0)),
                   pl.BlockSpec((1, tm, LANES), lambda b, t: (b, t, 0)),
                   pl.BlockSpec((tm, GROUP_WIDTH), lambda b, t: (t, b)),
                   pl.BlockSpec((tm, GROUP_WIDTH), lambda b, t: (t, b))],
        out_shape=[jax.ShapeDtypeStruct((bsz, T_ALL, P_WIDTH), BF16),
                   jax.ShapeDtypeStruct((bsz, T_ALL, LANES), F32),
                   jax.ShapeDtypeStruct((T_ALL, bsz * GROUP_WIDTH), BF16),
                   jax.ShapeDtypeStruct((T_ALL, bsz * GROUP_WIDTH), BF16)],
        scratch_shapes=[pltpu.VMEM((tm + BF16_ROWS, D_MODEL), BF16), pltpu.VMEM((tm, LANES), F32)],
        compiler_params=_params(2),
        name="in_proj",
    )(*operands, mod_l, mod_l, g.reshape(1, D_MODEL), w_main, w_gate, dft_c)


def _group_spec(col_block):
    return pl.BlockSpec((1, T_ALL, GROUP_WIDTH), lambda b: (b, 0, col_block))


def _softmax_pv(parts):
    m = functools.reduce(jnp.maximum, [jnp.max(s, axis=-1, keepdims=True) for s, _ in parts])
    es = [jnp.exp(s - m) for s, _ in parts]
    den = functools.reduce(jnp.add, [jnp.sum(e, axis=-1, keepdims=True) for e in es])
    num = functools.reduce(jnp.add, [_dot(e.astype(BF16), v) for e, (_, v) in zip(es, parts)])
    return num / den


def _attn_parts(q_ref, k_ref, v_ref, bias_ref, o_ref, ctx_out):
    scale = HEAD_DIM ** -0.5
    pairs = [slice(i * LANES, (i + 1) * LANES) for i in range(GROUP_WIDTH // LANES)]

    def stack_pair(q2):
        low = lax.broadcasted_iota(jnp.int32, q2.shape, 1) < HEAD_DIM
        zero = jnp.zeros_like(q2)
        return low, jnp.concatenate([jnp.where(low, q2, zero), jnp.where(low, zero, q2)], axis=0)

    def unstack_pair(low, o):
        n = o.shape[0] // 2
        return jnp.where(low, o[:n], o[n:]).astype(BF16)

    ctx = slice(SEQ, T_ALL)

    def context_rows():
        if ctx_out:
            ctx_tiles = []
            for ps in pairs:
                low, qm = stack_pair(q_ref[0, ctx, ps] * scale)
                ctx_tiles.append((low, _dot_nt(qm, k_ref[0, ctx, ps])))
            for ps, (low, s) in zip(pairs, ctx_tiles):
                o_ref[0, ctx, ps] = unstack_pair(low, _softmax_pv([(s, v_ref[0, ctx, ps])]))
        else:
            o_ref[0, ctx, :] = jnp.zeros((CTX_LEN, GROUP_WIDTH), BF16)

    rows_per_trip = 4
    tile_rows = 2 * GRID_W

    def trip(g, carry):
        tiles = [(dr, i) for dr in range(rows_per_trip) for i in range(len(pairs))]
        q0s = [pl.multiple_of((g * rows_per_trip + dr) * GRID_W, GRID_W) for dr in range(rows_per_trip)]
        stacked = {(dr, i): stack_pair(q_ref[0, pl.ds(q0s[dr], GRID_W), pairs[i]] * scale) for dr, i in tiles}
        s_ctx = [_dot_nt(jnp.concatenate([stacked[dr, i][1] for dr in range(rows_per_trip)], axis=0),
                         k_ref[0, ctx, pairs[i]]) for i in range(len(pairs))]

        def window_scores(dr, i):
            r = g * rows_per_trip + dr
            rs = jnp.clip(r - WIN_ROWS // 2, 0, GRID_ROWS - WIN_ROWS)
            k0 = pl.multiple_of(rs * GRID_W, GRID_W)
            s_loc = _dot_nt(stacked[dr, i][1], k_ref[0, pl.ds(k0, WIN_ROWS * GRID_W), pairs[i]])
            return s_loc + bias_ref[i, r - rs], k0

        partial = {}
        cur = window_scores(*tiles[0])
        for j, (dr, i) in enumerate(tiles):
            nxt = window_scores(*tiles[j + 1]) if j + 1 < len(tiles) else None
            s_loc, k0 = cur
            s_c = s_ctx[i][dr * tile_rows:(dr + 1) * tile_rows, :]
            m = jnp.maximum(jnp.max(s_loc, axis=-1, keepdims=True), jnp.max(s_c, axis=-1, keepdims=True))
            e_loc, e_ctx = jnp.exp(s_loc - m), jnp.exp(s_c - m)
            den = jnp.sum(e_loc, axis=-1, keepdims=True) + jnp.sum(e_ctx, axis=-1, keepdims=True)
            num = _dot(e_loc.astype(BF16), v_ref[0, pl.ds(k0, WIN_ROWS * GRID_W), pairs[i]])
            partial[dr, i] = (num, e_ctx.astype(BF16), den)
            cur = nxt

        for i, ps in enumerate(pairs):
            num_ctx = _dot(jnp.concatenate([partial[dr, i][1] for dr in range(rows_per_trip)], axis=0),
                           v_ref[0, ctx, ps])
            for dr in range(rows_per_trip):
                num, _, den = partial[dr, i]
                o = (num + num_ctx[dr * tile_rows:(dr + 1) * tile_rows, :]) / den
                o_ref[0, pl.ds(q0s[dr], GRID_W), ps] = unstack_pair(stacked[dr, i][0], o)
        return carry

    return context_rows, trip, GRID_ROWS // rows_per_trip


def _attention_bias(rpb):
    c = jnp.arange(GRID_W)
    qstart = jnp.clip(c - WIN_COLS // 2, 0, GRID_W - WIN_COLS)
    in_win = (c[None, :] >= qstart[:, None]) & (c[None, :] < qstart[:, None] + WIN_COLS)
    col = jnp.clip(c[None, :] - c[:, None] + WIN_COLS - 1, 0, 2 * WIN_COLS - 2)
    pick_col = (col[:, :, None] == jnp.arange(2 * WIN_COLS - 1)).astype(F32)
    by_drow = jnp.einsum('hde,qke->hdqk', rpb.astype(F32), pick_col, precision=lax.Precision.HIGHEST)
    return jnp.where(in_win, by_drow, NEG_INF)


def _expand_bias(by_drow_ref, bias_scr):
    for pair in range(GROUP_HEADS // 2):
        for v in range(WIN_ROWS):
            for hh in range(2):
                for j in range(0, WIN_ROWS, 2):
                    two_rows = [by_drow_ref[2 * pair + hh, jj - v + WIN_ROWS - 1] for jj in (j, j + 1)]
                    bias_scr[pair, v, hh * GRID_W:(hh + 1) * GRID_W, j * GRID_W:(j + 2) * GRID_W] = (
                        jnp.concatenate(two_rows, axis=1))


def _gmlp_chunk(n, u_ref, z_ref, ws_ref, bs_ref, g_ref, o_ref):
    rows = slice(n * CHUNK, (n + 1) * CHUNK)
    z = _rms(jax.nn.gelu(z_ref[0, rows, :].astype(F32)), g_ref[...]).astype(BF16)
    u = jax.nn.gelu(u_ref[0, rows, :].astype(F32))
    for h in range(GROUP_HEADS):
        hs = slice(h * HEAD_DIM, (h + 1) * HEAD_DIM)
        mixed = _dot(ws_ref[h], z[:, hs]) + bs_ref[:, hs]
        o_ref[0, rows, hs] = (u[:, hs] * mixed).astype(BF16)


def _mlstm_parts(q_ref, k_ref, v_ref, og_ref, gate_ref, wc_ref, bg_ref, cos_ref, sina_ref,
                 sinb_ref, gln_ref, tri_ref, o_ref,
                 qtz_scr, k_scr, kbd_scr, vat_scr, acol_scr, rows_scr, ht_scr, c_scr, m_scr):
    half = HEAD_DIM // 4
    n_chains = 2 * GROUP_HEADS

    def conv_act(ref, n, w0, w1, w2, rope, post):
        rows = pl.ds(pl.multiple_of(n * CHUNK, CHUNK), CHUNK)
        cur = ref[0, rows, :].astype(F32)
        first = (n == 0) | (n == LAT_CHUNKS)
        last = (n == LAT_CHUNKS - 1) | (n == N_CHUNKS - 1)
        before = pl.ds(pl.multiple_of(jnp.maximum(n * CHUNK - BF16_ROWS, 0), BF16_ROWS), BF16_ROWS)
        after = pl.ds(pl.multiple_of(jnp.minimum((n + 1) * CHUNK, T_ALL - BF16_ROWS), BF16_ROWS), BF16_ROWS)
        tail = jnp.where(first, 0.0, 1.0) * ref[0, before, :][BF16_ROWS - 1:BF16_ROWS, :].astype(F32)
        head = jnp.where(last, 0.0, 1.0) * ref[0, after, :][0:1, :].astype(F32)
        sub = lax.broadcasted_iota(jnp.int32, (CHUNK, 1), 0)
        x_prev = jnp.where(sub == 0, tail, pltpu.roll(cur, 1, 0))
        x_next = jnp.where(sub == CHUNK - 1, head, pltpu.roll(cur, CHUNK - 1, 0))
        y = w0 * x_prev + w1 * cur + w2 * x_next
        y = y * _sigmoid(y)
        if rope:
            y = (y * cos_ref[rows, :] + pltpu.roll(y, GROUP_WIDTH - half, 1) * sina_ref[rows, :]
                 + pltpu.roll(y, half, 1) * sinb_ref[rows, :])
        return y * post

    def prep(n, rope):
        rows = pl.ds(pl.multiple_of(n * CHUNK, CHUNK), CHUNK)
        gw = GROUP_WIDTH
        q = conv_act(q_ref, n, wc_ref[0:1, :gw], wc_ref[1:2, :gw], wc_ref[2:3, :gw], rope, 1.0)
        k = conv_act(k_ref, n, wc_ref[0:1, gw:], wc_ref[1:2, gw:], wc_ref[2:3, gw:], rope, HEAD_DIM ** -0.5)
        q_t = q.T.astype(BF16)
        v_t = v_ref[0, rows, :].astype(F32).T.astype(BF16)
        k = k.astype(BF16)
        k_scr[n] = k
        sub = lax.broadcasted_iota(jnp.int32, (HEAD_DIM, CHUNK), 0)
        ones_row = jnp.where(sub == 0, 1.0, 0.0).astype(BF16)
        zeros = jnp.zeros((HEAD_DIM, CHUNK), BF16)
        low = lax.broadcasted_iota(jnp.int32, (CHUNK, LANES), 1) < HEAD_DIM
        for pair in range(GROUP_HEADS // 2):
            h0 = slice(2 * pair * HEAD_DIM, (2 * pair + 1) * HEAD_DIM)
            h1 = slice((2 * pair + 1) * HEAD_DIM, (2 * pair + 2) * HEAD_DIM)
            qtz_scr[n, pair] = jnp.concatenate([jnp.concatenate([q_t[h0, :], zeros], axis=1),
                                                jnp.concatenate([zeros, q_t[h1, :]], axis=1)], axis=0)
            k_pair = k[:, pair * LANES:(pair + 1) * LANES]
            kbd_scr[n, pair] = jnp.concatenate([jnp.where(low, k_pair, jnp.zeros_like(k_pair)),
                                                jnp.where(low, jnp.zeros_like(k_pair), k_pair)], axis=0)
            vat_scr[n, pair] = jnp.concatenate([jnp.concatenate([v_t[h0, :], ones_row], axis=0),
                                                jnp.concatenate([v_t[h1, :], ones_row], axis=0)], axis=1)
        ht_scr[n] = jnp.zeros((GROUP_WIDTH, CHUNK), F32)

        g_i = gate_ref[0, rows, :] + bg_ref[...]
        lf = pltpu.roll(jax.nn.log_sigmoid(g_i), LANES - n_chains, 1)
        pre = _dot_f32(tri_ref[...], lf)
        suf = pre[CHUNK - 1:CHUNK, :] - pre + lf
        lane = lax.broadcasted_iota(jnp.int32, (CHUNK, LANES), 1)
        b = jnp.where(lane % 4 < 2, pre, suf)
        a = g_i - b
        acol_scr[n] = a
        a_t = a.T[0:n_chains, :]
        b_t = b.T[0:n_chains, :]
        chain = lax.broadcasted_iota(jnp.int32, (n_chains, CHUNK), 0)
        b_end = jnp.where(chain % 4 < 2, b_t[:, CHUNK - 1:CHUNK], b_t[:, 0:1])
        b_end = jnp.broadcast_to(b_end, (n_chains, CHUNK))
        a_max = jnp.broadcast_to(jnp.max(a_t, axis=1, keepdims=True), (n_chains, CHUNK))
        for kind, rows8 in enumerate((a_t, b_t, b_end, a_max)):
            rows_scr[n, kind] = jnp.concatenate([rows8[0:4, :], rows8[4:8, :]], axis=1)

    def reset_state():
        c_scr[...] = jnp.zeros(c_scr.shape, F32)
        m_scr[...] = jnp.zeros(m_scr.shape, F32)

    s_idx = lax.broadcasted_iota(jnp.int32, (CHUNK, CHUNK), 0)
    t_idx = lax.broadcasted_iota(jnp.int32, (CHUNK, CHUNK), 1)

    zeros = jnp.zeros((CHUNK, CHUNK), BF16)
    low = lax.broadcasted_iota(jnp.int32, (1, LANES), 1) < HEAD_DIM

    def scan(i, carry):
        chunk_of = (jnp.where(i < CTX_CHUNKS, LAT_CHUNKS + i, i - CTX_CHUNKS), N_CHUNKS - 1 - i)
        steps = [(rev, pair, chunk_of[rev]) for rev in range(2) for pair in range(GROUP_HEADS // 2)]

        scores = []
        for rev, pair, n in steps:
            dp = 2 * rev + pair
            causal = (s_idx >= t_idx) if rev else (s_idx <= t_idx)
            arg = jnp.concatenate([jnp.where(causal, acol_scr[n, :, dp:dp + 1], NEG_INF),
                                   jnp.where(causal, acol_scr[n, :, 4 + dp:5 + dp], NEG_INF)], axis=1)
            s = _dot(k_scr[n, :, pair * LANES:(pair + 1) * LANES], qtz_scr[n, pair])
            scores.append((arg, jnp.max(arg, axis=0, keepdims=True), s))

        inter = []
        for rev, pair, n in steps:
            dp = 2 * rev + pair
            a_row = rows_scr[n, 0, dp:dp + 1, :]
            b_end = rows_scr[n, 2, dp:dp + 1, :]
            a_max = rows_scr[n, 3, dp:dp + 1, :]
            m_mem = m_scr[dp:dp + 1, :]
            c_mem = c_scr[dp]
            inter.append((_dot(c_mem.astype(BF16), qtz_scr[n, pair]), m_mem))
            m_new = b_end + jnp.maximum(m_mem, a_max)
            w_src = jnp.exp(b_end + a_row - m_new)
            decay = jnp.exp(b_end + m_mem - m_new)
            decay = jnp.where(low, decay[:, :CHUNK], decay[:, CHUNK:])
            c_scr[dp] = decay * c_mem + _dot((vat_scr[n, pair].astype(F32) * w_src).astype(BF16),
                                             kbd_scr[n, pair])
            m_scr[dp:dp + 1, :] = m_new

        for (rev, pair, n), (arg, cm, s), (x2, m_mem) in zip(steps, scores, inter):
            dp = 2 * rev + pair
            g = (s * jnp.exp(arg - cm)).astype(BF16)
            g_bd = jnp.concatenate([jnp.concatenate([g[:, :CHUNK], zeros], axis=1),
                                    jnp.concatenate([zeros, g[:, CHUNK:]], axis=1)], axis=0)
            x1 = _dot(vat_scr[n, pair], g_bd)
            b_row = rows_scr[n, 1, dp:dp + 1, :]
            mu = jnp.maximum(cm, m_mem)
            both = x1 * jnp.exp(cm - mu) + x2 * jnp.exp(m_mem - mu)
            den = both[HEAD_DIM:HEAD_DIM + 1, :]
            inv = 1.0 / jnp.maximum(jnp.abs(den), jnp.exp(-(b_row + mu)))
            h_t = both[0:HEAD_DIM, :] * inv
            ht_scr[n, pair * LANES:(pair + 1) * LANES, :] += jnp.concatenate([h_t[:, :CHUNK], h_t[:, CHUNK:]],
                                                                             axis=0)
        return carry

    def finish(n, carry):
        rows = pl.ds(pl.multiple_of(n * CHUNK, CHUNK), CHUNK)
        gate = _sigmoid(og_ref[0, rows, :].astype(F32))
        ys = []
        for h in range(GROUP_HEADS):
            x = ht_scr[n, h * HEAD_DIM:(h + 1) * HEAD_DIM, :]
            mu = jnp.mean(x, axis=0, keepdims=True)
            var = jnp.mean(jnp.square(x - mu), axis=0, keepdims=True)
            ys.append((x - mu) * lax.rsqrt(var + EPS))
        y = jnp.concatenate(ys, axis=0).T * gln_ref[...]
        o_ref[0, rows, :] = (gate * y).astype(BF16)
        return carry

    return prep, reset_state, scan, finish


def _attn_mlstm_body(aq_ref, ak_ref, av_ref, by_drow_ref, mq_ref, mk_ref, mv_ref, og_ref, gate_ref, wc_ref,
                     bg_ref, cos_ref, sina_ref, sinb_ref, gln_ref, tri_ref, a_ref, m_ref, bias_scr, *scratch,
                     ctx_out):
    @pl.when(pl.program_id(0) == 0)
    def _():
        _expand_bias(by_drow_ref, bias_scr)

    context_rows, trip, n_trips = _attn_parts(aq_ref, ak_ref, av_ref, bias_scr, a_ref, ctx_out)
    prep, reset_state, scan, finish = _mlstm_parts(mq_ref, mk_ref, mv_ref, og_ref, gate_ref, wc_ref, bg_ref,
                                                   cos_ref, sina_ref, sinb_ref, gln_ref, tri_ref, m_ref, *scratch)
    per_trip = LAT_CHUNKS // n_trips

    def prep_context(n, carry):
        prep(n, False)
        return carry

    def trip_and_prep(g, carry):
        for j in range(per_trip):
            prep(g * per_trip + j, True)
        trip(g, carry)
        return carry

    lax.fori_loop(LAT_CHUNKS, N_CHUNKS, prep_context, 0)
    context_rows()
    lax.fori_loop(0, n_trips, trip_and_prep, 0)
    reset_state()
    lax.fori_loop(0, N_CHUNKS, scan, 0, unroll=3)
    lax.fori_loop(0, N_CHUNKS, finish, 0, unroll=3)


def _rope_tables():
    lane = jnp.arange(GROUP_WIDTH)
    m = HEAD_DIM // 4
    inv = ROPE_THETA ** (-(lane % m).astype(F32) / m)
    by_row = ((lane % HEAD_DIM) // (HEAD_DIM // 2) == 0)[None, None, :]
    low = ((lane % (2 * m)) < m)[None, None, :]
    ang = jnp.arange(GRID_W, dtype=F32)[:, None] * inv[None, :]
    cos, sin = jnp.cos(ang), jnp.sin(ang)

    def spread(tab):
        return jnp.where(by_row, tab[:GRID_ROWS, None, :], tab[None, :, :]).reshape(SEQ, GROUP_WIDTH)

    cos, sin = spread(cos), spread(sin)
    low = jnp.broadcast_to(low, (GRID_ROWS, GRID_W, GROUP_WIDTH)).reshape(SEQ, GROUP_WIDTH)
    return cos, jnp.where(low, -sin, 0.0), jnp.where(low, 0.0, sin)


def _attn_mlstm(p, by_drow, gates, w_conv, b_gate, g_mlstm, rope, ctx_out):
    bsz = p.shape[0]
    i = jnp.arange(CHUNK)
    tri = (i[:, None] >= i[None, :]).astype(F32)
    bg = jnp.pad(b_gate[jnp.array(GATE_I_COLS + GATE_F_COLS)], (0, LANES - N_GATES)).reshape(1, LANES)
    cos, sina, sinb = rope
    n_pairs = GROUP_HEADS // 2
    group_out = pl.BlockSpec((1, T_ALL, GROUP_WIDTH), lambda b: (b, 0, 0))
    return pl.pallas_call(
        functools.partial(_attn_mlstm_body, ctx_out=ctx_out),
        grid=(bsz,),
        in_specs=[_group_spec(0), _group_spec(1), _group_spec(2), _resident(by_drow.shape),
                  _group_spec(5), _group_spec(6), _group_spec(7), _group_spec(8),
                  pl.BlockSpec((1, T_ALL, LANES), lambda b: (b, 0, 0)),
                  _resident((CONV_W, 2 * GROUP_WIDTH)),
                  _resident((1, LANES)),
                  _resident((SEQ, GROUP_WIDTH)), _resident((SEQ, GROUP_WIDTH)), _resident((SEQ, GROUP_WIDTH)),
                  _resident((1, GROUP_WIDTH)),
                  _resident((CHUNK, CHUNK))],
        out_specs=[group_out, group_out],
        out_shape=[jax.ShapeDtypeStruct((bsz, T_ALL, GROUP_WIDTH), BF16)] * 2,
        scratch_shapes=[pltpu.VMEM((n_pairs, WIN_ROWS, 2 * GRID_W, WIN_ROWS * GRID_W), F32),
                        pltpu.VMEM((N_CHUNKS, n_pairs, LANES, 2 * CHUNK), BF16),
                        pltpu.VMEM((N_CHUNKS, CHUNK, GROUP_WIDTH), BF16),
                        pltpu.VMEM((N_CHUNKS, n_pairs, 2 * CHUNK, LANES), BF16),
                        pltpu.VMEM((N_CHUNKS, n_pairs, LANES, 2 * CHUNK), BF16),
                        pltpu.VMEM((N_CHUNKS, CHUNK, LANES), F32),
                        pltpu.VMEM((N_CHUNKS, 4, 2 * n_pairs, 2 * CHUNK), F32),
                        pltpu.VMEM((N_CHUNKS, GROUP_WIDTH, CHUNK), F32),
                        pltpu.VMEM((2 * n_pairs, LANES, LANES), F32),
                        pltpu.VMEM((2 * n_pairs, 2 * CHUNK), F32)],
        compiler_params=_params(1),
        name="attn_mlstm",
    )(p, p, p, by_drow, p, p, p, p, gates, w_conv, bg, cos, sina, sinb, g_mlstm.reshape(1, GROUP_WIDTH), tri)


def _dft_tables(n, scale):
    f = 1 << (n.bit_length() // 2)
    s = jnp.arange(n, dtype=jnp.int32)[None, :]
    ang_a = ((f * jnp.arange(n // f, dtype=jnp.int32)[:, None] * s) % n).astype(F32) * (2.0 * jnp.pi / n)
    ang_b = ((jnp.arange(f, dtype=jnp.int32)[:, None] * s) % n).astype(F32) * (2.0 * jnp.pi / n)
    ca, sa = jnp.cos(ang_a)[:, None, :], jnp.sin(ang_a)[:, None, :]
    cb, sb = jnp.cos(ang_b)[None, :, :], jnp.sin(ang_b)[None, :, :]
    cos = (ca * cb - sa * sb).reshape(n, n)
    sin = (sa * cb + ca * sb).reshape(n, n)
    return (cos * scale).astype(BF16), (-sin * scale).astype(BF16)


def _channel_dft():
    gc = GROUP_WIDTH // FNET_GROUPS
    j = jnp.arange(GROUP_WIDTH, dtype=jnp.int32)
    same = (j[:, None] // gc) == (j[None, :] // gc)
    ang = (((j[:, None] % gc) * (j[None, :] % gc)) % gc).astype(F32) * (2.0 * jnp.pi / gc)
    c = jnp.where(same, jnp.cos(ang), 0.0)
    s = jnp.where(same, jnp.sin(ang), 0.0)
    return jnp.concatenate([c, s], axis=1).astype(BF16)


def _fnet_gmlp_body(zc_ref, zs_ref, cl_ref, sl_ref, cc_ref, sc_ref, w_ref, u_ref, z_ref, ws_ref, bs_ref, g_ref,
                    o_ref, gm_ref):
    gmlp = (u_ref, z_ref, ws_ref, bs_ref, g_ref, gm_ref)
    half = N_CHUNKS // 2
    y = _dot(cl_ref[...], zc_ref[0:SEQ, :])
    for n in range(half):
        _gmlp_chunk(n, *gmlp)
    y = y + _dot(sl_ref[...], zs_ref[0:SEQ, :])
    for n in range(half, N_CHUNKS):
        _gmlp_chunk(n, *gmlp)
    o_ref[0:SEQ, :] = _dot(y.astype(BF16), w_ref[...]).astype(BF16)
    y = _dot(cc_ref[...], zc_ref[SEQ:, :]) + _dot(sc_ref[...], zs_ref[SEQ:, :])
    o_ref[SEQ:, :] = _dot(y.astype(BF16), w_ref[...]).astype(BF16)


def _fnet_gmlp(zc, zs, tabs, w_fnet, p, w_spatial, b_spatial, g_gmlp):
    bsz = p.shape[0]
    cl, sl, cc, sc = tabs
    col = pl.BlockSpec((T_ALL, GROUP_WIDTH), lambda b: (0, b))
    bias = jnp.repeat(b_spatial.T, HEAD_DIM, axis=1)
    return pl.pallas_call(
        _fnet_gmlp_body,
        grid=(bsz,),
        in_specs=[col, col, _resident((SEQ, SEQ)), _resident((SEQ, SEQ)),
                  _resident((CTX_LEN, CTX_LEN)), _resident((CTX_LEN, CTX_LEN)),
                  _resident((GROUP_WIDTH, GROUP_WIDTH)),
                  _group_spec(3), _group_spec(4),
                  _resident((GROUP_HEADS, CHUNK, CHUNK)),
                  _resident((CHUNK, GROUP_WIDTH)),
                  _resident((1, GROUP_WIDTH))],
        out_specs=[col, pl.BlockSpec((1, T_ALL, GROUP_WIDTH), lambda b: (b, 0, 0))],
        out_shape=[jax.ShapeDtypeStruct((T_ALL, bsz * GROUP_WIDTH), BF16),
                   jax.ShapeDtypeStruct((bsz, T_ALL, GROUP_WIDTH), BF16)],
        compiler_params=_params(1),
        name="fnet_gmlp",
    )(zc, zs, cl, sl, cc, sc, w_fnet.astype(BF16), p, p, w_spatial.astype(BF16), bias,
      g_gmlp.reshape(1, GROUP_WIDTH))


def _post_body(a_ref, b_ref, c_ref, d_ref, x_ref, cx_ref, ml_ref, mc_ref, g_ref, wo_ref, w1_ref, w2_ref, gf_ref,
               o_ref, h_scr, acc_scr, *, final):
    gw = GROUP_WIDTH
    tm = x_ref.shape[1]
    mod = _mod_rows(ml_ref, mc_ref, tm)
    y = (_dot(a_ref[0], wo_ref[0, 0:gw, :]) + _dot(b_ref[0], wo_ref[0, gw:2 * gw, :])
         + _dot(c_ref[0], wo_ref[0, 2 * gw:3 * gw, :]) + _dot(d_ref[...], wo_ref[0, 3 * gw:, :]))
    if final:
        x = x_ref[0]
    else:
        ctx_tail = pl.program_id(1) == pl.num_programs(1) - 1
        x = jnp.concatenate([x_ref[0, 0:tm - CTX_LEN, :], _tail_rows(x_ref, cx_ref, ctx_tail)], axis=0)
    x1 = x + mod(2) * y
    h = _rms(x1, g_ref[...]) * (1.0 + mod(4)) + mod(3)
    h_scr[...] = h.astype(BF16)
    step = 512

    def up(j):
        return jnp.maximum(_dot(h_scr[...], w1_ref[0, :, j:j + step]), 0.0)

    a = up(0)
    for j in range(0, D_FF, step):
        nxt = up(j + step) if j + step < D_FF else None
        part = _dot((a * a).astype(BF16), w2_ref[0, j:j + step, :])
        if j == 0:
            acc_scr[...] = part
        else:
            acc_scr[...] += part
        a = nxt
    x2 = x1 + mod(5) * acc_scr[...]
    if final:
        x2 = _rms(x2, gf_ref[...])
    o_ref[0] = x2


def _post(a, b_, c_, d, tokens, ctx, mod_l, g_ffn, w_out, w_ff1, w_ff2, layer, g_final, final):
    bsz = tokens.shape[0]
    tm, out_len = (LATENT_TILE, SEQ) if final else (TOKEN_TILE, T_ALL)
    tok = pl.BlockSpec((1, tm, D_MODEL), lambda b, t: (b, t, 0))
    grp = pl.BlockSpec((1, tm, GROUP_WIDTH), lambda b, t: (b, t, 0))
    operands, token_specs = _token_operands(tokens, ctx, tm)
    return pl.pallas_call(
        functools.partial(_post_body, final=final),
        grid=(bsz, out_len // tm),
        in_specs=[grp, grp, grp, pl.BlockSpec((tm, GROUP_WIDTH), lambda b, t: (t, b))] + token_specs + _MOD_SPECS
                 + [_resident((1, D_MODEL)),
                    _layer_resident((D_MODEL, D_MODEL), layer),
                    _layer_resident((D_MODEL, D_FF), layer),
                    _layer_resident((D_FF, D_MODEL), layer),
                    _resident((1, D_MODEL))],
        out_specs=tok,
        out_shape=jax.ShapeDtypeStruct((bsz, out_len, D_MODEL), F32),
        scratch_shapes=[pltpu.VMEM((tm, D_MODEL), BF16), pltpu.VMEM((tm, D_MODEL), F32)],
        compiler_params=_params(2),
        name="out_proj_mlp",
    )(a, b_, c_, d, *operands, mod_l, mod_l, g_ffn.reshape(1, D_MODEL), w_out, w_ff1, w_ff2,
      g_final.reshape(1, D_MODEL))


def kernel(x, c, ctx, c_ctx, w_ada, b_ada, g_norm_mix, g_norm_ffn, w_in, b_gate, w_conv_qk, rpb, w_spatial,
           b_spatial, g_gmlp, g_mlstm, w_fnet, w_out, w_ff1, w_ff2, g_final):
    bsz = x.shape[0]
    assert bsz <= CTX_MOD_ROW and x.shape[1:] == (SEQ, D_MODEL) and ctx.shape[1:] == (CTX_LEN, D_MODEL)
    depth = w_ada.shape[0]
    tokens, ctx_rows = x, ctx
    cc = jnp.zeros((MOD_ROWS, D_MODEL), F32).at[:bsz].set(c).at[CTX_MOD_ROW].set(c_ctx)
    mod = _ada_mod(cc, w_ada, b_ada).reshape(depth, MOD_ROWS, N_MOD, D_MODEL)

    rope = _rope_tables()
    dft_c = _channel_dft()
    tabs = (_dft_tables(SEQ, (SEQ * GROUP_WIDTH // FNET_GROUPS) ** -0.5)
            + _dft_tables(CTX_LEN, (CTX_LEN * GROUP_WIDTH // FNET_GROUPS) ** -0.5))

    w_main, w_gate = _projection_weights(w_in)
    w_out, w_ff1, w_ff2 = w_out.astype(BF16), w_ff1.astype(BF16), w_ff2.astype(BF16)
    for l in range(depth):
        p, gates, zc, zs = _in_proj(tokens, ctx_rows, mod[l], g_norm_mix[l], w_main, w_gate, l, dft_c)
        a, c_ = _attn_mlstm(p, _attention_bias(rpb[l]), gates, w_conv_qk[l], b_gate[l], g_mlstm[l], rope,
                            ctx_out=(l < depth - 1))
        d, b_ = _fnet_gmlp(zc, zs, tabs, w_fnet[l], p, w_spatial[l], b_spatial[l], g_gmlp[l])
        tokens = _post(a, b_, c_, d, tokens, ctx_rows, mod[l], g_norm_ffn[l], w_out, w_ff1, w_ff2, l,
                       g_final, final=(l == depth - 1))
        ctx_rows = None
    return tokens
```

```python
import functools

import jax
import jax.numpy as jnp
from jax import lax
from jax.experimental import pallas as pl
from jax.experimental.pallas import tpu as pltpu

D_MODEL = 1024
SEQ = 2048
DEPTH = 2
GRID_W = 64
GRID_ROWS = SEQ // GRID_W
CTX_LEN = 256
T_ALL = CTX_LEN + SEQ
HEAD_DIM = 64
GROUP_WIDTH = 256
GROUP_HEADS = 4
WIN_ROWS = 8
WIN_COLS = 16
CHUNK = 128
N_CHUNKS = T_ALL // CHUNK
CTX_CHUNKS = CTX_LEN // CHUNK
LAT_CHUNKS = SEQ // CHUNK
CONV_W = 3
FNET_GROUPS = 4
ROPE_THETA = 10000.0
D_FF = 4 * D_MODEL
N_MOD = 6
EPS = 1e-6
NEG_INF = -1e30
N_GATES = 4 * GROUP_HEADS
OFF_G = 10 * GROUP_WIDTH
D_IN = OFF_G + N_GATES
MOD_ROWS = 24
CTX_MOD_ROW = 16
LANES = 128
BF16_ROWS = 16
GATE_I_COLS = (0, 2, 8, 10, 1, 3, 9, 11)
GATE_F_COLS = (4, 6, 12, 14, 5, 7, 13, 15)
TOKEN_TILE = 768
PROJ_TILE = 1152
LATENT_TILE = 512
P_WIDTH = 9 * GROUP_WIDTH
VMEM_LIMIT = 56 * 1024 * 1024

F32 = jnp.float32
BF16 = jnp.bfloat16


def _dot(a, b):
    return jnp.dot(a, b, preferred_element_type=F32)


def _dot_nt(a, b):
    return lax.dot_general(a, b, (((1,), (1,)), ((), ())), preferred_element_type=F32)


def _dot_f32(a, b):
    return jnp.dot(a, b, preferred_element_type=F32, precision=lax.Precision.HIGHEST)


def _sigmoid(x):
    return 0.5 * (1.0 + jnp.tanh(0.5 * x))


def _resident(shape):
    nd = len(shape)
    return pl.BlockSpec(shape, lambda *_: (0,) * nd, pipeline_mode=pl.Buffered(1))


def _layer_resident(shape, layer):
    nd = len(shape)
    return pl.BlockSpec((1,) + tuple(shape), lambda *_: (layer,) + (0,) * nd, pipeline_mode=pl.Buffered(1))


def _params(n_axes):
    return pltpu.CompilerParams(dimension_semantics=("arbitrary",) * n_axes,
                                vmem_limit_bytes=VMEM_LIMIT)


def _mod_body(c_ref, w_ref, b_ref, o_ref):
    s = c_ref[...]
    s = s * jax.nn.sigmoid(s)
    o_ref[0] = _dot(s.astype(BF16), w_ref[0].astype(BF16)) + b_ref[0]


def _ada_mod(cc, w_ada, b_ada):
    depth, d, n = w_ada.shape
    tn = 1536
    return pl.pallas_call(
        _mod_body,
        grid=(depth, n // tn),
        in_specs=[pl.BlockSpec((MOD_ROWS, d), lambda l, j: (0, 0)),
                  pl.BlockSpec((1, d, tn), lambda l, j: (l, 0, j)),
                  pl.BlockSpec((1, 1, tn), lambda l, j: (l, 0, j))],
        out_specs=pl.BlockSpec((1, MOD_ROWS, tn), lambda l, j: (l, 0, j)),
        out_shape=jax.ShapeDtypeStruct((depth, MOD_ROWS, n), F32),
        compiler_params=_params(2),
        name="ada_mod",
    )(cc, w_ada, b_ada.reshape(depth, 1, n))


_MOD_SPECS = [pl.BlockSpec((1, N_MOD, D_MODEL), lambda b, t: (b, 0, 0)),
              pl.BlockSpec((1, N_MOD, D_MODEL), lambda b, t: (CTX_MOD_ROW, 0, 0))]


def _mod_rows(mod_lat_ref, mod_ctx_ref, tile):
    tok = pl.program_id(1) * tile + lax.broadcasted_iota(jnp.int32, (tile, 1), 0)
    is_ctx = tok >= SEQ
    return lambda k: jnp.where(is_ctx, mod_ctx_ref[0, k:k + 1, :], mod_lat_ref[0, k:k + 1, :])


def _rms(x, g):
    return x * lax.rsqrt(jnp.mean(x * x, axis=-1, keepdims=True) + EPS) * g


def _tail_rows(x_ref, c_ref, ctx_tail):
    head = x_ref.shape[1] - CTX_LEN
    return jnp.where(ctx_tail, c_ref[0], x_ref[0, head:, :])


def _inproj_body(x_ref, c_ref, ml_ref, mc_ref, g_ref, w_ref, wg_ref, dft_ref, p_ref, gate_ref, zc_ref, zs_ref,
                 lhs_scr, inv_scr):
    tm = x_ref.shape[1]
    head = tm - CTX_LEN
    ctx_tail = pl.program_id(1) == pl.num_programs(1) - 1
    shift_tail = jnp.where(ctx_tail, mc_ref[0, 0:1, :], ml_ref[0, 0:1, :])
    scale_tail = jnp.where(ctx_tail, mc_ref[0, 1:2, :], ml_ref[0, 1:2, :])
    lead = BF16_ROWS
    lhs_scr[0:lead, :] = jnp.concatenate([ml_ref[0, 0:1, :], shift_tail,
                                          jnp.zeros((lead - 2, D_MODEL), F32)], axis=0).astype(BF16)
    x_head = x_ref[0, 0:head, :]
    x_tail = _tail_rows(x_ref, c_ref, ctx_tail)
    lhs_scr[lead:lead + head, :] = (x_head * (g_ref[...] * (1.0 + ml_ref[0, 1:2, :]))).astype(BF16)
    lhs_scr[lead + head:, :] = (x_tail * (g_ref[...] * (1.0 + scale_tail))).astype(BF16)
    for rows, x in ((slice(0, head), x_head), (slice(head, tm), x_tail)):
        inv = lax.rsqrt(jnp.mean(x * x, axis=-1, keepdims=True) + EPS)
        inv_scr[rows, :] = jnp.broadcast_to(inv, (x.shape[0], LANES))

    def project(cols=None):
        w = wg_ref[0] if cols is None else w_ref[0, :, cols]
        n = w.shape[1] // LANES
        r = _dot(lhs_scr[...], w)
        inv_head = jnp.concatenate([inv_scr[0:head, :]] * n, axis=1)
        inv_tail = jnp.concatenate([inv_scr[head:, :]] * n, axis=1)
        return jnp.concatenate([inv_head * r[lead:lead + head, :] + r[0:1, :],
                                inv_tail * r[lead + head:, :] + r[1:2, :]], axis=0)

    step = 2 * GROUP_WIDTH
    f = project(slice(P_WIDTH, OFF_G)).astype(BF16)
    p_ref[0, :, 0:step] = project(slice(0, step)).astype(BF16)
    z = _dot(f, dft_ref[...])
    zc_ref[...] = z[:, :GROUP_WIDTH].astype(BF16)
    zs_ref[...] = z[:, GROUP_WIDTH:].astype(BF16)
    for j in range(step, P_WIDTH - GROUP_WIDTH, step):
        p_ref[0, :, j:j + step] = project(slice(j, j + step)).astype(BF16)
    j = P_WIDTH - GROUP_WIDTH
    p_ref[0, :, j:] = project(slice(j, P_WIDTH)).astype(BF16)
    gate_ref[0] = project()


def _projection_weights(w_in):
    place = (jnp.array(GATE_I_COLS + GATE_F_COLS)[None, :] == jnp.arange(N_GATES)[:, None]).astype(F32)
    place = jnp.pad(place, ((0, 0), (0, LANES - N_GATES)))
    w_gate = jnp.einsum('ldg,gc->ldc', w_in[:, :, OFF_G:], place, precision=lax.Precision.HIGHEST)
    return w_in[:, :, :OFF_G].astype(BF16), w_gate.astype(BF16)


def _token_operands(tokens, ctx, tm):
    tile = pl.BlockSpec((1, tm, D_MODEL), lambda b, t: (b, t, 0))
    if ctx is None:
        return (tokens, tokens), [tile, pl.BlockSpec((1, CTX_LEN, D_MODEL), lambda b, t: (b, SEQ // CTX_LEN, 0))]
    return (tokens, ctx), [tile, pl.BlockSpec((1, CTX_LEN, D_MODEL), lambda b, t: (b, 0, 0))]


def _in_proj(tokens, ctx, mod_l, g, w_main, w_gate, layer, dft_c):
    bsz = tokens.shape[0]
    tm = PROJ_TILE
    operands, token_specs = _token_operands(tokens, ctx, tm)
    return pl.pallas_call(
        _inproj_body,
        grid=(bsz, T_ALL // tm),
        in_specs=token_specs + _MOD_SPECS
                 + [_resident((1, D_MODEL)),
                    _layer_resident(w_main.shape[1:], layer),
                    _layer_resident(w_gate.shape[1:], layer),
                    _resident((GROUP_WIDTH, 2 * GROUP_WIDTH))],
        out_specs=[pl.BlockSpec((1, tm, P_WIDTH), lambda b, t: (b, t, 0)),
                   pl.BlockSpec((1, tm, LANES), lambda b, t: (b, t, 0)),
                   pl.BlockSpec((tm, GROUP_WIDTH), lambda b, t: (t, b)),
                   pl.BlockSpec((tm, GROUP_WIDTH), lambda b, t: (t, b))],
        out_shape=[jax.ShapeDtypeStruct((bsz, T_ALL, P_WIDTH), BF16),
                   jax.ShapeDtypeStruct((bsz, T_ALL, LANES), F32),
                   jax.ShapeDtypeStruct((T_ALL, bsz * GROUP_WIDTH), BF16),
                   jax.ShapeDtypeStruct((T_ALL, bsz * GROUP_WIDTH), BF16)],
        scratch_shapes=[pltpu.VMEM((tm + BF16_ROWS, D_MODEL), BF16), pltpu.VMEM((tm, LANES), F32)],
        compiler_params=_params(2),
        name="in_proj",
    )(*operands, mod_l, mod_l, g.reshape(1, D_MODEL), w_main, w_gate, dft_c)


def _group_spec(col_block):
    return pl.BlockSpec((1, T_ALL, GROUP_WIDTH), lambda b: (b, 0, col_block))


def _softmax_pv(parts):
    m = functools.reduce(jnp.maximum, [jnp.max(s, axis=-1, keepdims=True) for s, _ in parts])
    es = [jnp.exp(s - m) for s, _ in parts]
    den = functools.reduce(jnp.add, [jnp.sum(e, axis=-1, keepdims=True) for e in es])
    num = functools.reduce(jnp.add, [_dot(e.astype(BF16), v) for e, (_, v) in zip(es, parts)])
    return num / den


def _attn_parts(q_ref, k_ref, v_ref, bias_ref, o_ref, ctx_out):
    scale = HEAD_DIM ** -0.5
    pairs = [slice(i * LANES, (i + 1) * LANES) for i in range(GROUP_WIDTH // LANES)]

    def stack_pair(q2):
        low = lax.broadcasted_iota(jnp.int32, q2.shape, 1) < HEAD_DIM
        zero = jnp.zeros_like(q2)
        return low, jnp.concatenate([jnp.where(low, q2, zero), jnp.where(low, zero, q2)], axis=0)

    def unstack_pair(low, o):
        n = o.shape[0] // 2
        return jnp.where(low, o[:n], o[n:]).astype(BF16)

    ctx = slice(SEQ, T_ALL)

    def context_rows():
        if ctx_out:
            ctx_tiles = []
            for ps in pairs:
                low, qm = stack_pair(q_ref[0, ctx, ps] * scale)
                ctx_tiles.append((low, _dot_nt(qm, k_ref[0, ctx, ps])))
            for ps, (low, s) in zip(pairs, ctx_tiles):
                o_ref[0, ctx, ps] = unstack_pair(low, _softmax_pv([(s, v_ref[0, ctx, ps])]))
        else:
            o_ref[0, ctx, :] = jnp.zeros((CTX_LEN, GROUP_WIDTH), BF16)

    rows_per_trip = 4
    tile_rows = 2 * GRID_W

    def trip(g, carry):
        tiles = [(dr, i) for dr in range(rows_per_trip) for i in range(len(pairs))]
        q0s = [pl.multiple_of((g * rows_per_trip + dr) * GRID_W, GRID_W) for dr in range(rows_per_trip)]
        stacked = {(dr, i): stack_pair(q_ref[0, pl.ds(q0s[dr], GRID_W), pairs[i]] * scale) for dr, i in tiles}
        s_ctx = [_dot_nt(jnp.concatenate([stacked[dr, i][1] for dr in range(rows_per_trip)], axis=0),
                         k_ref[0, ctx, pairs[i]]) for i in range(len(pairs))]

        def window_scores(dr, i):
            r = g * rows_per_trip + dr
            rs = jnp.clip(r - WIN_ROWS // 2, 0, GRID_ROWS - WIN_ROWS)
            k0 = pl.multiple_of(rs * GRID_W, GRID_W)
            s_loc = _dot_nt(stacked[dr, i][1], k_ref[0, pl.ds(k0, WIN_ROWS * GRID_W), pairs[i]])
            return s_loc + bias_ref[i, r - rs], k0

        partial = {}
        cur = window_scores(*tiles[0])
        for j, (dr, i) in enumerate(tiles):
            nxt = window_scores(*tiles[j + 1]) if j + 1 < len(tiles) else None
            s_loc, k0 = cur
            s_c = s_ctx[i][dr * tile_rows:(dr + 1) * tile_rows, :]
            m = jnp.maximum(jnp.max(s_loc, axis=-1, keepdims=True), jnp.max(s_c, axis=-1, keepdims=True))
            e_loc, e_ctx = jnp.exp(s_loc - m), jnp.exp(s_c - m)
            den = jnp.sum(e_loc, axis=-1, keepdims=True) + jnp.sum(e_ctx, axis=-1, keepdims=True)
            num = _dot(e_loc.astype(BF16), v_ref[0, pl.ds(k0, WIN_ROWS * GRID_W), pairs[i]])
            partial[dr, i] = (num, e_ctx.astype(BF16), den)
            cur = nxt

        for i, ps in enumerate(pairs):
            num_ctx = _dot(jnp.concatenate([partial[dr, i][1] for dr in range(rows_per_trip)], axis=0),
                           v_ref[0, ctx, ps])
            for dr in range(rows_per_trip):
                num, _, den = partial[dr, i]
                o = (num + num_ctx[dr * tile_rows:(dr + 1) * tile_rows, :]) / den
                o_ref[0, pl.ds(q0s[dr], GRID_W), ps] = unstack_pair(stacked[dr, i][0], o)
        return carry

    return context_rows, trip, GRID_ROWS // rows_per_trip


def _attention_bias(rpb):
    c = jnp.arange(GRID_W)
    qstart = jnp.clip(c - WIN_COLS // 2, 0, GRID_W - WIN_COLS)
    in_win = (c[None, :] >= qstart[:, None]) & (c[None, :] < qstart[:, None] + WIN_COLS)
    col = jnp.clip(c[None, :] - c[:, None] + WIN_COLS - 1, 0, 2 * WIN_COLS - 2)
    pick_col = (col[:, :, None] == jnp.arange(2 * WIN_COLS - 1)).astype(F32)
    by_drow = jnp.einsum('hde,qke->hdqk', rpb.astype(F32), pick_col, precision=lax.Precision.HIGHEST)
    return jnp.where(in_win, by_drow, NEG_INF)


def _expand_bias(by_drow_ref, bias_scr):
    for pair in range(GROUP_HEADS // 2):
        for v in range(WIN_ROWS):
            for hh in range(2):
                for j in range(0, WIN_ROWS, 2):
                    two_rows = [by_drow_ref[2 * pair + hh, jj - v + WIN_ROWS - 1] for jj in (j, j + 1)]
                    bias_scr[pair, v, hh * GRID_W:(hh + 1) * GRID_W, j * GRID_W:(j + 2) * GRID_W] = (
                        jnp.concatenate(two_rows, axis=1))


def _gmlp_chunk(n, u_ref, z_ref, ws_ref, bs_ref, g_ref, o_ref):
    rows = slice(n * CHUNK, (n + 1) * CHUNK)
    z = _rms(jax.nn.gelu(z_ref[0, rows, :].astype(F32)), g_ref[...]).astype(BF16)
    u = jax.nn.gelu(u_ref[0, rows, :].astype(F32))
    for h in range(GROUP_HEADS):
        hs = slice(h * HEAD_DIM, (h + 1) * HEAD_DIM)
        mixed = _dot(ws_ref[h], z[:, hs]) + bs_ref[:, hs]
        o_ref[0, rows, hs] = (u[:, hs] * mixed).astype(BF16)


def _mlstm_parts(q_ref, k_ref, v_ref, og_ref, gate_ref, wc_ref, bg_ref, cos_ref, sina_ref,
                 sinb_ref, gln_ref, tri_ref, o_ref,
                 qtz_scr, k_scr, kbd_scr, vat_scr, acol_scr, rows_scr, ht_scr, c_scr, m_scr):
    half = HEAD_DIM // 4
    n_chains = 2 * GROUP_HEADS

    def conv_act(ref, n, w0, w1, w2, rope, post):
        rows = pl.ds(pl.multiple_of(n * CHUNK, CHUNK), CHUNK)
        cur = ref[0, rows, :].astype(F32)
        first = (n == 0) | (n == LAT_CHUNKS)
        last = (n == LAT_CHUNKS - 1) | (n == N_CHUNKS - 1)
        before = pl.ds(pl.multiple_of(jnp.maximum(n * CHUNK - BF16_ROWS, 0), BF16_ROWS), BF16_ROWS)
        after = pl.ds(pl.multiple_of(jnp.minimum((n + 1) * CHUNK, T_ALL - BF16_ROWS), BF16_ROWS), BF16_ROWS)
        tail = jnp.where(first, 0.0, 1.0) * ref[0, before, :][BF16_ROWS - 1:BF16_ROWS, :].astype(F32)
        head = jnp.where(last, 0.0, 1.0) * ref[0, after, :][0:1, :].astype(F32)
        sub = lax.broadcasted_iota(jnp.int32, (CHUNK, 1), 0)
        x_prev = jnp.where(sub == 0, tail, pltpu.roll(cur, 1, 0))
        x_next = jnp.where(sub == CHUNK - 1, head, pltpu.roll(cur, CHUNK - 1, 0))
        y = w0 * x_prev + w1 * cur + w2 * x_next
        y = y * _sigmoid(y)
        if rope:
            y = (y * cos_ref[rows, :] + pltpu.roll(y, GROUP_WIDTH - half, 1) * sina_ref[rows, :]
                 + pltpu.roll(y, half, 1) * sinb_ref[rows, :])
        return y * post

    def prep(n, rope):
        rows = pl.ds(pl.multiple_of(n * CHUNK, CHUNK), CHUNK)
        gw = GROUP_WIDTH
        q = conv_act(q_ref, n, wc_ref[0:1, :gw], wc_ref[1:2, :gw], wc_ref[2:3, :gw], rope, 1.0)
        k = conv_act(k_ref, n, wc_ref[0:1, gw:], wc_ref[1:2, gw:], wc_ref[2:3, gw:], rope, HEAD_DIM ** -0.5)
        q_t = q.T.astype(BF16)
        v_t = v_ref[0, rows, :].astype(F32).T.astype(BF16)
        k = k.astype(BF16)
        k_scr[n] = k
        sub = lax.broadcasted_iota(jnp.int32, (HEAD_DIM, CHUNK), 0)
        ones_row = jnp.where(sub == 0, 1.0, 0.0).astype(BF16)
        zeros = jnp.zeros((HEAD_DIM, CHUNK), BF16)
        low = lax.broadcasted_iota(jnp.int32, (CHUNK, LANES), 1) < HEAD_DIM
        for pair in range(GROUP_HEADS // 2):
            h0 = slice(2 * pair * HEAD_DIM, (2 * pair + 1) * HEAD_DIM)
            h1 = slice((2 * pair + 1) * HEAD_DIM, (2 * pair + 2) * HEAD_DIM)
            qtz_scr[n, pair] = jnp.concatenate([jnp.concatenate([q_t[h0, :], zeros], axis=1),
                                                jnp.concatenate([zeros, q_t[h1, :]], axis=1)], axis=0)
            k_pair = k[:, pair * LANES:(pair + 1) * LANES]
            kbd_scr[n, pair] = jnp.concatenate([jnp.where(low, k_pair, jnp.zeros_like(k_pair)),
                                                jnp.where(low, jnp.zeros_like(k_pair), k_pair)], axis=0)
            vat_scr[n, pair] = jnp.concatenate([jnp.concatenate([v_t[h0, :], ones_row], axis=0),
                                                jnp.concatenate([v_t[h1, :], ones_row], axis=0)], axis=1)
        ht_scr[n] = jnp.zeros((GROUP_WIDTH, CHUNK), F32)

        g_i = gate_ref[0, rows, :] + bg_ref[...]
        lf = pltpu.roll(jax.nn.log_sigmoid(g_i), LANES - n_chains, 1)
        pre = _dot_f32(tri_ref[...], lf)
        suf = pre[CHUNK - 1:CHUNK, :] - pre + lf
        lane = lax.broadcasted_iota(jnp.int32, (CHUNK, LANES), 1)
        b = jnp.where(lane % 4 < 2, pre, suf)
        a = g_i - b
        acol_scr[n] = a
        a_t = a.T[0:n_chains, :]
        b_t = b.T[0:n_chains, :]
        chain = lax.broadcasted_iota(jnp.int32, (n_chains, CHUNK), 0)
        b_end = jnp.where(chain % 4 < 2, b_t[:, CHUNK - 1:CHUNK], b_t[:, 0:1])
        b_end = jnp.broadcast_to(b_end, (n_chains, CHUNK))
        a_max = jnp.broadcast_to(jnp.max(a_t, axis=1, keepdims=True), (n_chains, CHUNK))
        for kind, rows8 in enumerate((a_t, b_t, b_end, a_max)):
            rows_scr[n, kind] = jnp.concatenate([rows8[0:4, :], rows8[4:8, :]], axis=1)

    def reset_state():
        c_scr[...] = jnp.zeros(c_scr.shape, F32)
        m_scr[...] = jnp.zeros(m_scr.shape, F32)

    s_idx = lax.broadcasted_iota(jnp.int32, (CHUNK, CHUNK), 0)
    t_idx = lax.broadcasted_iota(jnp.int32, (CHUNK, CHUNK), 1)

    zeros = jnp.zeros((CHUNK, CHUNK), BF16)
    low = lax.broadcasted_iota(jnp.int32, (1, LANES), 1) < HEAD_DIM

    def scan(i, carry):
        chunk_of = (jnp.where(i < CTX_CHUNKS, LAT_CHUNKS + i, i - CTX_CHUNKS), N_CHUNKS - 1 - i)
        steps = [(rev, pair, chunk_of[rev]) for rev in range(2) for pair in range(GROUP_HEADS // 2)]

        scores = []
        for rev, pair, n in steps:
            dp = 2 * rev + pair
            causal = (s_idx >= t_idx) if rev else (s_idx <= t_idx)
            arg = jnp.concatenate([jnp.where(causal, acol_scr[n, :, dp:dp + 1], NEG_INF),
                                   jnp.where(causal, acol_scr[n, :, 4 + dp:5 + dp], NEG_INF)], axis=1)
            s = _dot(k_scr[n, :, pair * LANES:(pair + 1) * LANES], qtz_scr[n, pair])
            scores.append((arg, jnp.max(arg, axis=0, keepdims=True), s))

        inter = []
        for rev, pair, n in steps:
            dp = 2 * rev + pair
            a_row = rows_scr[n, 0, dp:dp + 1, :]
            b_end = rows_scr[n, 2, dp:dp + 1, :]
            a_max = rows_scr[n, 3, dp:dp + 1, :]
            m_mem = m_scr[dp:dp + 1, :]
            c_mem = c_scr[dp]
            inter.append((_dot(c_mem.astype(BF16), qtz_scr[n, pair]), m_mem))
            m_new = b_end + jnp.maximum(m_mem, a_max)
            w_src = jnp.exp(b_end + a_row - m_new)
            decay = jnp.exp(b_end + m_mem - m_new)
            decay = jnp.where(low, decay[:, :CHUNK], decay[:, CHUNK:])
            c_scr[dp] = decay * c_mem + _dot((vat_scr[n, pair].astype(F32) * w_src).astype(BF16),
                                             kbd_scr[n, pair])
            m_scr[dp:dp + 1, :] = m_new

        for (rev, pair, n), (arg, cm, s), (x2, m_mem) in zip(steps, scores, inter):
            dp = 2 * rev + pair
            g = (s * jnp.exp(arg - cm)).astype(BF16)
            g_bd = jnp.concatenate([jnp.concatenate([g[:, :CHUNK], zeros], axis=1),
                                    jnp.concatenate([zeros, g[:, CHUNK:]], axis=1)], axis=0)
            x1 = _dot(vat_scr[n, pair], g_bd)
            b_row = rows_scr[n, 1, dp:dp + 1, :]
            mu = jnp.maximum(cm, m_mem)
            both = x1 * jnp.exp(cm - mu) + x2 * jnp.exp(m_mem - mu)
            den = both[HEAD_DIM:HEAD_DIM + 1, :]
            inv = 1.0 / jnp.maximum(jnp.abs(den), jnp.exp(-(b_row + mu)))
            h_t = both[0:HEAD_DIM, :] * inv
            ht_scr[n, pair * LANES:(pair + 1) * LANES, :] += jnp.concatenate([h_t[:, :CHUNK], h_t[:, CHUNK:]],
                                                                             axis=0)
        return carry

    def finish(n, carry):
        rows = pl.ds(pl.multiple_of(n * CHUNK, CHUNK), CHUNK)
        gate = _sigmoid(og_ref[0, rows, :].astype(F32))
        ys = []
        for h in range(GROUP_HEADS):
            x = ht_scr[n, h * HEAD_DIM:(h + 1) * HEAD_DIM, :]
            mu = jnp.mean(x, axis=0, keepdims=True)
            var = jnp.mean(jnp.square(x - mu), axis=0, keepdims=True)
            ys.append((x - mu) * lax.rsqrt(var + EPS))
        y = jnp.concatenate(ys, axis=0).T * gln_ref[...]
        o_ref[0, rows, :] = (gate * y).astype(BF16)
        return carry

    return prep, reset_state, scan, finish


def _attn_mlstm_body(aq_ref, ak_ref, av_ref, by_drow_ref, mq_ref, mk_ref, mv_ref, og_ref, gate_ref, wc_ref,
                     bg_ref, cos_ref, sina_ref, sinb_ref, gln_ref, tri_ref, a_ref, m_ref, bias_scr, *scratch,
                     ctx_out):
    @pl.when(pl.program_id(0) == 0)
    def _():
        _expand_bias(by_drow_ref, bias_scr)

    context_rows, trip, n_trips = _attn_parts(aq_ref, ak_ref, av_ref, bias_scr, a_ref, ctx_out)
    prep, reset_state, scan, finish = _mlstm_parts(mq_ref, mk_ref, mv_ref, og_ref, gate_ref, wc_ref, bg_ref,
                                                   cos_ref, sina_ref, sinb_ref, gln_ref, tri_ref, m_ref, *scratch)
    per_trip = LAT_CHUNKS // n_trips

    def prep_context(n, carry):
        prep(n, False)
        return carry

    def trip_and_prep(g, carry):
        for j in range(per_trip):
            prep(g * per_trip + j, True)
        trip(g, carry)
        return carry

    lax.fori_loop(LAT_CHUNKS, N_CHUNKS, prep_context, 0)
    context_rows()
    lax.fori_loop(0, n_trips, trip_and_prep, 0)
    reset_state()
    lax.fori_loop(0, N_CHUNKS, scan, 0, unroll=6)
    lax.fori_loop(0, N_CHUNKS, finish, 0, unroll=3)


def _rope_tables():
    lane = jnp.arange(GROUP_WIDTH)
    m = HEAD_DIM // 4
    inv = ROPE_THETA ** (-(lane % m).astype(F32) / m)
    by_row = ((lane % HEAD_DIM) // (HEAD_DIM // 2) == 0)[None, None, :]
    low = ((lane % (2 * m)) < m)[None, None, :]
    ang = jnp.arange(GRID_W, dtype=F32)[:, None] * inv[None, :]
    cos, sin = jnp.cos(ang), jnp.sin(ang)

    def spread(tab):
        return jnp.where(by_row, tab[:GRID_ROWS, None, :], tab[None, :, :]).reshape(SEQ, GROUP_WIDTH)

    cos, sin = spread(cos), spread(sin)
    low = jnp.broadcast_to(low, (GRID_ROWS, GRID_W, GROUP_WIDTH)).reshape(SEQ, GROUP_WIDTH)
    return cos, jnp.where(low, -sin, 0.0), jnp.where(low, 0.0, sin)


def _attn_mlstm(p, by_drow, gates, w_conv, b_gate, g_mlstm, rope, ctx_out):
    bsz = p.shape[0]
    i = jnp.arange(CHUNK)
    tri = (i[:, None] >= i[None, :]).astype(F32)
    bg = jnp.pad(b_gate[jnp.array(GATE_I_COLS + GATE_F_COLS)], (0, LANES - N_GATES)).reshape(1, LANES)
    cos, sina, sinb = rope
    n_pairs = GROUP_HEADS // 2
    group_out = pl.BlockSpec((1, T_ALL, GROUP_WIDTH), lambda b: (b, 0, 0))
    return pl.pallas_call(
        functools.partial(_attn_mlstm_body, ctx_out=ctx_out),
        grid=(bsz,),
        in_specs=[_group_spec(0), _group_spec(1), _group_spec(2), _resident(by_drow.shape),
                  _group_spec(5), _group_spec(6), _group_spec(7), _group_spec(8),
                  pl.BlockSpec((1, T_ALL, LANES), lambda b: (b, 0, 0)),
                  _resident((CONV_W, 2 * GROUP_WIDTH)),
                  _resident((1, LANES)),
                  _resident((SEQ, GROUP_WIDTH)), _resident((SEQ, GROUP_WIDTH)), _resident((SEQ, GROUP_WIDTH)),
                  _resident((1, GROUP_WIDTH)),
                  _resident((CHUNK, CHUNK))],
        out_specs=[group_out, group_out],
        out_shape=[jax.ShapeDtypeStruct((bsz, T_ALL, GROUP_WIDTH), BF16)] * 2,
        scratch_shapes=[pltpu.VMEM((n_pairs, WIN_ROWS, 2 * GRID_W, WIN_ROWS * GRID_W), F32),
                        pltpu.VMEM((N_CHUNKS, n_pairs, LANES, 2 * CHUNK), BF16),
                        pltpu.VMEM((N_CHUNKS, CHUNK, GROUP_WIDTH), BF16),
                        pltpu.VMEM((N_CHUNKS, n_pairs, 2 * CHUNK, LANES), BF16),
                        pltpu.VMEM((N_CHUNKS, n_pairs, LANES, 2 * CHUNK), BF16),
                        pltpu.VMEM((N_CHUNKS, CHUNK, LANES), F32),
                        pltpu.VMEM((N_CHUNKS, 4, 2 * n_pairs, 2 * CHUNK), F32),
                        pltpu.VMEM((N_CHUNKS, GROUP_WIDTH, CHUNK), F32),
                        pltpu.VMEM((2 * n_pairs, LANES, LANES), F32),
                        pltpu.VMEM((2 * n_pairs, 2 * CHUNK), F32)],
        compiler_params=_params(1),
        name="attn_mlstm",
    )(p, p, p, by_drow, p, p, p, p, gates, w_conv, bg, cos, sina, sinb, g_mlstm.reshape(1, GROUP_WIDTH), tri)


def _dft_tables(n, scale):
    f = 1 << (n.bit_length() // 2)
    s = jnp.arange(n, dtype=jnp.int32)[None, :]
    ang_a = ((f * jnp.arange(n // f, dtype=jnp.int32)[:, None] * s) % n).astype(F32) * (2.0 * jnp.pi / n)
    ang_b = ((jnp.arange(f, dtype=jnp.int32)[:, None] * s) % n).astype(F32) * (2.0 * jnp.pi / n)
    ca, sa = jnp.cos(ang_a)[:, None, :], jnp.sin(ang_a)[:, None, :]
    cb, sb = jnp.cos(ang_b)[None, :, :], jnp.sin(ang_b)[None, :, :]
    cos = (ca * cb - sa * sb).reshape(n, n)
    sin = (sa * cb + ca * sb).reshape(n, n)
    return (cos * scale).astype(BF16), (-sin * scale).astype(BF16)


def _channel_dft():
    gc = GROUP_WIDTH // FNET_GROUPS
    j = jnp.arange(GROUP_WIDTH, dtype=jnp.int32)
    same = (j[:, None] // gc) == (j[None, :] // gc)
    ang = (((j[:, None] % gc) * (j[None, :] % gc)) % gc).astype(F32) * (2.0 * jnp.pi / gc)
    c = jnp.where(same, jnp.cos(ang), 0.0)
    s = jnp.where(same, jnp.sin(ang), 0.0)
    return jnp.concatenate([c, s], axis=1).astype(BF16)


def _fnet_gmlp_body(zc_ref, zs_ref, cl_ref, sl_ref, cc_ref, sc_ref, w_ref, u_ref, z_ref, ws_ref, bs_ref, g_ref,
                    o_ref, gm_ref):
    gmlp = (u_ref, z_ref, ws_ref, bs_ref, g_ref, gm_ref)
    half = N_CHUNKS // 2
    y = _dot(cl_ref[...], zc_ref[0:SEQ, :])
    for n in range(half):
        _gmlp_chunk(n, *gmlp)
    y = y + _dot(sl_ref[...], zs_ref[0:SEQ, :])
    for n in range(half, N_CHUNKS):
        _gmlp_chunk(n, *gmlp)
    o_ref[0:SEQ, :] = _dot(y.astype(BF16), w_ref[...]).astype(BF16)
    y = _dot(cc_ref[...], zc_ref[SEQ:, :]) + _dot(sc_ref[...], zs_ref[SEQ:, :])
    o_ref[SEQ:, :] = _dot(y.astype(BF16), w_ref[...]).astype(BF16)


def _fnet_gmlp(zc, zs, tabs, w_fnet, p, w_spatial, b_spatial, g_gmlp):
    bsz = p.shape[0]
    cl, sl, cc, sc = tabs
    col = pl.BlockSpec((T_ALL, GROUP_WIDTH), lambda b: (0, b))
    bias = jnp.repeat(b_spatial.T, HEAD_DIM, axis=1)
    return pl.pallas_call(
        _fnet_gmlp_body,
        grid=(bsz,),
        in_specs=[col, col, _resident((SEQ, SEQ)), _resident((SEQ, SEQ)),
                  _resident((CTX_LEN, CTX_LEN)), _resident((CTX_LEN, CTX_LEN)),
                  _resident((GROUP_WIDTH, GROUP_WIDTH)),
                  _group_spec(3), _group_spec(4),
                  _resident((GROUP_HEADS, CHUNK, CHUNK)),
                  _resident((CHUNK, GROUP_WIDTH)),
                  _resident((1, GROUP_WIDTH))],
        out_specs=[col, pl.BlockSpec((1, T_ALL, GROUP_WIDTH), lambda b: (b, 0, 0))],
        out_shape=[jax.ShapeDtypeStruct((T_ALL, bsz * GROUP_WIDTH), BF16),
                   jax.ShapeDtypeStruct((bsz, T_ALL, GROUP_WIDTH), BF16)],
        compiler_params=_params(1),
        name="fnet_gmlp",
    )(zc, zs, cl, sl, cc, sc, w_fnet.astype(BF16), p, p, w_spatial.astype(BF16), bias,
      g_gmlp.reshape(1, GROUP_WIDTH))


def _post_body(a_ref, b_ref, c_ref, d_ref, x_ref, cx_ref, ml_ref, mc_ref, g_ref, wo_ref, w1_ref, w2_ref, gf_ref,
               o_ref, h_scr, acc_scr, *, final):
    gw = GROUP_WIDTH
    tm = x_ref.shape[1]
    mod = _mod_rows(ml_ref, mc_ref, tm)
    y = (_dot(a_ref[0], wo_ref[0, 0:gw, :]) + _dot(b_ref[0], wo_ref[0, gw:2 * gw, :])
         + _dot(c_ref[0], wo_ref[0, 2 * gw:3 * gw, :]) + _dot(d_ref[...], wo_ref[0, 3 * gw:, :]))
    if final:
        x = x_ref[0]
    else:
        ctx_tail = pl.program_id(1) == pl.num_programs(1) - 1
        x = jnp.concatenate([x_ref[0, 0:tm - CTX_LEN, :], _tail_rows(x_ref, cx_ref, ctx_tail)], axis=0)
    x1 = x + mod(2) * y
    h = _rms(x1, g_ref[...]) * (1.0 + mod(4)) + mod(3)
    h_scr[...] = h.astype(BF16)
    step = 512

    def up(j):
        return jnp.maximum(_dot(h_scr[...], w1_ref[0, :, j:j + step]), 0.0)

    a = up(0)
    for j in range(0, D_FF, step):
        nxt = up(j + step) if j + step < D_FF else None
        part = _dot((a * a).astype(BF16), w2_ref[0, j:j + step, :])
        if j == 0:
            acc_scr[...] = part
        else:
            acc_scr[...] += part
        a = nxt
    x2 = x1 + mod(5) * acc_scr[...]
    if final:
        x2 = _rms(x2, gf_ref[...])
    o_ref[0] = x2


def _post(a, b_, c_, d, tokens, ctx, mod_l, g_ffn, w_out, w_ff1, w_ff2, layer, g_final, final):
    bsz = tokens.shape[0]
    tm, out_len = (LATENT_TILE, SEQ) if final else (TOKEN_TILE, T_ALL)
    tok = pl.BlockSpec((1, tm, D_MODEL), lambda b, t: (b, t, 0))
    grp = pl.BlockSpec((1, tm, GROUP_WIDTH), lambda b, t: (b, t, 0))
    operands, token_specs = _token_operands(tokens, ctx, tm)
    return pl.pallas_call(
        functools.partial(_post_body, final=final),
        grid=(bsz, out_len // tm),
        in_specs=[grp, grp, grp, pl.BlockSpec((tm, GROUP_WIDTH), lambda b, t: (t, b))] + token_specs + _MOD_SPECS
                 + [_resident((1, D_MODEL)),
                    _layer_resident((D_MODEL, D_MODEL), layer),
                    _layer_resident((D_MODEL, D_FF), layer),
                    _layer_resident((D_FF, D_MODEL), layer),
                    _resident((1, D_MODEL))],
        out_specs=tok,
        out_shape=jax.ShapeDtypeStruct((bsz, out_len, D_MODEL), F32),
        scratch_shapes=[pltpu.VMEM((tm, D_MODEL), BF16), pltpu.VMEM((tm, D_MODEL), F32)],
        compiler_params=_params(2),
        name="out_proj_mlp",
    )(a, b_, c_, d, *operands, mod_l, mod_l, g_ffn.reshape(1, D_MODEL), w_out, w_ff1, w_ff2,
      g_final.reshape(1, D_MODEL))


def kernel(x, c, ctx, c_ctx, w_ada, b_ada, g_norm_mix, g_norm_ffn, w_in, b_gate, w_conv_qk, rpb, w_spatial,
           b_spatial, g_gmlp, g_mlstm, w_fnet, w_out, w_ff1, w_ff2, g_final):
    bsz = x.shape[0]
    assert bsz <= CTX_MOD_ROW and x.shape[1:] == (SEQ, D_MODEL) and ctx.shape[1:] == (CTX_LEN, D_MODEL)
    depth = w_ada.shape[0]
    tokens, ctx_rows = x, ctx
    cc = jnp.zeros((MOD_ROWS, D_MODEL), F32).at[:bsz].set(c).at[CTX_MOD_ROW].set(c_ctx)
    mod = _ada_mod(cc, w_ada, b_ada).reshape(depth, MOD_ROWS, N_MOD, D_MODEL)

    rope = _rope_tables()
    dft_c = _channel_dft()
    tabs = (_dft_tables(SEQ, (SEQ * GROUP_WIDTH // FNET_GROUPS) ** -0.5)
            + _dft_tables(CTX_LEN, (CTX_LEN * GROUP_WIDTH // FNET_GROUPS) ** -0.5))

    w_main, w_gate = _projection_weights(w_in)
    w_out, w_ff1, w_ff2 = w_out.astype(BF16), w_ff1.astype(BF16), w_ff2.astype(BF16)
    for l in range(depth):
        p, gates, zc, zs = _in_proj(tokens, ctx_rows, mod[l], g_norm_mix[l], w_main, w_gate, l, dft_c)
        a, c_ = _attn_mlstm(p, _attention_bias(rpb[l]), gates, w_conv_qk[l], b_gate[l], g_mlstm[l], rope,
                            ctx_out=(l < depth - 1))
        d, b_ = _fnet_gmlp(zc, zs, tabs, w_fnet[l], p, w_spatial[l], b_spatial[l], g_gmlp[l])
        tokens = _post(a, b_, c_, d, tokens, ctx_rows, mod[l], g_norm_ffn[l], w_out, w_ff1, w_ff2, l,
                       g_final, final=(l == depth - 1))
        ctx_rows = None
    return tokens
```

```python
import functools

import jax
import jax.numpy as jnp
from jax import lax
from jax.experimental import pallas as pl
from jax.experimental.pallas import tpu as pltpu

D_MODEL = 1024
SEQ = 2048
GRID_W = 64
GRID_ROWS = SEQ // GRID_W
CTX_LEN = 256
T_ALL = CTX_LEN + SEQ
HEAD_DIM = 64
GROUP_WIDTH = 256
GROUP_HEADS = 4
WIN_ROWS = 8
WIN_COLS = 16
CHUNK = 128
N_CHUNKS = T_ALL // CHUNK
CTX_CHUNKS = CTX_LEN // CHUNK
LAT_CHUNKS = SEQ // CHUNK
CONV_W = 3
FNET_GROUPS = 4
ROPE_THETA = 10000.0
D_FF = 4 * D_MODEL
N_MOD = 6
EPS = 1e-6
NEG_INF = -1e30
N_GATES = 4 * GROUP_HEADS
OFF_G = 10 * GROUP_WIDTH
MOD_ROWS = 24
CTX_MOD_ROW = 16
LANES = 128
BF16_ROWS = 16
GATE_I_COLS = (0, 2, 8, 10, 1, 3, 9, 11)
GATE_F_COLS = (4, 6, 12, 14, 5, 7, 13, 15)
TOKEN_TILE = 768
PROJ_TILE = 1152
LATENT_TILE = 512
P_WIDTH = 9 * GROUP_WIDTH
VMEM_LIMIT = 56 * 1024 * 1024

F32 = jnp.float32
BF16 = jnp.bfloat16


def _dot(a, b):
    return jnp.dot(a, b, preferred_element_type=F32)


def _dot_nt(a, b):
    return lax.dot_general(a, b, (((1,), (1,)), ((), ())), preferred_element_type=F32)


def _dot_f32(a, b):
    return jnp.dot(a, b, preferred_element_type=F32, precision=lax.Precision.HIGHEST)


def _sigmoid(x):
    return 0.5 * (1.0 + jnp.tanh(0.5 * x))


def _resident(shape):
    nd = len(shape)
    return pl.BlockSpec(shape, lambda *_: (0,) * nd, pipeline_mode=pl.Buffered(1))


def _layer_resident(shape, layer):
    nd = len(shape)
    return pl.BlockSpec((1,) + tuple(shape), lambda *_: (layer,) + (0,) * nd, pipeline_mode=pl.Buffered(1))


def _params(n_axes):
    return pltpu.CompilerParams(dimension_semantics=("arbitrary",) * n_axes,
                                vmem_limit_bytes=VMEM_LIMIT)


def _mod_body(c_ref, w_ref, b_ref, o_ref):
    s = c_ref[...]
    s = s * jax.nn.sigmoid(s)
    o_ref[0] = _dot(s.astype(BF16), w_ref[0].astype(BF16)) + b_ref[0]


def _ada_mod(cc, w_ada, b_ada):
    depth, d, n = w_ada.shape
    tn = 1536
    return pl.pallas_call(
        _mod_body,
        grid=(depth, n // tn),
        in_specs=[pl.BlockSpec((MOD_ROWS, d), lambda l, j: (0, 0)),
                  pl.BlockSpec((1, d, tn), lambda l, j: (l, 0, j)),
                  pl.BlockSpec((1, 1, tn), lambda l, j: (l, 0, j))],
        out_specs=pl.BlockSpec((1, MOD_ROWS, tn), lambda l, j: (l, 0, j)),
        out_shape=jax.ShapeDtypeStruct((depth, MOD_ROWS, n), F32),
        compiler_params=_params(2),
        name="ada_mod",
    )(cc, w_ada, b_ada.reshape(depth, 1, n))


_MOD_SPECS = [pl.BlockSpec((1, N_MOD, D_MODEL), lambda b, t: (b, 0, 0)),
              pl.BlockSpec((1, N_MOD, D_MODEL), lambda b, t: (CTX_MOD_ROW, 0, 0))]


def _mod_rows(mod_lat_ref, mod_ctx_ref, tile):
    tok = pl.program_id(1) * tile + lax.broadcasted_iota(jnp.int32, (tile, 1), 0)
    is_ctx = tok >= SEQ
    return lambda k: jnp.where(is_ctx, mod_ctx_ref[0, k:k + 1, :], mod_lat_ref[0, k:k + 1, :])


def _rms(x, g):
    return x * lax.rsqrt(jnp.mean(x * x, axis=-1, keepdims=True) + EPS) * g


def _tail_rows(x_ref, c_ref, ctx_tail):
    head = x_ref.shape[1] - CTX_LEN
    return jnp.where(ctx_tail, c_ref[0], x_ref[0, head:, :])


def _inproj_body(x_ref, c_ref, ml_ref, mc_ref, g_ref, w_ref, wg_ref, dft_ref, p_ref, gate_ref, zc_ref, zs_ref,
                 lhs_scr, inv_scr):
    tm = x_ref.shape[1]
    head = tm - CTX_LEN
    ctx_tail = pl.program_id(1) == pl.num_programs(1) - 1
    shift_tail = jnp.where(ctx_tail, mc_ref[0, 0:1, :], ml_ref[0, 0:1, :])
    scale_tail = jnp.where(ctx_tail, mc_ref[0, 1:2, :], ml_ref[0, 1:2, :])
    lead = BF16_ROWS
    lhs_scr[0:lead, :] = jnp.concatenate([ml_ref[0, 0:1, :], shift_tail,
                                          jnp.zeros((lead - 2, D_MODEL), F32)], axis=0).astype(BF16)
    x_head = x_ref[0, 0:head, :]
    x_tail = _tail_rows(x_ref, c_ref, ctx_tail)
    lhs_scr[lead:lead + head, :] = (x_head * (g_ref[...] * (1.0 + ml_ref[0, 1:2, :]))).astype(BF16)
    lhs_scr[lead + head:, :] = (x_tail * (g_ref[...] * (1.0 + scale_tail))).astype(BF16)
    for rows, x in ((slice(0, head), x_head), (slice(head, tm), x_tail)):
        inv = lax.rsqrt(jnp.mean(x * x, axis=-1, keepdims=True) + EPS)
        inv_scr[rows, :] = jnp.broadcast_to(inv, (x.shape[0], LANES))

    def project(cols=None):
        w_t = wg_ref[0] if cols is None else w_ref[0, cols, :]
        n = w_t.shape[0] // LANES
        r = _dot_nt(lhs_scr[...], w_t)
        inv_head = jnp.concatenate([inv_scr[0:head, :]] * n, axis=1)
        inv_tail = jnp.concatenate([inv_scr[head:, :]] * n, axis=1)
        return jnp.concatenate([inv_head * r[lead:lead + head, :] + r[0:1, :],
                                inv_tail * r[lead + head:, :] + r[1:2, :]], axis=0)

    step = 2 * GROUP_WIDTH
    f = project(slice(P_WIDTH, OFF_G)).astype(BF16)
    p_ref[0, :, 0:step] = project(slice(0, step)).astype(BF16)
    z = _dot(f, dft_ref[...])
    zc_ref[...] = z[:, :GROUP_WIDTH].astype(BF16)
    zs_ref[...] = z[:, GROUP_WIDTH:].astype(BF16)
    for j in range(step, P_WIDTH - GROUP_WIDTH, step):
        p_ref[0, :, j:j + step] = project(slice(j, j + step)).astype(BF16)
    j = P_WIDTH - GROUP_WIDTH
    p_ref[0, :, j:] = project(slice(j, P_WIDTH)).astype(BF16)
    gate_ref[0] = project()


def _projection_weights(w_in):
    place = (jnp.array(GATE_I_COLS + GATE_F_COLS)[None, :] == jnp.arange(N_GATES)[:, None]).astype(F32)
    place = jnp.pad(place, ((0, 0), (0, LANES - N_GATES)))
    w_gate_t = jnp.einsum('ldg,gc->lcd', w_in[:, :, OFF_G:], place, precision=lax.Precision.HIGHEST)
    return jnp.swapaxes(w_in[:, :, :OFF_G], 1, 2).astype(BF16), w_gate_t.astype(BF16)


def _token_operands(tokens, ctx, tm):
    tile = pl.BlockSpec((1, tm, D_MODEL), lambda b, t: (b, t, 0))
    if ctx is None:
        return (tokens, tokens), [tile, pl.BlockSpec((1, CTX_LEN, D_MODEL), lambda b, t: (b, SEQ // CTX_LEN, 0))]
    return (tokens, ctx), [tile, pl.BlockSpec((1, CTX_LEN, D_MODEL), lambda b, t: (b, 0, 0))]


def _in_proj(tokens, ctx, mod_l, g, w_main, w_gate, layer, dft_c):
    bsz = tokens.shape[0]
    tm = PROJ_TILE
    operands, token_specs = _token_operands(tokens, ctx, tm)
    return pl.pallas_call(
        _inproj_body,
        grid=(bsz, T_ALL // tm),
        in_specs=token_specs + _MOD_SPECS
                 + [_resident((1, D_MODEL)),
                    _layer_resident(w_main.shape[1:], layer),
                    _layer_resident(w_gate.shape[1:], layer),
                    _resident((GROUP_WIDTH, 2 * GROUP_WIDTH))],
        out_specs=[pl.BlockSpec((1, tm, P_WIDTH), lambda b, t: (b, t, 0)),
                   pl.BlockSpec((1, tm, LANES), lambda b, t: (b, t, 0)),
                   pl.BlockSpec((tm, GROUP_WIDTH), lambda b, t: (t, b)),
                   pl.BlockSpec((tm, GROUP_WIDTH), lambda b, t: (t, b))],
        out_shape=[jax.ShapeDtypeStruct((bsz, T_ALL, P_WIDTH), BF16),
                   jax.ShapeDtypeStruct((bsz, T_ALL, LANES), F32),
                   jax.ShapeDtypeStruct((T_ALL, bsz * GROUP_WIDTH), BF16),
                   jax.ShapeDtypeStruct((T_ALL, bsz * GROUP_WIDTH), BF16)],
        scratch_shapes=[pltpu.VMEM((tm + BF16_ROWS, D_MODEL), BF16), pltpu.VMEM((tm, LANES), F32)],
        compiler_params=_params(2),
        name="in_proj",
    )(*operands, mod_l, mod_l, g.reshape(1, D_MODEL), w_main, w_gate, dft_c)


def _group_spec(col_block):
    return pl.BlockSpec((1, T_ALL, GROUP_WIDTH), lambda b: (b, 0, col_block))


def _softmax_pv(parts):
    m = functools.reduce(jnp.maximum, [jnp.max(s, axis=-1, keepdims=True) for s, _ in parts])
    es = [jnp.exp(s - m) for s, _ in parts]
    den = functools.reduce(jnp.add, [jnp.sum(e, axis=-1, keepdims=True) for e in es])
    num = functools.reduce(jnp.add, [_dot(e.astype(BF16), v) for e, (_, v) in zip(es, parts)])
    return num / den


def _attn_parts(q_ref, k_ref, v_ref, bias_ref, o_ref, ctx_out):
    scale = HEAD_DIM ** -0.5
    pairs = [slice(i * LANES, (i + 1) * LANES) for i in range(GROUP_WIDTH // LANES)]

    def stack_pair(q2):
        low = lax.broadcasted_iota(jnp.int32, q2.shape, 1) < HEAD_DIM
        zero = jnp.zeros_like(q2)
        return low, jnp.concatenate([jnp.where(low, q2, zero), jnp.where(low, zero, q2)], axis=0)

    def unstack_pair(low, o):
        n = o.shape[0] // 2
        return jnp.where(low, o[:n], o[n:]).astype(BF16)

    ctx = slice(SEQ, T_ALL)

    def context_rows():
        if ctx_out:
            ctx_tiles = []
            for ps in pairs:
                low, qm = stack_pair(q_ref[0, ctx, ps] * scale)
                ctx_tiles.append((low, _dot_nt(qm, k_ref[0, ctx, ps])))
            for ps, (low, s) in zip(pairs, ctx_tiles):
                o_ref[0, ctx, ps] = unstack_pair(low, _softmax_pv([(s, v_ref[0, ctx, ps])]))
        else:
            o_ref[0, ctx, :] = jnp.zeros((CTX_LEN, GROUP_WIDTH), BF16)

    rows_per_trip = 4
    tile_rows = 2 * GRID_W

    def trip(g, carry):
        tiles = [(dr, i) for dr in range(rows_per_trip) for i in range(len(pairs))]
        q0s = [pl.multiple_of((g * rows_per_trip + dr) * GRID_W, GRID_W) for dr in range(rows_per_trip)]
        stacked = {(dr, i): stack_pair(q_ref[0, pl.ds(q0s[dr], GRID_W), pairs[i]] * scale) for dr, i in tiles}
        s_ctx = [_dot_nt(jnp.concatenate([stacked[dr, i][1] for dr in range(rows_per_trip)], axis=0),
                         k_ref[0, ctx, pairs[i]]) for i in range(len(pairs))]

        def window_scores(dr, i):
            r = g * rows_per_trip + dr
            rs = jnp.clip(r - WIN_ROWS // 2, 0, GRID_ROWS - WIN_ROWS)
            k0 = pl.multiple_of(rs * GRID_W, GRID_W)
            s_loc = _dot_nt(stacked[dr, i][1], k_ref[0, pl.ds(k0, WIN_ROWS * GRID_W), pairs[i]])
            return s_loc + bias_ref[i, r - rs], k0

        partial = {}
        cur = window_scores(*tiles[0])
        for j, (dr, i) in enumerate(tiles):
            nxt = window_scores(*tiles[j + 1]) if j + 1 < len(tiles) else None
            s_loc, k0 = cur
            s_c = s_ctx[i][dr * tile_rows:(dr + 1) * tile_rows, :]
            m = jnp.maximum(jnp.max(s_loc, axis=-1, keepdims=True), jnp.max(s_c, axis=-1, keepdims=True))
            e_loc, e_ctx = jnp.exp(s_loc - m), jnp.exp(s_c - m)
            den = jnp.sum(e_loc, axis=-1, keepdims=True) + jnp.sum(e_ctx, axis=-1, keepdims=True)
            num = _dot(e_loc.astype(BF16), v_ref[0, pl.ds(k0, WIN_ROWS * GRID_W), pairs[i]])
            partial[dr, i] = (num, e_ctx.astype(BF16), den)
            cur = nxt

        for i, ps in enumerate(pairs):
            num_ctx = _dot(jnp.concatenate([partial[dr, i][1] for dr in range(rows_per_trip)], axis=0),
                           v_ref[0, ctx, ps])
            for dr in range(rows_per_trip):
                num, _, den = partial[dr, i]
                o = (num + num_ctx[dr * tile_rows:(dr + 1) * tile_rows, :]) / den
                o_ref[0, pl.ds(q0s[dr], GRID_W), ps] = unstack_pair(stacked[dr, i][0], o)
        return carry

    return context_rows, trip, GRID_ROWS // rows_per_trip


def _attention_bias(rpb):
    c = jnp.arange(GRID_W)
    qstart = jnp.clip(c - WIN_COLS // 2, 0, GRID_W - WIN_COLS)
    in_win = (c[None, :] >= qstart[:, None]) & (c[None, :] < qstart[:, None] + WIN_COLS)
    col = jnp.clip(c[None, :] - c[:, None] + WIN_COLS - 1, 0, 2 * WIN_COLS - 2)
    pick_col = (col[:, :, None] == jnp.arange(2 * WIN_COLS - 1)).astype(F32)
    by_drow = jnp.einsum('hde,qke->hdqk', rpb.astype(F32), pick_col, precision=lax.Precision.HIGHEST)
    return jnp.where(in_win, by_drow, NEG_INF)


def _expand_bias(by_drow_ref, bias_scr):
    for pair in range(GROUP_HEADS // 2):
        for v in range(WIN_ROWS):
            for hh in range(2):
                for j in range(0, WIN_ROWS, 2):
                    two_rows = [by_drow_ref[2 * pair + hh, jj - v + WIN_ROWS - 1] for jj in (j, j + 1)]
                    bias_scr[pair, v, hh * GRID_W:(hh + 1) * GRID_W, j * GRID_W:(j + 2) * GRID_W] = (
                        jnp.concatenate(two_rows, axis=1))


def _gmlp_chunk(n, u_ref, z_ref, ws_ref, bs_ref, g_ref, o_ref):
    rows = slice(n * CHUNK, (n + 1) * CHUNK)
    z = _rms(jax.nn.gelu(z_ref[0, rows, :].astype(F32)), g_ref[...]).astype(BF16)
    u = jax.nn.gelu(u_ref[0, rows, :].astype(F32))
    for h in range(GROUP_HEADS):
        hs = slice(h * HEAD_DIM, (h + 1) * HEAD_DIM)
        mixed = _dot(ws_ref[h], z[:, hs]) + bs_ref[:, hs]
        o_ref[0, rows, hs] = (u[:, hs] * mixed).astype(BF16)


def _mlstm_parts(q_ref, k_ref, v_ref, og_ref, gate_ref, wc_ref, bg_ref, cos_ref, sina_ref,
                 sinb_ref, gln_ref, tri_ref, o_ref,
                 qtz_scr, k_scr, kbd_scr, vat_scr, acol_scr, rows_scr, ht_scr, c_scr, m_scr):
    half = HEAD_DIM // 4
    n_chains = 2 * GROUP_HEADS

    def conv_act(ref, n, w0, w1, w2, rope, post):
        rows = pl.ds(pl.multiple_of(n * CHUNK, CHUNK), CHUNK)
        cur = ref[0, rows, :].astype(F32)
        first = (n == 0) | (n == LAT_CHUNKS)
        last = (n == LAT_CHUNKS - 1) | (n == N_CHUNKS - 1)
        before = pl.ds(pl.multiple_of(jnp.maximum(n * CHUNK - BF16_ROWS, 0), BF16_ROWS), BF16_ROWS)
        after = pl.ds(pl.multiple_of(jnp.minimum((n + 1) * CHUNK, T_ALL - BF16_ROWS), BF16_ROWS), BF16_ROWS)
        tail = jnp.where(first, 0.0, 1.0) * ref[0, before, :][BF16_ROWS - 1:BF16_ROWS, :].astype(F32)
        head = jnp.where(last, 0.0, 1.0) * ref[0, after, :][0:1, :].astype(F32)
        sub = lax.broadcasted_iota(jnp.int32, (CHUNK, 1), 0)
        x_prev = jnp.where(sub == 0, tail, pltpu.roll(cur, 1, 0))
        x_next = jnp.where(sub == CHUNK - 1, head, pltpu.roll(cur, CHUNK - 1, 0))
        y = w0 * x_prev + w1 * cur + w2 * x_next
        y = y * _sigmoid(y)
        if rope:
            y = (y * cos_ref[rows, :] + pltpu.roll(y, GROUP_WIDTH - half, 1) * sina_ref[rows, :]
                 + pltpu.roll(y, half, 1) * sinb_ref[rows, :])
        return y * post

    def prep(n, rope):
        rows = pl.ds(pl.multiple_of(n * CHUNK, CHUNK), CHUNK)
        gw = GROUP_WIDTH
        q = conv_act(q_ref, n, wc_ref[0:1, :gw], wc_ref[1:2, :gw], wc_ref[2:3, :gw], rope, 1.0)
        k = conv_act(k_ref, n, wc_ref[0:1, gw:], wc_ref[1:2, gw:], wc_ref[2:3, gw:], rope, HEAD_DIM ** -0.5)
        q_t = q.T.astype(BF16)
        v_t = v_ref[0, rows, :].astype(F32).T.astype(BF16)
        k = k.astype(BF16)
        k_scr[n] = k
        sub = lax.broadcasted_iota(jnp.int32, (HEAD_DIM, CHUNK), 0)
        ones_row = jnp.where(sub == 0, 1.0, 0.0).astype(BF16)
        zeros = jnp.zeros((HEAD_DIM, CHUNK), BF16)
        low = lax.broadcasted_iota(jnp.int32, (CHUNK, LANES), 1) < HEAD_DIM
        for pair in range(GROUP_HEADS // 2):
            h0 = slice(2 * pair * HEAD_DIM, (2 * pair + 1) * HEAD_DIM)
            h1 = slice((2 * pair + 1) * HEAD_DIM, (2 * pair + 2) * HEAD_DIM)
            qtz_scr[n, pair] = jnp.concatenate([jnp.concatenate([q_t[h0, :], zeros], axis=1),
                                                jnp.concatenate([zeros, q_t[h1, :]], axis=1)], axis=0)
            k_pair = k[:, pair * LANES:(pair + 1) * LANES]
            kbd_scr[n, pair] = jnp.concatenate([jnp.where(low, k_pair, jnp.zeros_like(k_pair)),
                                                jnp.where(low, jnp.zeros_like(k_pair), k_pair)], axis=0)
            vat_scr[n, pair] = jnp.concatenate([jnp.concatenate([v_t[h0, :], ones_row], axis=0),
                                                jnp.concatenate([v_t[h1, :], ones_row], axis=0)], axis=1)
        ht_scr[n] = jnp.zeros((GROUP_WIDTH, CHUNK), F32)

        g_i = gate_ref[0, rows, :] + bg_ref[...]
        lf = pltpu.roll(jax.nn.log_sigmoid(g_i), LANES - n_chains, 1)
        pre = _dot_f32(tri_ref[...], lf)
        suf = pre[CHUNK - 1:CHUNK, :] - pre + lf
        lane = lax.broadcasted_iota(jnp.int32, (CHUNK, LANES), 1)
        b = jnp.where(lane % 4 < 2, pre, suf)
        a = g_i - b
        acol_scr[n] = a
        a_t = a.T[0:n_chains, :]
        b_t = b.T[0:n_chains, :]
        chain = lax.broadcasted_iota(jnp.int32, (n_chains, CHUNK), 0)
        b_end = jnp.where(chain % 4 < 2, b_t[:, CHUNK - 1:CHUNK], b_t[:, 0:1])
        b_end = jnp.broadcast_to(b_end, (n_chains, CHUNK))
        a_max = jnp.broadcast_to(jnp.max(a_t, axis=1, keepdims=True), (n_chains, CHUNK))
        for kind, rows8 in enumerate((a_t, b_t, b_end, a_max)):
            rows_scr[n, kind] = jnp.concatenate([rows8[0:4, :], rows8[4:8, :]], axis=1)

    def reset_state():
        c_scr[...] = jnp.zeros(c_scr.shape, F32)
        m_scr[...] = jnp.zeros(m_scr.shape, F32)

    s_idx = lax.broadcasted_iota(jnp.int32, (CHUNK, CHUNK), 0)
    t_idx = lax.broadcasted_iota(jnp.int32, (CHUNK, CHUNK), 1)

    zeros = jnp.zeros((CHUNK, CHUNK), BF16)
    low = lax.broadcasted_iota(jnp.int32, (1, LANES), 1) < HEAD_DIM

    def scan(i, carry):
        chunk_of = (jnp.where(i < CTX_CHUNKS, LAT_CHUNKS + i, i - CTX_CHUNKS), N_CHUNKS - 1 - i)
        steps = [(rev, pair, chunk_of[rev]) for rev in range(2) for pair in range(GROUP_HEADS // 2)]

        scores = []
        for rev, pair, n in steps:
            dp = 2 * rev + pair
            causal = (s_idx >= t_idx) if rev else (s_idx <= t_idx)
            arg = jnp.concatenate([jnp.where(causal, acol_scr[n, :, dp:dp + 1], NEG_INF),
                                   jnp.where(causal, acol_scr[n, :, 4 + dp:5 + dp], NEG_INF)], axis=1)
            s = _dot(k_scr[n, :, pair * LANES:(pair + 1) * LANES], qtz_scr[n, pair])
            scores.append((arg, jnp.max(arg, axis=0, keepdims=True), s))

        inter = []
        for rev, pair, n in steps:
            dp = 2 * rev + pair
            a_row = rows_scr[n, 0, dp:dp + 1, :]
            b_end = rows_scr[n, 2, dp:dp + 1, :]
            a_max = rows_scr[n, 3, dp:dp + 1, :]
            m_mem = m_scr[dp:dp + 1, :]
            c_mem = c_scr[dp]
            inter.append((_dot(c_mem.astype(BF16), qtz_scr[n, pair]), m_mem))
            m_new = b_end + jnp.maximum(m_mem, a_max)
            w_src = jnp.exp(b_end + a_row - m_new)
            decay = jnp.exp(b_end + m_mem - m_new)
            decay = jnp.where(low, decay[:, :CHUNK], decay[:, CHUNK:])
            c_scr[dp] = decay * c_mem + _dot((vat_scr[n, pair].astype(F32) * w_src).astype(BF16),
                                             kbd_scr[n, pair])
            m_scr[dp:dp + 1, :] = m_new

        for (rev, pair, n), (arg, cm, s), (x2, m_mem) in zip(steps, scores, inter):
            dp = 2 * rev + pair
            g = (s * jnp.exp(arg - cm)).astype(BF16)
            g_bd = jnp.concatenate([jnp.concatenate([g[:, :CHUNK], zeros], axis=1),
                                    jnp.concatenate([zeros, g[:, CHUNK:]], axis=1)], axis=0)
            x1 = _dot(vat_scr[n, pair], g_bd)
            b_row = rows_scr[n, 1, dp:dp + 1, :]
            mu = jnp.maximum(cm, m_mem)
            both = x1 * jnp.exp(cm - mu) + x2 * jnp.exp(m_mem - mu)
            den = both[HEAD_DIM:HEAD_DIM + 1, :]
            inv = 1.0 / jnp.maximum(jnp.abs(den), jnp.exp(-(b_row + mu)))
            h_t = both[0:HEAD_DIM, :] * inv
            ht_scr[n, pair * LANES:(pair + 1) * LANES, :] += jnp.concatenate([h_t[:, :CHUNK], h_t[:, CHUNK:]],
                                                                             axis=0)
        return carry

    def finish(n, carry):
        rows = pl.ds(pl.multiple_of(n * CHUNK, CHUNK), CHUNK)
        gate = _sigmoid(og_ref[0, rows, :].astype(F32))
        ys = []
        for h in range(GROUP_HEADS):
            x = ht_scr[n, h * HEAD_DIM:(h + 1) * HEAD_DIM, :]
            mu = jnp.mean(x, axis=0, keepdims=True)
            var = jnp.mean(jnp.square(x - mu), axis=0, keepdims=True)
            ys.append((x - mu) * lax.rsqrt(var + EPS))
        y = jnp.concatenate(ys, axis=0).T * gln_ref[...]
        o_ref[0, rows, :] = (gate * y).astype(BF16)
        return carry

    return prep, reset_state, scan, finish


def _attn_mlstm_body(aq_ref, ak_ref, av_ref, by_drow_ref, mq_ref, mk_ref, mv_ref, og_ref, gate_ref, wc_ref,
                     bg_ref, cos_ref, sina_ref, sinb_ref, gln_ref, tri_ref, a_ref, m_ref, bias_scr, *scratch,
                     ctx_out):
    @pl.when(pl.program_id(0) == 0)
    def _():
        _expand_bias(by_drow_ref, bias_scr)

    context_rows, trip, n_trips = _attn_parts(aq_ref, ak_ref, av_ref, bias_scr, a_ref, ctx_out)
    prep, reset_state, scan, finish = _mlstm_parts(mq_ref, mk_ref, mv_ref, og_ref, gate_ref, wc_ref, bg_ref,
                                                   cos_ref, sina_ref, sinb_ref, gln_ref, tri_ref, m_ref, *scratch)
    per_trip = LAT_CHUNKS // n_trips

    def prep_context(n, carry):
        prep(n, False)
        return carry

    def trip_and_prep(g, carry):
        for j in range(per_trip):
            prep(g * per_trip + j, True)
        trip(g, carry)
        return carry

    lax.fori_loop(LAT_CHUNKS, N_CHUNKS, prep_context, 0)
    context_rows()
    lax.fori_loop(0, n_trips, trip_and_prep, 0)
    reset_state()
    lax.fori_loop(0, N_CHUNKS, scan, 0, unroll=6)
    lax.fori_loop(0, N_CHUNKS, finish, 0, unroll=3)


def _rope_tables():
    lane = jnp.arange(GROUP_WIDTH)
    m = HEAD_DIM // 4
    inv = ROPE_THETA ** (-(lane % m).astype(F32) / m)
    by_row = ((lane % HEAD_DIM) // (HEAD_DIM // 2) == 0)[None, None, :]
    low = ((lane % (2 * m)) < m)[None, None, :]
    ang = jnp.arange(GRID_W, dtype=F32)[:, None] * inv[None, :]
    cos, sin = jnp.cos(ang), jnp.sin(ang)

    def spread(tab):
        return jnp.where(by_row, tab[:GRID_ROWS, None, :], tab[None, :, :]).reshape(SEQ, GROUP_WIDTH)

    cos, sin = spread(cos), spread(sin)
    low = jnp.broadcast_to(low, (GRID_ROWS, GRID_W, GROUP_WIDTH)).reshape(SEQ, GROUP_WIDTH)
    return cos, jnp.where(low, -sin, 0.0), jnp.where(low, 0.0, sin)


def _attn_mlstm(p, by_drow, gates, w_conv, b_gate, g_mlstm, rope, ctx_out):
    bsz = p.shape[0]
    i = jnp.arange(CHUNK)
    tri = (i[:, None] >= i[None, :]).astype(F32)
    bg = jnp.pad(b_gate[jnp.array(GATE_I_COLS + GATE_F_COLS)], (0, LANES - N_GATES)).reshape(1, LANES)
    cos, sina, sinb = rope
    n_pairs = GROUP_HEADS // 2
    group_out = pl.BlockSpec((1, T_ALL, GROUP_WIDTH), lambda b: (b, 0, 0))
    return pl.pallas_call(
        functools.partial(_attn_mlstm_body, ctx_out=ctx_out),
        grid=(bsz,),
        in_specs=[_group_spec(0), _group_spec(1), _group_spec(2), _resident(by_drow.shape),
                  _group_spec(5), _group_spec(6), _group_spec(7), _group_spec(8),
                  pl.BlockSpec((1, T_ALL, LANES), lambda b: (b, 0, 0)),
                  _resident((CONV_W, 2 * GROUP_WIDTH)),
                  _resident((1, LANES)),
                  _resident((SEQ, GROUP_WIDTH)), _resident((SEQ, GROUP_WIDTH)), _resident((SEQ, GROUP_WIDTH)),
                  _resident((1, GROUP_WIDTH)),
                  _resident((CHUNK, CHUNK))],
        out_specs=[group_out, group_out],
        out_shape=[jax.ShapeDtypeStruct((bsz, T_ALL, GROUP_WIDTH), BF16)] * 2,
        scratch_shapes=[pltpu.VMEM((n_pairs, WIN_ROWS, 2 * GRID_W, WIN_ROWS * GRID_W), F32),
                        pltpu.VMEM((N_CHUNKS, n_pairs, LANES, 2 * CHUNK), BF16),
                        pltpu.VMEM((N_CHUNKS, CHUNK, GROUP_WIDTH), BF16),
                        pltpu.VMEM((N_CHUNKS, n_pairs, 2 * CHUNK, LANES), BF16),
                        pltpu.VMEM((N_CHUNKS, n_pairs, LANES, 2 * CHUNK), BF16),
                        pltpu.VMEM((N_CHUNKS, CHUNK, LANES), F32),
                        pltpu.VMEM((N_CHUNKS, 4, 2 * n_pairs, 2 * CHUNK), F32),
                        pltpu.VMEM((N_CHUNKS, GROUP_WIDTH, CHUNK), F32),
                        pltpu.VMEM((2 * n_pairs, LANES, LANES), F32),
                        pltpu.VMEM((2 * n_pairs, 2 * CHUNK), F32)],
        compiler_params=_params(1),
        name="attn_mlstm",
    )(p, p, p, by_drow, p, p, p, p, gates, w_conv, bg, cos, sina, sinb, g_mlstm.reshape(1, GROUP_WIDTH), tri)


def _dft_tables(n, scale):
    f = 1 << (n.bit_length() // 2)
    s = jnp.arange(n, dtype=jnp.int32)[None, :]
    ang_a = ((f * jnp.arange(n // f, dtype=jnp.int32)[:, None] * s) % n).astype(F32) * (2.0 * jnp.pi / n)
    ang_b = ((jnp.arange(f, dtype=jnp.int32)[:, None] * s) % n).astype(F32) * (2.0 * jnp.pi / n)
    ca, sa = jnp.cos(ang_a)[:, None, :], jnp.sin(ang_a)[:, None, :]
    cb, sb = jnp.cos(ang_b)[None, :, :], jnp.sin(ang_b)[None, :, :]
    cos = (ca * cb - sa * sb).reshape(n, n)
    sin = (sa * cb + ca * sb).reshape(n, n)
    return (cos * scale).astype(BF16), (-sin * scale).astype(BF16)


def _channel_dft():
    gc = GROUP_WIDTH // FNET_GROUPS
    j = jnp.arange(GROUP_WIDTH, dtype=jnp.int32)
    same = (j[:, None] // gc) == (j[None, :] // gc)
    ang = (((j[:, None] % gc) * (j[None, :] % gc)) % gc).astype(F32) * (2.0 * jnp.pi / gc)
    c = jnp.where(same, jnp.cos(ang), 0.0)
    s = jnp.where(same, jnp.sin(ang), 0.0)
    return jnp.concatenate([c, s], axis=1).astype(BF16)


def _fnet_gmlp_body(zc_ref, zs_ref, cl_ref, sl_ref, cc_ref, sc_ref, w_ref, u_ref, z_ref, ws_ref, bs_ref, g_ref,
                    o_ref, gm_ref):
    gmlp = (u_ref, z_ref, ws_ref, bs_ref, g_ref, gm_ref)
    half = N_CHUNKS // 2
    y = _dot(cl_ref[...], zc_ref[0:SEQ, :])
    for n in range(half):
        _gmlp_chunk(n, *gmlp)
    y = y + _dot(sl_ref[...], zs_ref[0:SEQ, :])
    for n in range(half, N_CHUNKS):
        _gmlp_chunk(n, *gmlp)
    o_ref[0:SEQ, :] = _dot(y.astype(BF16), w_ref[...]).astype(BF16)
    y = _dot(cc_ref[...], zc_ref[SEQ:, :]) + _dot(sc_ref[...], zs_ref[SEQ:, :])
    o_ref[SEQ:, :] = _dot(y.astype(BF16), w_ref[...]).astype(BF16)


def _fnet_gmlp(zc, zs, tabs, w_fnet, p, w_spatial, b_spatial, g_gmlp):
    bsz = p.shape[0]
    cl, sl, cc, sc = tabs
    col = pl.BlockSpec((T_ALL, GROUP_WIDTH), lambda b: (0, b))
    bias = jnp.repeat(b_spatial.T, HEAD_DIM, axis=1)
    return pl.pallas_call(
        _fnet_gmlp_body,
        grid=(bsz,),
        in_specs=[col, col, _resident((SEQ, SEQ)), _resident((SEQ, SEQ)),
                  _resident((CTX_LEN, CTX_LEN)), _resident((CTX_LEN, CTX_LEN)),
                  _resident((GROUP_WIDTH, GROUP_WIDTH)),
                  _group_spec(3), _group_spec(4),
                  _resident((GROUP_HEADS, CHUNK, CHUNK)),
                  _resident((CHUNK, GROUP_WIDTH)),
                  _resident((1, GROUP_WIDTH))],
        out_specs=[col, pl.BlockSpec((1, T_ALL, GROUP_WIDTH), lambda b: (b, 0, 0))],
        out_shape=[jax.ShapeDtypeStruct((T_ALL, bsz * GROUP_WIDTH), BF16),
                   jax.ShapeDtypeStruct((bsz, T_ALL, GROUP_WIDTH), BF16)],
        compiler_params=_params(1),
        name="fnet_gmlp",
    )(zc, zs, cl, sl, cc, sc, w_fnet.astype(BF16), p, p, w_spatial.astype(BF16), bias,
      g_gmlp.reshape(1, GROUP_WIDTH))


def _post_body(a_ref, b_ref, c_ref, d_ref, x_ref, cx_ref, ml_ref, mc_ref, g_ref, wo_ref, w1_ref, w2_ref, gf_ref,
               o_ref, h_scr, acc_scr, *, final):
    gw = GROUP_WIDTH
    tm = x_ref.shape[1]
    mod = _mod_rows(ml_ref, mc_ref, tm)
    y = (_dot(a_ref[0], wo_ref[0, 0:gw, :]) + _dot(b_ref[0], wo_ref[0, gw:2 * gw, :])
         + _dot(c_ref[0], wo_ref[0, 2 * gw:3 * gw, :]) + _dot(d_ref[...], wo_ref[0, 3 * gw:, :]))
    if final:
        x = x_ref[0]
    else:
        ctx_tail = pl.program_id(1) == pl.num_programs(1) - 1
        x = jnp.concatenate([x_ref[0, 0:tm - CTX_LEN, :], _tail_rows(x_ref, cx_ref, ctx_tail)], axis=0)
    x1 = x + mod(2) * y
    h = _rms(x1, g_ref[...]) * (1.0 + mod(4)) + mod(3)
    h_scr[...] = h.astype(BF16)
    step = 512

    def up(j):
        return jnp.maximum(_dot(h_scr[...], w1_ref[0, :, j:j + step]), 0.0)

    a = up(0)
    for j in range(0, D_FF, step):
        nxt = up(j + step) if j + step < D_FF else None
        part = _dot((a * a).astype(BF16), w2_ref[0, j:j + step, :])
        if j == 0:
            acc_scr[...] = part
        else:
            acc_scr[...] += part
        a = nxt
    x2 = x1 + mod(5) * acc_scr[...]
    if final:
        x2 = _rms(x2, gf_ref[...])
    o_ref[0] = x2


def _post(a, b_, c_, d, tokens, ctx, mod_l, g_ffn, w_out, w_ff1, w_ff2, layer, g_final, final):
    bsz = tokens.shape[0]
    tm, out_len = (LATENT_TILE, SEQ) if final else (TOKEN_TILE, T_ALL)
    tok = pl.BlockSpec((1, tm, D_MODEL), lambda b, t: (b, t, 0))
    grp = pl.BlockSpec((1, tm, GROUP_WIDTH), lambda b, t: (b, t, 0))
    operands, token_specs = _token_operands(tokens, ctx, tm)
    return pl.pallas_call(
        functools.partial(_post_body, final=final),
        grid=(bsz, out_len // tm),
        in_specs=[grp, grp, grp, pl.BlockSpec((tm, GROUP_WIDTH), lambda b, t: (t, b))] + token_specs + _MOD_SPECS
                 + [_resident((1, D_MODEL)),
                    _layer_resident((D_MODEL, D_MODEL), layer),
                    _layer_resident((D_MODEL, D_FF), layer),
                    _layer_resident((D_FF, D_MODEL), layer),
                    _resident((1, D_MODEL))],
        out_specs=tok,
        out_shape=jax.ShapeDtypeStruct((bsz, out_len, D_MODEL), F32),
        scratch_shapes=[pltpu.VMEM((tm, D_MODEL), BF16), pltpu.VMEM((tm, D_MODEL), F32)],
        compiler_params=_params(2),
        name="out_proj_mlp",
    )(a, b_, c_, d, *operands, mod_l, mod_l, g_ffn.reshape(1, D_MODEL), w_out, w_ff1, w_ff2,
      g_final.reshape(1, D_MODEL))


def kernel(x, c, ctx, c_ctx, w_ada, b_ada, g_norm_mix, g_norm_ffn, w_in, b_gate, w_conv_qk, rpb, w_spatial,
           b_spatial, g_gmlp, g_mlstm, w_fnet, w_out, w_ff1, w_ff2, g_final):
    bsz = x.shape[0]
    assert bsz <= CTX_MOD_ROW and x.shape[1:] == (SEQ, D_MODEL) and ctx.shape[1:] == (CTX_LEN, D_MODEL)
    depth = w_ada.shape[0]
    tokens, ctx_rows = x, ctx
    cc = jnp.zeros((MOD_ROWS, D_MODEL), F32).at[:bsz].set(c).at[CTX_MOD_ROW].set(c_ctx)
    mod = _ada_mod(cc, w_ada, b_ada).reshape(depth, MOD_ROWS, N_MOD, D_MODEL)

    rope = _rope_tables()
    dft_c = _channel_dft()
    tabs = (_dft_tables(SEQ, (SEQ * GROUP_WIDTH // FNET_GROUPS) ** -0.5)
            + _dft_tables(CTX_LEN, (CTX_LEN * GROUP_WIDTH // FNET_GROUPS) ** -0.5))

    w_main, w_gate = _projection_weights(w_in)
    w_out, w_ff1, w_ff2 = w_out.astype(BF16), w_ff1.astype(BF16), w_ff2.astype(BF16)
    for l in range(depth):
        p, gates, zc, zs = _in_proj(tokens, ctx_rows, mod[l], g_norm_mix[l], w_main, w_gate, l, dft_c)
        a, c_ = _attn_mlstm(p, _attention_bias(rpb[l]), gates, w_conv_qk[l], b_gate[l], g_mlstm[l], rope,
                            ctx_out=(l < depth - 1))
        d, b_ = _fnet_gmlp(zc, zs, tabs, w_fnet[l], p, w_spatial[l], b_spatial[l], g_gmlp[l])
        tokens = _post(a, b_, c_, d, tokens, ctx_rows, mod[l], g_norm_ffn[l], w_out, w_ff1, w_ff2, l,
                       g_final, final=(l == depth - 1))
        ctx_rows = None
    return tokens
```

```python
import functools

import jax
import jax.numpy as jnp
from jax import lax
from jax.experimental import pallas as pl
from jax.experimental.pallas import tpu as pltpu

D_MODEL = 1024
SEQ = 2048
GRID_W = 64
GRID_ROWS = SEQ // GRID_W
CTX_LEN = 256
T_ALL = CTX_LEN + SEQ
HEAD_DIM = 64
GROUP_WIDTH = 256
GROUP_HEADS = 4
WIN_ROWS = 8
WIN_COLS = 16
CHUNK = 128
N_CHUNKS = T_ALL // CHUNK
CTX_CHUNKS = CTX_LEN // CHUNK
LAT_CHUNKS = SEQ // CHUNK
CONV_W = 3
FNET_GROUPS = 4
ROPE_THETA = 10000.0
D_FF = 4 * D_MODEL
N_MOD = 6
EPS = 1e-6
NEG_INF = -1e30
N_GATES = 4 * GROUP_HEADS
OFF_G = 10 * GROUP_WIDTH
MOD_ROWS = 24
CTX_MOD_ROW = 16
LANES = 128
BF16_ROWS = 16
GATE_I_COLS = (0, 2, 8, 10, 1, 3, 9, 11)
GATE_F_COLS = (4, 6, 12, 14, 5, 7, 13, 15)
TOKEN_TILE = 768
PROJ_TILE = 1152
LATENT_TILE = 512
P_WIDTH = 9 * GROUP_WIDTH
VMEM_LIMIT = 56 * 1024 * 1024

F32 = jnp.float32
BF16 = jnp.bfloat16


def _dot(a, b):
    return jnp.dot(a, b, preferred_element_type=F32)


def _dot_nt(a, b):
    return lax.dot_general(a, b, (((1,), (1,)), ((), ())), preferred_element_type=F32)


def _dot_f32(a, b):
    return jnp.dot(a, b, preferred_element_type=F32, precision=lax.Precision.HIGHEST)


def _sigmoid(x):
    return 0.5 * (1.0 + jnp.tanh(0.5 * x))


def _resident(shape):
    nd = len(shape)
    return pl.BlockSpec(shape, lambda *_: (0,) * nd, pipeline_mode=pl.Buffered(1))


def _layer_resident(shape, layer):
    nd = len(shape)
    return pl.BlockSpec((1,) + tuple(shape), lambda *_: (layer,) + (0,) * nd, pipeline_mode=pl.Buffered(1))


def _params(n_axes):
    return pltpu.CompilerParams(dimension_semantics=("arbitrary",) * n_axes,
                                vmem_limit_bytes=VMEM_LIMIT)


def _mod_body(c_ref, w_ref, b_ref, o_ref):
    s = c_ref[...]
    s = s * jax.nn.sigmoid(s)
    o_ref[0] = _dot(s.astype(BF16), w_ref[0].astype(BF16)) + b_ref[0]


def _ada_mod(cc, w_ada, b_ada):
    depth, d, n = w_ada.shape
    tn = 1536
    return pl.pallas_call(
        _mod_body,
        grid=(depth, n // tn),
        in_specs=[pl.BlockSpec((MOD_ROWS, d), lambda l, j: (0, 0)),
                  pl.BlockSpec((1, d, tn), lambda l, j: (l, 0, j)),
                  pl.BlockSpec((1, 1, tn), lambda l, j: (l, 0, j))],
        out_specs=pl.BlockSpec((1, MOD_ROWS, tn), lambda l, j: (l, 0, j)),
        out_shape=jax.ShapeDtypeStruct((depth, MOD_ROWS, n), F32),
        compiler_params=_params(2),
        name="ada_mod",
    )(cc, w_ada, b_ada.reshape(depth, 1, n))


_MOD_SPECS = [pl.BlockSpec((1, N_MOD, D_MODEL), lambda b, t: (b, 0, 0)),
              pl.BlockSpec((1, N_MOD, D_MODEL), lambda b, t: (CTX_MOD_ROW, 0, 0))]


def _mod_rows(mod_lat_ref, mod_ctx_ref, tile):
    tok = pl.program_id(1) * tile + lax.broadcasted_iota(jnp.int32, (tile, 1), 0)
    is_ctx = tok >= SEQ
    return lambda k: jnp.where(is_ctx, mod_ctx_ref[0, k:k + 1, :], mod_lat_ref[0, k:k + 1, :])


def _rms(x, g):
    return x * lax.rsqrt(jnp.mean(x * x, axis=-1, keepdims=True) + EPS) * g


def _tail_rows(x_ref, c_ref, ctx_tail):
    head = x_ref.shape[1] - CTX_LEN
    return jnp.where(ctx_tail, c_ref[0], x_ref[0, head:, :])


def _inproj_body(x_ref, c_ref, ml_ref, mc_ref, g_ref, w_ref, wg_ref, dft_ref, p_ref, gate_ref, zc_ref, zs_ref,
                 lhs_scr, inv_scr):
    tm = x_ref.shape[1]
    head = tm - CTX_LEN
    ctx_tail = pl.program_id(1) == pl.num_programs(1) - 1
    shift_tail = jnp.where(ctx_tail, mc_ref[0, 0:1, :], ml_ref[0, 0:1, :])
    scale_tail = jnp.where(ctx_tail, mc_ref[0, 1:2, :], ml_ref[0, 1:2, :])
    lead = BF16_ROWS
    lhs_scr[0:lead, :] = jnp.concatenate([ml_ref[0, 0:1, :], shift_tail,
                                          jnp.zeros((lead - 2, D_MODEL), F32)], axis=0).astype(BF16)
    x_head = x_ref[0, 0:head, :]
    x_tail = _tail_rows(x_ref, c_ref, ctx_tail)
    lhs_scr[lead:lead + head, :] = (x_head * (g_ref[...] * (1.0 + ml_ref[0, 1:2, :]))).astype(BF16)
    lhs_scr[lead + head:, :] = (x_tail * (g_ref[...] * (1.0 + scale_tail))).astype(BF16)
    for rows, x in ((slice(0, head), x_head), (slice(head, tm), x_tail)):
        inv = lax.rsqrt(jnp.mean(x * x, axis=-1, keepdims=True) + EPS)
        inv_scr[rows, :] = jnp.broadcast_to(inv, (x.shape[0], LANES))

    def project(cols=None):
        w_t = wg_ref[0] if cols is None else w_ref[0, cols, :]
        n = w_t.shape[0] // LANES
        r = _dot_nt(lhs_scr[...], w_t)
        inv_head = jnp.concatenate([inv_scr[0:head, :]] * n, axis=1)
        inv_tail = jnp.concatenate([inv_scr[head:, :]] * n, axis=1)
        return jnp.concatenate([inv_head * r[lead:lead + head, :] + r[0:1, :],
                                inv_tail * r[lead + head:, :] + r[1:2, :]], axis=0)

    step = 2 * GROUP_WIDTH
    f = project(slice(P_WIDTH, OFF_G)).astype(BF16)
    p_ref[0, :, 0:step] = project(slice(0, step)).astype(BF16)
    z = _dot(f, dft_ref[...])
    zc_ref[...] = z[:, :GROUP_WIDTH].astype(BF16)
    zs_ref[...] = z[:, GROUP_WIDTH:].astype(BF16)
    for j in range(step, P_WIDTH - GROUP_WIDTH, step):
        p_ref[0, :, j:j + step] = project(slice(j, j + step)).astype(BF16)
    j = P_WIDTH - GROUP_WIDTH
    p_ref[0, :, j:] = project(slice(j, P_WIDTH)).astype(BF16)
    gate_ref[0] = project()


def _projection_weights(w_in):
    place = (jnp.array(GATE_I_COLS + GATE_F_COLS)[None, :] == jnp.arange(N_GATES)[:, None]).astype(F32)
    place = jnp.pad(place, ((0, 0), (0, LANES - N_GATES)))
    w_gate_t = jnp.einsum('ldg,gc->lcd', w_in[:, :, OFF_G:], place, precision=lax.Precision.HIGHEST)
    return jnp.swapaxes(w_in[:, :, :OFF_G], 1, 2).astype(BF16), w_gate_t.astype(BF16)


def _token_operands(tokens, ctx, tm):
    tile = pl.BlockSpec((1, tm, D_MODEL), lambda b, t: (b, t, 0))
    if ctx is None:
        return (tokens, tokens), [tile, pl.BlockSpec((1, CTX_LEN, D_MODEL), lambda b, t: (b, SEQ // CTX_LEN, 0))]
    return (tokens, ctx), [tile, pl.BlockSpec((1, CTX_LEN, D_MODEL), lambda b, t: (b, 0, 0))]


def _in_proj(tokens, ctx, mod_l, g, w_main, w_gate, layer, dft_c):
    bsz = tokens.shape[0]
    tm = PROJ_TILE
    operands, token_specs = _token_operands(tokens, ctx, tm)
    return pl.pallas_call(
        _inproj_body,
        grid=(bsz, T_ALL // tm),
        in_specs=token_specs + _MOD_SPECS
                 + [_resident((1, D_MODEL)),
                    _layer_resident(w_main.shape[1:], layer),
                    _layer_resident(w_gate.shape[1:], layer),
                    _resident((GROUP_WIDTH, 2 * GROUP_WIDTH))],
        out_specs=[pl.BlockSpec((1, tm, P_WIDTH), lambda b, t: (b, t, 0)),
                   pl.BlockSpec((1, tm, LANES), lambda b, t: (b, t, 0)),
                   pl.BlockSpec((tm, GROUP_WIDTH), lambda b, t: (t, b)),
                   pl.BlockSpec((tm, GROUP_WIDTH), lambda b, t: (t, b))],
        out_shape=[jax.ShapeDtypeStruct((bsz, T_ALL, P_WIDTH), BF16),
                   jax.ShapeDtypeStruct((bsz, T_ALL, LANES), F32),
                   jax.ShapeDtypeStruct((T_ALL, bsz * GROUP_WIDTH), BF16),
                   jax.ShapeDtypeStruct((T_ALL, bsz * GROUP_WIDTH), BF16)],
        scratch_shapes=[pltpu.VMEM((tm + BF16_ROWS, D_MODEL), BF16), pltpu.VMEM((tm, LANES), F32)],
        compiler_params=_params(2),
        name="in_proj",
    )(*operands, mod_l, mod_l, g.reshape(1, D_MODEL), w_main, w_gate, dft_c)


def _group_spec(col_block):
    return pl.BlockSpec((1, T_ALL, GROUP_WIDTH), lambda b: (b, 0, col_block))


def _softmax_pv(parts):
    m = functools.reduce(jnp.maximum, [jnp.max(s, axis=-1, keepdims=True) for s, _ in parts])
    es = [jnp.exp(s - m) for s, _ in parts]
    den = functools.reduce(jnp.add, [jnp.sum(e, axis=-1, keepdims=True) for e in es])
    num = functools.reduce(jnp.add, [_dot(e.astype(BF16), v) for e, (_, v) in zip(es, parts)])
    return num / den


def _attn_parts(q_ref, k_ref, v_ref, bias_ref, o_ref, ctx_out):
    scale = HEAD_DIM ** -0.5
    pairs = [slice(i * LANES, (i + 1) * LANES) for i in range(GROUP_WIDTH // LANES)]

    def stack_pair(q2):
        low = lax.broadcasted_iota(jnp.int32, q2.shape, 1) < HEAD_DIM
        zero = jnp.zeros_like(q2)
        return low, jnp.concatenate([jnp.where(low, q2, zero), jnp.where(low, zero, q2)], axis=0)

    def unstack_pair(low, o):
        n = o.shape[0] // 2
        return jnp.where(low, o[:n], o[n:]).astype(BF16)

    ctx = slice(SEQ, T_ALL)

    def context_rows():
        if ctx_out:
            ctx_tiles = []
            for ps in pairs:
                low, qm = stack_pair(q_ref[0, ctx, ps] * scale)
                ctx_tiles.append((low, _dot_nt(qm, k_ref[0, ctx, ps])))
            for ps, (low, s) in zip(pairs, ctx_tiles):
                o_ref[0, ctx, ps] = unstack_pair(low, _softmax_pv([(s, v_ref[0, ctx, ps])]))
        else:
            o_ref[0, ctx, :] = jnp.zeros((CTX_LEN, GROUP_WIDTH), BF16)

    rows_per_trip = 4
    tile_rows = 2 * GRID_W

    def trip(g, carry):
        tiles = [(dr, i) for dr in range(rows_per_trip) for i in range(len(pairs))]
        q0s = [pl.multiple_of((g * rows_per_trip + dr) * GRID_W, GRID_W) for dr in range(rows_per_trip)]
        stacked = {(dr, i): stack_pair(q_ref[0, pl.ds(q0s[dr], GRID_W), pairs[i]] * scale) for dr, i in tiles}
        s_ctx = [_dot_nt(jnp.concatenate([stacked[dr, i][1] for dr in range(rows_per_trip)], axis=0),
                         k_ref[0, ctx, pairs[i]]) for i in range(len(pairs))]

        def window_scores(dr, i):
            r = g * rows_per_trip + dr
            rs = jnp.clip(r - WIN_ROWS // 2, 0, GRID_ROWS - WIN_ROWS)
            k0 = pl.multiple_of(rs * GRID_W, GRID_W)
            s_loc = _dot_nt(stacked[dr, i][1], k_ref[0, pl.ds(k0, WIN_ROWS * GRID_W), pairs[i]])
            return s_loc + bias_ref[i, r - rs], k0

        partial = {}
        cur = window_scores(*tiles[0])
        for j, (dr, i) in enumerate(tiles):
            nxt = window_scores(*tiles[j + 1]) if j + 1 < len(tiles) else None
            s_loc, k0 = cur
            s_c = s_ctx[i][dr * tile_rows:(dr + 1) * tile_rows, :]
            m = jnp.maximum(jnp.max(s_loc, axis=-1, keepdims=True), jnp.max(s_c, axis=-1, keepdims=True))
            e_loc, e_ctx = jnp.exp(s_loc - m), jnp.exp(s_c - m)
            den = jnp.sum(e_loc, axis=-1, keepdims=True) + jnp.sum(e_ctx, axis=-1, keepdims=True)
            num = _dot(e_loc.astype(BF16), v_ref[0, pl.ds(k0, WIN_ROWS * GRID_W), pairs[i]])
            partial[dr, i] = (num, e_ctx.astype(BF16), den)
            cur = nxt

        for i, ps in enumerate(pairs):
            num_ctx = _dot(jnp.concatenate([partial[dr, i][1] for dr in range(rows_per_trip)], axis=0),
                           v_ref[0, ctx, ps])
            for dr in range(rows_per_trip):
                num, _, den = partial[dr, i]
                o = (num + num_ctx[dr * tile_rows:(dr + 1) * tile_rows, :]) / den
                o_ref[0, pl.ds(q0s[dr], GRID_W), ps] = unstack_pair(stacked[dr, i][0], o)
        return carry

    return context_rows, trip, GRID_ROWS // rows_per_trip


def _attention_bias(rpb):
    c = jnp.arange(GRID_W)
    qstart = jnp.clip(c - WIN_COLS // 2, 0, GRID_W - WIN_COLS)
    in_win = (c[None, :] >= qstart[:, None]) & (c[None, :] < qstart[:, None] + WIN_COLS)
    col = jnp.clip(c[None, :] - c[:, None] + WIN_COLS - 1, 0, 2 * WIN_COLS - 2)
    pick_col = (col[:, :, None] == jnp.arange(2 * WIN_COLS - 1)).astype(F32)
    by_drow = jnp.einsum('hde,qke->hdqk', rpb.astype(F32), pick_col, precision=lax.Precision.HIGHEST)
    return jnp.where(in_win, by_drow, NEG_INF)


def _expand_bias(by_drow_ref, bias_scr):
    for pair in range(GROUP_HEADS // 2):
        for v in range(WIN_ROWS):
            for hh in range(2):
                for j in range(0, WIN_ROWS, 2):
                    two_rows = [by_drow_ref[2 * pair + hh, jj - v + WIN_ROWS - 1] for jj in (j, j + 1)]
                    bias_scr[pair, v, hh * GRID_W:(hh + 1) * GRID_W, j * GRID_W:(j + 2) * GRID_W] = (
                        jnp.concatenate(two_rows, axis=1))


def _gmlp_chunk(n, u_ref, z_ref, ws_ref, bs_ref, g_ref, o_ref):
    rows = slice(n * CHUNK, (n + 1) * CHUNK)
    z = _rms(jax.nn.gelu(z_ref[0, rows, :].astype(F32)), g_ref[...]).astype(BF16)
    u = jax.nn.gelu(u_ref[0, rows, :].astype(F32))
    for h in range(GROUP_HEADS):
        hs = slice(h * HEAD_DIM, (h + 1) * HEAD_DIM)
        mixed = _dot(ws_ref[h], z[:, hs]) + bs_ref[:, hs]
        o_ref[0, rows, hs] = (u[:, hs] * mixed).astype(BF16)


def _mlstm_parts(q_ref, k_ref, v_ref, og_ref, gate_ref, wc_ref, bg_ref, cos_ref, sina_ref,
                 sinb_ref, gln_ref, tri_ref, o_ref,
                 qtz_scr, k_scr, kbd_scr, vat_scr, acol_scr, rows_scr, ht_scr, c_scr, m_scr):
    half = HEAD_DIM // 4
    n_chains = 2 * GROUP_HEADS

    def conv_act(ref, n, w0, w1, w2, rope, post):
        rows = pl.ds(pl.multiple_of(n * CHUNK, CHUNK), CHUNK)
        cur = ref[0, rows, :].astype(F32)
        first = (n == 0) | (n == LAT_CHUNKS)
        last = (n == LAT_CHUNKS - 1) | (n == N_CHUNKS - 1)
        before = pl.ds(pl.multiple_of(jnp.maximum(n * CHUNK - BF16_ROWS, 0), BF16_ROWS), BF16_ROWS)
        after = pl.ds(pl.multiple_of(jnp.minimum((n + 1) * CHUNK, T_ALL - BF16_ROWS), BF16_ROWS), BF16_ROWS)
        tail = jnp.where(first, 0.0, 1.0) * ref[0, before, :][BF16_ROWS - 1:BF16_ROWS, :].astype(F32)
        head = jnp.where(last, 0.0, 1.0) * ref[0, after, :][0:1, :].astype(F32)
        sub = lax.broadcasted_iota(jnp.int32, (CHUNK, 1), 0)
        x_prev = jnp.where(sub == 0, tail, pltpu.roll(cur, 1, 0))
        x_next = jnp.where(sub == CHUNK - 1, head, pltpu.roll(cur, CHUNK - 1, 0))
        y = w0 * x_prev + w1 * cur + w2 * x_next
        y = y * _sigmoid(y)
        if rope:
            y = (y * cos_ref[rows, :] + pltpu.roll(y, GROUP_WIDTH - half, 1) * sina_ref[rows, :]
                 + pltpu.roll(y, half, 1) * sinb_ref[rows, :])
        return y * post

    def prep(n, rope):
        rows = pl.ds(pl.multiple_of(n * CHUNK, CHUNK), CHUNK)
        gw = GROUP_WIDTH
        q = conv_act(q_ref, n, wc_ref[0:1, :gw], wc_ref[1:2, :gw], wc_ref[2:3, :gw], rope, 1.0)
        k = conv_act(k_ref, n, wc_ref[0:1, gw:], wc_ref[1:2, gw:], wc_ref[2:3, gw:], rope, HEAD_DIM ** -0.5)
        q_t = q.T.astype(BF16)
        v_t = v_ref[0, rows, :].astype(F32).T.astype(BF16)
        k = k.astype(BF16)
        k_scr[n] = k
        sub = lax.broadcasted_iota(jnp.int32, (HEAD_DIM, CHUNK), 0)
        ones_row = jnp.where(sub == 0, 1.0, 0.0).astype(BF16)
        zeros = jnp.zeros((HEAD_DIM, CHUNK), BF16)
        low = lax.broadcasted_iota(jnp.int32, (CHUNK, LANES), 1) < HEAD_DIM
        for pair in range(GROUP_HEADS // 2):
            h0 = slice(2 * pair * HEAD_DIM, (2 * pair + 1) * HEAD_DIM)
            h1 = slice((2 * pair + 1) * HEAD_DIM, (2 * pair + 2) * HEAD_DIM)
            qtz_scr[n, pair] = jnp.concatenate([jnp.concatenate([q_t[h0, :], zeros], axis=1),
                                                jnp.concatenate([zeros, q_t[h1, :]], axis=1)], axis=0)
            k_pair = k[:, pair * LANES:(pair + 1) * LANES]
            kbd_scr[n, pair] = jnp.concatenate([jnp.where(low, k_pair, jnp.zeros_like(k_pair)),
                                                jnp.where(low, jnp.zeros_like(k_pair), k_pair)], axis=0)
            vat_scr[n, pair] = jnp.concatenate([jnp.concatenate([v_t[h0, :], ones_row], axis=0),
                                                jnp.concatenate([v_t[h1, :], ones_row], axis=0)], axis=1)
        ht_scr[n] = jnp.zeros((GROUP_WIDTH, CHUNK), F32)

        g_i = gate_ref[0, rows, :] + bg_ref[...]
        lf = pltpu.roll(jax.nn.log_sigmoid(g_i), LANES - n_chains, 1)
        pre = _dot_f32(tri_ref[...], lf)
        suf = pre[CHUNK - 1:CHUNK, :] - pre + lf
        lane = lax.broadcasted_iota(jnp.int32, (CHUNK, LANES), 1)
        b = jnp.where(lane % 4 < 2, pre, suf)
        a = g_i - b
        acol_scr[n] = a
        a_t = a.T[0:n_chains, :]
        b_t = b.T[0:n_chains, :]
        chain = lax.broadcasted_iota(jnp.int32, (n_chains, CHUNK), 0)
        b_end = jnp.where(chain % 4 < 2, b_t[:, CHUNK - 1:CHUNK], b_t[:, 0:1])
        b_end = jnp.broadcast_to(b_end, (n_chains, CHUNK))
        a_max = jnp.broadcast_to(jnp.max(a_t, axis=1, keepdims=True), (n_chains, CHUNK))
        for kind, rows8 in enumerate((a_t, b_t, b_end, a_max)):
            rows_scr[n, kind] = jnp.concatenate([rows8[0:4, :], rows8[4:8, :]], axis=1)

    def reset_state():
        c_scr[...] = jnp.zeros(c_scr.shape, F32)
        m_scr[...] = jnp.zeros(m_scr.shape, F32)

    s_idx = lax.broadcasted_iota(jnp.int32, (CHUNK, CHUNK), 0)
    t_idx = lax.broadcasted_iota(jnp.int32, (CHUNK, CHUNK), 1)

    zeros = jnp.zeros((CHUNK, CHUNK), BF16)
    low = lax.broadcasted_iota(jnp.int32, (1, LANES), 1) < HEAD_DIM

    def scan(i, carry):
        chunk_of = (jnp.where(i < CTX_CHUNKS, LAT_CHUNKS + i, i - CTX_CHUNKS), N_CHUNKS - 1 - i)
        steps = [(rev, pair, chunk_of[rev]) for rev in range(2) for pair in range(GROUP_HEADS // 2)]

        scores = []
        for rev, pair, n in steps:
            dp = 2 * rev + pair
            causal = (s_idx >= t_idx) if rev else (s_idx <= t_idx)
            arg = jnp.concatenate([jnp.where(causal, acol_scr[n, :, dp:dp + 1], NEG_INF),
                                   jnp.where(causal, acol_scr[n, :, 4 + dp:5 + dp], NEG_INF)], axis=1)
            s = _dot(k_scr[n, :, pair * LANES:(pair + 1) * LANES], qtz_scr[n, pair])
            scores.append((arg, jnp.max(arg, axis=0, keepdims=True), s))

        inter = []
        for rev, pair, n in steps:
            dp = 2 * rev + pair
            a_row = rows_scr[n, 0, dp:dp + 1, :]
            b_end = rows_scr[n, 2, dp:dp + 1, :]
            a_max = rows_scr[n, 3, dp:dp + 1, :]
            m_mem = m_scr[dp:dp + 1, :]
            c_mem = c_scr[dp]
            inter.append((_dot(c_mem.astype(BF16), qtz_scr[n, pair]), m_mem))
            m_new = b_end + jnp.maximum(m_mem, a_max)
            w_src = jnp.exp(b_end + a_row - m_new)
            decay = jnp.exp(b_end + m_mem - m_new)
            decay = jnp.where(low, decay[:, :CHUNK], decay[:, CHUNK:])
            c_scr[dp] = decay * c_mem + _dot((vat_scr[n, pair].astype(F32) * w_src).astype(BF16),
                                             kbd_scr[n, pair])
            m_scr[dp:dp + 1, :] = m_new

        for (rev, pair, n), (arg, cm, s), (x2, m_mem) in zip(steps, scores, inter):
            dp = 2 * rev + pair
            g = (s * jnp.exp(arg - cm)).astype(BF16)
            g_bd = jnp.concatenate([jnp.concatenate([g[:, :CHUNK], zeros], axis=1),
                                    jnp.concatenate([zeros, g[:, CHUNK:]], axis=1)], axis=0)
            x1 = _dot(vat_scr[n, pair], g_bd)
            b_row = rows_scr[n, 1, dp:dp + 1, :]
            mu = jnp.maximum(cm, m_mem)
            both = x1 * jnp.exp(cm - mu) + x2 * jnp.exp(m_mem - mu)
            den = both[HEAD_DIM:HEAD_DIM + 1, :]
            inv = 1.0 / jnp.maximum(jnp.abs(den), jnp.exp(-(b_row + mu)))
            h_t = both[0:HEAD_DIM, :] * inv
            ht_scr[n, pair * LANES:(pair + 1) * LANES, :] += jnp.concatenate([h_t[:, :CHUNK], h_t[:, CHUNK:]],
                                                                             axis=0)
        return carry

    def finish(n, carry):
        rows = pl.ds(pl.multiple_of(n * CHUNK, CHUNK), CHUNK)
        gate = _sigmoid(og_ref[0, rows, :].astype(F32))
        ys = []
        for h in range(GROUP_HEADS):
            x = ht_scr[n, h * HEAD_DIM:(h + 1) * HEAD_DIM, :]
            mu = jnp.mean(x, axis=0, keepdims=True)
            var = jnp.mean(jnp.square(x - mu), axis=0, keepdims=True)
            ys.append((x - mu) * lax.rsqrt(var + EPS))
        y = jnp.concatenate(ys, axis=0).T * gln_ref[...]
        o_ref[0, rows, :] = (gate * y).astype(BF16)
        return carry

    return prep, reset_state, scan, finish


def _attn_mlstm_body(aq_ref, ak_ref, av_ref, by_drow_ref, mq_ref, mk_ref, mv_ref, og_ref, gate_ref, wc_ref,
                     bg_ref, cos_ref, sina_ref, sinb_ref, gln_ref, tri_ref, a_ref, m_ref, bias_scr, *scratch,
                     ctx_out):
    @pl.when(pl.program_id(0) == 0)
    def _():
        _expand_bias(by_drow_ref, bias_scr)

    context_rows, trip, n_trips = _attn_parts(aq_ref, ak_ref, av_ref, bias_scr, a_ref, ctx_out)
    prep, reset_state, scan, finish = _mlstm_parts(mq_ref, mk_ref, mv_ref, og_ref, gate_ref, wc_ref, bg_ref,
                                                   cos_ref, sina_ref, sinb_ref, gln_ref, tri_ref, m_ref, *scratch)
    per_trip = LAT_CHUNKS // n_trips

    def prep_context(n, carry):
        prep(n, False)
        return carry

    def trip_and_prep(g, carry):
        for j in range(per_trip):
            prep(g * per_trip + j, True)
        trip(g, carry)
        return carry

    lax.fori_loop(LAT_CHUNKS, N_CHUNKS, prep_context, 0)
    context_rows()
    lax.fori_loop(0, n_trips, trip_and_prep, 0)
    reset_state()
    lax.fori_loop(0, N_CHUNKS, scan, 0, unroll=6)
    lax.fori_loop(0, N_CHUNKS, finish, 0, unroll=3)


def _rope_tables():
    lane = jnp.arange(GROUP_WIDTH)
    m = HEAD_DIM // 4
    inv = ROPE_THETA ** (-(lane % m).astype(F32) / m)
    by_row = ((lane % HEAD_DIM) // (HEAD_DIM // 2) == 0)[None, None, :]
    low = ((lane % (2 * m)) < m)[None, None, :]
    ang = jnp.arange(GRID_W, dtype=F32)[:, None] * inv[None, :]
    cos, sin = jnp.cos(ang), jnp.sin(ang)

    def spread(tab):
        return jnp.where(by_row, tab[:GRID_ROWS, None, :], tab[None, :, :]).reshape(SEQ, GROUP_WIDTH)

    cos, sin = spread(cos), spread(sin)
    low = jnp.broadcast_to(low, (GRID_ROWS, GRID_W, GROUP_WIDTH)).reshape(SEQ, GROUP_WIDTH)
    return cos, jnp.where(low, -sin, 0.0), jnp.where(low, 0.0, sin)


def _attn_mlstm(p, by_drow, gates, w_conv, b_gate, g_mlstm, rope, ctx_out):
    bsz = p.shape[0]
    i = jnp.arange(CHUNK)
    tri = (i[:, None] >= i[None, :]).astype(F32)
    bg = jnp.pad(b_gate[jnp.array(GATE_I_COLS + GATE_F_COLS)], (0, LANES - N_GATES)).reshape(1, LANES)
    cos, sina, sinb = rope
    n_pairs = GROUP_HEADS // 2
    group_out = pl.BlockSpec((1, T_ALL, GROUP_WIDTH), lambda b: (b, 0, 0))
    return pl.pallas_call(
        functools.partial(_attn_mlstm_body, ctx_out=ctx_out),
        grid=(bsz,),
        in_specs=[_group_spec(0), _group_spec(1), _group_spec(2), _resident(by_drow.shape),
                  _group_spec(5), _group_spec(6), _group_spec(7), _group_spec(8),
                  pl.BlockSpec((1, T_ALL, LANES), lambda b: (b, 0, 0)),
                  _resident((CONV_W, 2 * GROUP_WIDTH)),
                  _resident((1, LANES)),
                  _resident((SEQ, GROUP_WIDTH)), _resident((SEQ, GROUP_WIDTH)), _resident((SEQ, GROUP_WIDTH)),
                  _resident((1, GROUP_WIDTH)),
                  _resident((CHUNK, CHUNK))],
        out_specs=[group_out, group_out],
        out_shape=[jax.ShapeDtypeStruct((bsz, T_ALL, GROUP_WIDTH), BF16)] * 2,
        scratch_shapes=[pltpu.VMEM((n_pairs, WIN_ROWS, 2 * GRID_W, WIN_ROWS * GRID_W), F32),
                        pltpu.VMEM((N_CHUNKS, n_pairs, LANES, 2 * CHUNK), BF16),
                        pltpu.VMEM((N_CHUNKS, CHUNK, GROUP_WIDTH), BF16),
                        pltpu.VMEM((N_CHUNKS, n_pairs, 2 * CHUNK, LANES), BF16),
                        pltpu.VMEM((N_CHUNKS, n_pairs, LANES, 2 * CHUNK), BF16),
                        pltpu.VMEM((N_CHUNKS, CHUNK, LANES), F32),
                        pltpu.VMEM((N_CHUNKS, 4, 2 * n_pairs, 2 * CHUNK), F32),
                        pltpu.VMEM((N_CHUNKS, GROUP_WIDTH, CHUNK), F32),
                        pltpu.VMEM((2 * n_pairs, LANES, LANES), F32),
                        pltpu.VMEM((2 * n_pairs, 2 * CHUNK), F32)],
        compiler_params=_params(1),
        name="attn_mlstm",
    )(p, p, p, by_drow, p, p, p, p, gates, w_conv, bg, cos, sina, sinb, g_mlstm.reshape(1, GROUP_WIDTH), tri)


def _dft_tables(n, scale):
    f = 1 << (n.bit_length() // 2)
    s = jnp.arange(n, dtype=jnp.int32)[None, :]
    ang_a = ((f * jnp.arange(n // f, dtype=jnp.int32)[:, None] * s) % n).astype(F32) * (2.0 * jnp.pi / n)
    ang_b = ((jnp.arange(f, dtype=jnp.int32)[:, None] * s) % n).astype(F32) * (2.0 * jnp.pi / n)
    ca, sa = jnp.cos(ang_a)[:, None, :], jnp.sin(ang_a)[:, None, :]
    cb, sb = jnp.cos(ang_b)[None, :, :], jnp.sin(ang_b)[None, :, :]
    cos = (ca * cb - sa * sb).reshape(n, n)
    sin = (sa * cb + ca * sb).reshape(n, n)
    return (cos * scale).astype(BF16), (-sin * scale).astype(BF16)


def _channel_dft():
    gc = GROUP_WIDTH // FNET_GROUPS
    j = jnp.arange(GROUP_WIDTH, dtype=jnp.int32)
    same = (j[:, None] // gc) == (j[None, :] // gc)
    ang = (((j[:, None] % gc) * (j[None, :] % gc)) % gc).astype(F32) * (2.0 * jnp.pi / gc)
    c = jnp.where(same, jnp.cos(ang), 0.0)
    s = jnp.where(same, jnp.sin(ang), 0.0)
    return jnp.concatenate([c, s], axis=1).astype(BF16)


def _fnet_gmlp_body(zc_ref, zs_ref, cl_ref, sl_ref, cc_ref, sc_ref, w_ref, u_ref, z_ref, ws_ref, bs_ref, g_ref,
                    o_ref, gm_ref):
    y = _dot(cl_ref[...], zc_ref[0:SEQ, :]) + _dot(sl_ref[...], zs_ref[0:SEQ, :])
    for n in range(N_CHUNKS):
        _gmlp_chunk(n, u_ref, z_ref, ws_ref, bs_ref, g_ref, gm_ref)
    o_ref[0:SEQ, :] = _dot(y.astype(BF16), w_ref[...]).astype(BF16)
    y = _dot(cc_ref[...], zc_ref[SEQ:, :]) + _dot(sc_ref[...], zs_ref[SEQ:, :])
    o_ref[SEQ:, :] = _dot(y.astype(BF16), w_ref[...]).astype(BF16)


def _fnet_gmlp(zc, zs, tabs, w_fnet, p, w_spatial, b_spatial, g_gmlp):
    bsz = p.shape[0]
    cl, sl, cc, sc = tabs
    col = pl.BlockSpec((T_ALL, GROUP_WIDTH), lambda b: (0, b))
    bias = jnp.repeat(b_spatial.T, HEAD_DIM, axis=1)
    return pl.pallas_call(
        _fnet_gmlp_body,
        grid=(bsz,),
        in_specs=[col, col, _resident((SEQ, SEQ)), _resident((SEQ, SEQ)),
                  _resident((CTX_LEN, CTX_LEN)), _resident((CTX_LEN, CTX_LEN)),
                  _resident((GROUP_WIDTH, GROUP_WIDTH)),
                  _group_spec(3), _group_spec(4),
                  _resident((GROUP_HEADS, CHUNK, CHUNK)),
                  _resident((CHUNK, GROUP_WIDTH)),
                  _resident((1, GROUP_WIDTH))],
        out_specs=[col, pl.BlockSpec((1, T_ALL, GROUP_WIDTH), lambda b: (b, 0, 0))],
        out_shape=[jax.ShapeDtypeStruct((T_ALL, bsz * GROUP_WIDTH), BF16),
                   jax.ShapeDtypeStruct((bsz, T_ALL, GROUP_WIDTH), BF16)],
        compiler_params=_params(1),
        name="fnet_gmlp",
    )(zc, zs, cl, sl, cc, sc, w_fnet.astype(BF16), p, p, w_spatial.astype(BF16), bias,
      g_gmlp.reshape(1, GROUP_WIDTH))


def _post_body(a_ref, b_ref, c_ref, d_ref, x_ref, cx_ref, ml_ref, mc_ref, g_ref, wo_ref, w1_ref, w2_ref, gf_ref,
               o_ref, h_scr, acc_scr, *, final):
    gw = GROUP_WIDTH
    tm = x_ref.shape[1]
    mod = _mod_rows(ml_ref, mc_ref, tm)
    y = (_dot(a_ref[0], wo_ref[0, 0:gw, :]) + _dot(b_ref[0], wo_ref[0, gw:2 * gw, :])
         + _dot(c_ref[0], wo_ref[0, 2 * gw:3 * gw, :]) + _dot(d_ref[...], wo_ref[0, 3 * gw:, :]))
    if final:
        x = x_ref[0]
    else:
        ctx_tail = pl.program_id(1) == pl.num_programs(1) - 1
        x = jnp.concatenate([x_ref[0, 0:tm - CTX_LEN, :], _tail_rows(x_ref, cx_ref, ctx_tail)], axis=0)
    x1 = x + mod(2) * y
    h = _rms(x1, g_ref[...]) * (1.0 + mod(4)) + mod(3)
    h_scr[...] = h.astype(BF16)
    step = 512

    def up(j):
        return jnp.maximum(_dot(h_scr[...], w1_ref[0, :, j:j + step]), 0.0)

    a = up(0)
    for j in range(0, D_FF, step):
        nxt = up(j + step) if j + step < D_FF else None
        part = _dot((a * a).astype(BF16), w2_ref[0, j:j + step, :])
        if j == 0:
            acc_scr[...] = part
        else:
            acc_scr[...] += part
        a = nxt
    x2 = x1 + mod(5) * acc_scr[...]
    if final:
        x2 = _rms(x2, gf_ref[...])
    o_ref[0] = x2


def _post(a, b_, c_, d, tokens, ctx, mod_l, g_ffn, w_out, w_ff1, w_ff2, layer, g_final, final):
    bsz = tokens.shape[0]
    tm, out_len = (LATENT_TILE, SEQ) if final else (TOKEN_TILE, T_ALL)
    tok = pl.BlockSpec((1, tm, D_MODEL), lambda b, t: (b, t, 0))
    grp = pl.BlockSpec((1, tm, GROUP_WIDTH), lambda b, t: (b, t, 0))
    operands, token_specs = _token_operands(tokens, ctx, tm)
    return pl.pallas_call(
        functools.partial(_post_body, final=final),
        grid=(bsz, out_len // tm),
        in_specs=[grp, grp, grp, pl.BlockSpec((tm, GROUP_WIDTH), lambda b, t: (t, b))] + token_specs + _MOD_SPECS
                 + [_resident((1, D_MODEL)),
                    _layer_resident((D_MODEL, D_MODEL), layer),
                    _layer_resident((D_MODEL, D_FF), layer),
                    _layer_resident((D_FF, D_MODEL), layer),
                    _resident((1, D_MODEL))],
        out_specs=tok,
        out_shape=jax.ShapeDtypeStruct((bsz, out_len, D_MODEL), F32),
        scratch_shapes=[pltpu.VMEM((tm, D_MODEL), BF16), pltpu.VMEM((tm, D_MODEL), F32)],
        compiler_params=_params(2),
        name="out_proj_mlp",
    )(a, b_, c_, d, *operands, mod_l, mod_l, g_ffn.reshape(1, D_MODEL), w_out, w_ff1, w_ff2,
      g_final.reshape(1, D_MODEL))


def kernel(x, c, ctx, c_ctx, w_ada, b_ada, g_norm_mix, g_norm_ffn, w_in, b_gate, w_conv_qk, rpb, w_spatial,
           b_spatial, g_gmlp, g_mlstm, w_fnet, w_out, w_ff1, w_ff2, g_final):
    bsz = x.shape[0]
    assert bsz <= CTX_MOD_ROW and x.shape[1:] == (SEQ, D_MODEL) and ctx.shape[1:] == (CTX_LEN, D_MODEL)
    depth = w_ada.shape[0]
    tokens, ctx_rows = x, ctx
    cc = jnp.zeros((MOD_ROWS, D_MODEL), F32).at[:bsz].set(c).at[CTX_MOD_ROW].set(c_ctx)
    mod = _ada_mod(cc, w_ada, b_ada).reshape(depth, MOD_ROWS, N_MOD, D_MODEL)

    rope = _rope_tables()
    dft_c = _channel_dft()
    tabs = (_dft_tables(SEQ, (SEQ * GROUP_WIDTH // FNET_GROUPS) ** -0.5)
            + _dft_tables(CTX_LEN, (CTX_LEN * GROUP_WIDTH // FNET_GROUPS) ** -0.5))

    w_main, w_gate = _projection_weights(w_in)
    w_out, w_ff1, w_ff2 = w_out.astype(BF16), w_ff1.astype(BF16), w_ff2.astype(BF16)
    for l in range(depth):
        p, gates, zc, zs = _in_proj(tokens, ctx_rows, mod[l], g_norm_mix[l], w_main, w_gate, l, dft_c)
        a, c_ = _attn_mlstm(p, _attention_bias(rpb[l]), gates, w_conv_qk[l], b_gate[l], g_mlstm[l], rope,
                            ctx_out=(l < depth - 1))
        d, b_ = _fnet_gmlp(zc, zs, tabs, w_fnet[l], p, w_spatial[l], b_spatial[l], g_gmlp[l])
        tokens = _post(a, b_, c_, d, tokens, ctx_rows, mod[l], g_norm_ffn[l], w_out, w_ff1, w_ff2, l,
                       g_final, final=(l == depth - 1))
        ctx_rows = None
    return tokens
```

```python
import functools

import jax
import jax.numpy as jnp
from jax import lax
from jax.experimental import pallas as pl
from jax.experimental.pallas import tpu as pltpu

D_MODEL = 1024
SEQ = 2048
GRID_W = 64
GRID_ROWS = SEQ // GRID_W
CTX_LEN = 256
T_ALL = CTX_LEN + SEQ
HEAD_DIM = 64
GROUP_WIDTH = 256
GROUP_HEADS = 4
WIN_ROWS = 8
WIN_COLS = 16
CHUNK = 128
N_CHUNKS = T_ALL // CHUNK
CTX_CHUNKS = CTX_LEN // CHUNK
LAT_CHUNKS = SEQ // CHUNK
CONV_W = 3
FNET_GROUPS = 4
ROPE_THETA = 10000.0
D_FF = 4 * D_MODEL
N_MOD = 6
EPS = 1e-6
NEG_INF = -1e30
N_GATES = 4 * GROUP_HEADS
OFF_G = 10 * GROUP_WIDTH
MOD_ROWS = 24
CTX_MOD_ROW = 16
LANES = 128
BF16_ROWS = 16
GATE_I_COLS = (0, 2, 8, 10, 1, 3, 9, 11)
GATE_F_COLS = (4, 6, 12, 14, 5, 7, 13, 15)
TOKEN_TILE = 768
PROJ_TILE = 1152
LATENT_TILE = 512
P_WIDTH = 9 * GROUP_WIDTH
VMEM_LIMIT = 56 * 1024 * 1024

F32 = jnp.float32
BF16 = jnp.bfloat16


def _dot(a, b):
    return jnp.dot(a, b, preferred_element_type=F32)


def _dot_nt(a, b):
    return lax.dot_general(a, b, (((1,), (1,)), ((), ())), preferred_element_type=F32)


def _dot_f32(a, b):
    return jnp.dot(a, b, preferred_element_type=F32, precision=lax.Precision.HIGHEST)


def _sigmoid(x):
    return 0.5 * (1.0 + jnp.tanh(0.5 * x))


def _resident(shape):
    nd = len(shape)
    return pl.BlockSpec(shape, lambda *_: (0,) * nd, pipeline_mode=pl.Buffered(1))


def _layer_resident(shape, layer):
    nd = len(shape)
    return pl.BlockSpec((1,) + tuple(shape), lambda *_: (layer,) + (0,) * nd, pipeline_mode=pl.Buffered(1))


def _params(n_axes):
    return pltpu.CompilerParams(dimension_semantics=("arbitrary",) * n_axes,
                                vmem_limit_bytes=VMEM_LIMIT)


def _mod_body(c_ref, w_ref, b_ref, o_ref):
    s = c_ref[...]
    s = s * jax.nn.sigmoid(s)
    o_ref[0] = _dot(s.astype(BF16), w_ref[0].astype(BF16)) + b_ref[0]


def _ada_mod(cc, w_ada, b_ada):
    depth, d, n = w_ada.shape
    tn = 1536
    return pl.pallas_call(
        _mod_body,
        grid=(depth, n // tn),
        in_specs=[pl.BlockSpec((MOD_ROWS, d), lambda l, j: (0, 0)),
                  pl.BlockSpec((1, d, tn), lambda l, j: (l, 0, j)),
                  pl.BlockSpec((1, 1, tn), lambda l, j: (l, 0, j))],
        out_specs=pl.BlockSpec((1, MOD_ROWS, tn), lambda l, j: (l, 0, j)),
        out_shape=jax.ShapeDtypeStruct((depth, MOD_ROWS, n), F32),
        compiler_params=_params(2),
        name="ada_mod",
    )(cc, w_ada, b_ada.reshape(depth, 1, n))


_MOD_SPECS = [pl.BlockSpec((1, N_MOD, D_MODEL), lambda b, t: (b, 0, 0)),
              pl.BlockSpec((1, N_MOD, D_MODEL), lambda b, t: (CTX_MOD_ROW, 0, 0))]


def _mod_rows(mod_lat_ref, mod_ctx_ref, tile):
    tok = pl.program_id(1) * tile + lax.broadcasted_iota(jnp.int32, (tile, 1), 0)
    is_ctx = tok >= SEQ
    return lambda k: jnp.where(is_ctx, mod_ctx_ref[0, k:k + 1, :], mod_lat_ref[0, k:k + 1, :])


def _rms(x, g):
    return x * lax.rsqrt(jnp.mean(x * x, axis=-1, keepdims=True) + EPS) * g


def _tail_rows(x_ref, c_ref, ctx_tail):
    head = x_ref.shape[1] - CTX_LEN
    return jnp.where(ctx_tail, c_ref[0], x_ref[0, head:, :])


def _inproj_body(x_ref, c_ref, ml_ref, mc_ref, g_ref, w_ref, wg_ref, dft_ref, p_ref, gate_ref, zc_ref, zs_ref,
                 lhs_scr, inv_scr):
    tm = x_ref.shape[1]
    head = tm - CTX_LEN
    ctx_tail = pl.program_id(1) == pl.num_programs(1) - 1
    shift_tail = jnp.where(ctx_tail, mc_ref[0, 0:1, :], ml_ref[0, 0:1, :])
    scale_tail = jnp.where(ctx_tail, mc_ref[0, 1:2, :], ml_ref[0, 1:2, :])
    lead = BF16_ROWS
    lhs_scr[0:lead, :] = jnp.concatenate([ml_ref[0, 0:1, :], shift_tail,
                                          jnp.zeros((lead - 2, D_MODEL), F32)], axis=0).astype(BF16)
    x_head = x_ref[0, 0:head, :]
    x_tail = _tail_rows(x_ref, c_ref, ctx_tail)
    lhs_scr[lead:lead + head, :] = (x_head * (g_ref[...] * (1.0 + ml_ref[0, 1:2, :]))).astype(BF16)
    lhs_scr[lead + head:, :] = (x_tail * (g_ref[...] * (1.0 + scale_tail))).astype(BF16)
    for rows, x in ((slice(0, head), x_head), (slice(head, tm), x_tail)):
        inv = lax.rsqrt(jnp.mean(x * x, axis=-1, keepdims=True) + EPS)
        inv_scr[rows, :] = jnp.broadcast_to(inv, (x.shape[0], LANES))

    def project(cols=None):
        w_t = wg_ref[0] if cols is None else w_ref[0, cols, :]
        n = w_t.shape[0] // LANES
        r = _dot_nt(lhs_scr[...], w_t)
        inv_head = jnp.concatenate([inv_scr[0:head, :]] * n, axis=1)
        inv_tail = jnp.concatenate([inv_scr[head:, :]] * n, axis=1)
        return jnp.concatenate([inv_head * r[lead:lead + head, :] + r[0:1, :],
                                inv_tail * r[lead + head:, :] + r[1:2, :]], axis=0)

    step = 2 * GROUP_WIDTH
    f = project(slice(P_WIDTH, OFF_G)).astype(BF16)
    p_ref[0, :, 0:step] = project(slice(0, step)).astype(BF16)
    z = _dot(f, dft_ref[...])
    zc_ref[...] = z[:, :GROUP_WIDTH].astype(BF16)
    zs_ref[...] = z[:, GROUP_WIDTH:].astype(BF16)
    for j in range(step, P_WIDTH - GROUP_WIDTH, step):
        p_ref[0, :, j:j + step] = project(slice(j, j + step)).astype(BF16)
    j = P_WIDTH - GROUP_WIDTH
    p_ref[0, :, j:] = project(slice(j, P_WIDTH)).astype(BF16)
    gate_ref[0] = project()


def _projection_weights(w_in):
    place = (jnp.array(GATE_I_COLS + GATE_F_COLS)[None, :] == jnp.arange(N_GATES)[:, None]).astype(F32)
    place = jnp.pad(place, ((0, 0), (0, LANES - N_GATES)))
    w_gate_t = jnp.einsum('ldg,gc->lcd', w_in[:, :, OFF_G:], place, precision=lax.Precision.HIGHEST)
    return jnp.swapaxes(w_in[:, :, :OFF_G], 1, 2).astype(BF16), w_gate_t.astype(BF16)


def _token_operands(tokens, ctx, tm):
    tile = pl.BlockSpec((1, tm, D_MODEL), lambda b, t: (b, t, 0))
    if ctx is None:
        return (tokens, tokens), [tile, pl.BlockSpec((1, CTX_LEN, D_MODEL), lambda b, t: (b, SEQ // CTX_LEN, 0))]
    return (tokens, ctx), [tile, pl.BlockSpec((1, CTX_LEN, D_MODEL), lambda b, t: (b, 0, 0))]


def _in_proj(tokens, ctx, mod_l, g, w_main, w_gate, layer, dft_c):
    bsz = tokens.shape[0]
    tm = PROJ_TILE
    operands, token_specs = _token_operands(tokens, ctx, tm)
    return pl.pallas_call(
        _inproj_body,
        grid=(bsz, T_ALL // tm),
        in_specs=token_specs + _MOD_SPECS
                 + [_resident((1, D_MODEL)),
                    _layer_resident(w_main.shape[1:], layer),
                    _layer_resident(w_gate.shape[1:], layer),
                    _resident((GROUP_WIDTH, 2 * GROUP_WIDTH))],
        out_specs=[pl.BlockSpec((1, tm, P_WIDTH), lambda b, t: (b, t, 0)),
                   pl.BlockSpec((1, tm, LANES), lambda b, t: (b, t, 0)),
                   pl.BlockSpec((tm, GROUP_WIDTH), lambda b, t: (t, b)),
                   pl.BlockSpec((tm, GROUP_WIDTH), lambda b, t: (t, b))],
        out_shape=[jax.ShapeDtypeStruct((bsz, T_ALL, P_WIDTH), BF16),
                   jax.ShapeDtypeStruct((bsz, T_ALL, LANES), F32),
                   jax.ShapeDtypeStruct((T_ALL, bsz * GROUP_WIDTH), BF16),
                   jax.ShapeDtypeStruct((T_ALL, bsz * GROUP_WIDTH), BF16)],
        scratch_shapes=[pltpu.VMEM((tm + BF16_ROWS, D_MODEL), BF16), pltpu.VMEM((tm, LANES), F32)],
        compiler_params=_params(2),
        name="in_proj",
    )(*operands, mod_l, mod_l, g.reshape(1, D_MODEL), w_main, w_gate, dft_c)


def _group_spec(col_block):
    return pl.BlockSpec((1, T_ALL, GROUP_WIDTH), lambda b: (b, 0, col_block))


def _softmax_pv(parts):
    m = functools.reduce(jnp.maximum, [jnp.max(s, axis=-1, keepdims=True) for s, _ in parts])
    es = [jnp.exp(s - m) for s, _ in parts]
    den = functools.reduce(jnp.add, [jnp.sum(e, axis=-1, keepdims=True) for e in es])
    num = functools.reduce(jnp.add, [_dot(e.astype(BF16), v) for e, (_, v) in zip(es, parts)])
    return num / den


def _attn_parts(q_ref, k_ref, v_ref, bias_ref, o_ref, ctx_out):
    scale = HEAD_DIM ** -0.5
    pairs = [slice(i * LANES, (i + 1) * LANES) for i in range(GROUP_WIDTH // LANES)]

    def stack_pair(q2):
        low = lax.broadcasted_iota(jnp.int32, q2.shape, 1) < HEAD_DIM
        zero = jnp.zeros_like(q2)
        return low, jnp.concatenate([jnp.where(low, q2, zero), jnp.where(low, zero, q2)], axis=0)

    def unstack_pair(low, o):
        n = o.shape[0] // 2
        return jnp.where(low, o[:n], o[n:]).astype(BF16)

    ctx = slice(SEQ, T_ALL)

    def context_rows():
        if ctx_out:
            ctx_tiles = []
            for ps in pairs:
                low, qm = stack_pair(q_ref[0, ctx, ps] * scale)
                ctx_tiles.append((low, _dot_nt(qm, k_ref[0, ctx, ps])))
            for ps, (low, s) in zip(pairs, ctx_tiles):
                o_ref[0, ctx, ps] = unstack_pair(low, _softmax_pv([(s, v_ref[0, ctx, ps])]))
        else:
            o_ref[0, ctx, :] = jnp.zeros((CTX_LEN, GROUP_WIDTH), BF16)

    rows_per_trip = 4
    tile_rows = 2 * GRID_W

    def trip(g, carry):
        tiles = [(dr, i) for dr in range(rows_per_trip) for i in range(len(pairs))]
        q0s = [pl.multiple_of((g * rows_per_trip + dr) * GRID_W, GRID_W) for dr in range(rows_per_trip)]
        stacked = {(dr, i): stack_pair(q_ref[0, pl.ds(q0s[dr], GRID_W), pairs[i]] * scale) for dr, i in tiles}
        s_ctx = [_dot_nt(jnp.concatenate([stacked[dr, i][1] for dr in range(rows_per_trip)], axis=0),
                         k_ref[0, ctx, pairs[i]]) for i in range(len(pairs))]

        def window_scores(dr, i):
            r = g * rows_per_trip + dr
            rs = jnp.clip(r - WIN_ROWS // 2, 0, GRID_ROWS - WIN_ROWS)
            k0 = pl.multiple_of(rs * GRID_W, GRID_W)
            s_loc = _dot_nt(stacked[dr, i][1], k_ref[0, pl.ds(k0, WIN_ROWS * GRID_W), pairs[i]])
            return s_loc + bias_ref[i, r - rs], k0

        partial = {}
        cur = window_scores(*tiles[0])
        for j, (dr, i) in enumerate(tiles):
            nxt = window_scores(*tiles[j + 1]) if j + 1 < len(tiles) else None
            s_loc, k0 = cur
            s_c = s_ctx[i][dr * tile_rows:(dr + 1) * tile_rows, :]
            m = jnp.maximum(jnp.max(s_loc, axis=-1, keepdims=True), jnp.max(s_c, axis=-1, keepdims=True))
            e_loc, e_ctx = jnp.exp(s_loc - m), jnp.exp(s_c - m)
            den = jnp.sum(e_loc, axis=-1, keepdims=True) + jnp.sum(e_ctx, axis=-1, keepdims=True)
            num = _dot(e_loc.astype(BF16), v_ref[0, pl.ds(k0, WIN_ROWS * GRID_W), pairs[i]])
            partial[dr, i] = (num, e_ctx.astype(BF16), den)
            cur = nxt

        for i, ps in enumerate(pairs):
            num_ctx = _dot(jnp.concatenate([partial[dr, i][1] for dr in range(rows_per_trip)], axis=0),
                           v_ref[0, ctx, ps])
            for dr in range(rows_per_trip):
                num, _, den = partial[dr, i]
                o = (num + num_ctx[dr * tile_rows:(dr + 1) * tile_rows, :]) / den
                o_ref[0, pl.ds(q0s[dr], GRID_W), ps] = unstack_pair(stacked[dr, i][0], o)
        return carry

    return context_rows, trip, GRID_ROWS // rows_per_trip


def _attention_bias(rpb):
    c = jnp.arange(GRID_W)
    qstart = jnp.clip(c - WIN_COLS // 2, 0, GRID_W - WIN_COLS)
    in_win = (c[None, :] >= qstart[:, None]) & (c[None, :] < qstart[:, None] + WIN_COLS)
    col = jnp.clip(c[None, :] - c[:, None] + WIN_COLS - 1, 0, 2 * WIN_COLS - 2)
    pick_col = (col[:, :, None] == jnp.arange(2 * WIN_COLS - 1)).astype(F32)
    by_drow = jnp.einsum('hde,qke->hdqk', rpb.astype(F32), pick_col, precision=lax.Precision.HIGHEST)
    return jnp.where(in_win, by_drow, NEG_INF)


def _expand_bias(by_drow_ref, bias_scr):
    for pair in range(GROUP_HEADS // 2):
        for v in range(WIN_ROWS):
            for hh in range(2):
                for j in range(0, WIN_ROWS, 2):
                    two_rows = [by_drow_ref[2 * pair + hh, jj - v + WIN_ROWS - 1] for jj in (j, j + 1)]
                    bias_scr[pair, v, hh * GRID_W:(hh + 1) * GRID_W, j * GRID_W:(j + 2) * GRID_W] = (
                        jnp.concatenate(two_rows, axis=1))


def _gmlp_chunk(n, u_ref, z_ref, ws_ref, bs_ref, g_ref, o_ref):
    rows = slice(n * CHUNK, (n + 1) * CHUNK)
    z = _rms(jax.nn.gelu(z_ref[0, rows, :].astype(F32)), g_ref[...]).astype(BF16)
    u = jax.nn.gelu(u_ref[0, rows, :].astype(F32))
    for h in range(GROUP_HEADS):
        hs = slice(h * HEAD_DIM, (h + 1) * HEAD_DIM)
        mixed = _dot(ws_ref[h], z[:, hs]) + bs_ref[:, hs]
        o_ref[0, rows, hs] = (u[:, hs] * mixed).astype(BF16)


def _mlstm_parts(q_ref, k_ref, v_ref, og_ref, gate_ref, wc_ref, bg_ref, cos_ref, sina_ref,
                 sinb_ref, gln_ref, tri_ref, o_ref,
                 qtz_scr, k_scr, kbd_scr, vat_scr, acol_scr, rows_scr, ht_scr, c_scr, m_scr):
    half = HEAD_DIM // 4
    n_chains = 2 * GROUP_HEADS

    def conv_act(ref, n, w0, w1, w2, rope, post):
        rows = pl.ds(pl.multiple_of(n * CHUNK, CHUNK), CHUNK)
        cur = ref[0, rows, :].astype(F32)
        first = (n == 0) | (n == LAT_CHUNKS)
        last = (n == LAT_CHUNKS - 1) | (n == N_CHUNKS - 1)
        before = pl.ds(pl.multiple_of(jnp.maximum(n * CHUNK - BF16_ROWS, 0), BF16_ROWS), BF16_ROWS)
        after = pl.ds(pl.multiple_of(jnp.minimum((n + 1) * CHUNK, T_ALL - BF16_ROWS), BF16_ROWS), BF16_ROWS)
        tail = jnp.where(first, 0.0, 1.0) * ref[0, before, :][BF16_ROWS - 1:BF16_ROWS, :].astype(F32)
        head = jnp.where(last, 0.0, 1.0) * ref[0, after, :][0:1, :].astype(F32)
        sub = lax.broadcasted_iota(jnp.int32, (CHUNK, 1), 0)
        x_prev = jnp.where(sub == 0, tail, pltpu.roll(cur, 1, 0))
        x_next = jnp.where(sub == CHUNK - 1, head, pltpu.roll(cur, CHUNK - 1, 0))
        y = w0 * x_prev + w1 * cur + w2 * x_next
        half_y = 0.5 * y
        y = half_y + half_y * jnp.tanh(half_y)
        if rope:
            y = (y * cos_ref[rows, :] + pltpu.roll(y, GROUP_WIDTH - half, 1) * sina_ref[rows, :]
                 + pltpu.roll(y, half, 1) * sinb_ref[rows, :])
        return y * post

    def prep(n, rope):
        rows = pl.ds(pl.multiple_of(n * CHUNK, CHUNK), CHUNK)
        gw = GROUP_WIDTH
        q = conv_act(q_ref, n, wc_ref[0:1, :gw], wc_ref[1:2, :gw], wc_ref[2:3, :gw], rope, 1.0)
        k = conv_act(k_ref, n, wc_ref[0:1, gw:], wc_ref[1:2, gw:], wc_ref[2:3, gw:], rope, HEAD_DIM ** -0.5)
        q_t = q.T.astype(BF16)
        v_t = v_ref[0, rows, :].astype(F32).T.astype(BF16)
        k = k.astype(BF16)
        k_scr[n] = k
        sub = lax.broadcasted_iota(jnp.int32, (HEAD_DIM, CHUNK), 0)
        ones_row = jnp.where(sub == 0, 1.0, 0.0).astype(BF16)
        zeros = jnp.zeros((HEAD_DIM, CHUNK), BF16)
        low = lax.broadcasted_iota(jnp.int32, (CHUNK, LANES), 1) < HEAD_DIM
        for pair in range(GROUP_HEADS // 2):
            h0 = slice(2 * pair * HEAD_DIM, (2 * pair + 1) * HEAD_DIM)
            h1 = slice((2 * pair + 1) * HEAD_DIM, (2 * pair + 2) * HEAD_DIM)
            qtz_scr[n, pair] = jnp.concatenate([jnp.concatenate([q_t[h0, :], zeros], axis=1),
                                                jnp.concatenate([zeros, q_t[h1, :]], axis=1)], axis=0)
            k_pair = k[:, pair * LANES:(pair + 1) * LANES]
            kbd_scr[n, pair] = jnp.concatenate([jnp.where(low, k_pair, jnp.zeros_like(k_pair)),
                                                jnp.where(low, jnp.zeros_like(k_pair), k_pair)], axis=0)
            vat_scr[n, pair] = jnp.concatenate([jnp.concatenate([v_t[h0, :], ones_row], axis=0),
                                                jnp.concatenate([v_t[h1, :], ones_row], axis=0)], axis=1)
        ht_scr[n] = jnp.zeros((GROUP_WIDTH, CHUNK), F32)

        g_i = gate_ref[0, rows, :] + bg_ref[...]
        lf = pltpu.roll(jax.nn.log_sigmoid(g_i), LANES - n_chains, 1)
        pre = _dot_f32(tri_ref[...], lf)
        suf = pre[CHUNK - 1:CHUNK, :] - pre + lf
        lane = lax.broadcasted_iota(jnp.int32, (CHUNK, LANES), 1)
        b = jnp.where(lane % 4 < 2, pre, suf)
        a = g_i - b
        acol_scr[n] = a
        a_t = a.T[0:n_chains, :]
        b_t = b.T[0:n_chains, :]
        chain = lax.broadcasted_iota(jnp.int32, (n_chains, CHUNK), 0)
        b_end = jnp.where(chain % 4 < 2, b_t[:, CHUNK - 1:CHUNK], b_t[:, 0:1])
        b_end = jnp.broadcast_to(b_end, (n_chains, CHUNK))
        a_max = jnp.broadcast_to(jnp.max(a_t, axis=1, keepdims=True), (n_chains, CHUNK))
        for kind, rows8 in enumerate((a_t, b_t, b_end, a_max)):
            rows_scr[n, kind] = jnp.concatenate([rows8[0:4, :], rows8[4:8, :]], axis=1)

    def reset_state():
        c_scr[...] = jnp.zeros(c_scr.shape, F32)
        m_scr[...] = jnp.zeros(m_scr.shape, F32)

    s_idx = lax.broadcasted_iota(jnp.int32, (CHUNK, CHUNK), 0)
    t_idx = lax.broadcasted_iota(jnp.int32, (CHUNK, CHUNK), 1)

    zeros = jnp.zeros((CHUNK, CHUNK), BF16)
    low = lax.broadcasted_iota(jnp.int32, (1, LANES), 1) < HEAD_DIM

    def scan(i, carry):
        chunk_of = (jnp.where(i < CTX_CHUNKS, LAT_CHUNKS + i, i - CTX_CHUNKS), N_CHUNKS - 1 - i)
        steps = [(rev, pair, chunk_of[rev]) for rev in range(2) for pair in range(GROUP_HEADS // 2)]

        scores = []
        for rev, pair, n in steps:
            dp = 2 * rev + pair
            causal = (s_idx >= t_idx) if rev else (s_idx <= t_idx)
            arg = jnp.concatenate([jnp.where(causal, acol_scr[n, :, dp:dp + 1], NEG_INF),
                                   jnp.where(causal, acol_scr[n, :, 4 + dp:5 + dp], NEG_INF)], axis=1)
            s = _dot(k_scr[n, :, pair * LANES:(pair + 1) * LANES], qtz_scr[n, pair])
            scores.append((arg, jnp.max(arg, axis=0, keepdims=True), s))

        inter = []
        for rev, pair, n in steps:
            dp = 2 * rev + pair
            a_row = rows_scr[n, 0, dp:dp + 1, :]
            b_end = rows_scr[n, 2, dp:dp + 1, :]
            a_max = rows_scr[n, 3, dp:dp + 1, :]
            m_mem = m_scr[dp:dp + 1, :]
            c_mem = c_scr[dp]
            inter.append((_dot(c_mem.astype(BF16), qtz_scr[n, pair]), m_mem))
            m_new = b_end + jnp.maximum(m_mem, a_max)
            w_src = jnp.exp(b_end + a_row - m_new)
            decay = jnp.exp(b_end + m_mem - m_new)
            decay = jnp.where(low, decay[:, :CHUNK], decay[:, CHUNK:])
            c_scr[dp] = decay * c_mem + _dot((vat_scr[n, pair].astype(F32) * w_src).astype(BF16),
                                             kbd_scr[n, pair])
            m_scr[dp:dp + 1, :] = m_new

        for (rev, pair, n), (arg, cm, s), (x2, m_mem) in zip(steps, scores, inter):
            dp = 2 * rev + pair
            g = (s * jnp.exp(arg - cm)).astype(BF16)
            g_bd = jnp.concatenate([jnp.concatenate([g[:, :CHUNK], zeros], axis=1),
                                    jnp.concatenate([zeros, g[:, CHUNK:]], axis=1)], axis=0)
            x1 = _dot(vat_scr[n, pair], g_bd)
            b_row = rows_scr[n, 1, dp:dp + 1, :]
            mu = jnp.maximum(cm, m_mem)
            both = x1 * jnp.exp(cm - mu) + x2 * jnp.exp(m_mem - mu)
            den = both[HEAD_DIM:HEAD_DIM + 1, :]
            inv = 1.0 / jnp.maximum(jnp.abs(den), jnp.exp(-(b_row + mu)))
            h_t = both[0:HEAD_DIM, :] * inv
            ht_scr[n, pair * LANES:(pair + 1) * LANES, :] += jnp.concatenate([h_t[:, :CHUNK], h_t[:, CHUNK:]],
                                                                             axis=0)
        return carry

    def finish(n, carry):
        rows = pl.ds(pl.multiple_of(n * CHUNK, CHUNK), CHUNK)
        gate = _sigmoid(og_ref[0, rows, :].astype(F32))
        ys = []
        for h in range(GROUP_HEADS):
            x = ht_scr[n, h * HEAD_DIM:(h + 1) * HEAD_DIM, :]
            mu = jnp.mean(x, axis=0, keepdims=True)
            var = jnp.mean(jnp.square(x - mu), axis=0, keepdims=True)
            ys.append((x - mu) * lax.rsqrt(var + EPS))
        y = jnp.concatenate(ys, axis=0).T * gln_ref[...]
        o_ref[0, rows, :] = (gate * y).astype(BF16)
        return carry

    return prep, reset_state, scan, finish


def _attn_mlstm_body(aq_ref, ak_ref, av_ref, by_drow_ref, mq_ref, mk_ref, mv_ref, og_ref, gate_ref, wc_ref,
                     bg_ref, cos_ref, sina_ref, sinb_ref, gln_ref, tri_ref, a_ref, m_ref, bias_scr, *scratch,
                     ctx_out):
    @pl.when(pl.program_id(0) == 0)
    def _():
        _expand_bias(by_drow_ref, bias_scr)

    context_rows, trip, n_trips = _attn_parts(aq_ref, ak_ref, av_ref, bias_scr, a_ref, ctx_out)
    prep, reset_state, scan, finish = _mlstm_parts(mq_ref, mk_ref, mv_ref, og_ref, gate_ref, wc_ref, bg_ref,
                                                   cos_ref, sina_ref, sinb_ref, gln_ref, tri_ref, m_ref, *scratch)
    per_trip = LAT_CHUNKS // n_trips

    def trip_and_prep(g, carry):
        for j in range(per_trip):
            prep(g * per_trip + j, True)
        trip(g, carry)
        return carry

    for n in range(LAT_CHUNKS, N_CHUNKS):
        prep(n, False)
    context_rows()
    lax.fori_loop(0, n_trips, trip_and_prep, 0)
    reset_state()
    lax.fori_loop(0, N_CHUNKS, scan, 0, unroll=9)
    lax.fori_loop(0, N_CHUNKS, finish, 0, unroll=3)


def _rope_tables():
    lane = jnp.arange(GROUP_WIDTH)
    m = HEAD_DIM // 4
    inv = ROPE_THETA ** (-(lane % m).astype(F32) / m)
    by_row = ((lane % HEAD_DIM) // (HEAD_DIM // 2) == 0)[None, None, :]
    low = ((lane % (2 * m)) < m)[None, None, :]
    ang = jnp.arange(GRID_W, dtype=F32)[:, None] * inv[None, :]
    cos, sin = jnp.cos(ang), jnp.sin(ang)

    def spread(tab):
        return jnp.where(by_row, tab[:GRID_ROWS, None, :], tab[None, :, :]).reshape(SEQ, GROUP_WIDTH)

    cos, sin = spread(cos), spread(sin)
    low = jnp.broadcast_to(low, (GRID_ROWS, GRID_W, GROUP_WIDTH)).reshape(SEQ, GROUP_WIDTH)
    return cos, jnp.where(low, -sin, 0.0), jnp.where(low, 0.0, sin)


def _attn_mlstm(p, by_drow, gates, w_conv, b_gate, g_mlstm, rope, ctx_out):
    bsz = p.shape[0]
    i = jnp.arange(CHUNK)
    tri = (i[:, None] >= i[None, :]).astype(F32)
    bg = jnp.pad(b_gate[jnp.array(GATE_I_COLS + GATE_F_COLS)], (0, LANES - N_GATES)).reshape(1, LANES)
    cos, sina, sinb = rope
    n_pairs = GROUP_HEADS // 2
    group_out = pl.BlockSpec((1, T_ALL, GROUP_WIDTH), lambda b: (b, 0, 0))
    return pl.pallas_call(
        functools.partial(_attn_mlstm_body, ctx_out=ctx_out),
        grid=(bsz,),
        in_specs=[_group_spec(0), _group_spec(1), _group_spec(2), _resident(by_drow.shape),
                  _group_spec(5), _group_spec(6), _group_spec(7), _group_spec(8),
                  pl.BlockSpec((1, T_ALL, LANES), lambda b: (b, 0, 0)),
                  _resident((CONV_W, 2 * GROUP_WIDTH)),
                  _resident((1, LANES)),
                  _resident((SEQ, GROUP_WIDTH)), _resident((SEQ, GROUP_WIDTH)), _resident((SEQ, GROUP_WIDTH)),
                  _resident((1, GROUP_WIDTH)),
                  _resident((CHUNK, CHUNK))],
        out_specs=[group_out, group_out],
        out_shape=[jax.ShapeDtypeStruct((bsz, T_ALL, GROUP_WIDTH), BF16)] * 2,
        scratch_shapes=[pltpu.VMEM((n_pairs, WIN_ROWS, 2 * GRID_W, WIN_ROWS * GRID_W), F32),
                        pltpu.VMEM((N_CHUNKS, n_pairs, LANES, 2 * CHUNK), BF16),
                        pltpu.VMEM((N_CHUNKS, CHUNK, GROUP_WIDTH), BF16),
                        pltpu.VMEM((N_CHUNKS, n_pairs, 2 * CHUNK, LANES), BF16),
                        pltpu.VMEM((N_CHUNKS, n_pairs, LANES, 2 * CHUNK), BF16),
                        pltpu.VMEM((N_CHUNKS, CHUNK, LANES), F32),
                        pltpu.VMEM((N_CHUNKS, 4, 2 * n_pairs, 2 * CHUNK), F32),
                        pltpu.VMEM((N_CHUNKS, GROUP_WIDTH, CHUNK), F32),
                        pltpu.VMEM((2 * n_pairs, LANES, LANES), F32),
                        pltpu.VMEM((2 * n_pairs, 2 * CHUNK), F32)],
        compiler_params=_params(1),
        name="attn_mlstm",
    )(p, p, p, by_drow, p, p, p, p, gates, w_conv, bg, cos, sina, sinb, g_mlstm.reshape(1, GROUP_WIDTH), tri)


def _dft_tables(n, scale):
    f = 1 << (n.bit_length() // 2)
    s = jnp.arange(n, dtype=jnp.int32)[None, :]
    ang_a = ((f * jnp.arange(n // f, dtype=jnp.int32)[:, None] * s) % n).astype(F32) * (2.0 * jnp.pi / n)
    ang_b = ((jnp.arange(f, dtype=jnp.int32)[:, None] * s) % n).astype(F32) * (2.0 * jnp.pi / n)
    ca, sa = jnp.cos(ang_a)[:, None, :], jnp.sin(ang_a)[:, None, :]
    cb, sb = jnp.cos(ang_b)[None, :, :], jnp.sin(ang_b)[None, :, :]
    cos = (ca * cb - sa * sb).reshape(n, n)
    sin = (sa * cb + ca * sb).reshape(n, n)
    return (cos * scale).astype(BF16), (-sin * scale).astype(BF16)


def _channel_dft():
    gc = GROUP_WIDTH // FNET_GROUPS
    j = jnp.arange(GROUP_WIDTH, dtype=jnp.int32)
    same = (j[:, None] // gc) == (j[None, :] // gc)
    ang = (((j[:, None] % gc) * (j[None, :] % gc)) % gc).astype(F32) * (2.0 * jnp.pi / gc)
    c = jnp.where(same, jnp.cos(ang), 0.0)
    s = jnp.where(same, jnp.sin(ang), 0.0)
    return jnp.concatenate([c, s], axis=1).astype(BF16)


def _fnet_gmlp_body(zc_ref, zs_ref, cl_ref, sl_ref, cc_ref, sc_ref, w_ref, u_ref, z_ref, ws_ref, bs_ref, g_ref,
                    o_ref, gm_ref):
    y = _dot(cl_ref[...], zc_ref[0:SEQ, :]) + _dot(sl_ref[...], zs_ref[0:SEQ, :])
    for n in range(N_CHUNKS):
        _gmlp_chunk(n, u_ref, z_ref, ws_ref, bs_ref, g_ref, gm_ref)
    o_ref[0:SEQ, :] = _dot(y.astype(BF16), w_ref[...]).astype(BF16)
    y = _dot(cc_ref[...], zc_ref[SEQ:, :]) + _dot(sc_ref[...], zs_ref[SEQ:, :])
    o_ref[SEQ:, :] = _dot(y.astype(BF16), w_ref[...]).astype(BF16)


def _fnet_gmlp(zc, zs, tabs, w_fnet, p, w_spatial, b_spatial, g_gmlp):
    bsz = p.shape[0]
    cl, sl, cc, sc = tabs
    col = pl.BlockSpec((T_ALL, GROUP_WIDTH), lambda b: (0, b))
    bias = jnp.repeat(b_spatial.T, HEAD_DIM, axis=1)
    return pl.pallas_call(
        _fnet_gmlp_body,
        grid=(bsz,),
        in_specs=[col, col, _resident((SEQ, SEQ)), _resident((SEQ, SEQ)),
                  _resident((CTX_LEN, CTX_LEN)), _resident((CTX_LEN, CTX_LEN)),
                  _resident((GROUP_WIDTH, GROUP_WIDTH)),
                  _group_spec(3), _group_spec(4),
                  _resident((GROUP_HEADS, CHUNK, CHUNK)),
                  _resident((CHUNK, GROUP_WIDTH)),
                  _resident((1, GROUP_WIDTH))],
        out_specs=[col, pl.BlockSpec((1, T_ALL, GROUP_WIDTH), lambda b: (b, 0, 0))],
        out_shape=[jax.ShapeDtypeStruct((T_ALL, bsz * GROUP_WIDTH), BF16),
                   jax.ShapeDtypeStruct((bsz, T_ALL, GROUP_WIDTH), BF16)],
        compiler_params=_params(1),
        name="fnet_gmlp",
    )(zc, zs, cl, sl, cc, sc, w_fnet.astype(BF16), p, p, w_spatial.astype(BF16), bias,
      g_gmlp.reshape(1, GROUP_WIDTH))


def _post_body(a_ref, b_ref, c_ref, d_ref, x_ref, cx_ref, ml_ref, mc_ref, g_ref, wo_ref, w1_ref, w2_ref, gf_ref,
               o_ref, h_scr, acc_scr, *, final):
    gw = GROUP_WIDTH
    tm = x_ref.shape[1]
    mod = _mod_rows(ml_ref, mc_ref, tm)
    y = (_dot(a_ref[0], wo_ref[0, 0:gw, :]) + _dot(b_ref[0], wo_ref[0, gw:2 * gw, :])
         + _dot(c_ref[0], wo_ref[0, 2 * gw:3 * gw, :]) + _dot(d_ref[...], wo_ref[0, 3 * gw:, :]))
    if final:
        x = x_ref[0]
    else:
        ctx_tail = pl.program_id(1) == pl.num_programs(1) - 1
        x = jnp.concatenate([x_ref[0, 0:tm - CTX_LEN, :], _tail_rows(x_ref, cx_ref, ctx_tail)], axis=0)
    x1 = x + mod(2) * y
    h = _rms(x1, g_ref[...]) * (1.0 + mod(4)) + mod(3)
    h_scr[...] = h.astype(BF16)
    step = 512

    def up(j):
        return jnp.maximum(_dot(h_scr[...], w1_ref[0, :, j:j + step]), 0.0)

    a = up(0)
    for j in range(0, D_FF, step):
        nxt = up(j + step) if j + step < D_FF else None
        part = _dot((a * a).astype(BF16), w2_ref[0, j:j + step, :])
        if j == 0:
            acc_scr[...] = part
        else:
            acc_scr[...] += part
        a = nxt
    x2 = x1 + mod(5) * acc_scr[...]
    if final:
        x2 = _rms(x2, gf_ref[...])
    o_ref[0] = x2


def _post(a, b_, c_, d, tokens, ctx, mod_l, g_ffn, w_out, w_ff1, w_ff2, layer, g_final, final):
    bsz = tokens.shape[0]
    tm, out_len = (LATENT_TILE, SEQ) if final else (TOKEN_TILE, T_ALL)
    tok = pl.BlockSpec((1, tm, D_MODEL), lambda b, t: (b, t, 0))
    grp = pl.BlockSpec((1, tm, GROUP_WIDTH), lambda b, t: (b, t, 0))
    operands, token_specs = _token_operands(tokens, ctx, tm)
    return pl.pallas_call(
        functools.partial(_post_body, final=final),
        grid=(bsz, out_len // tm),
        in_specs=[grp, grp, grp, pl.BlockSpec((tm, GROUP_WIDTH), lambda b, t: (t, b))] + token_specs + _MOD_SPECS
                 + [_resident((1, D_MODEL)),
                    _layer_resident((D_MODEL, D_MODEL), layer),
                    _layer_resident((D_MODEL, D_FF), layer),
                    _layer_resident((D_FF, D_MODEL), layer),
                    _resident((1, D_MODEL))],
        out_specs=tok,
        out_shape=jax.ShapeDtypeStruct((bsz, out_len, D_MODEL), F32),
        scratch_shapes=[pltpu.VMEM((tm, D_MODEL), BF16), pltpu.VMEM((tm, D_MODEL), F32)],
        compiler_params=_params(2),
        name="out_proj_mlp",
    )(a, b_, c_, d, *operands, mod_l, mod_l, g_ffn.reshape(1, D_MODEL), w_out, w_ff1, w_ff2,
      g_final.reshape(1, D_MODEL))


def kernel(x, c, ctx, c_ctx, w_ada, b_ada, g_norm_mix, g_norm_ffn, w_in, b_gate, w_conv_qk, rpb, w_spatial,
           b_spatial, g_gmlp, g_mlstm, w_fnet, w_out, w_ff1, w_ff2, g_final):
    bsz = x.shape[0]
    assert bsz <= CTX_MOD_ROW and x.shape[1:] == (SEQ, D_MODEL) and ctx.shape[1:] == (CTX_LEN, D_MODEL)
    depth = w_ada.shape[0]
    tokens, ctx_rows = x, ctx
    cc = jnp.zeros((MOD_ROWS, D_MODEL), F32).at[:bsz].set(c).at[CTX_MOD_ROW].set(c_ctx)
    mod = _ada_mod(cc, w_ada, b_ada).reshape(depth, MOD_ROWS, N_MOD, D_MODEL)

    rope = _rope_tables()
    dft_c = _channel_dft()
    tabs = (_dft_tables(SEQ, (SEQ * GROUP_WIDTH // FNET_GROUPS) ** -0.5)
            + _dft_tables(CTX_LEN, (CTX_LEN * GROUP_WIDTH // FNET_GROUPS) ** -0.5))

    w_main, w_gate = _projection_weights(w_in)
    w_out, w_ff1, w_ff2 = w_out.astype(BF16), w_ff1.astype(BF16), w_ff2.astype(BF16)
    for l in range(depth):
        p, gates, zc, zs = _in_proj(tokens, ctx_rows, mod[l], g_norm_mix[l], w_main, w_gate, l, dft_c)
        a, c_ = _attn_mlstm(p, _attention_bias(rpb[l]), gates, w_conv_qk[l], b_gate[l], g_mlstm[l], rope,
                            ctx_out=(l < depth - 1))
        d, b_ = _fnet_gmlp(zc, zs, tabs, w_fnet[l], p, w_spatial[l], b_spatial[l], g_gmlp[l])
        tokens = _post(a, b_, c_, d, tokens, ctx_rows, mod[l], g_norm_ffn[l], w_out, w_ff1, w_ff2, l,
                       g_final, final=(l == depth - 1))
        ctx_rows = None
    return tokens
```
